```python
import jax, jax.numpy as jnp
from jax import lax
import numpy as np

D_MODEL = 1024
BATCH = 8
SEQ = 2048
DEPTH = 1
DEC_BATCH = 128
DEC_SEQ = 4
PAST_LEN = 16384
PAGE_SIZE = 128

CONV_CH = D_MODEL // 2
CONV_K = 31
RET_HEADS = 8
RET_W = D_MODEL - CONV_CH
RET_DK = RET_W // RET_HEADS
RET_DV = RET_W // RET_HEADS
D_IN = 2 * CONV_CH + 4 * RET_W
D_FF = 4 * D_MODEL
RET_CHUNK = 128
ROPE_THETA = 10000.0
EPS = 1e-6
N_MOD = 6

kernel_name = 'hybrid_conformer_retention_adaln_step'


def rmsnorm(x, g):
    xf = x.astype(jnp.float32)
    y = xf * lax.rsqrt(jnp.mean(xf * xf, axis=-1, keepdims=True) + EPS)
    return (y * g.astype(jnp.float32)).astype(x.dtype)


def layernorm(x, g, b):
    xf = x.astype(jnp.float32)
    mu = jnp.mean(xf, axis=-1, keepdims=True)
    var = jnp.mean(jnp.square(xf - mu), axis=-1, keepdims=True)
    y = (xf - mu) * lax.rsqrt(var + EPS) * g.astype(jnp.float32) + b.astype(jnp.float32)
    return y.astype(x.dtype)


def rotary(x, pos):
    half = x.shape[-1] // 2
    inv = ROPE_THETA ** (-jnp.arange(half, dtype=jnp.float32) / half)
    ang = pos.astype(jnp.float32)[:, None] * inv[None, :]
    cos = jnp.cos(ang)[None, :, None, :]
    sin = jnp.sin(ang)[None, :, None, :]
    xf = x.astype(jnp.float32)
    x1, x2 = xf[..., :half], xf[..., half:]
    return jnp.concatenate([x1 * cos - x2 * sin, x1 * sin + x2 * cos], axis=-1).astype(x.dtype)


def retention_log_decay():
    return jnp.log1p(-jnp.exp2(-5.0 - jnp.arange(RET_HEADS, dtype=jnp.float32)))


def retention_chunk(S, qkv):
    q, k, v = qkv
    C = q.shape[2]
    log_g = retention_log_decay()
    idx = jnp.arange(C, dtype=jnp.float32)
    diff = idx[:, None] - idx[None, :]
    decay = jnp.where(diff >= 0, jnp.exp(log_g[:, None, None] * jnp.maximum(diff, 0.0)), 0.0)
    scores = jnp.einsum('bhid,bhjd->bhij', q, k) * decay[None]
    o_intra = jnp.einsum('bhij,bhjv->bhiv', scores, v)
    q_dec = q * jnp.exp(log_g[:, None] * (idx[None, :] + 1.0))[None, :, :, None]
    o_cross = jnp.einsum('bhid,bhdv->bhiv', q_dec, S)
    k_dec = k * jnp.exp(log_g[:, None] * (C - 1.0 - idx[None, :]))[None, :, :, None]
    S_new = jnp.exp(log_g * C)[None, :, None, None] * S + jnp.einsum('bhjd,bhjv->bhdv', k_dec, v)
    return S_new, o_intra + o_cross


def retention_scan(q, k, v, S0):
    B, L, H, _ = q.shape
    C = RET_CHUNK if L % RET_CHUNK == 0 else L
    n = L // C

    def to_chunks(t):
        return t.astype(jnp.float32).reshape(B, n, C, H, t.shape[-1]).transpose(1, 0, 3, 2, 4)

    S_fin, o = lax.scan(retention_chunk, S0.astype(jnp.float32), (to_chunks(q), to_chunks(k), to_chunks(v)))
    o = o.transpose(1, 0, 3, 2, 4).reshape(B, L, H, RET_DV)
    return o, S_fin


def decoder_layer(x, c, conv_hist, ret_state, pos0, w_ada, b_ada, g_mix, w_in, conv_w, conv_b,
                  conv_ln_g, conv_ln_b, ret_ln_g, ret_ln_b, w_out, g_ffn, w_ff1, w_ff2):
    B, L, _ = x.shape
    mod = jax.nn.silu(c) @ w_ada + b_ada
    sh1, sc1, gt1, sh2, sc2, gt2 = [m[:, None, :] for m in jnp.split(mod, N_MOD, axis=-1)]

    h = rmsnorm(x, g_mix) * (1 + sc1) + sh1
    proj = h @ w_in
    cuts = [CONV_CH, 2 * CONV_CH, 2 * CONV_CH + RET_W, 2 * CONV_CH + 2 * RET_W, 2 * CONV_CH + 3 * RET_W]
    a, b, q, k, v, g = jnp.split(proj, cuts, axis=-1)

    u = a * jax.nn.sigmoid(b)
    u_ext = jnp.concatenate([conv_hist.astype(u.dtype), u], axis=1)
    dw = lax.conv_general_dilated(u_ext, conv_w[:, None, :].astype(u.dtype), window_strides=(1,),
                                  padding='VALID', dimension_numbers=('NWC', 'WIO', 'NWC'),
                                  feature_group_count=CONV_CH) + conv_b
    conv_out = jax.nn.silu(layernorm(dw, conv_ln_g, conv_ln_b))
    new_hist = u_ext[:, L:]

    pos = pos0 + jnp.arange(L)
    q = rotary(q.reshape(B, L, RET_HEADS, RET_DK), pos) * (RET_DK ** -0.5)
    k = rotary(k.reshape(B, L, RET_HEADS, RET_DK), pos)
    v = v.reshape(B, L, RET_HEADS, RET_DV)
    o, new_state = retention_scan(q, k, v, ret_state)
    o = layernorm(o, ret_ln_g.reshape(RET_HEADS, RET_DV), ret_ln_b.reshape(RET_HEADS, RET_DV))
    ret_out = o.astype(x.dtype).reshape(B, L, RET_W) * jax.nn.silu(g)

    mix = jnp.concatenate([conv_out, ret_out], axis=-1) @ w_out
    x = x + gt1 * mix

    h2 = rmsnorm(x, g_ffn) * (1 + sc2) + sh2
    ff = jnp.square(jax.nn.relu(h2 @ w_ff1)) @ w_ff2
    x = x + gt2 * ff
    return x, new_hist, new_state.astype(ret_state.dtype)


def setup_inputs(seed: int = 0) -> dict:
    key = jax.random.key(seed)
    ks = jax.random.split(key, 24)
    f32 = jnp.float32

    def nrm(k, shape, scale):
        return jax.random.normal(k, shape, f32) * scale

    return {
        'x_prompt': nrm(ks[0], (BATCH, SEQ, D_MODEL), 1.0),
        'x_sample': nrm(ks[1], (DEC_BATCH, DEC_SEQ, D_MODEL), 1.0),
        'cache_conv': nrm(ks[2], (DEPTH, DEC_BATCH, CONV_K - 1, CONV_CH), 0.5),
        'state_ret': nrm(ks[3], (DEPTH, DEC_BATCH, RET_HEADS, RET_DK, RET_DV), 1.0),
        'c_prompt': nrm(ks[4], (BATCH, D_MODEL), 1.0),
        'c_sample': nrm(ks[5], (DEC_BATCH, D_MODEL), 1.0),
        'w_ada': nrm(ks[6], (DEPTH, D_MODEL, N_MOD * D_MODEL), 0.5 * D_MODEL ** -0.5),
        'b_ada': nrm(ks[7], (DEPTH, N_MOD * D_MODEL), 0.02),
        'g_mix': 1.0 + nrm(ks[8], (DEPTH, D_MODEL), 0.02),
        'w_in': nrm(ks[9], (DEPTH, D_MODEL, D_IN), D_MODEL ** -0.5),
        'conv_w': nrm(ks[10], (DEPTH, CONV_K, CONV_CH), CONV_K ** -0.5),
        'conv_b': nrm(ks[11], (DEPTH, CONV_CH), 0.02),
        'conv_ln_g': 1.0 + nrm(ks[12], (DEPTH, CONV_CH), 0.02),
        'conv_ln_b': nrm(ks[13], (DEPTH, CONV_CH), 0.02),
        'ret_ln_g': 1.0 + nrm(ks[14], (DEPTH, RET_W), 0.02),
        'ret_ln_b': nrm(ks[15], (DEPTH, RET_W), 0.02),
        'w_out': nrm(ks[16], (DEPTH, D_MODEL, D_MODEL), D_MODEL ** -0.5),
        'g_ffn': 1.0 + nrm(ks[17], (DEPTH, D_MODEL), 0.02),
        'w_ff1': nrm(ks[18], (DEPTH, D_MODEL, D_FF), D_MODEL ** -0.5),
        'w_ff2': nrm(ks[19], (DEPTH, D_FF, D_MODEL), D_FF ** -0.5),
        'g_final': 1.0 + nrm(ks[20], (D_MODEL,), 0.02),
    }


def reference(x_prompt, x_sample, cache_conv, state_ret, c_prompt, c_sample, w_ada, b_ada, g_mix,
              w_in, conv_w, conv_b, conv_ln_g, conv_ln_b, ret_ln_g, ret_ln_b, w_out, g_ffn,
              w_ff1, w_ff2, g_final):
    hp, hs = x_prompt, x_sample
    bp = x_prompt.shape[0]
    conv_p, ret_p, conv_s, ret_s = [], [], [], []
    for l in range(DEPTH):
        params = (w_ada[l], b_ada[l], g_mix[l], w_in[l], conv_w[l], conv_b[l], conv_ln_g[l],
                  conv_ln_b[l], ret_ln_g[l], ret_ln_b[l], w_out[l], g_ffn[l], w_ff1[l], w_ff2[l])
        zero_hist = jnp.zeros((bp, CONV_K - 1, CONV_CH), x_prompt.dtype)
        zero_state = jnp.zeros((bp, RET_HEADS, RET_DK, RET_DV), state_ret.dtype)
        hp, cp, rp = decoder_layer(hp, c_prompt, zero_hist, zero_state, 0, *params)
        hs, cs, rs = decoder_layer(hs, c_sample, cache_conv[l], state_ret[l], PAST_LEN, *params)
        conv_p.append(cp)
        ret_p.append(rp)
        conv_s.append(cs)
        ret_s.append(rs)
    y_prompt = rmsnorm(hp, g_final)
    y_sample = rmsnorm(hs, g_final)
    return (y_prompt, y_sample, jnp.stack(conv_p), jnp.stack(ret_p), jnp.stack(conv_s), jnp.stack(ret_s))
```

```python
import functools

import jax
import jax.numpy as jnp
from jax import lax
from jax.experimental import pallas as pl
from jax.experimental.pallas import tpu as pltpu

D_MODEL = 1024
CONV_CH = 512
CONV_K = 31
RET_HEADS = 8
RET_W = 512
RET_DK = 64
RET_DV = 64
D_FF = 4 * D_MODEL
RET_CHUNK = 128
ROPE_THETA = 10000.0
EPS = 1e-6
N_MOD = 6
PAST_LEN = 16384

HIST_ROWS = 32
PAIR_LANES = 2 * RET_DK
N_PAIRS = RET_HEADS // 2
VMEM_LIMIT = 56 * 1024 * 1024

F32 = jnp.float32
BF16 = jnp.bfloat16


def _log_decay(head):
    return jnp.log1p(-jnp.exp2(-5.0 - head))


def _rope_kernel(pos0, cos_ref, sin_ref):
    rows = cos_ref.shape[0]
    lane = lax.broadcasted_iota(jnp.int32, (rows, PAIR_LANES), 1)
    row = lax.broadcasted_iota(jnp.int32, (rows, PAIR_LANES), 0)
    half = RET_DK // 2
    freq = (lane % half).astype(F32)
    inv = jnp.power(jnp.full_like(freq, ROPE_THETA), -freq / half)
    ang = (row + pos0).astype(F32) * inv
    first = (lane % RET_DK) < half
    cos_ref[...] = jnp.cos(ang)
    sin_ref[...] = jnp.where(first, -jnp.sin(ang), jnp.sin(ang))


def _rope_tables(rows, pos0):
    return pl.pallas_call(
        functools.partial(_rope_kernel, pos0),
        out_shape=(jax.ShapeDtypeStruct((rows, PAIR_LANES), F32),
                   jax.ShapeDtypeStruct((rows, PAIR_LANES), F32)),
        name="rope_tables",
    )()


def _decay_kernel(chunk, dec_ref, qdec_ref, kdec_ref, sdec_ref):
    c = chunk
    h = lax.broadcasted_iota(jnp.int32, (RET_HEADS, c, c), 0).astype(F32)
    i = lax.broadcasted_iota(jnp.int32, (RET_HEADS, c, c), 1).astype(F32)
    j = lax.broadcasted_iota(jnp.int32, (RET_HEADS, c, c), 2).astype(F32)
    diff = i - j
    dec_ref[...] = jnp.where(diff >= 0, jnp.exp(_log_decay(h) * jnp.maximum(diff, 0.0)), 0.0)
    lane_h = (lax.broadcasted_iota(jnp.int32, (c, RET_W), 1) // RET_DK).astype(F32)
    idx = lax.broadcasted_iota(jnp.int32, (c, RET_W), 0).astype(F32)
    lg = _log_decay(lane_h)
    qdec_ref[...] = jnp.exp(lg * (idx + 1.0))
    kdec_ref[...] = jnp.exp(lg * (c - 1.0 - idx))
    p = lax.broadcasted_iota(jnp.int32, (N_PAIRS, PAIR_LANES, PAIR_LANES), 0)
    r = lax.broadcasted_iota(jnp.int32, (N_PAIRS, PAIR_LANES, PAIR_LANES), 1) // RET_DK
    q = lax.broadcasted_iota(jnp.int32, (N_PAIRS, PAIR_LANES, PAIR_LANES), 2) // RET_DK
    hh = (2 * p + r).astype(F32)
    sdec_ref[...] = jnp.where(r == q, jnp.exp(_log_decay(hh) * float(c)), 0.0)


def _decay_tables(chunk):
    return pl.pallas_call(
        functools.partial(_decay_kernel, chunk),
        out_shape=(jax.ShapeDtypeStruct((RET_HEADS, chunk, chunk), F32),
                   jax.ShapeDtypeStruct((chunk, RET_W), F32),
                   jax.ShapeDtypeStruct((chunk, RET_W), F32),
                   jax.ShapeDtypeStruct((N_PAIRS, PAIR_LANES, PAIR_LANES), F32)),
        name="decay_tables",
    )()


def _mod_kernel(c_ref, w_ref, b_ref, o_ref):
    c = c_ref[...]
    s = c * jax.nn.sigmoid(c)
    o_ref[...] = jnp.dot(s.astype(BF16), w_ref[...].astype(BF16),
                         preferred_element_type=F32) + b_ref[...]


def _modulation(c, w_ada, b_ada):
    n, d = c.shape
    width = w_ada.shape[1]
    blk = D_MODEL
    return pl.pallas_call(
        _mod_kernel,
        grid=(width // blk,),
        in_specs=[pl.BlockSpec((n, d), lambda i: (0, 0)),
                  pl.BlockSpec((d, blk), lambda i: (0, i)),
                  pl.BlockSpec((1, blk), lambda i: (0, i))],
        out_specs=pl.BlockSpec((n, blk), lambda i: (0, i)),
        out_shape=jax.ShapeDtypeStruct((n, width), F32),
        compiler_params=pltpu.CompilerParams(vmem_limit_bytes=VMEM_LIMIT),
        name="adaln_mod",
    )(c, w_ada, b_ada.reshape(1, width))


def _rmsnorm_mod(x, g, scale, shift):
    y = x * lax.rsqrt(jnp.mean(x * x, axis=-1, keepdims=True) + EPS)
    return (y * g) * (1.0 + scale) + shift


def _layernorm(x, g, b):
    mu = jnp.mean(x, axis=-1, keepdims=True)
    d = x - mu
    var = jnp.mean(d * d, axis=-1, keepdims=True)
    return d * lax.rsqrt(var + EPS) * g + b


def _silu(x):
    return x * jax.nn.sigmoid(x)


def _rotary_pair(x, cos, sin_signed, first_half):
    partner = jnp.where(first_half, pltpu.roll(x, PAIR_LANES - RET_DK // 2, 1),
                        pltpu.roll(x, RET_DK // 2, 1))
    return x * cos + partner * sin_signed


def _pair_groupnorm(o, lo, g, b):
    inv_n = 1.0 / RET_DV
    s_lo = jnp.sum(jnp.where(lo, o, 0.0), axis=-1, keepdims=True)
    s_hi = jnp.sum(jnp.where(lo, 0.0, o), axis=-1, keepdims=True)
    d = o - jnp.where(lo, s_lo, s_hi) * inv_n
    d2 = d * d
    v_lo = jnp.sum(jnp.where(lo, d2, 0.0), axis=-1, keepdims=True)
    v_hi = jnp.sum(jnp.where(lo, 0.0, d2), axis=-1, keepdims=True)
    var = jnp.where(lo, v_lo, v_hi) * inv_n
    return d * lax.rsqrt(var + EPS) * g + b


def _dot(a, b):
    return jnp.dot(a, b, preferred_element_type=F32)


def _dot_nt(a, b):
    return lax.dot_general(a, b, (((1,), (1,)), ((), ())), preferred_element_type=F32)


def _dot_tn(a, b):
    return lax.dot_general(a, b, (((0,), (0,)), ((), ())), preferred_element_type=F32)


def _mix_prompt_kernel(tile, x_ref, mod_ref, gmix_ref, win_ref, convw_ref, convb_ref,
                       clng_ref, clnb_ref, rlng_ref, rlnb_ref, wout_ref,
                       cos_ref, sin_ref, dec_ref, qdec_ref, kdec_ref, sdec_ref,
                       x1_ref, hist_ref, state_ref,
                       ubuf, sbuf, mixbuf):
    j = pl.program_id(1)
    nj = pl.num_programs(1)

    @pl.when(j == 0)
    def _():
        ubuf[0:HIST_ROWS, :] = jnp.zeros((HIST_ROWS, CONV_CH), F32)
        sbuf[...] = jnp.zeros(sbuf.shape, F32)

    x = x_ref[0]
    shift, scale, gate = mod_ref[0, 0:1, :], mod_ref[0, 1:2, :], mod_ref[0, 2:3, :]
    hb = _rmsnorm_mod(x, gmix_ref[...], scale, shift).astype(BF16)

    def proj(k):
        return _dot(hb, win_ref[:, k * CONV_CH:(k + 1) * CONV_CH])

    u = proj(0) * jax.nn.sigmoid(proj(1))
    ubuf[HIST_ROWS:HIST_ROWS + tile, :] = u
    off = HIST_ROWS - (CONV_K - 1)
    dw = jnp.broadcast_to(convb_ref[...], (tile, CONV_CH))
    for k in range(CONV_K):
        dw = dw + convw_ref[k:k + 1, :] * ubuf[off + k:off + k + tile, :]
    ubuf[0:HIST_ROWS, :] = ubuf[tile:tile + HIST_ROWS, :]
    conv_out = _silu(_layernorm(dw, clng_ref[...], clnb_ref[...]))
    mixbuf[:, 0:CONV_CH] = conv_out.astype(BF16)

    q = proj(2)
    kk = proj(3)
    v = proj(4)
    gt = proj(5)
    lane = lax.broadcasted_iota(jnp.int32, (1, PAIR_LANES), 1)
    lo = lane < RET_DK
    first_half = (lane % RET_DK) < (RET_DK // 2)
    lo_b = lo.astype(BF16)
    hi_b = (~lo).astype(BF16)
    cos = cos_ref[...]
    sin = sin_ref[...]
    for p in range(N_PAIRS):
        sl = slice(p * PAIR_LANES, (p + 1) * PAIR_LANES)
        qr = _rotary_pair(q[:, sl], cos, sin, first_half)
        kr = _rotary_pair(kk[:, sl], cos, sin, first_half)
        vb = v[:, sl].astype(BF16)
        for c in range(tile // RET_CHUNK):
            rs = slice(c * RET_CHUNK, (c + 1) * RET_CHUNK)
            qc, kc, vc = qr[rs], kr[rs], vb[rs]
            qb = qc.astype(BF16)
            kb = kc.astype(BF16)
            s_a = _dot_nt(qb, kb * lo_b) * dec_ref[2 * p]
            s_b = _dot_nt(qb, kb * hi_b) * dec_ref[2 * p + 1]
            state = sbuf[p]
            o = (_dot(s_a.astype(BF16), vc * lo_b) + _dot(s_b.astype(BF16), vc * hi_b)
                 + _dot((qc * qdec_ref[:, sl]).astype(BF16), state.astype(BF16)))
            kd = (kc * kdec_ref[:, sl]).astype(BF16)
            sbuf[p] = state * sdec_ref[p] + jnp.where(sdec_ref[p] > 0.0, _dot_tn(kd, vc), 0.0)
            on = _pair_groupnorm(o, lo, rlng_ref[:, sl], rlnb_ref[:, sl])
            mixbuf[rs, CONV_CH + p * PAIR_LANES:CONV_CH + (p + 1) * PAIR_LANES] = (
                on * _silu(gt[rs, sl])).astype(BF16)

    mix = _dot(mixbuf[...], wout_ref[...])
    x1_ref[0] = x + gate * mix

    @pl.when(j == nj - 1)
    def _():
        hist_ref[0] = ubuf[0:HIST_ROWS, :]
        state_ref[0] = sbuf[...]


def _mix_prompt(x, mod1, g_mix, w_in, conv_w, conv_b, cln_g, cln_b, rln_g, rln_b, w_out,
                rope, decay, tile):
    bsz, seq, d = x.shape
    nj = seq // tile
    cos, sin = rope
    dec, qdec, kdec, sdec = decay
    const2 = lambda b, j: (0, 0)
    const3 = lambda b, j: (0, 0, 0)
    row = lambda a: a.reshape(1, -1)
    return pl.pallas_call(
        functools.partial(_mix_prompt_kernel, tile),
        grid=(bsz, nj),
        in_specs=[
            pl.BlockSpec((1, tile, d), lambda b, j: (b, j, 0)),
            pl.BlockSpec((1, 3, d), lambda b, j: (b, 0, 0)),
            pl.BlockSpec((1, d), const2),
            pl.BlockSpec(w_in.shape, const2),
            pl.BlockSpec(conv_w.shape, const2),
            pl.BlockSpec((1, CONV_CH), const2),
            pl.BlockSpec((1, CONV_CH), const2),
            pl.BlockSpec((1, CONV_CH), const2),
            pl.BlockSpec((1, RET_W), const2),
            pl.BlockSpec((1, RET_W), const2),
            pl.BlockSpec(w_out.shape, const2),
            pl.BlockSpec((tile, PAIR_LANES), lambda b, j: (j, 0)),
            pl.BlockSpec((tile, PAIR_LANES), lambda b, j: (j, 0)),
            pl.BlockSpec(dec.shape, const3),
            pl.BlockSpec(qdec.shape, const2),
            pl.BlockSpec(kdec.shape, const2),
            pl.BlockSpec(sdec.shape, const3),
        ],
        out_specs=[
            pl.BlockSpec((1, tile, d), lambda b, j: (b, j, 0)),
            pl.BlockSpec((1, HIST_ROWS, CONV_CH), lambda b, j: (b, 0, 0)),
            pl.BlockSpec((1, N_PAIRS, PAIR_LANES, PAIR_LANES), lambda b, j: (b, 0, 0, 0)),
        ],
        out_shape=(
            jax.ShapeDtypeStruct((bsz, seq, d), F32),
            jax.ShapeDtypeStruct((bsz, HIST_ROWS, CONV_CH), F32),
            jax.ShapeDtypeStruct((bsz, N_PAIRS, PAIR_LANES, PAIR_LANES), F32),
        ),
        scratch_shapes=[
            pltpu.VMEM((HIST_ROWS + tile, CONV_CH), F32),
            pltpu.VMEM((N_PAIRS, PAIR_LANES, PAIR_LANES), F32),
            pltpu.VMEM((tile, D_MODEL), BF16),
        ],
        compiler_params=pltpu.CompilerParams(
            dimension_semantics=("arbitrary", "arbitrary"), vmem_limit_bytes=VMEM_LIMIT),
        name="mix_prompt",
    )(x, mod1, row(g_mix), w_in, conv_w, row(conv_b), row(cln_g), row(cln_b),
      row(rln_g), row(rln_b), w_out, cos, sin, dec, qdec, kdec, sdec)


def _ffn_body(x, mod_ref, gffn_ref, w1_ref, w2_ref, gfin_ref, y_ref):
    shift, scale, gate = mod_ref[0, 0], mod_ref[0, 1], mod_ref[0, 2]
    hb = _rmsnorm_mod(x, gffn_ref[...], scale, shift).astype(BF16)
    a = jnp.maximum(_dot(hb, w1_ref[...]), 0.0)
    ff = _dot((a * a).astype(BF16), w2_ref[...])
    z = x + gate * ff
    y_ref[0] = z * lax.rsqrt(jnp.mean(z * z, axis=-1, keepdims=True) + EPS) * gfin_ref[...]


def _ffn_kernel(x_ref, mod_ref, gffn_ref, w1_ref, w2_ref, gfin_ref, y_ref):
    _ffn_body(x_ref[0], mod_ref, gffn_ref, w1_ref, w2_ref, gfin_ref, y_ref)


def _outproj_ffn_kernel(x_ref, conv_ref, ret_ref, gate1_ref, wout_ref,
                        mod_ref, gffn_ref, w1_ref, w2_ref, gfin_ref, y_ref):
    mix = (_dot(conv_ref[0], wout_ref[0:CONV_CH, :])
           + _dot(ret_ref[0], wout_ref[CONV_CH:CONV_CH + RET_W, :]))
    _ffn_body(x_ref[0] + gate1_ref[...] * mix, mod_ref, gffn_ref, w1_ref, w2_ref, gfin_ref, y_ref)


def _ffn(x, mod2, g_ffn, w1, w2, g_final, tile, out_proj=None):
    bsz, seq, d = x.shape
    nj = seq // tile
    if mod2.shape[2] == 1:
        mod_spec = pl.BlockSpec((1, 3, 1, d), lambda b, j: (b, 0, 0, 0))
    else:
        mod_spec = pl.BlockSpec((1, 3, tile, d), lambda b, j: (0, 0, j, 0))
    const2 = lambda b, j: (0, 0)
    tok = lambda width: pl.BlockSpec((1, tile, width), lambda b, j: (b, j, 0))
    in_specs = [tok(d)]
    args = [x]
    body = _ffn_kernel
    if out_proj is not None:
        conv_out, ret_out, gate1, w_out = out_proj
        in_specs += [tok(CONV_CH), tok(RET_W),
                     pl.BlockSpec((tile, d), lambda b, j: (j, 0)),
                     pl.BlockSpec(w_out.shape, const2)]
        args += [conv_out, ret_out, gate1, w_out]
        body = _outproj_ffn_kernel
    in_specs += [mod_spec,
                 pl.BlockSpec((1, d), const2),
                 pl.BlockSpec(w1.shape, const2),
                 pl.BlockSpec(w2.shape, const2),
                 pl.BlockSpec((1, d), const2)]
    args += [mod2, g_ffn.reshape(1, d), w1, w2, g_final.reshape(1, d)]
    return pl.pallas_call(
        body,
        grid=(bsz, nj),
        in_specs=in_specs,
        out_specs=tok(d),
        out_shape=jax.ShapeDtypeStruct((bsz, seq, d), F32),
        compiler_params=pltpu.CompilerParams(
            dimension_semantics=("arbitrary", "arbitrary"), vmem_limit_bytes=VMEM_LIMIT),
        name="ffn" if out_proj is None else "outproj_ffn",
    )(*args)


def _power_table_kernel(o_ref):
    lane_h = (lax.broadcasted_iota(jnp.int32, o_ref.shape, 1) // RET_DK).astype(F32)
    n = lax.broadcasted_iota(jnp.int32, o_ref.shape, 0).astype(F32)
    o_ref[...] = jnp.exp(_log_decay(lane_h) * n)


def _power_table():
    return pl.pallas_call(_power_table_kernel,
                          out_shape=jax.ShapeDtypeStruct((8, RET_W), F32),
                          name="decay_powers")()


def _proj_sample_kernel(n_pos, n_seq, x_ref, mod_ref, gmix_ref, win_ref, cache_ref, convw_ref,
                        convb_ref, clng_ref, clnb_ref, cos_ref, sin_ref,
                        u_ref, conv_ref, q_ref, k_ref, v_ref, g_ref):
    shift, scale = mod_ref[0], mod_ref[1]
    hb = jnp.concatenate(
        [_rmsnorm_mod(x_ref[i], gmix_ref[...], scale, shift).astype(BF16) for i in range(n_pos)],
        axis=0)

    def proj(k):
        return _dot(hb, win_ref[:, k * CONV_CH:(k + 1) * CONV_CH])

    u = proj(0) * jax.nn.sigmoid(proj(1))
    n_hist = CONV_K - 1
    rows = lambda a, i: a[i * n_seq:(i + 1) * n_seq]
    outs = []
    for i in range(n_pos):
        u_ref[i] = rows(u, i)
        dw = jnp.broadcast_to(convb_ref[...], (n_seq, CONV_CH))
        for k in range(CONV_K):
            m = i + k
            src = cache_ref[m] if m < n_hist else rows(u, m - n_hist)
            dw = dw + convw_ref[k:k + 1, :] * src
        outs.append(_silu(_layernorm(dw, clng_ref[...], clnb_ref[...])).astype(BF16))
    conv_ref[...] = jnp.concatenate(outs, axis=0)

    lane = lax.broadcasted_iota(jnp.int32, (1, PAIR_LANES), 1)
    first_half = (lane % RET_DK) < (RET_DK // 2)
    q = proj(2)
    kk = proj(3)
    for i in range(n_pos):
        rs = slice(i * n_seq, (i + 1) * n_seq)
        cos = cos_ref[i:i + 1, :]
        sin = sin_ref[i:i + 1, :]
        for p in range(N_PAIRS):
            sl = slice(p * PAIR_LANES, (p + 1) * PAIR_LANES)
            q_ref[rs, sl] = _rotary_pair(q[rs, sl], cos, sin, first_half)
            k_ref[rs, sl] = _rotary_pair(kk[rs, sl], cos, sin, first_half)
    v_ref[...] = proj(4)
    g_ref[...] = proj(5)


def _proj_sample(xs, mod1, g_mix, w_in, cache_t, conv_w, conv_b, cln_g, cln_b, rope):
    n_pos, n_seq, d = xs.shape
    n_tok = n_pos * n_seq
    cos, sin = rope
    row = lambda a: a.reshape(1, -1)
    tok = lambda dt: jax.ShapeDtypeStruct((n_tok, RET_W), dt)
    return pl.pallas_call(
        functools.partial(_proj_sample_kernel, n_pos, n_seq),
        out_shape=(jax.ShapeDtypeStruct((n_pos, n_seq, CONV_CH), F32), tok(BF16),
                   tok(F32), tok(F32), tok(F32), tok(F32)),
        compiler_params=pltpu.CompilerParams(vmem_limit_bytes=VMEM_LIMIT),
        name="proj_sample",
    )(xs, mod1, row(g_mix), w_in, cache_t, conv_w, row(conv_b), row(cln_g), row(cln_b), cos, sin)


RET_LANE_BLOCK = 2048


def _ret_sample_kernel(n_pos, n_seq, q_ref, k_ref, v_ref, g_ref, s_ref, pw_ref, rlng_ref,
                       rlnb_ref, out_ref, snew_ref):
    lane = lax.broadcasted_iota(jnp.int32, (1, PAIR_LANES), 1)
    lo = lane < RET_DK
    rows = lambda a, i: a[i * n_seq:(i + 1) * n_seq]
    pw = pw_ref[...]
    q = q_ref[...]
    k = k_ref[...]
    qb = [rows(q, i).astype(BF16) for i in range(n_pos)]
    kb = [rows(k, i).astype(BF16) for i in range(n_pos)]
    vb = [rows(v_ref[...], i).astype(BF16) for i in range(n_pos)]

    def head_bcast(x, a):
        other = pltpu.roll(x, RET_DK, 1)
        return jnp.where(lo, x, other) if a == 0 else jnp.where(lo, other, x)

    o = []
    for i in range(n_pos):
        acc = jnp.zeros((n_seq, PAIR_LANES), F32)
        for j in range(i + 1):
            prod = qb[i].astype(F32) * kb[j].astype(F32)
            s_lo = jnp.sum(jnp.where(lo, prod, 0.0), axis=-1, keepdims=True)
            s_hi = jnp.sum(jnp.where(lo, 0.0, prod), axis=-1, keepdims=True)
            score = jnp.where(lo, s_lo, s_hi) * pw[i - j:i - j + 1, :]
            acc = acc + score.astype(BF16).astype(F32) * vb[j].astype(F32)
        o.append(acc)

    qd = jnp.concatenate([(rows(q, i) * pw[i + 1:i + 2, :]).astype(BF16) for i in range(n_pos)], axis=0)
    kd = jnp.concatenate([(rows(k, j) * pw[n_pos - 1 - j:n_pos - j, :]).astype(BF16)
                          for j in range(n_pos)], axis=0)
    qk = jnp.concatenate([qd, kd], axis=0)
    head_lanes = RET_DK * RET_DV
    cross = [[jnp.zeros((n_seq, PAIR_LANES), F32) for _ in range(2)] for _ in range(n_pos)]
    for blk in range(2 * head_lanes // RET_LANE_BLOCK):
        c0 = blk * RET_LANE_BLOCK
        a = c0 // head_lanes
        col = lax.broadcasted_iota(jnp.int32, (PAIR_LANES, RET_LANE_BLOCK), 1) + c0
        r = lax.broadcasted_iota(jnp.int32, (PAIR_LANES, RET_LANE_BLOCK), 0)
        expand = jnp.where(col // RET_DV == r, 1.0, 0.0).astype(BF16)
        big = _dot(qk, expand)
        s_old = s_ref[:, c0:c0 + RET_LANE_BLOCK]
        s_b = s_old.astype(BF16).astype(F32)
        gam = pw[n_pos:n_pos + 1, a * RET_DK:a * RET_DK + 1]
        s_new = s_old * gam
        reps = RET_LANE_BLOCK // PAIR_LANES
        for i in range(n_pos):
            prod = rows(big, i) * s_b
            t = prod[:, 0:PAIR_LANES]
            for e in range(1, reps):
                t = t + prod[:, e * PAIR_LANES:(e + 1) * PAIR_LANES]
            cross[i][a] = cross[i][a] + t
            vrep = jnp.tile(head_bcast(vb[i].astype(F32), a), (1, reps))
            s_new = s_new + rows(big, n_pos + i) * vrep
        snew_ref[:, c0:c0 + RET_LANE_BLOCK] = s_new

    for i in range(n_pos):
        fold = [c + pltpu.roll(c, RET_DK, 1) for c in cross[i]]
        oi = o[i] + jnp.where(lo, fold[0], fold[1])
        on = _pair_groupnorm(oi, lo, rlng_ref[...], rlnb_ref[...])
        out_ref[i * n_seq:(i + 1) * n_seq, :] = (on * _silu(rows(g_ref[...], i))).astype(BF16)


def _ret_sample(q, k, v, g, state, pw, rln_g, rln_b, n_pos, n_seq):
    n_tok = n_pos * n_seq
    pair_state = 2 * RET_DK * RET_DV
    slab = pl.BlockSpec((n_tok, PAIR_LANES), lambda p: (0, p))
    return pl.pallas_call(
        functools.partial(_ret_sample_kernel, n_pos, n_seq),
        grid=(N_PAIRS,),
        in_specs=[slab, slab, slab, slab,
                  pl.BlockSpec((n_seq, pair_state), lambda p: (0, p)),
                  pl.BlockSpec((8, PAIR_LANES), lambda p: (0, p)),
                  pl.BlockSpec((1, PAIR_LANES), lambda p: (0, p)),
                  pl.BlockSpec((1, PAIR_LANES), lambda p: (0, p))],
        out_specs=[slab, pl.BlockSpec((n_seq, pair_state), lambda p: (0, p))],
        out_shape=(jax.ShapeDtypeStruct((n_tok, RET_W), BF16),
                   jax.ShapeDtypeStruct(state.shape, F32)),
        compiler_params=pltpu.CompilerParams(
            dimension_semantics=("arbitrary",), vmem_limit_bytes=VMEM_LIMIT),
        name="ret_sample",
    )(q, k, v, g, state, pw, rln_g.reshape(1, RET_W), rln_b.reshape(1, RET_W))


def _unpack_state(s2):
    blocks = []
    for p in range(N_PAIRS):
        for a in range(2):
            sl = slice(a * RET_DK, (a + 1) * RET_DK)
            blocks.append(s2[:, p, sl, sl])
    return jnp.stack(blocks, axis=1)


def _cast_weights(w_in, w_out, w_ff1, w_ff2):
    col_scale = jnp.ones((w_in.shape[1],), F32).at[2 * CONV_CH:2 * CONV_CH + RET_W].set(RET_DK ** -0.5)
    return ((w_in * col_scale).astype(BF16), w_out.astype(BF16), w_ff1.astype(BF16),
            w_ff2.astype(BF16))


def _prompt_path(x, mod, g_mix, w_in, conv_w, conv_b, cln_g, cln_b, rln_g, rln_b,
                 w_out, g_ffn, w_ff1, w_ff2, g_final, mix_tile=256, ffn_tile=256):
    bsz, seq, d = x.shape
    rope = _rope_tables(seq, 0)
    decay = _decay_tables(RET_CHUNK)
    x1, hist, s2 = _mix_prompt(x, mod[:, 0:3], g_mix, w_in, conv_w, conv_b, cln_g, cln_b,
                               rln_g, rln_b, w_out, rope, decay, mix_tile)
    y = _ffn(x1, mod[:, 3:6].reshape(bsz, 3, 1, d), g_ffn, w_ff1, w_ff2, g_final, ffn_tile)
    return y, hist[:, HIST_ROWS - (CONV_K - 1):], _unpack_state(s2)


def _sample_path(x, mod, cache, state, pos0, g_mix, w_in, conv_w, conv_b, cln_g, cln_b,
                 rln_g, rln_b, w_out, g_ffn, w_ff1, w_ff2, g_final):
    n_seq, n_pos, d = x.shape
    xs = x.transpose(1, 0, 2)
    modt = mod.transpose(1, 0, 2)
    rope = _rope_tables(8, pos0)
    u, conv_out, q, k, v, g = _proj_sample(xs, modt[0:2], g_mix, w_in, cache.transpose(1, 0, 2),
                                           conv_w, conv_b, cln_g, cln_b, rope)
    ret_out, s_new = _ret_sample(q, k, v, g, state.reshape(n_seq, -1), _power_table(),
                                 rln_g, rln_b, n_pos, n_seq)
    y = _ffn(xs, modt[3:6][None], g_ffn, w_ff1, w_ff2, g_final, n_seq,
             out_proj=(conv_out.reshape(n_pos, n_seq, CONV_CH),
                       ret_out.reshape(n_pos, n_seq, RET_W), modt[2], w_out))
    new_cache = jnp.concatenate([cache[:, n_pos:], u.transpose(1, 0, 2)], axis=1)
    return y.transpose(1, 0, 2), new_cache, s_new.reshape(state.shape)


def kernel(x_prompt, x_sample, cache_conv, state_ret, c_prompt, c_sample, w_ada, b_ada, g_mix, w_in, conv_w, conv_b, conv_ln_g, conv_ln_b, ret_ln_g, ret_ln_b, w_out, g_ffn, w_ff1, w_ff2, g_final):
    bp = x_prompt.shape[0]
    d = x_prompt.shape[-1]
    w_in_b, w_out_b, w_ff1_b, w_ff2_b = _cast_weights(w_in[0], w_out[0], w_ff1[0], w_ff2[0])
    mod = _modulation(jnp.concatenate([c_prompt, c_sample], axis=0), w_ada[0], b_ada[0])
    mod = mod.reshape(-1, N_MOD, d)
    params = (g_mix[0], w_in_b, conv_w[0], conv_b[0], conv_ln_g[0], conv_ln_b[0], ret_ln_g[0],
              ret_ln_b[0], w_out_b, g_ffn[0], w_ff1_b, w_ff2_b, g_final)
    y_p, conv_p, ret_p = _prompt_path(x_prompt, mod[:bp], *params)
    y_s, conv_s, ret_s = _sample_path(x_sample, mod[bp:], cache_conv[0], state_ret[0], PAST_LEN,
                                      *params)
    return (y_p, y_s, conv_p[None], ret_p[None], conv_s[None], ret_s[None])
```

```python
import functools

import jax
import jax.numpy as jnp
from jax import lax
from jax.experimental import pallas as pl
from jax.experimental.pallas import tpu as pltpu

D_MODEL = 1024
CONV_CH = 512
CONV_K = 31
RET_HEADS = 8
RET_W = 512
RET_DK = 64
RET_DV = 64
D_FF = 4 * D_MODEL
RET_CHUNK = 128
ROPE_THETA = 10000.0
EPS = 1e-6
N_MOD = 6
PAST_LEN = 16384

SUBLANES = 8
HIST_ROWS = 32
CONV_ROW_BLOCK = 32
PROMPT_TILE = 256
FFN_CHUNK = 512
MLP_STAGES_AFTER = {
    "glu": 2,
    "shift": {3: 1},
    "conv": {0: 1, 2: 1, 3: 1, 4: 1, 6: 1, 7: 1},
    "qkvg": 1,
    "pair": {0: 1, 1: 1},
}
PAIR_LANES = 2 * RET_DK
N_PAIRS = RET_HEADS // 2
VMEM_LIMIT = 56 * 1024 * 1024

F32 = jnp.float32
BF16 = jnp.bfloat16


def _log_decay(head):
    return jnp.log1p(-jnp.exp2(-5.0 - head))


def _rope_kernel(pos0, cos_ref, sin_ref):
    rows = cos_ref.shape[0]
    lane = lax.broadcasted_iota(jnp.int32, (rows, PAIR_LANES), 1)
    row = lax.broadcasted_iota(jnp.int32, (rows, PAIR_LANES), 0)
    half = RET_DK // 2
    freq = (lane % half).astype(F32)
    inv = jnp.power(jnp.full_like(freq, ROPE_THETA), -freq / half)
    ang = (row + pos0).astype(F32) * inv
    first = (lane % RET_DK) < half
    cos_ref[...] = jnp.cos(ang)
    sin_ref[...] = jnp.where(first, -jnp.sin(ang), jnp.sin(ang))


def _rope_tables(rows, pos0):
    return pl.pallas_call(
        functools.partial(_rope_kernel, pos0),
        out_shape=(jax.ShapeDtypeStruct((rows, PAIR_LANES), F32),
                   jax.ShapeDtypeStruct((rows, PAIR_LANES), F32)),
        name="rope_tables",
    )()


def _decay_kernel(chunk, dec_ref, qdec_ref, kdec_ref, sdec_ref):
    c = chunk
    h = lax.broadcasted_iota(jnp.int32, (RET_HEADS, c, c), 0).astype(F32)
    i = lax.broadcasted_iota(jnp.int32, (RET_HEADS, c, c), 1).astype(F32)
    j = lax.broadcasted_iota(jnp.int32, (RET_HEADS, c, c), 2).astype(F32)
    diff = i - j
    dec_ref[...] = jnp.where(diff >= 0, jnp.exp(_log_decay(h) * jnp.maximum(diff, 0.0)), 0.0)
    lane_h = (lax.broadcasted_iota(jnp.int32, (c, RET_W), 1) // RET_DK).astype(F32)
    idx = lax.broadcasted_iota(jnp.int32, (c, RET_W), 0).astype(F32)
    lg = _log_decay(lane_h)
    qdec_ref[...] = jnp.exp(lg * (idx + 1.0))
    kdec_ref[...] = jnp.exp(lg * (c - 1.0 - idx))
    p = lax.broadcasted_iota(jnp.int32, (N_PAIRS, PAIR_LANES, PAIR_LANES), 0)
    r = lax.broadcasted_iota(jnp.int32, (N_PAIRS, PAIR_LANES, PAIR_LANES), 1) // RET_DK
    q = lax.broadcasted_iota(jnp.int32, (N_PAIRS, PAIR_LANES, PAIR_LANES), 2) // RET_DK
    hh = (2 * p + r).astype(F32)
    sdec_ref[...] = jnp.where(r == q, jnp.exp(_log_decay(hh) * float(c)), 0.0)


def _decay_tables(chunk):
    return pl.pallas_call(
        functools.partial(_decay_kernel, chunk),
        out_shape=(jax.ShapeDtypeStruct((RET_HEADS, chunk, chunk), F32),
                   jax.ShapeDtypeStruct((chunk, RET_W), F32),
                   jax.ShapeDtypeStruct((chunk, RET_W), F32),
                   jax.ShapeDtypeStruct((N_PAIRS, PAIR_LANES, PAIR_LANES), F32)),
        name="decay_tables",
    )()


def _mod_kernel(c_ref, w_ref, b_ref, o_ref):
    c = c_ref[...]
    s = c * jax.nn.sigmoid(c)
    o_ref[...] = jnp.dot(s.astype(BF16), w_ref[...].astype(BF16),
                         preferred_element_type=F32) + b_ref[...]


def _modulation(c, w_ada, b_ada):
    n, d = c.shape
    width = w_ada.shape[1]
    blk = D_MODEL
    return pl.pallas_call(
        _mod_kernel,
        grid=(width // blk,),
        in_specs=[pl.BlockSpec((n, d), lambda i: (0, 0)),
                  pl.BlockSpec((d, blk), lambda i: (0, i)),
                  pl.BlockSpec((1, blk), lambda i: (0, i))],
        out_specs=pl.BlockSpec((n, blk), lambda i: (0, i)),
        out_shape=jax.ShapeDtypeStruct((n, width), F32),
        compiler_params=pltpu.CompilerParams(vmem_limit_bytes=VMEM_LIMIT),
        name="adaln_mod",
    )(c, w_ada, b_ada.reshape(1, width))


def _rmsnorm_mod(x, g, scale, shift):
    y = x * lax.rsqrt(jnp.mean(x * x, axis=-1, keepdims=True) + EPS)
    return (y * g) * (1.0 + scale) + shift


def _layernorm(x, g, b):
    mu = jnp.mean(x, axis=-1, keepdims=True)
    d = x - mu
    var = jnp.mean(d * d, axis=-1, keepdims=True)
    return d * lax.rsqrt(var + EPS) * g + b


def _silu(x):
    return x * jax.nn.sigmoid(x)


def _rotary_pair(x, cos, sin_signed, first_half):
    partner = jnp.where(first_half, pltpu.roll(x, PAIR_LANES - RET_DK // 2, 1),
                        pltpu.roll(x, RET_DK // 2, 1))
    return x * cos + partner * sin_signed


def _pair_groupnorm(o, lo, g, b):
    inv_n = 1.0 / RET_DV
    s_lo = jnp.sum(jnp.where(lo, o, 0.0), axis=-1, keepdims=True)
    s_hi = jnp.sum(jnp.where(lo, 0.0, o), axis=-1, keepdims=True)
    d = o - jnp.where(lo, s_lo, s_hi) * inv_n
    d2 = d * d
    v_lo = jnp.sum(jnp.where(lo, d2, 0.0), axis=-1, keepdims=True)
    v_hi = jnp.sum(jnp.where(lo, 0.0, d2), axis=-1, keepdims=True)
    var = jnp.where(lo, v_lo, v_hi) * inv_n
    return d * lax.rsqrt(var + EPS) * g + b


def _dot(a, b):
    return jnp.dot(a, b, preferred_element_type=F32)


def _dot_nt(a, b):
    return lax.dot_general(a, b, (((1,), (1,)), ((), ())), preferred_element_type=F32)


def _dot_tn(a, b):
    return lax.dot_general(a, b, (((0,), (0,)), ((), ())), preferred_element_type=F32)


def _layer_prompt_kernel(tile, nj, x_ref, mod_ref, gmix_ref, win_ref, convw_ref, convb_ref,
                         clng_ref, clnb_ref, rlng_ref, rlnb_ref, wout_ref,
                         cos_ref, sin_ref, dec_ref, qdec_ref, kdec_ref, sdec_ref,
                         mod2_ref, gffn_ref, w1_ref, w2_ref, gfin_ref,
                         y_ref, hist_ref, state_ref,
                         ubuf, ushift, sbuf, mixbuf, x1buf):
    g = pl.program_id(0)
    total = pl.num_programs(0) - 1
    j = lax.rem(jnp.minimum(g, total - 1), nj)
    live = g < total

    @pl.when(g == 0)
    def _():
        x1buf[...] = jnp.zeros((tile, D_MODEL), F32)

    @pl.when(jnp.logical_and(j == 0, live))
    def _():
        ubuf[0:HIST_ROWS, :] = jnp.zeros((HIST_ROWS, CONV_CH), F32)
        sbuf[...] = jnp.zeros(sbuf.shape, F32)

    ffn = _ffn_stages(x1buf[...], mod2_ref, gffn_ref, w1_ref, w2_ref, gfin_ref, y_ref)

    def mlp(n):
        for _ in range(n):
            next(ffn, None)

    mlp(1)
    x = x_ref[0]
    shift, scale, gate = mod_ref[0, 0:1, :], mod_ref[0, 1:2, :], mod_ref[0, 2:3, :]
    hb = _rmsnorm_mod(x, gmix_ref[...], scale, shift).astype(BF16)

    def proj(k):
        return _dot(hb, win_ref[:, k * CONV_CH:(k + 1) * CONV_CH])

    u = proj(0) * jax.nn.sigmoid(proj(1))
    ubuf[HIST_ROWS:HIST_ROWS + tile, :] = u
    mlp(MLP_STAGES_AFTER["glu"])
    off = HIST_ROWS - (CONV_K - 1)
    for r in range(SUBLANES):
        n = tile + SUBLANES * ((CONV_K - 1 - r) // SUBLANES)
        ushift[r, 0:n, :] = ubuf[off + r:off + r + n, :]
        mlp(MLP_STAGES_AFTER["shift"].get(r, 0))
    for rb in range(tile // CONV_ROW_BLOCK):
        r0 = rb * CONV_ROW_BLOCK
        dw = jnp.broadcast_to(convb_ref[...], (CONV_ROW_BLOCK, CONV_CH))
        for k in range(CONV_K):
            s0 = r0 + SUBLANES * (k // SUBLANES)
            wk = jnp.concatenate([convw_ref[k]] * (CONV_ROW_BLOCK // SUBLANES), axis=0)
            dw = dw + wk * ushift[k % SUBLANES, s0:s0 + CONV_ROW_BLOCK, :]
        conv_out = _silu(_layernorm(dw, clng_ref[...], clnb_ref[...]))
        mixbuf[r0:r0 + CONV_ROW_BLOCK, 0:CONV_CH] = conv_out.astype(BF16)
        mlp(MLP_STAGES_AFTER["conv"].get(rb, 0))
    ubuf[0:HIST_ROWS, :] = ubuf[tile:tile + HIST_ROWS, :]

    q = proj(2)
    kk = proj(3)
    v = proj(4)
    gt = proj(5)
    mlp(MLP_STAGES_AFTER["qkvg"])
    lane = lax.broadcasted_iota(jnp.int32, (1, PAIR_LANES), 1)
    lo = lane < RET_DK
    first_half = (lane % RET_DK) < (RET_DK // 2)
    lo_b = lo.astype(BF16)
    hi_b = (~lo).astype(BF16)
    cos = cos_ref[...]
    sin = sin_ref[...]
    for p in range(N_PAIRS):
        sl = slice(p * PAIR_LANES, (p + 1) * PAIR_LANES)
        qr = _rotary_pair(q[:, sl], cos, sin, first_half)
        kr = _rotary_pair(kk[:, sl], cos, sin, first_half)
        vb = v[:, sl].astype(BF16)
        for c in range(tile // RET_CHUNK):
            rs = slice(c * RET_CHUNK, (c + 1) * RET_CHUNK)
            qc, kc, vc = qr[rs], kr[rs], vb[rs]
            qb = qc.astype(BF16)
            kb = kc.astype(BF16)
            s_a = _dot_nt(qb, kb * lo_b) * dec_ref[2 * p]
            s_b = _dot_nt(qb, kb * hi_b) * dec_ref[2 * p + 1]
            state = sbuf[p]
            o = (_dot(s_a.astype(BF16), vc * lo_b) + _dot(s_b.astype(BF16), vc * hi_b)
                 + _dot((qc * qdec_ref[:, sl]).astype(BF16), state.astype(BF16)))
            kd = (kc * kdec_ref[:, sl]).astype(BF16)
            sbuf[p] = state * sdec_ref[p] + jnp.where(sdec_ref[p] > 0.0, _dot_tn(kd, vc), 0.0)
            on = _pair_groupnorm(o, lo, rlng_ref[:, sl], rlnb_ref[:, sl])
            mixbuf[rs, CONV_CH + p * PAIR_LANES:CONV_CH + (p + 1) * PAIR_LANES] = (
                on * _silu(gt[rs, sl])).astype(BF16)
        mlp(MLP_STAGES_AFTER["pair"].get(p, 0))
    for _ in ffn:
        pass

    mix = _dot(mixbuf[...], wout_ref[...])
    x1buf[...] = x + gate * mix

    @pl.when(jnp.logical_and(j == nj - 1, live))
    def _():
        hist_ref[0] = ubuf[0:HIST_ROWS, :]
        state_ref[0] = sbuf[...]


def _layer_prompt(x, mod, g_mix, w_in, conv_w, conv_b, cln_g, cln_b, rln_g, rln_b, w_out,
                  g_ffn, w1, w2, g_final, rope, decay, tile):
    bsz, seq, d = x.shape
    nj = seq // tile
    total = bsz * nj
    cos, sin = rope
    dec, qdec, kdec, sdec = decay
    mix_tile = lambda g: jnp.minimum(g, total - 1)
    ffn_tile = lambda g: jnp.maximum(g - 1, 0)
    resident = lambda shape: pl.BlockSpec(shape, lambda g: (0,) * len(shape),
                                          pipeline_mode=pl.Buffered(1))
    row = lambda a: a.reshape(1, -1)
    conv_w_rows = jnp.broadcast_to(conv_w[:, None, :], (CONV_K, SUBLANES, CONV_CH))
    rope_spec = pl.BlockSpec((tile, PAIR_LANES), lambda g: (mix_tile(g) % nj, 0))
    return pl.pallas_call(
        functools.partial(_layer_prompt_kernel, tile, nj),
        grid=(total + 1,),
        in_specs=[
            pl.BlockSpec((1, tile, d), lambda g: (mix_tile(g) // nj, mix_tile(g) % nj, 0)),
            pl.BlockSpec((1, 3, d), lambda g: (mix_tile(g) // nj, 0, 0)),
            resident((1, d)),
            resident(w_in.shape),
            resident((CONV_K, SUBLANES, CONV_CH)),
            resident((1, CONV_CH)),
            resident((1, CONV_CH)),
            resident((1, CONV_CH)),
            resident((1, RET_W)),
            resident((1, RET_W)),
            resident(w_out.shape),
            rope_spec,
            rope_spec,
            resident(dec.shape),
            resident(qdec.shape),
            resident(kdec.shape),
            resident(sdec.shape),
            pl.BlockSpec((1, 3, 1, d), lambda g: (ffn_tile(g) // nj, 0, 0, 0)),
            resident((1, d)),
            resident(w1.shape),
            resident(w2.shape),
            resident((1, d)),
        ],
        out_specs=[
            pl.BlockSpec((1, tile, d), lambda g: (ffn_tile(g) // nj, ffn_tile(g) % nj, 0)),
            pl.BlockSpec((1, HIST_ROWS, CONV_CH), lambda g: (mix_tile(g) // nj, 0, 0)),
            pl.BlockSpec((1, N_PAIRS, PAIR_LANES, PAIR_LANES),
                         lambda g: (mix_tile(g) // nj, 0, 0, 0)),
        ],
        out_shape=(
            jax.ShapeDtypeStruct((bsz, seq, d), F32),
            jax.ShapeDtypeStruct((bsz, HIST_ROWS, CONV_CH), F32),
            jax.ShapeDtypeStruct((bsz, N_PAIRS, PAIR_LANES, PAIR_LANES), F32),
        ),
        scratch_shapes=[
            pltpu.VMEM((HIST_ROWS + tile, CONV_CH), F32),
            pltpu.VMEM((SUBLANES, HIST_ROWS + tile, CONV_CH), F32),
            pltpu.VMEM((N_PAIRS, PAIR_LANES, PAIR_LANES), F32),
            pltpu.VMEM((tile, D_MODEL), BF16),
            pltpu.VMEM((tile, D_MODEL), F32),
        ],
        compiler_params=pltpu.CompilerParams(
            dimension_semantics=("arbitrary",), vmem_limit_bytes=VMEM_LIMIT),
        name="layer_prompt",
    )(x, mod[:, 0:3], row(g_mix), w_in, conv_w_rows, row(conv_b), row(cln_g), row(cln_b),
      row(rln_g), row(rln_b), w_out, cos, sin, dec, qdec, kdec, sdec,
      mod[:, 3:6].reshape(bsz, 3, 1, d), row(g_ffn), w1, w2, row(g_final))


def _ffn_stages(x, mod_ref, gffn_ref, w1_ref, w2_ref, gfin_ref, y_ref):
    shift, scale, gate = mod_ref[0, 0], mod_ref[0, 1], mod_ref[0, 2]
    hb = _rmsnorm_mod(x, gffn_ref[...], scale, shift).astype(BF16)
    yield
    ff = None
    for c in range(D_FF // (2 * FFN_CHUNK)):
        acts = []
        for h in range(2):
            c0 = (2 * c + h) * FFN_CHUNK
            a = jnp.maximum(_dot(hb, w1_ref[:, c0:c0 + FFN_CHUNK]), 0.0)
            acts.append((a * a).astype(BF16))
            yield
        part = _dot(jnp.concatenate(acts, axis=1),
                    w2_ref[2 * c * FFN_CHUNK:2 * (c + 1) * FFN_CHUNK, :])
        ff = part if ff is None else ff + part
        yield
    z = x + gate * ff
    y_ref[0] = z * lax.rsqrt(jnp.mean(z * z, axis=-1, keepdims=True) + EPS) * gfin_ref[...]


def _ffn_body(*args):
    for _ in _ffn_stages(*args):
        pass


def _ffn_kernel(x_ref, mod_ref, gffn_ref, w1_ref, w2_ref, gfin_ref, y_ref):
    _ffn_body(x_ref[0], mod_ref, gffn_ref, w1_ref, w2_ref, gfin_ref, y_ref)


def _outproj_ffn_kernel(x_ref, conv_ref, ret_ref, gate1_ref, wout_ref,
                        mod_ref, gffn_ref, w1_ref, w2_ref, gfin_ref, y_ref):
    mix = (_dot(conv_ref[0], wout_ref[0:CONV_CH, :])
           + _dot(ret_ref[0], wout_ref[CONV_CH:CONV_CH + RET_W, :]))
    _ffn_body(x_ref[0] + gate1_ref[...] * mix, mod_ref, gffn_ref, w1_ref, w2_ref, gfin_ref, y_ref)


def _ffn(x, mod2, g_ffn, w1, w2, g_final, tile, out_proj=None):
    bsz, seq, d = x.shape
    nj = seq // tile
    if mod2.shape[2] == 1:
        mod_spec = pl.BlockSpec((1, 3, 1, d), lambda b, j: (b, 0, 0, 0))
    else:
        mod_spec = pl.BlockSpec((1, 3, tile, d), lambda b, j: (0, 0, j, 0))
    const2 = lambda b, j: (0, 0)
    tok = lambda width: pl.BlockSpec((1, tile, width), lambda b, j: (b, j, 0))
    in_specs = [tok(d)]
    args = [x]
    body = _ffn_kernel
    if out_proj is not None:
        conv_out, ret_out, gate1, w_out = out_proj
        in_specs += [tok(CONV_CH), tok(RET_W),
                     pl.BlockSpec((tile, d), lambda b, j: (j, 0)),
                     pl.BlockSpec(w_out.shape, const2)]
        args += [conv_out, ret_out, gate1, w_out]
        body = _outproj_ffn_kernel
    in_specs += [mod_spec,
                 pl.BlockSpec((1, d), const2),
                 pl.BlockSpec(w1.shape, const2),
                 pl.BlockSpec(w2.shape, const2),
                 pl.BlockSpec((1, d), const2)]
    args += [mod2, g_ffn.reshape(1, d), w1, w2, g_final.reshape(1, d)]
    return pl.pallas_call(
        body,
        grid=(bsz, nj),
        in_specs=in_specs,
        out_specs=tok(d),
        out_shape=jax.ShapeDtypeStruct((bsz, seq, d), F32),
        compiler_params=pltpu.CompilerParams(
            dimension_semantics=("arbitrary", "arbitrary"), vmem_limit_bytes=VMEM_LIMIT),
        name="ffn" if out_proj is None else "outproj_ffn",
    )(*args)


def _power_table_kernel(o_ref):
    lane_h = (lax.broadcasted_iota(jnp.int32, o_ref.shape, 1) // RET_DK).astype(F32)
    n = lax.broadcasted_iota(jnp.int32, o_ref.shape, 0).astype(F32)
    o_ref[...] = jnp.exp(_log_decay(lane_h) * n)


def _power_table():
    return pl.pallas_call(_power_table_kernel,
                          out_shape=jax.ShapeDtypeStruct((8, RET_W), F32),
                          name="decay_powers")()


def _proj_sample_kernel(n_pos, n_seq, x_ref, mod_ref, gmix_ref, win_ref, cache_ref, convw_ref,
                        convb_ref, clng_ref, clnb_ref, cos_ref, sin_ref,
                        u_ref, conv_ref, q_ref, k_ref, v_ref, g_ref):
    shift, scale = mod_ref[0], mod_ref[1]
    hb = jnp.concatenate(
        [_rmsnorm_mod(x_ref[i], gmix_ref[...], scale, shift).astype(BF16) for i in range(n_pos)],
        axis=0)

    def proj(k):
        return _dot(hb, win_ref[:, k * CONV_CH:(k + 1) * CONV_CH])

    u = proj(0) * jax.nn.sigmoid(proj(1))
    n_hist = CONV_K - 1
    rows = lambda a, i: a[i * n_seq:(i + 1) * n_seq]
    outs = []
    for i in range(n_pos):
        u_ref[i] = rows(u, i)
        dw = jnp.broadcast_to(convb_ref[...], (n_seq, CONV_CH))
        for k in range(CONV_K):
            m = i + k
            src = cache_ref[m] if m < n_hist else rows(u, m - n_hist)
            dw = dw + convw_ref[k:k + 1, :] * src
        outs.append(_silu(_layernorm(dw, clng_ref[...], clnb_ref[...])).astype(BF16))
    conv_ref[...] = jnp.concatenate(outs, axis=0)

    lane = lax.broadcasted_iota(jnp.int32, (1, PAIR_LANES), 1)
    first_half = (lane % RET_DK) < (RET_DK // 2)
    q = proj(2)
    kk = proj(3)
    for i in range(n_pos):
        rs = slice(i * n_seq, (i + 1) * n_seq)
        cos = cos_ref[i:i + 1, :]
        sin = sin_ref[i:i + 1, :]
        for p in range(N_PAIRS):
            sl = slice(p * PAIR_LANES, (p + 1) * PAIR_LANES)
            q_ref[rs, sl] = _rotary_pair(q[rs, sl], cos, sin, first_half)
            k_ref[rs, sl] = _rotary_pair(kk[rs, sl], cos, sin, first_half)
    v_ref[...] = proj(4)
    g_ref[...] = proj(5)


def _proj_sample(xs, mod1, g_mix, w_in, cache_t, conv_w, conv_b, cln_g, cln_b, rope):
    n_pos, n_seq, d = xs.shape
    n_tok = n_pos * n_seq
    cos, sin = rope
    row = lambda a: a.reshape(1, -1)
    tok = lambda dt: jax.ShapeDtypeStruct((n_tok, RET_W), dt)
    return pl.pallas_call(
        functools.partial(_proj_sample_kernel, n_pos, n_seq),
        out_shape=(jax.ShapeDtypeStruct((n_pos, n_seq, CONV_CH), F32), tok(BF16),
                   tok(F32), tok(F32), tok(F32), tok(F32)),
        compiler_params=pltpu.CompilerParams(vmem_limit_bytes=VMEM_LIMIT),
        name="proj_sample",
    )(xs, mod1, row(g_mix), w_in, cache_t, conv_w, row(conv_b), row(cln_g), row(cln_b), cos, sin)


RET_LANE_BLOCK = 2048


def _ret_sample_kernel(n_pos, n_seq, q_ref, k_ref, v_ref, g_ref, s_ref, pw_ref, rlng_ref,
                       rlnb_ref, out_ref, snew_ref):
    lane = lax.broadcasted_iota(jnp.int32, (1, PAIR_LANES), 1)
    lo = lane < RET_DK
    rows = lambda a, i: a[i * n_seq:(i + 1) * n_seq]
    pw = pw_ref[...]
    q = q_ref[...]
    k = k_ref[...]
    qb = [rows(q, i).astype(BF16) for i in range(n_pos)]
    kb = [rows(k, i).astype(BF16) for i in range(n_pos)]
    vb = [rows(v_ref[...], i).astype(BF16) for i in range(n_pos)]

    def head_bcast(x, a):
        other = pltpu.roll(x, RET_DK, 1)
        return jnp.where(lo, x, other) if a == 0 else jnp.where(lo, other, x)

    o = []
    for i in range(n_pos):
        acc = jnp.zeros((n_seq, PAIR_LANES), F32)
        for j in range(i + 1):
            prod = qb[i].astype(F32) * kb[j].astype(F32)
            s_lo = jnp.sum(jnp.where(lo, prod, 0.0), axis=-1, keepdims=True)
            s_hi = jnp.sum(jnp.where(lo, 0.0, prod), axis=-1, keepdims=True)
            score = jnp.where(lo, s_lo, s_hi) * pw[i - j:i - j + 1, :]
            acc = acc + score.astype(BF16).astype(F32) * vb[j].astype(F32)
        o.append(acc)

    qd = jnp.concatenate([(rows(q, i) * pw[i + 1:i + 2, :]).astype(BF16) for i in range(n_pos)], axis=0)
    kd = jnp.concatenate([(rows(k, j) * pw[n_pos - 1 - j:n_pos - j, :]).astype(BF16)
                          for j in range(n_pos)], axis=0)
    qk = jnp.concatenate([qd, kd], axis=0)
    head_lanes = RET_DK * RET_DV
    cross = [[jnp.zeros((n_seq, PAIR_LANES), F32) for _ in range(2)] for _ in range(n_pos)]
    for blk in range(2 * head_lanes // RET_LANE_BLOCK):
        c0 = blk * RET_LANE_BLOCK
        a = c0 // head_lanes
        col = lax.broadcasted_iota(jnp.int32, (PAIR_LANES, RET_LANE_BLOCK), 1) + c0
        r = lax.broadcasted_iota(jnp.int32, (PAIR_LANES, RET_LANE_BLOCK), 0)
        expand = jnp.where(col // RET_DV == r, 1.0, 0.0).astype(BF16)
        big = _dot(qk, expand)
        s_old = s_ref[:, c0:c0 + RET_LANE_BLOCK]
        s_b = s_old.astype(BF16).astype(F32)
        gam = pw[n_pos:n_pos + 1, a * RET_DK:a * RET_DK + 1]
        s_new = s_old * gam
        reps = RET_LANE_BLOCK // PAIR_LANES
        for i in range(n_pos):
            prod = rows(big, i) * s_b
            t = prod[:, 0:PAIR_LANES]
            for e in range(1, reps):
                t = t + prod[:, e * PAIR_LANES:(e + 1) * PAIR_LANES]
            cross[i][a] = cross[i][a] + t
            vrep = jnp.tile(head_bcast(vb[i].astype(F32), a), (1, reps))
            s_new = s_new + rows(big, n_pos + i) * vrep
        snew_ref[:, c0:c0 + RET_LANE_BLOCK] = s_new

    for i in range(n_pos):
        fold = [c + pltpu.roll(c, RET_DK, 1) for c in cross[i]]
        oi = o[i] + jnp.where(lo, fold[0], fold[1])
        on = _pair_groupnorm(oi, lo, rlng_ref[...], rlnb_ref[...])
        out_ref[i * n_seq:(i + 1) * n_seq, :] = (on * _silu(rows(g_ref[...], i))).astype(BF16)


def _ret_sample(q, k, v, g, state, pw, rln_g, rln_b, n_pos, n_seq):
    n_tok = n_pos * n_seq
    pair_state = 2 * RET_DK * RET_DV
    slab = pl.BlockSpec((n_tok, PAIR_LANES), lambda p: (0, p))
    return pl.pallas_call(
        functools.partial(_ret_sample_kernel, n_pos, n_seq),
        grid=(N_PAIRS,),
        in_specs=[slab, slab, slab, slab,
                  pl.BlockSpec((n_seq, pair_state), lambda p: (0, p)),
                  pl.BlockSpec((8, PAIR_LANES), lambda p: (0, p)),
                  pl.BlockSpec((1, PAIR_LANES), lambda p: (0, p)),
                  pl.BlockSpec((1, PAIR_LANES), lambda p: (0, p))],
        out_specs=[slab, pl.BlockSpec((n_seq, pair_state), lambda p: (0, p))],
        out_shape=(jax.ShapeDtypeStruct((n_tok, RET_W), BF16),
                   jax.ShapeDtypeStruct(state.shape, F32)),
        compiler_params=pltpu.CompilerParams(
            dimension_semantics=("arbitrary",), vmem_limit_bytes=VMEM_LIMIT),
        name="ret_sample",
    )(q, k, v, g, state, pw, rln_g.reshape(1, RET_W), rln_b.reshape(1, RET_W))


def _unpack_state(s2):
    blocks = []
    for p in range(N_PAIRS):
        for a in range(2):
            sl = slice(a * RET_DK, (a + 1) * RET_DK)
            blocks.append(s2[:, p, sl, sl])
    return jnp.stack(blocks, axis=1)


def _cast_weights(w_in, w_out, w_ff1, w_ff2):
    col_scale = jnp.ones((w_in.shape[1],), F32).at[2 * CONV_CH:2 * CONV_CH + RET_W].set(RET_DK ** -0.5)
    return ((w_in * col_scale).astype(BF16), w_out.astype(BF16), w_ff1.astype(BF16),
            w_ff2.astype(BF16))


def _prompt_path(x, mod, g_mix, w_in, conv_w, conv_b, cln_g, cln_b, rln_g, rln_b,
                 w_out, g_ffn, w_ff1, w_ff2, g_final, tile=PROMPT_TILE):
    seq = x.shape[1]
    rope = _rope_tables(seq, 0)
    decay = _decay_tables(RET_CHUNK)
    y, hist, s2 = _layer_prompt(x, mod, g_mix, w_in, conv_w, conv_b, cln_g, cln_b, rln_g, rln_b,
                                w_out, g_ffn, w_ff1, w_ff2, g_final, rope, decay, tile)
    return y, hist[:, HIST_ROWS - (CONV_K - 1):], _unpack_state(s2)


def _sample_path(x, mod, cache, state, pos0, g_mix, w_in, conv_w, conv_b, cln_g, cln_b,
                 rln_g, rln_b, w_out, g_ffn, w_ff1, w_ff2, g_final):
    n_seq, n_pos, d = x.shape
    xs = x.transpose(1, 0, 2)
    modt = mod.transpose(1, 0, 2)
    rope = _rope_tables(8, pos0)
    u, conv_out, q, k, v, g = _proj_sample(xs, modt[0:2], g_mix, w_in, cache.transpose(1, 0, 2),
                                           conv_w, conv_b, cln_g, cln_b, rope)
    ret_out, s_new = _ret_sample(q, k, v, g, state.reshape(n_seq, -1), _power_table(),
                                 rln_g, rln_b, n_pos, n_seq)
    y = _ffn(xs, modt[3:6][None], g_ffn, w_ff1, w_ff2, g_final, n_seq,
             out_proj=(conv_out.reshape(n_pos, n_seq, CONV_CH),
                       ret_out.reshape(n_pos, n_seq, RET_W), modt[2], w_out))
    new_cache = jnp.concatenate([cache[:, n_pos:], u.transpose(1, 0, 2)], axis=1)
    return y.transpose(1, 0, 2), new_cache, s_new.reshape(state.shape)


def kernel(x_prompt, x_sample, cache_conv, state_ret, c_prompt, c_sample, w_ada, b_ada, g_mix, w_in, conv_w, conv_b, conv_ln_g, conv_ln_b, ret_ln_g, ret_ln_b, w_out, g_ffn, w_ff1, w_ff2, g_final):
    bp = x_prompt.shape[0]
    d = x_prompt.shape[-1]
    w_in_b, w_out_b, w_ff1_b, w_ff2_b = _cast_weights(w_in[0], w_out[0], w_ff1[0], w_ff2[0])
    mod = _modulation(jnp.concatenate([c_prompt, c_sample], axis=0), w_ada[0], b_ada[0])
    mod = mod.reshape(-1, N_MOD, d)
    params = (g_mix[0], w_in_b, conv_w[0], conv_b[0], conv_ln_g[0], conv_ln_b[0], ret_ln_g[0],
              ret_ln_b[0], w_out_b, g_ffn[0], w_ff1_b, w_ff2_b, g_final)
    y_p, conv_p, ret_p = _prompt_path(x_prompt, mod[:bp], *params)
    y_s, conv_s, ret_s = _sample_path(x_sample, mod[bp:], cache_conv[0], state_ret[0], PAST_LEN,
                                      *params)
    return (y_p, y_s, conv_p[None], ret_p[None], conv_s[None], ret_s[None])
```

```python
import functools

import jax
import jax.numpy as jnp
from jax import lax
from jax.experimental import pallas as pl
from jax.experimental.pallas import tpu as pltpu

D_MODEL = 1024
CONV_CH = 512
CONV_K = 31
RET_HEADS = 8
RET_W = 512
RET_DK = 64
RET_DV = 64
D_FF = 4 * D_MODEL
RET_CHUNK = 128
ROPE_THETA = 10000.0
EPS = 1e-6
N_MOD = 6
PAST_LEN = 16384

SUBLANES = 8
HIST_ROWS = 32
CONV_ROW_BLOCK = 32
PROMPT_TILE = 256
FFN_CHUNK = 512
MLP_STAGES_AFTER = {
    "glu": 0,
    "shift": {},
    "conv": {0: 1, 1: 1, 2: 1, 3: 1, 4: 1, 5: 1, 6: 1, 7: 1},
    "qkvg": 0,
    "ret": {(0, 0): 1, (0, 1): 1, (1, 0): 1, (1, 1): 1},
}
PAIR_LANES = 2 * RET_DK
N_PAIRS = RET_HEADS // 2
VMEM_LIMIT = 56 * 1024 * 1024

F32 = jnp.float32
BF16 = jnp.bfloat16


def _log_decay(head):
    return jnp.log1p(-jnp.exp2(-5.0 - head))


def _rope_kernel(pos0, cos_ref, sin_ref):
    rows = cos_ref.shape[0]
    lane = lax.broadcasted_iota(jnp.int32, (rows, PAIR_LANES), 1)
    row = lax.broadcasted_iota(jnp.int32, (rows, PAIR_LANES), 0)
    half = RET_DK // 2
    freq = (lane % half).astype(F32)
    inv = jnp.power(jnp.full_like(freq, ROPE_THETA), -freq / half)
    ang = (row + pos0).astype(F32) * inv
    first = (lane % RET_DK) < half
    cos_ref[...] = jnp.cos(ang)
    sin_ref[...] = jnp.where(first, -jnp.sin(ang), jnp.sin(ang))


def _rope_tables(rows, pos0):
    return pl.pallas_call(
        functools.partial(_rope_kernel, pos0),
        out_shape=(jax.ShapeDtypeStruct((rows, PAIR_LANES), F32),
                   jax.ShapeDtypeStruct((rows, PAIR_LANES), F32)),
        name="rope_tables",
    )()


def _decay_kernel(chunk, dec_ref, qdec_ref, kdec_ref, sdec_ref):
    c = chunk
    shape = (N_PAIRS, c, 2 * c)
    col = lax.broadcasted_iota(jnp.int32, shape, 2)
    h = (2 * lax.broadcasted_iota(jnp.int32, shape, 0) + col // c).astype(F32)
    i = lax.broadcasted_iota(jnp.int32, shape, 1).astype(F32)
    j = (col % c).astype(F32)
    diff = i - j
    dec_ref[...] = jnp.where(diff >= 0, jnp.exp(_log_decay(h) * jnp.maximum(diff, 0.0)), 0.0)
    lane_h = (lax.broadcasted_iota(jnp.int32, (c, RET_W), 1) // RET_DK).astype(F32)
    idx = lax.broadcasted_iota(jnp.int32, (c, RET_W), 0).astype(F32)
    lg = _log_decay(lane_h)
    qdec_ref[...] = jnp.exp(lg * (idx + 1.0))
    kdec_ref[...] = jnp.exp(lg * (c - 1.0 - idx))
    p = lax.broadcasted_iota(jnp.int32, (N_PAIRS, PAIR_LANES, PAIR_LANES), 0)
    r = lax.broadcasted_iota(jnp.int32, (N_PAIRS, PAIR_LANES, PAIR_LANES), 1) // RET_DK
    q = lax.broadcasted_iota(jnp.int32, (N_PAIRS, PAIR_LANES, PAIR_LANES), 2) // RET_DK
    hh = (2 * p + r).astype(F32)
    sdec_ref[...] = jnp.where(r == q, jnp.exp(_log_decay(hh) * float(c)), 0.0)


def _decay_tables(chunk):
    return pl.pallas_call(
        functools.partial(_decay_kernel, chunk),
        out_shape=(jax.ShapeDtypeStruct((N_PAIRS, chunk, 2 * chunk), F32),
                   jax.ShapeDtypeStruct((chunk, RET_W), F32),
                   jax.ShapeDtypeStruct((chunk, RET_W), F32),
                   jax.ShapeDtypeStruct((N_PAIRS, PAIR_LANES, PAIR_LANES), F32)),
        name="decay_tables",
    )()


def _mod_kernel(c_ref, w_ref, b_ref, o_ref):
    c = c_ref[...]
    s = c * jax.nn.sigmoid(c)
    o_ref[...] = jnp.dot(s.astype(BF16), w_ref[...].astype(BF16),
                         preferred_element_type=F32) + b_ref[...]


def _modulation(c, w_ada, b_ada):
    n, d = c.shape
    width = w_ada.shape[1]
    blk = D_MODEL
    return pl.pallas_call(
        _mod_kernel,
        grid=(width // blk,),
        in_specs=[pl.BlockSpec((n, d), lambda i: (0, 0)),
                  pl.BlockSpec((d, blk), lambda i: (0, i)),
                  pl.BlockSpec((1, blk), lambda i: (0, i))],
        out_specs=pl.BlockSpec((n, blk), lambda i: (0, i)),
        out_shape=jax.ShapeDtypeStruct((n, width), F32),
        compiler_params=pltpu.CompilerParams(vmem_limit_bytes=VMEM_LIMIT),
        name="adaln_mod",
    )(c, w_ada, b_ada.reshape(1, width))


def _rmsnorm_mod(x, g, scale, shift):
    y = x * lax.rsqrt(jnp.mean(x * x, axis=-1, keepdims=True) + EPS)
    return (y * g) * (1.0 + scale) + shift


def _layernorm(x, g, b):
    mu = jnp.mean(x, axis=-1, keepdims=True)
    d = x - mu
    var = jnp.mean(d * d, axis=-1, keepdims=True)
    return d * lax.rsqrt(var + EPS) * g + b


def _silu(x):
    return x * jax.nn.sigmoid(x)


def _rotary_pair(x, cos, sin_signed, first_half):
    partner = jnp.where(first_half, pltpu.roll(x, PAIR_LANES - RET_DK // 2, 1),
                        pltpu.roll(x, RET_DK // 2, 1))
    return x * cos + partner * sin_signed


def _pair_groupnorm(o, lo, g, b):
    inv_n = 1.0 / RET_DV
    s_lo = jnp.sum(jnp.where(lo, o, 0.0), axis=-1, keepdims=True)
    s_hi = jnp.sum(jnp.where(lo, 0.0, o), axis=-1, keepdims=True)
    d = o - jnp.where(lo, s_lo, s_hi) * inv_n
    d2 = d * d
    v_lo = jnp.sum(jnp.where(lo, d2, 0.0), axis=-1, keepdims=True)
    v_hi = jnp.sum(jnp.where(lo, 0.0, d2), axis=-1, keepdims=True)
    var = jnp.where(lo, v_lo, v_hi) * inv_n
    return d * lax.rsqrt(var + EPS) * g + b


def _dot(a, b):
    return jnp.dot(a, b, preferred_element_type=F32)


def _dot_nt(a, b):
    return lax.dot_general(a, b, (((1,), (1,)), ((), ())), preferred_element_type=F32)


def _dot_tn(a, b):
    return lax.dot_general(a, b, (((0,), (0,)), ((), ())), preferred_element_type=F32)


def _layer_prompt_kernel(tile, nj, x_ref, mod_ref, gmix_ref, win_ref, convw_ref, convb_ref,
                         clng_ref, clnb_ref, rlng_ref, rlnb_ref, wout_ref,
                         cos_ref, sin_ref, dec_ref, qdec_ref, kdec_ref, sdec_ref,
                         mod2_ref, gffn_ref, w1_ref, w2_ref, gfin_ref,
                         y_ref, hist_ref, state_ref,
                         ubuf, ushift, sbuf, mixbuf, x1buf):
    g = pl.program_id(0)
    total = pl.num_programs(0) - 1
    j = lax.rem(jnp.minimum(g, total - 1), nj)
    live = g < total

    @pl.when(g == 0)
    def _():
        x1buf[...] = jnp.zeros((tile, D_MODEL), F32)

    @pl.when(jnp.logical_and(j == 0, live))
    def _():
        ubuf[0:HIST_ROWS, :] = jnp.zeros((HIST_ROWS, CONV_CH), F32)
        sbuf[...] = jnp.zeros(sbuf.shape, F32)

    ffn = _ffn_stages(x1buf[...], mod2_ref, gffn_ref, w1_ref, w2_ref, gfin_ref, y_ref)

    def mlp(n):
        for _ in range(n):
            next(ffn, None)

    mlp(1)
    x = x_ref[0]
    shift, scale, gate = mod_ref[0, 0:1, :], mod_ref[0, 1:2, :], mod_ref[0, 2:3, :]
    hb = _rmsnorm_mod(x, gmix_ref[...], scale, shift).astype(BF16)

    def proj(k):
        return _dot(hb, win_ref[:, k * CONV_CH:(k + 1) * CONV_CH])

    u = proj(0) * jax.nn.sigmoid(proj(1))
    ubuf[HIST_ROWS:HIST_ROWS + tile, :] = u
    mlp(MLP_STAGES_AFTER["glu"])
    off = HIST_ROWS - (CONV_K - 1)
    for r in range(SUBLANES):
        n = tile + SUBLANES * ((CONV_K - 1 - r) // SUBLANES)
        ushift[r, 0:n, :] = ubuf[off + r:off + r + n, :]
        mlp(MLP_STAGES_AFTER["shift"].get(r, 0))
    for rb in range(tile // CONV_ROW_BLOCK):
        r0 = rb * CONV_ROW_BLOCK
        dw = jnp.broadcast_to(convb_ref[...], (CONV_ROW_BLOCK, CONV_CH))
        for k in range(CONV_K):
            s0 = r0 + SUBLANES * (k // SUBLANES)
            wk = jnp.concatenate([convw_ref[k]] * (CONV_ROW_BLOCK // SUBLANES), axis=0)
            dw = dw + wk * ushift[k % SUBLANES, s0:s0 + CONV_ROW_BLOCK, :]
        conv_out = _silu(_layernorm(dw, clng_ref[...], clnb_ref[...]))
        mixbuf[r0:r0 + CONV_ROW_BLOCK, 0:CONV_CH] = conv_out.astype(BF16)
        mlp(MLP_STAGES_AFTER["conv"].get(rb, 0))
    ubuf[0:HIST_ROWS, :] = ubuf[tile:tile + HIST_ROWS, :]

    q = proj(2)
    kk = proj(3)
    v = proj(4)
    gt = proj(5)
    mlp(MLP_STAGES_AFTER["qkvg"])
    lane = lax.broadcasted_iota(jnp.int32, (1, PAIR_LANES), 1)
    lo = lane < RET_DK
    first_half = (lane % RET_DK) < (RET_DK // 2)
    lo_b = lo.astype(BF16)
    hi_b = (~lo).astype(BF16)
    pairs = range(N_PAIRS)
    lanes = [slice(p * PAIR_LANES, (p + 1) * PAIR_LANES) for p in pairs]
    for c in range(tile // RET_CHUNK):
        rs = slice(c * RET_CHUNK, (c + 1) * RET_CHUNK)
        cos = cos_ref[rs, :]
        sin = sin_ref[rs, :]
        qc = [_rotary_pair(q[rs, sl], cos, sin, first_half) for sl in lanes]
        kc = [_rotary_pair(kk[rs, sl], cos, sin, first_half) for sl in lanes]
        vc = [v[rs, sl].astype(BF16) for sl in lanes]
        scores = []
        for p in pairs:
            kb = kc[p].astype(BF16)
            k_cat = jnp.concatenate([kb * lo_b, kb * hi_b], axis=0)
            scores.append(_dot_nt(qc[p].astype(BF16), k_cat))
        mlp(MLP_STAGES_AFTER["ret"].get((c, 0), 0))
        outs = []
        for p in pairs:
            state = sbuf[p]
            lhs = jnp.concatenate([(scores[p] * dec_ref[p]).astype(BF16),
                                   (qc[p] * qdec_ref[:, lanes[p]]).astype(BF16)], axis=1)
            rhs = jnp.concatenate([vc[p] * lo_b, vc[p] * hi_b, state.astype(BF16)], axis=0)
            outs.append(_dot(lhs, rhs))
            kd = (kc[p] * kdec_ref[:, lanes[p]]).astype(BF16)
            sbuf[p] = state * sdec_ref[p] + jnp.where(sdec_ref[p] > 0.0, _dot_tn(kd, vc[p]), 0.0)
        mlp(MLP_STAGES_AFTER["ret"].get((c, 1), 0))
        for p in pairs:
            on = _pair_groupnorm(outs[p], lo, rlng_ref[:, lanes[p]], rlnb_ref[:, lanes[p]])
            mixbuf[rs, CONV_CH + p * PAIR_LANES:CONV_CH + (p + 1) * PAIR_LANES] = (
                on * _silu(gt[rs, lanes[p]])).astype(BF16)
        mlp(MLP_STAGES_AFTER["ret"].get((c, 2), 0))
    for _ in ffn:
        pass

    mix = _dot(mixbuf[...], wout_ref[...])
    x1buf[...] = x + gate * mix

    @pl.when(jnp.logical_and(j == nj - 1, live))
    def _():
        hist_ref[0] = ubuf[0:HIST_ROWS, :]
        state_ref[0] = sbuf[...]


def _layer_prompt(x, mod, g_mix, w_in, conv_w, conv_b, cln_g, cln_b, rln_g, rln_b, w_out,
                  g_ffn, w1, w2, g_final, rope, decay, tile):
    bsz, seq, d = x.shape
    nj = seq // tile
    total = bsz * nj
    cos, sin = rope
    dec, qdec, kdec, sdec = decay
    mix_tile = lambda g: jnp.minimum(g, total - 1)
    ffn_tile = lambda g: jnp.maximum(g - 1, 0)
    resident = lambda shape: pl.BlockSpec(shape, lambda g: (0,) * len(shape),
                                          pipeline_mode=pl.Buffered(1))
    row = lambda a: a.reshape(1, -1)
    conv_w_rows = jnp.broadcast_to(conv_w[:, None, :], (CONV_K, SUBLANES, CONV_CH))
    rope_spec = pl.BlockSpec((tile, PAIR_LANES), lambda g: (mix_tile(g) % nj, 0))
    return pl.pallas_call(
        functools.partial(_layer_prompt_kernel, tile, nj),
        grid=(total + 1,),
        in_specs=[
            pl.BlockSpec((1, tile, d), lambda g: (mix_tile(g) // nj, mix_tile(g) % nj, 0)),
            pl.BlockSpec((1, 3, d), lambda g: (mix_tile(g) // nj, 0, 0)),
            resident((1, d)),
            resident(w_in.shape),
            resident((CONV_K, SUBLANES, CONV_CH)),
            resident((1, CONV_CH)),
            resident((1, CONV_CH)),
            resident((1, CONV_CH)),
            resident((1, RET_W)),
            resident((1, RET_W)),
            resident(w_out.shape),
            rope_spec,
            rope_spec,
            resident(dec.shape),
            resident(qdec.shape),
            resident(kdec.shape),
            resident(sdec.shape),
            pl.BlockSpec((1, 3, 1, d), lambda g: (ffn_tile(g) // nj, 0, 0, 0)),
            resident((1, d)),
            resident(w1.shape),
            resident(w2.shape),
            resident((1, d)),
        ],
        out_specs=[
            pl.BlockSpec((1, tile, d), lambda g: (ffn_tile(g) // nj, ffn_tile(g) % nj, 0)),
            pl.BlockSpec((1, HIST_ROWS, CONV_CH), lambda g: (mix_tile(g) // nj, 0, 0)),
            pl.BlockSpec((1, N_PAIRS, PAIR_LANES, PAIR_LANES),
                         lambda g: (mix_tile(g) // nj, 0, 0, 0)),
        ],
        out_shape=(
            jax.ShapeDtypeStruct((bsz, seq, d), F32),
            jax.ShapeDtypeStruct((bsz, HIST_ROWS, CONV_CH), F32),
            jax.ShapeDtypeStruct((bsz, N_PAIRS, PAIR_LANES, PAIR_LANES), F32),
        ),
        scratch_shapes=[
            pltpu.VMEM((HIST_ROWS + tile, CONV_CH), F32),
            pltpu.VMEM((SUBLANES, HIST_ROWS + tile, CONV_CH), F32),
            pltpu.VMEM((N_PAIRS, PAIR_LANES, PAIR_LANES), F32),
            pltpu.VMEM((tile, D_MODEL), BF16),
            pltpu.VMEM((tile, D_MODEL), F32),
        ],
        compiler_params=pltpu.CompilerParams(
            dimension_semantics=("arbitrary",), vmem_limit_bytes=VMEM_LIMIT),
        name="layer_prompt",
    )(x, mod[:, 0:3], row(g_mix), w_in, conv_w_rows, row(conv_b), row(cln_g), row(cln_b),
      row(rln_g), row(rln_b), w_out, cos, sin, dec, qdec, kdec, sdec,
      mod[:, 3:6].reshape(bsz, 3, 1, d), row(g_ffn), w1, w2, row(g_final))


def _ffn_stages(x, mod_ref, gffn_ref, w1_ref, w2_ref, gfin_ref, y_ref):
    shift, scale, gate = mod_ref[0, 0], mod_ref[0, 1], mod_ref[0, 2]
    hb = _rmsnorm_mod(x, gffn_ref[...], scale, shift).astype(BF16)
    yield
    ff = None
    for c in range(D_FF // (2 * FFN_CHUNK)):
        acts = []
        for h in range(2):
            c0 = (2 * c + h) * FFN_CHUNK
            a = jnp.maximum(_dot(hb, w1_ref[:, c0:c0 + FFN_CHUNK]), 0.0)
            acts.append((a * a).astype(BF16))
            yield
        part = _dot(jnp.concatenate(acts, axis=1),
                    w2_ref[2 * c * FFN_CHUNK:2 * (c + 1) * FFN_CHUNK, :])
        ff = part if ff is None else ff + part
        yield
    z = x + gate * ff
    y_ref[0] = z * lax.rsqrt(jnp.mean(z * z, axis=-1, keepdims=True) + EPS) * gfin_ref[...]


def _ffn_body(*args):
    for _ in _ffn_stages(*args):
        pass


def _ffn_kernel(x_ref, mod_ref, gffn_ref, w1_ref, w2_ref, gfin_ref, y_ref):
    _ffn_body(x_ref[0], mod_ref, gffn_ref, w1_ref, w2_ref, gfin_ref, y_ref)


def _outproj_ffn_kernel(x_ref, conv_ref, ret_ref, gate1_ref, wout_ref,
                        mod_ref, gffn_ref, w1_ref, w2_ref, gfin_ref, y_ref):
    mix = (_dot(conv_ref[0], wout_ref[0:CONV_CH, :])
           + _dot(ret_ref[0], wout_ref[CONV_CH:CONV_CH + RET_W, :]))
    _ffn_body(x_ref[0] + gate1_ref[...] * mix, mod_ref, gffn_ref, w1_ref, w2_ref, gfin_ref, y_ref)


def _ffn(x, mod2, g_ffn, w1, w2, g_final, tile, out_proj=None):
    bsz, seq, d = x.shape
    nj = seq // tile
    if mod2.shape[2] == 1:
        mod_spec = pl.BlockSpec((1, 3, 1, d), lambda b, j: (b, 0, 0, 0))
    else:
        mod_spec = pl.BlockSpec((1, 3, tile, d), lambda b, j: (0, 0, j, 0))
    const2 = lambda b, j: (0, 0)
    tok = lambda width: pl.BlockSpec((1, tile, width), lambda b, j: (b, j, 0))
    in_specs = [tok(d)]
    args = [x]
    body = _ffn_kernel
    if out_proj is not None:
        conv_out, ret_out, gate1, w_out = out_proj
        in_specs += [tok(CONV_CH), tok(RET_W),
                     pl.BlockSpec((tile, d), lambda b, j: (j, 0)),
                     pl.BlockSpec(w_out.shape, const2)]
        args += [conv_out, ret_out, gate1, w_out]
        body = _outproj_ffn_kernel
    in_specs += [mod_spec,
                 pl.BlockSpec((1, d), const2),
                 pl.BlockSpec(w1.shape, const2),
                 pl.BlockSpec(w2.shape, const2),
                 pl.BlockSpec((1, d), const2)]
    args += [mod2, g_ffn.reshape(1, d), w1, w2, g_final.reshape(1, d)]
    return pl.pallas_call(
        body,
        grid=(bsz, nj),
        in_specs=in_specs,
        out_specs=tok(d),
        out_shape=jax.ShapeDtypeStruct((bsz, seq, d), F32),
        compiler_params=pltpu.CompilerParams(
            dimension_semantics=("arbitrary", "arbitrary"), vmem_limit_bytes=VMEM_LIMIT),
        name="ffn" if out_proj is None else "outproj_ffn",
    )(*args)


def _power_table_kernel(o_ref):
    lane_h = (lax.broadcasted_iota(jnp.int32, o_ref.shape, 1) // RET_DK).astype(F32)
    n = lax.broadcasted_iota(jnp.int32, o_ref.shape, 0).astype(F32)
    o_ref[...] = jnp.exp(_log_decay(lane_h) * n)


def _power_table():
    return pl.pallas_call(_power_table_kernel,
                          out_shape=jax.ShapeDtypeStruct((8, RET_W), F32),
                          name="decay_powers")()


def _proj_sample_kernel(n_pos, n_seq, x_ref, mod_ref, gmix_ref, win_ref, cache_ref, convw_ref,
                        convb_ref, clng_ref, clnb_ref, cos_ref, sin_ref,
                        u_ref, conv_ref, q_ref, k_ref, v_ref, g_ref):
    shift, scale = mod_ref[0], mod_ref[1]
    hb = jnp.concatenate(
        [_rmsnorm_mod(x_ref[i], gmix_ref[...], scale, shift).astype(BF16) for i in range(n_pos)],
        axis=0)

    def proj(k):
        return _dot(hb, win_ref[:, k * CONV_CH:(k + 1) * CONV_CH])

    u = proj(0) * jax.nn.sigmoid(proj(1))
    n_hist = CONV_K - 1
    rows = lambda a, i: a[i * n_seq:(i + 1) * n_seq]
    outs = []
    for i in range(n_pos):
        u_ref[i] = rows(u, i)
        dw = jnp.broadcast_to(convb_ref[...], (n_seq, CONV_CH))
        for k in range(CONV_K):
            m = i + k
            src = cache_ref[m] if m < n_hist else rows(u, m - n_hist)
            dw = dw + convw_ref[k:k + 1, :] * src
        outs.append(_silu(_layernorm(dw, clng_ref[...], clnb_ref[...])).astype(BF16))
    conv_ref[...] = jnp.concatenate(outs, axis=0)

    lane = lax.broadcasted_iota(jnp.int32, (1, PAIR_LANES), 1)
    first_half = (lane % RET_DK) < (RET_DK // 2)
    q = proj(2)
    kk = proj(3)
    for i in range(n_pos):
        rs = slice(i * n_seq, (i + 1) * n_seq)
        cos = cos_ref[i:i + 1, :]
        sin = sin_ref[i:i + 1, :]
        for p in range(N_PAIRS):
            sl = slice(p * PAIR_LANES, (p + 1) * PAIR_LANES)
            q_ref[rs, sl] = _rotary_pair(q[rs, sl], cos, sin, first_half)
            k_ref[rs, sl] = _rotary_pair(kk[rs, sl], cos, sin, first_half)
    v_ref[...] = proj(4)
    g_ref[...] = proj(5)


def _proj_sample(xs, mod1, g_mix, w_in, cache_t, conv_w, conv_b, cln_g, cln_b, rope):
    n_pos, n_seq, d = xs.shape
    n_tok = n_pos * n_seq
    cos, sin = rope
    row = lambda a: a.reshape(1, -1)
    tok = lambda dt: jax.ShapeDtypeStruct((n_tok, RET_W), dt)
    return pl.pallas_call(
        functools.partial(_proj_sample_kernel, n_pos, n_seq),
        out_shape=(jax.ShapeDtypeStruct((n_pos, n_seq, CONV_CH), F32), tok(BF16),
                   tok(F32), tok(F32), tok(F32), tok(F32)),
        compiler_params=pltpu.CompilerParams(vmem_limit_bytes=VMEM_LIMIT),
        name="proj_sample",
    )(xs, mod1, row(g_mix), w_in, cache_t, conv_w, row(conv_b), row(cln_g), row(cln_b), cos, sin)


RET_LANE_BLOCK = 2048


def _ret_sample_kernel(n_pos, n_seq, q_ref, k_ref, v_ref, g_ref, s_ref, pw_ref, rlng_ref,
                       rlnb_ref, out_ref, snew_ref):
    lane = lax.broadcasted_iota(jnp.int32, (1, PAIR_LANES), 1)
    lo = lane < RET_DK
    rows = lambda a, i: a[i * n_seq:(i + 1) * n_seq]
    pw = pw_ref[...]
    q = q_ref[...]
    k = k_ref[...]
    qb = [rows(q, i).astype(BF16) for i in range(n_pos)]
    kb = [rows(k, i).astype(BF16) for i in range(n_pos)]
    vb = [rows(v_ref[...], i).astype(BF16) for i in range(n_pos)]

    def head_bcast(x, a):
        other = pltpu.roll(x, RET_DK, 1)
        return jnp.where(lo, x, other) if a == 0 else jnp.where(lo, other, x)

    o = []
    for i in range(n_pos):
        acc = jnp.zeros((n_seq, PAIR_LANES), F32)
        for j in range(i + 1):
            prod = qb[i].astype(F32) * kb[j].astype(F32)
            s_lo = jnp.sum(jnp.where(lo, prod, 0.0), axis=-1, keepdims=True)
            s_hi = jnp.sum(jnp.where(lo, 0.0, prod), axis=-1, keepdims=True)
            score = jnp.where(lo, s_lo, s_hi) * pw[i - j:i - j + 1, :]
            acc = acc + score.astype(BF16).astype(F32) * vb[j].astype(F32)
        o.append(acc)

    qd = jnp.concatenate([(rows(q, i) * pw[i + 1:i + 2, :]).astype(BF16) for i in range(n_pos)], axis=0)
    kd = jnp.concatenate([(rows(k, j) * pw[n_pos - 1 - j:n_pos - j, :]).astype(BF16)
                          for j in range(n_pos)], axis=0)
    qk = jnp.concatenate([qd, kd], axis=0)
    head_lanes = RET_DK * RET_DV
    cross = [[jnp.zeros((n_seq, PAIR_LANES), F32) for _ in range(2)] for _ in range(n_pos)]
    for blk in range(2 * head_lanes // RET_LANE_BLOCK):
        c0 = blk * RET_LANE_BLOCK
        a = c0 // head_lanes
        col = lax.broadcasted_iota(jnp.int32, (PAIR_LANES, RET_LANE_BLOCK), 1) + c0
        r = lax.broadcasted_iota(jnp.int32, (PAIR_LANES, RET_LANE_BLOCK), 0)
        expand = jnp.where(col // RET_DV == r, 1.0, 0.0).astype(BF16)
        big = _dot(qk, expand)
        s_old = s_ref[:, c0:c0 + RET_LANE_BLOCK]
        s_b = s_old.astype(BF16).astype(F32)
        gam = pw[n_pos:n_pos + 1, a * RET_DK:a * RET_DK + 1]
        s_new = s_old * gam
        reps = RET_LANE_BLOCK // PAIR_LANES
        for i in range(n_pos):
            prod = rows(big, i) * s_b
            t = prod[:, 0:PAIR_LANES]
            for e in range(1, reps):
                t = t + prod[:, e * PAIR_LANES:(e + 1) * PAIR_LANES]
            cross[i][a] = cross[i][a] + t
            vrep = jnp.tile(head_bcast(vb[i].astype(F32), a), (1, reps))
            s_new = s_new + rows(big, n_pos + i) * vrep
        snew_ref[:, c0:c0 + RET_LANE_BLOCK] = s_new

    for i in range(n_pos):
        fold = [c + pltpu.roll(c, RET_DK, 1) for c in cross[i]]
        oi = o[i] + jnp.where(lo, fold[0], fold[1])
        on = _pair_groupnorm(oi, lo, rlng_ref[...], rlnb_ref[...])
        out_ref[i * n_seq:(i + 1) * n_seq, :] = (on * _silu(rows(g_ref[...], i))).astype(BF16)


def _ret_sample(q, k, v, g, state, pw, rln_g, rln_b, n_pos, n_seq):
    n_tok = n_pos * n_seq
    pair_state = 2 * RET_DK * RET_DV
    slab = pl.BlockSpec((n_tok, PAIR_LANES), lambda p: (0, p))
    return pl.pallas_call(
        functools.partial(_ret_sample_kernel, n_pos, n_seq),
        grid=(N_PAIRS,),
        in_specs=[slab, slab, slab, slab,
                  pl.BlockSpec((n_seq, pair_state), lambda p: (0, p)),
                  pl.BlockSpec((8, PAIR_LANES), lambda p: (0, p)),
                  pl.BlockSpec((1, PAIR_LANES), lambda p: (0, p)),
                  pl.BlockSpec((1, PAIR_LANES), lambda p: (0, p))],
        out_specs=[slab, pl.BlockSpec((n_seq, pair_state), lambda p: (0, p))],
        out_shape=(jax.ShapeDtypeStruct((n_tok, RET_W), BF16),
                   jax.ShapeDtypeStruct(state.shape, F32)),
        compiler_params=pltpu.CompilerParams(
            dimension_semantics=("arbitrary",), vmem_limit_bytes=VMEM_LIMIT),
        name="ret_sample",
    )(q, k, v, g, state, pw, rln_g.reshape(1, RET_W), rln_b.reshape(1, RET_W))


def _unpack_state(s2):
    blocks = []
    for p in range(N_PAIRS):
        for a in range(2):
            sl = slice(a * RET_DK, (a + 1) * RET_DK)
            blocks.append(s2[:, p, sl, sl])
    return jnp.stack(blocks, axis=1)


def _cast_weights(w_in, w_out, w_ff1, w_ff2):
    col_scale = jnp.ones((w_in.shape[1],), F32).at[2 * CONV_CH:2 * CONV_CH + RET_W].set(RET_DK ** -0.5)
    return ((w_in * col_scale).astype(BF16), w_out.astype(BF16), w_ff1.astype(BF16),
            w_ff2.astype(BF16))


def _prompt_path(x, mod, g_mix, w_in, conv_w, conv_b, cln_g, cln_b, rln_g, rln_b,
                 w_out, g_ffn, w_ff1, w_ff2, g_final, tile=PROMPT_TILE):
    seq = x.shape[1]
    rope = _rope_tables(seq, 0)
    decay = _decay_tables(RET_CHUNK)
    y, hist, s2 = _layer_prompt(x, mod, g_mix, w_in, conv_w, conv_b, cln_g, cln_b, rln_g, rln_b,
                                w_out, g_ffn, w_ff1, w_ff2, g_final, rope, decay, tile)
    return y, hist[:, HIST_ROWS - (CONV_K - 1):], _unpack_state(s2)


def _sample_path(x, mod, cache, state, pos0, g_mix, w_in, conv_w, conv_b, cln_g, cln_b,
                 rln_g, rln_b, w_out, g_ffn, w_ff1, w_ff2, g_final):
    n_seq, n_pos, d = x.shape
    xs = x.transpose(1, 0, 2)
    modt = mod.transpose(1, 0, 2)
    rope = _rope_tables(8, pos0)
    u, conv_out, q, k, v, g = _proj_sample(xs, modt[0:2], g_mix, w_in, cache.transpose(1, 0, 2),
                                           conv_w, conv_b, cln_g, cln_b, rope)
    ret_out, s_new = _ret_sample(q, k, v, g, state.reshape(n_seq, -1), _power_table(),
                                 rln_g, rln_b, n_pos, n_seq)
    y = _ffn(xs, modt[3:6][None], g_ffn, w_ff1, w_ff2, g_final, n_seq,
             out_proj=(conv_out.reshape(n_pos, n_seq, CONV_CH),
                       ret_out.reshape(n_pos, n_seq, RET_W), modt[2], w_out))
    new_cache = jnp.concatenate([cache[:, n_pos:], u.transpose(1, 0, 2)], axis=1)
    return y.transpose(1, 0, 2), new_cache, s_new.reshape(state.shape)


def kernel(x_prompt, x_sample, cache_conv, state_ret, c_prompt, c_sample, w_ada, b_ada, g_mix, w_in, conv_w, conv_b, conv_ln_g, conv_ln_b, ret_ln_g, ret_ln_b, w_out, g_ffn, w_ff1, w_ff2, g_final):
    bp = x_prompt.shape[0]
    d = x_prompt.shape[-1]
    w_in_b, w_out_b, w_ff1_b, w_ff2_b = _cast_weights(w_in[0], w_out[0], w_ff1[0], w_ff2[0])
    mod = _modulation(jnp.concatenate([c_prompt, c_sample], axis=0), w_ada[0], b_ada[0])
    mod = mod.reshape(-1, N_MOD, d)
    params = (g_mix[0], w_in_b, conv_w[0], conv_b[0], conv_ln_g[0], conv_ln_b[0], ret_ln_g[0],
              ret_ln_b[0], w_out_b, g_ffn[0], w_ff1_b, w_ff2_b, g_final)
    y_p, conv_p, ret_p = _prompt_path(x_prompt, mod[:bp], *params)
    y_s, conv_s, ret_s = _sample_path(x_sample, mod[bp:], cache_conv[0], state_ret[0], PAST_LEN,
                                      *params)
    return (y_p, y_s, conv_p[None], ret_p[None], conv_s[None], ret_s[None])
```

```python
import functools

import jax
import jax.numpy as jnp
from jax import lax
from jax.experimental import pallas as pl
from jax.experimental.pallas import tpu as pltpu

D_MODEL = 1024
CONV_CH = 512
CONV_K = 31
RET_HEADS = 8
RET_W = 512
RET_DK = 64
RET_DV = 64
D_FF = 4 * D_MODEL
RET_CHUNK = 128
ROPE_THETA = 10000.0
EPS = 1e-6
N_MOD = 6
PAST_LEN = 16384

SUBLANES = 8
HIST_ROWS = 32
CONV_ROW_BLOCK = 32
PROMPT_TILE = 256
FFN_CHUNK = 512
SAMPLE_FFN_CHUNK = 1024
MLP_STAGES_AFTER = {
    "glu": 0,
    "shift": {},
    "conv": {0: 1, 1: 1, 2: 1, 3: 1, 4: 1, 5: 1, 6: 1, 7: 1},
    "qkvg": 0,
    "ret": {(0, 0): 1, (0, 1): 1, (1, 0): 1, (1, 1): 1},
}
PAIR_LANES = 2 * RET_DK
N_PAIRS = RET_HEADS // 2
VMEM_LIMIT = 56 * 1024 * 1024

F32 = jnp.float32
BF16 = jnp.bfloat16


def _log_decay(head):
    return jnp.log1p(-jnp.exp2(-5.0 - head))


def _rope_kernel(pos0, cos_ref, sin_ref):
    rows = cos_ref.shape[0]
    lane = lax.broadcasted_iota(jnp.int32, (rows, PAIR_LANES), 1)
    row = lax.broadcasted_iota(jnp.int32, (rows, PAIR_LANES), 0)
    half = RET_DK // 2
    freq = (lane % half).astype(F32)
    inv = jnp.power(jnp.full_like(freq, ROPE_THETA), -freq / half)
    ang = (row + pos0).astype(F32) * inv
    first = (lane % RET_DK) < half
    cos_ref[...] = jnp.cos(ang)
    sin_ref[...] = jnp.where(first, -jnp.sin(ang), jnp.sin(ang))


ROPE_BLOCK = 16


def _rope_blocked_kernel(pos0, cos_ref, sin_ref):
    rows = cos_ref.shape[0]
    n_blk = rows // ROPE_BLOCK
    half = RET_DK // 2

    def angles(n, step, start):
        lane = lax.broadcasted_iota(jnp.int32, (n, PAIR_LANES), 1)
        row = lax.broadcasted_iota(jnp.int32, (n, PAIR_LANES), 0)
        freq = (lane % half).astype(F32)
        inv = jnp.power(jnp.full_like(freq, ROPE_THETA), -freq / half)
        return (row * step + start).astype(F32) * inv

    a = angles(n_blk, ROPE_BLOCK, pos0)
    b = angles(ROPE_BLOCK, 1, 0)
    lane = lax.broadcasted_iota(jnp.int32, (ROPE_BLOCK, PAIR_LANES), 1)
    sign = jnp.where((lane % RET_DK) < half, -1.0, 1.0)
    cos_a, sin_a = jnp.cos(a), jnp.sin(a)
    cos_b, sin_b = jnp.cos(b), jnp.sin(b)
    cos_bs, sin_bs = cos_b * sign, sin_b * sign
    for blk in range(n_blk):
        ca, sa = cos_a[blk:blk + 1, :], sin_a[blk:blk + 1, :]
        rs = slice(blk * ROPE_BLOCK, (blk + 1) * ROPE_BLOCK)
        cos_ref[rs, :] = ca * cos_b - sa * sin_b
        sin_ref[rs, :] = sa * cos_bs + ca * sin_bs


def _rope_tables(rows, pos0):
    blocked = rows % ROPE_BLOCK == 0 and rows // ROPE_BLOCK >= SUBLANES
    return pl.pallas_call(
        functools.partial(_rope_blocked_kernel if blocked else _rope_kernel, pos0),
        out_shape=(jax.ShapeDtypeStruct((rows, PAIR_LANES), F32),
                   jax.ShapeDtypeStruct((rows, PAIR_LANES), F32)),
        name="rope_tables",
    )()


def _decay_kernel(chunk, dec_ref, qdec_ref, kdec_ref, sdec_ref):
    c = chunk
    shape = (N_PAIRS, c, 2 * c)
    col = lax.broadcasted_iota(jnp.int32, shape, 2)
    h = (2 * lax.broadcasted_iota(jnp.int32, shape, 0) + col // c).astype(F32)
    i = lax.broadcasted_iota(jnp.int32, shape, 1).astype(F32)
    j = (col % c).astype(F32)
    diff = i - j
    dec_ref[...] = jnp.where(diff >= 0, jnp.exp(_log_decay(h) * jnp.maximum(diff, 0.0)), 0.0)
    lane_h = (lax.broadcasted_iota(jnp.int32, (c, RET_W), 1) // RET_DK).astype(F32)
    idx = lax.broadcasted_iota(jnp.int32, (c, RET_W), 0).astype(F32)
    lg = _log_decay(lane_h)
    qdec_ref[...] = jnp.exp(lg * (idx + 1.0))
    kdec_ref[...] = jnp.exp(lg * (c - 1.0 - idx))
    p = lax.broadcasted_iota(jnp.int32, (N_PAIRS, PAIR_LANES, PAIR_LANES), 0)
    r = lax.broadcasted_iota(jnp.int32, (N_PAIRS, PAIR_LANES, PAIR_LANES), 1) // RET_DK
    q = lax.broadcasted_iota(jnp.int32, (N_PAIRS, PAIR_LANES, PAIR_LANES), 2) // RET_DK
    hh = (2 * p + r).astype(F32)
    sdec_ref[...] = jnp.where(r == q, jnp.exp(_log_decay(hh) * float(c)), 0.0)


def _decay_tables(chunk):
    return pl.pallas_call(
        functools.partial(_decay_kernel, chunk),
        out_shape=(jax.ShapeDtypeStruct((N_PAIRS, chunk, 2 * chunk), F32),
                   jax.ShapeDtypeStruct((chunk, RET_W), F32),
                   jax.ShapeDtypeStruct((chunk, RET_W), F32),
                   jax.ShapeDtypeStruct((N_PAIRS, PAIR_LANES, PAIR_LANES), F32)),
        name="decay_tables",
    )()


def _mod_kernel(c_ref, w_ref, b_ref, o_ref):
    c = c_ref[...]
    s = c * jax.nn.sigmoid(c)
    o_ref[...] = jnp.dot(s.astype(BF16), w_ref[...].astype(BF16),
                         preferred_element_type=F32) + b_ref[...]


def _modulation(c, w_ada, b_ada):
    n, d = c.shape
    width = w_ada.shape[1]
    blk = D_MODEL
    return pl.pallas_call(
        _mod_kernel,
        grid=(width // blk,),
        in_specs=[pl.BlockSpec((n, d), lambda i: (0, 0)),
                  pl.BlockSpec((d, blk), lambda i: (0, i)),
                  pl.BlockSpec((1, blk), lambda i: (0, i))],
        out_specs=pl.BlockSpec((n, blk), lambda i: (0, i)),
        out_shape=jax.ShapeDtypeStruct((n, width), F32),
        compiler_params=pltpu.CompilerParams(vmem_limit_bytes=VMEM_LIMIT),
        name="adaln_mod",
    )(c, w_ada, b_ada.reshape(1, width))


def _rmsnorm_mod(x, g, scale, shift):
    y = x * lax.rsqrt(jnp.mean(x * x, axis=-1, keepdims=True) + EPS)
    return (y * g) * (1.0 + scale) + shift


def _layernorm(x, g, b):
    mu = jnp.mean(x, axis=-1, keepdims=True)
    d = x - mu
    var = jnp.mean(d * d, axis=-1, keepdims=True)
    return d * lax.rsqrt(var + EPS) * g + b


def _silu(x):
    return x * jax.nn.sigmoid(x)


def _rotary_pair(x, cos, sin_signed, first_half):
    partner = jnp.where(first_half, pltpu.roll(x, PAIR_LANES - RET_DK // 2, 1),
                        pltpu.roll(x, RET_DK // 2, 1))
    return x * cos + partner * sin_signed


def _pair_groupnorm(o, lo, g, b):
    inv_n = 1.0 / RET_DV
    s_lo = jnp.sum(jnp.where(lo, o, 0.0), axis=-1, keepdims=True)
    s_hi = jnp.sum(jnp.where(lo, 0.0, o), axis=-1, keepdims=True)
    d = o - jnp.where(lo, s_lo, s_hi) * inv_n
    d2 = d * d
    v_lo = jnp.sum(jnp.where(lo, d2, 0.0), axis=-1, keepdims=True)
    v_hi = jnp.sum(jnp.where(lo, 0.0, d2), axis=-1, keepdims=True)
    var = jnp.where(lo, v_lo, v_hi) * inv_n
    return d * lax.rsqrt(var + EPS) * g + b


def _dot(a, b):
    return jnp.dot(a, b, preferred_element_type=F32)


def _dot_nt(a, b):
    return lax.dot_general(a, b, (((1,), (1,)), ((), ())), preferred_element_type=F32)


def _dot_tn(a, b):
    return lax.dot_general(a, b, (((0,), (0,)), ((), ())), preferred_element_type=F32)


def _layer_prompt_kernel(tile, nj, x_ref, mod_ref, gmix_ref, win_ref, convw_ref, convb_ref,
                         clng_ref, clnb_ref, rlng_ref, rlnb_ref, wout_ref,
                         cos_ref, sin_ref, dec_ref, qdec_ref, kdec_ref, sdec_ref,
                         mod2_ref, gffn_ref, w1_ref, w2_ref, gfin_ref,
                         y_ref, hist_ref, state_ref,
                         ubuf, ushift, sbuf, mixbuf, x1buf):
    g = pl.program_id(0)
    total = pl.num_programs(0) - 1
    j = lax.rem(jnp.minimum(g, total - 1), nj)
    live = g < total

    @pl.when(g == 0)
    def _():
        x1buf[...] = jnp.zeros((tile, D_MODEL), F32)

    @pl.when(jnp.logical_and(j == 0, live))
    def _():
        ubuf[0:HIST_ROWS, :] = jnp.zeros((HIST_ROWS, CONV_CH), F32)
        sbuf[...] = jnp.zeros(sbuf.shape, F32)

    ffn = _ffn_stages(x1buf[...], mod2_ref, gffn_ref, w1_ref, w2_ref, gfin_ref, y_ref)

    def mlp(n):
        for _ in range(n):
            next(ffn, None)

    mlp(1)
    x = x_ref[0]
    shift, scale, gate = mod_ref[0, 0:1, :], mod_ref[0, 1:2, :], mod_ref[0, 2:3, :]
    hb = _rmsnorm_mod(x, gmix_ref[...], scale, shift).astype(BF16)

    def proj(k):
        return _dot(hb, win_ref[:, k * CONV_CH:(k + 1) * CONV_CH])

    u = proj(0) * jax.nn.sigmoid(proj(1))
    ubuf[HIST_ROWS:HIST_ROWS + tile, :] = u
    mlp(MLP_STAGES_AFTER["glu"])
    off = HIST_ROWS - (CONV_K - 1)
    for r in range(SUBLANES):
        n = tile + SUBLANES * ((CONV_K - 1 - r) // SUBLANES)
        ushift[r, 0:n, :] = ubuf[off + r:off + r + n, :]
        mlp(MLP_STAGES_AFTER["shift"].get(r, 0))
    for rb in range(tile // CONV_ROW_BLOCK):
        r0 = rb * CONV_ROW_BLOCK
        dw = jnp.broadcast_to(convb_ref[...], (CONV_ROW_BLOCK, CONV_CH))
        for k in range(CONV_K):
            s0 = r0 + SUBLANES * (k // SUBLANES)
            wk = jnp.concatenate([convw_ref[k]] * (CONV_ROW_BLOCK // SUBLANES), axis=0)
            dw = dw + wk * ushift[k % SUBLANES, s0:s0 + CONV_ROW_BLOCK, :]
        conv_out = _silu(_layernorm(dw, clng_ref[...], clnb_ref[...]))
        mixbuf[r0:r0 + CONV_ROW_BLOCK, 0:CONV_CH] = conv_out.astype(BF16)
        mlp(MLP_STAGES_AFTER["conv"].get(rb, 0))
    ubuf[0:HIST_ROWS, :] = ubuf[tile:tile + HIST_ROWS, :]

    q = proj(2)
    kk = proj(3)
    v = proj(4)
    gt = proj(5)
    mlp(MLP_STAGES_AFTER["qkvg"])
    lane = lax.broadcasted_iota(jnp.int32, (1, PAIR_LANES), 1)
    lo = lane < RET_DK
    first_half = (lane % RET_DK) < (RET_DK // 2)
    lo_b = lo.astype(BF16)
    hi_b = (~lo).astype(BF16)
    pairs = range(N_PAIRS)
    lanes = [slice(p * PAIR_LANES, (p + 1) * PAIR_LANES) for p in pairs]
    for c in range(tile // RET_CHUNK):
        rs = slice(c * RET_CHUNK, (c + 1) * RET_CHUNK)
        cos = cos_ref[rs, :]
        sin = sin_ref[rs, :]
        qc = [_rotary_pair(q[rs, sl], cos, sin, first_half) for sl in lanes]
        kc = [_rotary_pair(kk[rs, sl], cos, sin, first_half) for sl in lanes]
        vc = [v[rs, sl].astype(BF16) for sl in lanes]
        scores = []
        for p in pairs:
            kb = kc[p].astype(BF16)
            k_cat = jnp.concatenate([kb * lo_b, kb * hi_b], axis=0)
            scores.append(_dot_nt(qc[p].astype(BF16), k_cat))
        mlp(MLP_STAGES_AFTER["ret"].get((c, 0), 0))
        outs = []
        for p in pairs:
            state = sbuf[p]
            lhs = jnp.concatenate([(scores[p] * dec_ref[p]).astype(BF16),
                                   (qc[p] * qdec_ref[:, lanes[p]]).astype(BF16)], axis=1)
            rhs = jnp.concatenate([vc[p] * lo_b, vc[p] * hi_b, state.astype(BF16)], axis=0)
            outs.append(_dot(lhs, rhs))
            kd = (kc[p] * kdec_ref[:, lanes[p]]).astype(BF16)
            sbuf[p] = state * sdec_ref[p] + jnp.where(sdec_ref[p] > 0.0, _dot_tn(kd, vc[p]), 0.0)
        mlp(MLP_STAGES_AFTER["ret"].get((c, 1), 0))
        for p in pairs:
            on = _pair_groupnorm(outs[p], lo, rlng_ref[:, lanes[p]], rlnb_ref[:, lanes[p]])
            mixbuf[rs, CONV_CH + p * PAIR_LANES:CONV_CH + (p + 1) * PAIR_LANES] = (
                on * _silu(gt[rs, lanes[p]])).astype(BF16)
        mlp(MLP_STAGES_AFTER["ret"].get((c, 2), 0))
    for _ in ffn:
        pass

    mix = _dot(mixbuf[...], wout_ref[...])
    x1buf[...] = x + gate * mix

    @pl.when(jnp.logical_and(j == nj - 1, live))
    def _():
        hist_ref[0] = ubuf[0:HIST_ROWS, :]
        state_ref[0] = sbuf[...]


def _layer_prompt(x, mod, g_mix, w_in, conv_w, conv_b, cln_g, cln_b, rln_g, rln_b, w_out,
                  g_ffn, w1, w2, g_final, rope, decay, tile):
    bsz, seq, d = x.shape
    nj = seq // tile
    total = bsz * nj
    cos, sin = rope
    dec, qdec, kdec, sdec = decay
    mix_tile = lambda g: jnp.minimum(g, total - 1)
    ffn_tile = lambda g: jnp.maximum(g - 1, 0)
    resident = lambda shape: pl.BlockSpec(shape, lambda g: (0,) * len(shape),
                                          pipeline_mode=pl.Buffered(1))
    row = lambda a: a.reshape(1, -1)
    conv_w_rows = jnp.broadcast_to(conv_w[:, None, :], (CONV_K, SUBLANES, CONV_CH))
    rope_spec = pl.BlockSpec((tile, PAIR_LANES), lambda g: (mix_tile(g) % nj, 0))
    return pl.pallas_call(
        functools.partial(_layer_prompt_kernel, tile, nj),
        grid=(total + 1,),
        in_specs=[
            pl.BlockSpec((1, tile, d), lambda g: (mix_tile(g) // nj, mix_tile(g) % nj, 0)),
            pl.BlockSpec((1, 3, d), lambda g: (mix_tile(g) // nj, 0, 0)),
            resident((1, d)),
            resident(w_in.shape),
            resident((CONV_K, SUBLANES, CONV_CH)),
            resident((1, CONV_CH)),
            resident((1, CONV_CH)),
            resident((1, CONV_CH)),
            resident((1, RET_W)),
            resident((1, RET_W)),
            resident(w_out.shape),
            rope_spec,
            rope_spec,
            resident(dec.shape),
            resident(qdec.shape),
            resident(kdec.shape),
            resident(sdec.shape),
            pl.BlockSpec((1, 3, 1, d), lambda g: (ffn_tile(g) // nj, 0, 0, 0)),
            resident((1, d)),
            resident(w1.shape),
            resident(w2.shape),
            resident((1, d)),
        ],
        out_specs=[
            pl.BlockSpec((1, tile, d), lambda g: (ffn_tile(g) // nj, ffn_tile(g) % nj, 0)),
            pl.BlockSpec((1, HIST_ROWS, CONV_CH), lambda g: (mix_tile(g) // nj, 0, 0)),
            pl.BlockSpec((1, N_PAIRS, PAIR_LANES, PAIR_LANES),
                         lambda g: (mix_tile(g) // nj, 0, 0, 0)),
        ],
        out_shape=(
            jax.ShapeDtypeStruct((bsz, seq, d), F32),
            jax.ShapeDtypeStruct((bsz, HIST_ROWS, CONV_CH), F32),
            jax.ShapeDtypeStruct((bsz, N_PAIRS, PAIR_LANES, PAIR_LANES), F32),
        ),
        scratch_shapes=[
            pltpu.VMEM((HIST_ROWS + tile, CONV_CH), F32),
            pltpu.VMEM((SUBLANES, HIST_ROWS + tile, CONV_CH), F32),
            pltpu.VMEM((N_PAIRS, PAIR_LANES, PAIR_LANES), F32),
            pltpu.VMEM((tile, D_MODEL), BF16),
            pltpu.VMEM((tile, D_MODEL), F32),
        ],
        compiler_params=pltpu.CompilerParams(
            dimension_semantics=("arbitrary",), vmem_limit_bytes=VMEM_LIMIT),
        name="layer_prompt",
    )(x, mod[:, 0:3], row(g_mix), w_in, conv_w_rows, row(conv_b), row(cln_g), row(cln_b),
      row(rln_g), row(rln_b), w_out, cos, sin, dec, qdec, kdec, sdec,
      mod[:, 3:6].reshape(bsz, 3, 1, d), row(g_ffn), w1, w2, row(g_final))


def _ffn_stages(x, mod_ref, gffn_ref, w1_ref, w2_ref, gfin_ref, y_ref):
    shift, scale, gate = mod_ref[0, 0], mod_ref[0, 1], mod_ref[0, 2]
    hb = _rmsnorm_mod(x, gffn_ref[...], scale, shift).astype(BF16)
    yield
    ff = None
    for c in range(D_FF // (2 * FFN_CHUNK)):
        acts = []
        for h in range(2):
            c0 = (2 * c + h) * FFN_CHUNK
            a = jnp.maximum(_dot(hb, w1_ref[:, c0:c0 + FFN_CHUNK]), 0.0)
            acts.append((a * a).astype(BF16))
            yield
        part = _dot(jnp.concatenate(acts, axis=1),
                    w2_ref[2 * c * FFN_CHUNK:2 * (c + 1) * FFN_CHUNK, :])
        ff = part if ff is None else ff + part
        yield
    z = x + gate * ff
    y_ref[0] = z * lax.rsqrt(jnp.mean(z * z, axis=-1, keepdims=True) + EPS) * gfin_ref[...]


def _mlp_sample_kernel(n_pos, n_seq, x_ref, conv_ref, ret_ref, gate1_ref, wout_ref, mod_ref,
                       gffn_ref, w1_ref, w2_ref, gfin_ref, y_ref, x1buf, hbuf, acc):
    c = pl.program_id(0)
    rows = lambda i: slice(i * n_seq, (i + 1) * n_seq)

    @pl.when(c == 0)
    def _():
        mix = (_dot(conv_ref[...], wout_ref[0:CONV_CH, :])
               + _dot(ret_ref[...], wout_ref[CONV_CH:CONV_CH + RET_W, :]))
        for i in range(n_pos):
            x1 = x_ref[i] + gate1_ref[...] * mix[rows(i)]
            x1buf[rows(i), :] = x1
            hbuf[rows(i), :] = _rmsnorm_mod(x1, gffn_ref[...], mod_ref[1], mod_ref[0]).astype(BF16)

    a = jnp.maximum(_dot(hbuf[...], w1_ref[...]), 0.0)
    part = _dot((a * a).astype(BF16), w2_ref[...])

    @pl.when(c == 0)
    def _():
        acc[...] = part

    @pl.when(c > 0)
    def _():
        acc[...] += part

    @pl.when(c == pl.num_programs(0) - 1)
    def _():
        for i in range(n_pos):
            z = x1buf[rows(i), :] + mod_ref[2] * acc[rows(i), :]
            y_ref[i] = (z * lax.rsqrt(jnp.mean(z * z, axis=-1, keepdims=True) + EPS)
                        * gfin_ref[...])


def _mlp_sample(xs, conv_out, ret_out, gate1, w_out, mod2, g_ffn, w1, w2, g_final):
    n_pos, n_seq, d = xs.shape
    n_tok = n_pos * n_seq
    n_chunks = D_FF // SAMPLE_FFN_CHUNK
    whole = lambda shape: pl.BlockSpec(shape, lambda c: (0,) * len(shape),
                                       pipeline_mode=pl.Buffered(1))
    return pl.pallas_call(
        functools.partial(_mlp_sample_kernel, n_pos, n_seq),
        grid=(n_chunks,),
        in_specs=[whole(xs.shape), whole(conv_out.shape), whole(ret_out.shape),
                  whole(gate1.shape), whole(w_out.shape), whole(mod2.shape), whole((1, d)),
                  pl.BlockSpec((d, SAMPLE_FFN_CHUNK), lambda c: (0, c)),
                  pl.BlockSpec((SAMPLE_FFN_CHUNK, d), lambda c: (c, 0)),
                  whole((1, d))],
        out_specs=pl.BlockSpec(xs.shape, lambda c: (0, 0, 0)),
        out_shape=jax.ShapeDtypeStruct(xs.shape, F32),
        scratch_shapes=[pltpu.VMEM((n_tok, d), F32), pltpu.VMEM((n_tok, d), BF16),
                        pltpu.VMEM((n_tok, d), F32)],
        compiler_params=pltpu.CompilerParams(
            dimension_semantics=("arbitrary",), vmem_limit_bytes=VMEM_LIMIT),
        name="mlp_sample",
    )(xs, conv_out, ret_out, gate1, w_out, mod2, g_ffn.reshape(1, d), w1, w2,
      g_final.reshape(1, d))


def _power_table_kernel(o_ref):
    lane_h = (lax.broadcasted_iota(jnp.int32, o_ref.shape, 1) // RET_DK).astype(F32)
    n = lax.broadcasted_iota(jnp.int32, o_ref.shape, 0).astype(F32)
    o_ref[...] = jnp.exp(_log_decay(lane_h) * n)


def _power_table():
    return pl.pallas_call(_power_table_kernel,
                          out_shape=jax.ShapeDtypeStruct((8, RET_W), F32),
                          name="decay_powers")()


def _proj_sample_kernel(n_pos, n_seq, x_ref, mod_ref, gmix_ref, win_ref, cache_ref, convw_ref,
                        convb_ref, clng_ref, clnb_ref, cos_ref, sin_ref,
                        u_ref, conv_ref, q_ref, k_ref, v_ref, g_ref):
    shift, scale = mod_ref[0], mod_ref[1]
    hb = jnp.concatenate(
        [_rmsnorm_mod(x_ref[i], gmix_ref[...], scale, shift).astype(BF16) for i in range(n_pos)],
        axis=0)

    def proj(k):
        return _dot(hb, win_ref[:, k * CONV_CH:(k + 1) * CONV_CH])

    u = proj(0) * jax.nn.sigmoid(proj(1))
    n_hist = CONV_K - 1
    rows = lambda a, i: a[i * n_seq:(i + 1) * n_seq]
    outs = []
    for i in range(n_pos):
        u_ref[i] = rows(u, i)
        dw = jnp.broadcast_to(convb_ref[...], (n_seq, CONV_CH))
        for k in range(CONV_K):
            m = i + k
            src = cache_ref[m] if m < n_hist else rows(u, m - n_hist)
            dw = dw + convw_ref[k:k + 1, :] * src
        outs.append(_silu(_layernorm(dw, clng_ref[...], clnb_ref[...])).astype(BF16))
    conv_ref[...] = jnp.concatenate(outs, axis=0)

    lane = lax.broadcasted_iota(jnp.int32, (1, PAIR_LANES), 1)
    first_half = (lane % RET_DK) < (RET_DK // 2)
    q = proj(2)
    kk = proj(3)
    for i in range(n_pos):
        rs = slice(i * n_seq, (i + 1) * n_seq)
        cos = cos_ref[i:i + 1, :]
        sin = sin_ref[i:i + 1, :]
        for p in range(N_PAIRS):
            sl = slice(p * PAIR_LANES, (p + 1) * PAIR_LANES)
            q_ref[rs, sl] = _rotary_pair(q[rs, sl], cos, sin, first_half)
            k_ref[rs, sl] = _rotary_pair(kk[rs, sl], cos, sin, first_half)
    v_ref[...] = proj(4)
    g_ref[...] = proj(5)


def _proj_sample(xs, mod1, g_mix, w_in, cache_t, conv_w, conv_b, cln_g, cln_b, rope):
    n_pos, n_seq, d = xs.shape
    n_tok = n_pos * n_seq
    cos, sin = rope
    row = lambda a: a.reshape(1, -1)
    tok = lambda dt: jax.ShapeDtypeStruct((n_tok, RET_W), dt)
    return pl.pallas_call(
        functools.partial(_proj_sample_kernel, n_pos, n_seq),
        out_shape=(jax.ShapeDtypeStruct((n_pos, n_seq, CONV_CH), F32), tok(BF16),
                   tok(F32), tok(F32), tok(F32), tok(F32)),
        compiler_params=pltpu.CompilerParams(vmem_limit_bytes=VMEM_LIMIT),
        name="proj_sample",
    )(xs, mod1, row(g_mix), w_in, cache_t, conv_w, row(conv_b), row(cln_g), row(cln_b), cos, sin)


RET_LANE_BLOCK = 2048


def _ret_sample_kernel(n_pos, n_seq, q_ref, k_ref, v_ref, g_ref, s_ref, pw_ref, rlng_ref,
                       rlnb_ref, out_ref, snew_ref):
    lane = lax.broadcasted_iota(jnp.int32, (1, PAIR_LANES), 1)
    lo = lane < RET_DK
    rows = lambda a, i: a[i * n_seq:(i + 1) * n_seq]
    pw = pw_ref[...]
    q = q_ref[...]
    k = k_ref[...]
    qb = [rows(q, i).astype(BF16) for i in range(n_pos)]
    kb = [rows(k, i).astype(BF16) for i in range(n_pos)]
    vb = [rows(v_ref[...], i).astype(BF16) for i in range(n_pos)]

    def head_bcast(x, a):
        other = pltpu.roll(x, RET_DK, 1)
        return jnp.where(lo, x, other) if a == 0 else jnp.where(lo, other, x)

    o = []
    for i in range(n_pos):
        acc = jnp.zeros((n_seq, PAIR_LANES), F32)
        for j in range(i + 1):
            prod = qb[i].astype(F32) * kb[j].astype(F32)
            s_lo = jnp.sum(jnp.where(lo, prod, 0.0), axis=-1, keepdims=True)
            s_hi = jnp.sum(jnp.where(lo, 0.0, prod), axis=-1, keepdims=True)
            score = jnp.where(lo, s_lo, s_hi) * pw[i - j:i - j + 1, :]
            acc = acc + score.astype(BF16).astype(F32) * vb[j].astype(F32)
        o.append(acc)

    qd = jnp.concatenate([(rows(q, i) * pw[i + 1:i + 2, :]).astype(BF16) for i in range(n_pos)], axis=0)
    kd = jnp.concatenate([(rows(k, j) * pw[n_pos - 1 - j:n_pos - j, :]).astype(BF16)
                          for j in range(n_pos)], axis=0)
    qk = jnp.concatenate([qd, kd], axis=0)
    head_lanes = RET_DK * RET_DV
    cross = [[jnp.zeros((n_seq, PAIR_LANES), F32) for _ in range(2)] for _ in range(n_pos)]
    for blk in range(2 * head_lanes // RET_LANE_BLOCK):
        c0 = blk * RET_LANE_BLOCK
        a = c0 // head_lanes
        col = lax.broadcasted_iota(jnp.int32, (PAIR_LANES, RET_LANE_BLOCK), 1) + c0
        r = lax.broadcasted_iota(jnp.int32, (PAIR_LANES, RET_LANE_BLOCK), 0)
        expand = jnp.where(col // RET_DV == r, 1.0, 0.0).astype(BF16)
        big = _dot(qk, expand)
        s_old = s_ref[:, c0:c0 + RET_LANE_BLOCK]
        s_b = s_old.astype(BF16).astype(F32)
        gam = pw[n_pos:n_pos + 1, a * RET_DK:a * RET_DK + 1]
        s_new = s_old * gam
        reps = RET_LANE_BLOCK // PAIR_LANES
        for i in range(n_pos):
            prod = rows(big, i) * s_b
            t = prod[:, 0:PAIR_LANES]
            for e in range(1, reps):
                t = t + prod[:, e * PAIR_LANES:(e + 1) * PAIR_LANES]
            cross[i][a] = cross[i][a] + t
            vrep = jnp.tile(head_bcast(vb[i].astype(F32), a), (1, reps))
            s_new = s_new + rows(big, n_pos + i) * vrep
        snew_ref[:, c0:c0 + RET_LANE_BLOCK] = s_new

    for i in range(n_pos):
        fold = [c + pltpu.roll(c, RET_DK, 1) for c in cross[i]]
        oi = o[i] + jnp.where(lo, fold[0], fold[1])
        on = _pair_groupnorm(oi, lo, rlng_ref[...], rlnb_ref[...])
        out_ref[i * n_seq:(i + 1) * n_seq, :] = (on * _silu(rows(g_ref[...], i))).astype(BF16)


def _ret_sample(q, k, v, g, state, pw, rln_g, rln_b, n_pos, n_seq):
    n_tok = n_pos * n_seq
    pair_state = 2 * RET_DK * RET_DV
    slab = pl.BlockSpec((n_tok, PAIR_LANES), lambda p: (0, p))
    return pl.pallas_call(
        functools.partial(_ret_sample_kernel, n_pos, n_seq),
        grid=(N_PAIRS,),
        in_specs=[slab, slab, slab, slab,
                  pl.BlockSpec((n_seq, pair_state), lambda p: (0, p)),
                  pl.BlockSpec((8, PAIR_LANES), lambda p: (0, p)),
                  pl.BlockSpec((1, PAIR_LANES), lambda p: (0, p)),
                  pl.BlockSpec((1, PAIR_LANES), lambda p: (0, p))],
        out_specs=[slab, pl.BlockSpec((n_seq, pair_state), lambda p: (0, p))],
        out_shape=(jax.ShapeDtypeStruct((n_tok, RET_W), BF16),
                   jax.ShapeDtypeStruct(state.shape, F32)),
        compiler_params=pltpu.CompilerParams(
            dimension_semantics=("arbitrary",), vmem_limit_bytes=VMEM_LIMIT),
        name="ret_sample",
    )(q, k, v, g, state, pw, rln_g.reshape(1, RET_W), rln_b.reshape(1, RET_W))


def _unpack_state(s2):
    blocks = []
    for p in range(N_PAIRS):
        for a in range(2):
            sl = slice(a * RET_DK, (a + 1) * RET_DK)
            blocks.append(s2[:, p, sl, sl])
    return jnp.stack(blocks, axis=1)


def _cache_update_kernel(n_new, cache_ref, u_ref, o_ref):
    n_hist = cache_ref.shape[1]
    o_ref[:, 0:n_hist - n_new, :] = cache_ref[:, n_new:n_hist, :]
    o_ref[:, n_hist - n_new:n_hist, :] = u_ref[...]


def _cache_update(cache, u_new):
    n_seq, n_hist, ch = cache.shape
    n_new = u_new.shape[1]
    blk = 16
    return pl.pallas_call(
        functools.partial(_cache_update_kernel, n_new),
        grid=(n_seq // blk,),
        in_specs=[pl.BlockSpec((blk, n_hist, ch), lambda i: (i, 0, 0)),
                  pl.BlockSpec((blk, n_new, ch), lambda i: (i, 0, 0))],
        out_specs=pl.BlockSpec((blk, n_hist, ch), lambda i: (i, 0, 0)),
        out_shape=jax.ShapeDtypeStruct(cache.shape, cache.dtype),
        compiler_params=pltpu.CompilerParams(dimension_semantics=("arbitrary",)),
        name="cache_update",
    )(cache, u_new)


def _cast_weights(w_in, w_out, w_ff1, w_ff2):
    col_scale = jnp.ones((w_in.shape[1],), F32).at[2 * CONV_CH:2 * CONV_CH + RET_W].set(RET_DK ** -0.5)
    return ((w_in * col_scale).astype(BF16), w_out.astype(BF16), w_ff1.astype(BF16),
            w_ff2.astype(BF16))


def _prompt_path(x, mod, g_mix, w_in, conv_w, conv_b, cln_g, cln_b, rln_g, rln_b,
                 w_out, g_ffn, w_ff1, w_ff2, g_final, tile=PROMPT_TILE):
    seq = x.shape[1]
    rope = _rope_tables(seq, 0)
    decay = _decay_tables(RET_CHUNK)
    y, hist, s2 = _layer_prompt(x, mod, g_mix, w_in, conv_w, conv_b, cln_g, cln_b, rln_g, rln_b,
                                w_out, g_ffn, w_ff1, w_ff2, g_final, rope, decay, tile)
    return y, hist[:, HIST_ROWS - (CONV_K - 1):], _unpack_state(s2)


def _sample_path(x, mod, cache, state, pos0, g_mix, w_in, conv_w, conv_b, cln_g, cln_b,
                 rln_g, rln_b, w_out, g_ffn, w_ff1, w_ff2, g_final):
    n_seq, n_pos, d = x.shape
    xs = x.transpose(1, 0, 2)
    modt = mod.transpose(1, 0, 2)
    rope = _rope_tables(8, pos0)
    u, conv_out, q, k, v, g = _proj_sample(xs, modt[0:2], g_mix, w_in, cache.transpose(1, 0, 2),
                                           conv_w, conv_b, cln_g, cln_b, rope)
    ret_out, s_new = _ret_sample(q, k, v, g, state.reshape(n_seq, -1), _power_table(),
                                 rln_g, rln_b, n_pos, n_seq)
    y = _mlp_sample(xs, conv_out, ret_out, modt[2], w_out, modt[3:6], g_ffn, w_ff1, w_ff2,
                    g_final)
    new_cache = _cache_update(cache, u.transpose(1, 0, 2))
    return y.transpose(1, 0, 2), new_cache, s_new.reshape(state.shape)


def kernel(x_prompt, x_sample, cache_conv, state_ret, c_prompt, c_sample, w_ada, b_ada, g_mix, w_in, conv_w, conv_b, conv_ln_g, conv_ln_b, ret_ln_g, ret_ln_b, w_out, g_ffn, w_ff1, w_ff2, g_final):
    bp = x_prompt.shape[0]
    d = x_prompt.shape[-1]
    w_in_b, w_out_b, w_ff1_b, w_ff2_b = _cast_weights(w_in[0], w_out[0], w_ff1[0], w_ff2[0])
    mod = _modulation(jnp.concatenate([c_prompt, c_sample], axis=0), w_ada[0], b_ada[0])
    mod = mod.reshape(-1, N_MOD, d)
    params = (g_mix[0], w_in_b, conv_w[0], conv_b[0], conv_ln_g[0], conv_ln_b[0], ret_ln_g[0],
              ret_ln_b[0], w_out_b, g_ffn[0], w_ff1_b, w_ff2_b, g_final)
    y_p, conv_p, ret_p = _prompt_path(x_prompt, mod[:bp], *params)
    y_s, conv_s, ret_s = _sample_path(x_sample, mod[bp:], cache_conv[0], state_ret[0], PAST_LEN,
                                      *params)
    return (y_p, y_s, conv_p[None], ret_p[None], conv_s[None], ret_s[None])
```

```python
import functools

import jax
import jax.numpy as jnp
from jax import lax
from jax.experimental import pallas as pl
from jax.experimental.pallas import tpu as pltpu

D_MODEL = 1024
CONV_CH = 512
CONV_K = 31
RET_HEADS = 8
RET_W = 512
RET_DK = 64
RET_DV = 64
D_FF = 4 * D_MODEL
RET_CHUNK = 128
ROPE_THETA = 10000.0
EPS = 1e-6
N_MOD = 6
PAST_LEN = 16384

SUBLANES = 8
HIST_ROWS = 32
CONV_ROW_BLOCK = 32
PROMPT_TILE = 256
FFN_CHUNK = 512
SAMPLE_FFN_CHUNK = 1024
MLP_STAGES_AFTER = {
    "glu": 0,
    "shift": {},
    "conv": {0: 1, 1: 1, 2: 1, 3: 1, 4: 1, 5: 1, 6: 1, 7: 1},
    "qkvg": 0,
    "ret": {(0, 0, 1): 1, (0, 0, 3): 1, (0, 1, 1): 1, (0, 1, 3): 1,
            (1, 0, 1): 1, (1, 0, 3): 1, (1, 1, 1): 1, (1, 1, 3): 1},
}
PAIR_LANES = 2 * RET_DK
N_PAIRS = RET_HEADS // 2
VMEM_LIMIT = 56 * 1024 * 1024

F32 = jnp.float32
BF16 = jnp.bfloat16


def _log_decay(head):
    return jnp.log1p(-jnp.exp2(-5.0 - head))


def _rope_kernel(pos0, cos_ref, sin_ref):
    rows = cos_ref.shape[0]
    lane = lax.broadcasted_iota(jnp.int32, (rows, PAIR_LANES), 1)
    row = lax.broadcasted_iota(jnp.int32, (rows, PAIR_LANES), 0)
    half = RET_DK // 2
    freq = (lane % half).astype(F32)
    inv = jnp.power(jnp.full_like(freq, ROPE_THETA), -freq / half)
    ang = (row + pos0).astype(F32) * inv
    first = (lane % RET_DK) < half
    cos_ref[...] = jnp.cos(ang)
    sin_ref[...] = jnp.where(first, -jnp.sin(ang), jnp.sin(ang))


ROPE_BLOCK = 16


def _rope_blocked_kernel(pos0, cos_ref, sin_ref):
    rows = cos_ref.shape[0]
    n_blk = rows // ROPE_BLOCK
    half = RET_DK // 2

    def angles(n, step, start):
        lane = lax.broadcasted_iota(jnp.int32, (n, PAIR_LANES), 1)
        row = lax.broadcasted_iota(jnp.int32, (n, PAIR_LANES), 0)
        freq = (lane % half).astype(F32)
        inv = jnp.power(jnp.full_like(freq, ROPE_THETA), -freq / half)
        return (row * step + start).astype(F32) * inv

    a = angles(n_blk, ROPE_BLOCK, pos0)
    b = angles(ROPE_BLOCK, 1, 0)
    lane = lax.broadcasted_iota(jnp.int32, (ROPE_BLOCK, PAIR_LANES), 1)
    sign = jnp.where((lane % RET_DK) < half, -1.0, 1.0)
    cos_a, sin_a = jnp.cos(a), jnp.sin(a)
    cos_b, sin_b = jnp.cos(b), jnp.sin(b)
    cos_bs, sin_bs = cos_b * sign, sin_b * sign
    for blk in range(n_blk):
        ca, sa = cos_a[blk:blk + 1, :], sin_a[blk:blk + 1, :]
        rs = slice(blk * ROPE_BLOCK, (blk + 1) * ROPE_BLOCK)
        cos_ref[rs, :] = ca * cos_b - sa * sin_b
        sin_ref[rs, :] = sa * cos_bs + ca * sin_bs


def _rope_tables(rows, pos0):
    blocked = rows % ROPE_BLOCK == 0 and rows // ROPE_BLOCK >= SUBLANES
    return pl.pallas_call(
        functools.partial(_rope_blocked_kernel if blocked else _rope_kernel, pos0),
        out_shape=(jax.ShapeDtypeStruct((rows, PAIR_LANES), F32),
                   jax.ShapeDtypeStruct((rows, PAIR_LANES), F32)),
        name="rope_tables",
    )()


def _decay_kernel(chunk, dec_ref, qdec_ref, kdec_ref, sdec_ref):
    c = chunk
    shape = (N_PAIRS, c, 2 * c)
    col = lax.broadcasted_iota(jnp.int32, shape, 2)
    h = (2 * lax.broadcasted_iota(jnp.int32, shape, 0) + col // c).astype(F32)
    i = lax.broadcasted_iota(jnp.int32, shape, 1).astype(F32)
    j = (col % c).astype(F32)
    diff = i - j
    dec_ref[...] = jnp.where(diff >= 0, jnp.exp(_log_decay(h) * jnp.maximum(diff, 0.0)), 0.0)
    lane_h = (lax.broadcasted_iota(jnp.int32, (c, RET_W), 1) // RET_DK).astype(F32)
    idx = lax.broadcasted_iota(jnp.int32, (c, RET_W), 0).astype(F32)
    lg = _log_decay(lane_h)
    qdec_ref[...] = jnp.exp(lg * (idx + 1.0))
    kdec_ref[...] = jnp.exp(lg * (c - 1.0 - idx))
    p = lax.broadcasted_iota(jnp.int32, (N_PAIRS, PAIR_LANES, PAIR_LANES), 0)
    r = lax.broadcasted_iota(jnp.int32, (N_PAIRS, PAIR_LANES, PAIR_LANES), 1) // RET_DK
    q = lax.broadcasted_iota(jnp.int32, (N_PAIRS, PAIR_LANES, PAIR_LANES), 2) // RET_DK
    hh = (2 * p + r).astype(F32)
    sdec_ref[...] = jnp.where(r == q, jnp.exp(_log_decay(hh) * float(c)), 0.0)


def _decay_tables(chunk):
    return pl.pallas_call(
        functools.partial(_decay_kernel, chunk),
        out_shape=(jax.ShapeDtypeStruct((N_PAIRS, chunk, 2 * chunk), F32),
                   jax.ShapeDtypeStruct((chunk, RET_W), F32),
                   jax.ShapeDtypeStruct((chunk, RET_W), F32),
                   jax.ShapeDtypeStruct((N_PAIRS, PAIR_LANES, PAIR_LANES), F32)),
        name="decay_tables",
    )()


def _mod_kernel(c_ref, w_ref, b_ref, o_ref):
    c = c_ref[...]
    s = c * jax.nn.sigmoid(c)
    o_ref[...] = jnp.dot(s.astype(BF16), w_ref[...].astype(BF16),
                         preferred_element_type=F32) + b_ref[...]


def _modulation(c, w_ada, b_ada):
    n, d = c.shape
    width = w_ada.shape[1]
    blk = D_MODEL
    return pl.pallas_call(
        _mod_kernel,
        grid=(width // blk,),
        in_specs=[pl.BlockSpec((n, d), lambda i: (0, 0)),
                  pl.BlockSpec((d, blk), lambda i: (0, i)),
                  pl.BlockSpec((1, blk), lambda i: (0, i))],
        out_specs=pl.BlockSpec((n, blk), lambda i: (0, i)),
        out_shape=jax.ShapeDtypeStruct((n, width), F32),
        compiler_params=pltpu.CompilerParams(vmem_limit_bytes=VMEM_LIMIT),
        name="adaln_mod",
    )(c, w_ada, b_ada.reshape(1, width))


def _rmsnorm_mod(x, g, scale, shift):
    y = x * lax.rsqrt(jnp.mean(x * x, axis=-1, keepdims=True) + EPS)
    return (y * g) * (1.0 + scale) + shift


def _layernorm(x, g, b):
    mu = jnp.mean(x, axis=-1, keepdims=True)
    d = x - mu
    var = jnp.mean(d * d, axis=-1, keepdims=True)
    return d * lax.rsqrt(var + EPS) * g + b


def _silu(x):
    return x * jax.nn.sigmoid(x)


def _rotary_pair(x, cos, sin_signed, first_half):
    partner = jnp.where(first_half, pltpu.roll(x, PAIR_LANES - RET_DK // 2, 1),
                        pltpu.roll(x, RET_DK // 2, 1))
    return x * cos + partner * sin_signed


def _pair_groupnorm(o, lo, g, b):
    inv_n = 1.0 / RET_DV
    s_lo = jnp.sum(jnp.where(lo, o, 0.0), axis=-1, keepdims=True)
    s_hi = jnp.sum(jnp.where(lo, 0.0, o), axis=-1, keepdims=True)
    d = o - jnp.where(lo, s_lo, s_hi) * inv_n
    d2 = d * d
    v_lo = jnp.sum(jnp.where(lo, d2, 0.0), axis=-1, keepdims=True)
    v_hi = jnp.sum(jnp.where(lo, 0.0, d2), axis=-1, keepdims=True)
    var = jnp.where(lo, v_lo, v_hi) * inv_n
    return d * lax.rsqrt(var + EPS) * g + b


def _dot(a, b):
    return jnp.dot(a, b, preferred_element_type=F32)


def _dot_nt(a, b):
    return lax.dot_general(a, b, (((1,), (1,)), ((), ())), preferred_element_type=F32)


def _dot_tn(a, b):
    return lax.dot_general(a, b, (((0,), (0,)), ((), ())), preferred_element_type=F32)


def _layer_prompt_kernel(tile, nj, x_ref, mod_ref, gmix_ref, win_ref, convw_ref, convb_ref,
                         clng_ref, clnb_ref, rlng_ref, rlnb_ref, wout_ref,
                         cos_ref, sin_ref, dec_ref, qdec_ref, kdec_ref, sdec_ref,
                         mod2_ref, gffn_ref, w1_ref, w2_ref, gfin_ref,
                         y_ref, hist_ref, state_ref,
                         ubuf, ushift, sbuf, mixbuf, x1buf):
    g = pl.program_id(0)
    total = pl.num_programs(0) - 1
    j = lax.rem(jnp.minimum(g, total - 1), nj)
    live = g < total

    @pl.when(g == 0)
    def _():
        x1buf[...] = jnp.zeros((tile, D_MODEL), F32)

    @pl.when(jnp.logical_and(j == 0, live))
    def _():
        ubuf[0:HIST_ROWS, :] = jnp.zeros((HIST_ROWS, CONV_CH), F32)
        sbuf[...] = jnp.zeros(sbuf.shape, F32)

    ffn = _ffn_stages(x1buf[...], mod2_ref, gffn_ref, w1_ref, w2_ref, gfin_ref, y_ref)

    def mlp(n):
        for _ in range(n):
            next(ffn, None)

    mlp(1)
    x = x_ref[0]
    shift, scale, gate = mod_ref[0, 0:1, :], mod_ref[0, 1:2, :], mod_ref[0, 2:3, :]
    hb = _rmsnorm_mod(x, gmix_ref[...], scale, shift).astype(BF16)

    def proj(k):
        return _dot(hb, win_ref[:, k * CONV_CH:(k + 1) * CONV_CH])

    u = proj(0) * jax.nn.sigmoid(proj(1))
    ubuf[HIST_ROWS:HIST_ROWS + tile, :] = u
    mlp(MLP_STAGES_AFTER["glu"])
    off = HIST_ROWS - (CONV_K - 1)
    for r in range(SUBLANES):
        n = tile + SUBLANES * ((CONV_K - 1 - r) // SUBLANES)
        ushift[r, 0:n, :] = ubuf[off + r:off + r + n, :]
        mlp(MLP_STAGES_AFTER["shift"].get(r, 0))
    for rb in range(tile // CONV_ROW_BLOCK):
        r0 = rb * CONV_ROW_BLOCK
        dw = jnp.broadcast_to(convb_ref[...], (CONV_ROW_BLOCK, CONV_CH))
        for k in range(CONV_K):
            s0 = r0 + SUBLANES * (k // SUBLANES)
            wk = jnp.concatenate([convw_ref[k]] * (CONV_ROW_BLOCK // SUBLANES), axis=0)
            dw = dw + wk * ushift[k % SUBLANES, s0:s0 + CONV_ROW_BLOCK, :]
        conv_out = _silu(_layernorm(dw, clng_ref[...], clnb_ref[...]))
        mixbuf[r0:r0 + CONV_ROW_BLOCK, 0:CONV_CH] = conv_out.astype(BF16)
        mlp(MLP_STAGES_AFTER["conv"].get(rb, 0))
    ubuf[0:HIST_ROWS, :] = ubuf[tile:tile + HIST_ROWS, :]

    q = proj(2)
    kk = proj(3)
    v = proj(4)
    gt = proj(5)
    mlp(MLP_STAGES_AFTER["qkvg"])
    lane = lax.broadcasted_iota(jnp.int32, (1, PAIR_LANES), 1)
    lo = lane < RET_DK
    first_half = (lane % RET_DK) < (RET_DK // 2)
    lo_b = lo.astype(BF16)
    hi_b = (~lo).astype(BF16)
    pairs = range(N_PAIRS)
    lanes = [slice(p * PAIR_LANES, (p + 1) * PAIR_LANES) for p in pairs]
    for c in range(tile // RET_CHUNK):
        rs = slice(c * RET_CHUNK, (c + 1) * RET_CHUNK)
        cos = cos_ref[rs, :]
        sin = sin_ref[rs, :]
        qc = [_rotary_pair(q[rs, sl], cos, sin, first_half) for sl in lanes]
        kc = [_rotary_pair(kk[rs, sl], cos, sin, first_half) for sl in lanes]
        vc = [v[rs, sl].astype(BF16) for sl in lanes]
        scores = []
        for p in pairs:
            kb = kc[p].astype(BF16)
            k_cat = jnp.concatenate([kb * lo_b, kb * hi_b], axis=0)
            scores.append(_dot_nt(qc[p].astype(BF16), k_cat))
            mlp(MLP_STAGES_AFTER["ret"].get((c, 0, p), 0))
        outs = []
        for p in pairs:
            state = sbuf[p]
            lhs = jnp.concatenate([(scores[p] * dec_ref[p]).astype(BF16),
                                   (qc[p] * qdec_ref[:, lanes[p]]).astype(BF16)], axis=1)
            rhs = jnp.concatenate([vc[p] * lo_b, vc[p] * hi_b, state.astype(BF16)], axis=0)
            outs.append(_dot(lhs, rhs))
            kd = (kc[p] * kdec_ref[:, lanes[p]]).astype(BF16)
            sbuf[p] = state * sdec_ref[p] + jnp.where(sdec_ref[p] > 0.0, _dot_tn(kd, vc[p]), 0.0)
            mlp(MLP_STAGES_AFTER["ret"].get((c, 1, p), 0))
        for p in pairs:
            on = _pair_groupnorm(outs[p], lo, rlng_ref[:, lanes[p]], rlnb_ref[:, lanes[p]])
            mixbuf[rs, CONV_CH + p * PAIR_LANES:CONV_CH + (p + 1) * PAIR_LANES] = (
                on * _silu(gt[rs, lanes[p]])).astype(BF16)
            mlp(MLP_STAGES_AFTER["ret"].get((c, 2, p), 0))
    for _ in ffn:
        pass

    mix = _dot(mixbuf[...], wout_ref[...])
    x1buf[...] = x + gate * mix

    @pl.when(jnp.logical_and(j == nj - 1, live))
    def _():
        hist_ref[0] = ubuf[0:HIST_ROWS, :]
        state_ref[0] = sbuf[...]


def _layer_prompt(x, mod, g_mix, w_in, conv_w, conv_b, cln_g, cln_b, rln_g, rln_b, w_out,
                  g_ffn, w1, w2, g_final, rope, decay, tile):
    bsz, seq, d = x.shape
    nj = seq // tile
    total = bsz * nj
    cos, sin = rope
    dec, qdec, kdec, sdec = decay
    mix_tile = lambda g: jnp.minimum(g, total - 1)
    ffn_tile = lambda g: jnp.maximum(g - 1, 0)
    resident = lambda shape: pl.BlockSpec(shape, lambda g: (0,) * len(shape),
                                          pipeline_mode=pl.Buffered(1))
    row = lambda a: a.reshape(1, -1)
    conv_w_rows = jnp.broadcast_to(conv_w[:, None, :], (CONV_K, SUBLANES, CONV_CH))
    rope_spec = pl.BlockSpec((tile, PAIR_LANES), lambda g: (mix_tile(g) % nj, 0))
    return pl.pallas_call(
        functools.partial(_layer_prompt_kernel, tile, nj),
        grid=(total + 1,),
        in_specs=[
            pl.BlockSpec((1, tile, d), lambda g: (mix_tile(g) // nj, mix_tile(g) % nj, 0)),
            pl.BlockSpec((1, 3, d), lambda g: (mix_tile(g) // nj, 0, 0)),
            resident((1, d)),
            resident(w_in.shape),
            resident((CONV_K, SUBLANES, CONV_CH)),
            resident((1, CONV_CH)),
            resident((1, CONV_CH)),
            resident((1, CONV_CH)),
            resident((1, RET_W)),
            resident((1, RET_W)),
            resident(w_out.shape),
            rope_spec,
            rope_spec,
            resident(dec.shape),
            resident(qdec.shape),
            resident(kdec.shape),
            resident(sdec.shape),
            pl.BlockSpec((1, 3, 1, d), lambda g: (ffn_tile(g) // nj, 0, 0, 0)),
            resident((1, d)),
            resident(w1.shape),
            resident(w2.shape),
            resident((1, d)),
        ],
        out_specs=[
            pl.BlockSpec((1, tile, d), lambda g: (ffn_tile(g) // nj, ffn_tile(g) % nj, 0)),
            pl.BlockSpec((1, HIST_ROWS, CONV_CH), lambda g: (mix_tile(g) // nj, 0, 0)),
            pl.BlockSpec((1, N_PAIRS, PAIR_LANES, PAIR_LANES),
                         lambda g: (mix_tile(g) // nj, 0, 0, 0)),
        ],
        out_shape=(
            jax.ShapeDtypeStruct((bsz, seq, d), F32),
            jax.ShapeDtypeStruct((bsz, HIST_ROWS, CONV_CH), F32),
            jax.ShapeDtypeStruct((bsz, N_PAIRS, PAIR_LANES, PAIR_LANES), F32),
        ),
        scratch_shapes=[
            pltpu.VMEM((HIST_ROWS + tile, CONV_CH), F32),
            pltpu.VMEM((SUBLANES, HIST_ROWS + tile, CONV_CH), F32),
            pltpu.VMEM((N_PAIRS, PAIR_LANES, PAIR_LANES), F32),
            pltpu.VMEM((tile, D_MODEL), BF16),
            pltpu.VMEM((tile, D_MODEL), F32),
        ],
        compiler_params=pltpu.CompilerParams(
            dimension_semantics=("arbitrary",), vmem_limit_bytes=VMEM_LIMIT),
        name="layer_prompt",
    )(x, mod[:, 0:3], row(g_mix), w_in, conv_w_rows, row(conv_b), row(cln_g), row(cln_b),
      row(rln_g), row(rln_b), w_out, cos, sin, dec, qdec, kdec, sdec,
      mod[:, 3:6].reshape(bsz, 3, 1, d), row(g_ffn), w1, w2, row(g_final))


def _ffn_stages(x, mod_ref, gffn_ref, w1_ref, w2_ref, gfin_ref, y_ref):
    shift, scale, gate = mod_ref[0, 0], mod_ref[0, 1], mod_ref[0, 2]
    hb = _rmsnorm_mod(x, gffn_ref[...], scale, shift).astype(BF16)
    yield
    half = D_MODEL // 2
    ff = [None, None]
    for c in range(D_FF // (2 * FFN_CHUNK)):
        acts = []
        for h in range(2):
            c0 = (2 * c + h) * FFN_CHUNK
            a = jnp.maximum(_dot(hb, w1_ref[:, c0:c0 + FFN_CHUNK]), 0.0)
            acts.append((a * a).astype(BF16))
            yield
        act = jnp.concatenate(acts, axis=1)
        for n in range(2):
            part = _dot(act, w2_ref[2 * c * FFN_CHUNK:2 * (c + 1) * FFN_CHUNK,
                                    n * half:(n + 1) * half])
            ff[n] = part if ff[n] is None else ff[n] + part
            yield
    z = x + gate * jnp.concatenate(ff, axis=1)
    y_ref[0] = z * lax.rsqrt(jnp.mean(z * z, axis=-1, keepdims=True) + EPS) * gfin_ref[...]


def _mlp_sample_kernel(n_pos, n_seq, x_ref, conv_ref, ret_ref, gate1_ref, wout_ref, mod_ref,
                       gffn_ref, w1_ref, w2_ref, gfin_ref, y_ref, x1buf, hbuf, acc):
    c = pl.program_id(0)
    rows = lambda i: slice(i * n_seq, (i + 1) * n_seq)

    @pl.when(c == 0)
    def _():
        mix = (_dot(conv_ref[...], wout_ref[0:CONV_CH, :])
               + _dot(ret_ref[...], wout_ref[CONV_CH:CONV_CH + RET_W, :]))
        for i in range(n_pos):
            x1 = x_ref[i] + gate1_ref[...] * mix[rows(i)]
            x1buf[rows(i), :] = x1
            hbuf[rows(i), :] = _rmsnorm_mod(x1, gffn_ref[...], mod_ref[1], mod_ref[0]).astype(BF16)

    a = jnp.maximum(_dot(hbuf[...], w1_ref[...]), 0.0)
    part = _dot((a * a).astype(BF16), w2_ref[...])

    @pl.when(c == 0)
    def _():
        acc[...] = part

    @pl.when(c > 0)
    def _():
        acc[...] += part

    @pl.when(c == pl.num_programs(0) - 1)
    def _():
        for i in range(n_pos):
            z = x1buf[rows(i), :] + mod_ref[2] * acc[rows(i), :]
            y_ref[i] = (z * lax.rsqrt(jnp.mean(z * z, axis=-1, keepdims=True) + EPS)
                        * gfin_ref[...])


def _mlp_sample(xs, conv_out, ret_out, gate1, w_out, mod2, g_ffn, w1, w2, g_final):
    n_pos, n_seq, d = xs.shape
    n_tok = n_pos * n_seq
    n_chunks = D_FF // SAMPLE_FFN_CHUNK
    whole = lambda shape: pl.BlockSpec(shape, lambda c: (0,) * len(shape),
                                       pipeline_mode=pl.Buffered(1))
    return pl.pallas_call(
        functools.partial(_mlp_sample_kernel, n_pos, n_seq),
        grid=(n_chunks,),
        in_specs=[whole(xs.shape), whole(conv_out.shape), whole(ret_out.shape),
                  whole(gate1.shape), whole(w_out.shape), whole(mod2.shape), whole((1, d)),
                  pl.BlockSpec((d, SAMPLE_FFN_CHUNK), lambda c: (0, c)),
                  pl.BlockSpec((SAMPLE_FFN_CHUNK, d), lambda c: (c, 0)),
                  whole((1, d))],
        out_specs=pl.BlockSpec(xs.shape, lambda c: (0, 0, 0)),
        out_shape=jax.ShapeDtypeStruct(xs.shape, F32),
        scratch_shapes=[pltpu.VMEM((n_tok, d), F32), pltpu.VMEM((n_tok, d), BF16),
                        pltpu.VMEM((n_tok, d), F32)],
        compiler_params=pltpu.CompilerParams(
            dimension_semantics=("arbitrary",), vmem_limit_bytes=VMEM_LIMIT),
        name="mlp_sample",
    )(xs, conv_out, ret_out, gate1, w_out, mod2, g_ffn.reshape(1, d), w1, w2,
      g_final.reshape(1, d))


def _power_table_kernel(o_ref):
    lane_h = (lax.broadcasted_iota(jnp.int32, o_ref.shape, 1) // RET_DK).astype(F32)
    n = lax.broadcasted_iota(jnp.int32, o_ref.shape, 0).astype(F32)
    o_ref[...] = jnp.exp(_log_decay(lane_h) * n)


def _power_table():
    return pl.pallas_call(_power_table_kernel,
                          out_shape=jax.ShapeDtypeStruct((8, RET_W), F32),
                          name="decay_powers")()


def _proj_sample_kernel(n_pos, n_seq, x_ref, mod_ref, gmix_ref, win_ref, cache_ref, convw_ref,
                        convb_ref, clng_ref, clnb_ref, cos_ref, sin_ref,
                        u_ref, conv_ref, q_ref, k_ref, v_ref, g_ref):
    shift, scale = mod_ref[0], mod_ref[1]
    hb = jnp.concatenate(
        [_rmsnorm_mod(x_ref[i], gmix_ref[...], scale, shift).astype(BF16) for i in range(n_pos)],
        axis=0)

    def proj(k):
        return _dot(hb, win_ref[:, k * CONV_CH:(k + 1) * CONV_CH])

    u = proj(0) * jax.nn.sigmoid(proj(1))
    n_hist = CONV_K - 1
    rows = lambda a, i: a[i * n_seq:(i + 1) * n_seq]
    outs = []
    for i in range(n_pos):
        u_ref[i] = rows(u, i)
        dw = jnp.broadcast_to(convb_ref[...], (n_seq, CONV_CH))
        for k in range(CONV_K):
            m = i + k
            src = cache_ref[m] if m < n_hist else rows(u, m - n_hist)
            dw = dw + convw_ref[k:k + 1, :] * src
        outs.append(_silu(_layernorm(dw, clng_ref[...], clnb_ref[...])).astype(BF16))
    conv_ref[...] = jnp.concatenate(outs, axis=0)

    lane = lax.broadcasted_iota(jnp.int32, (1, PAIR_LANES), 1)
    first_half = (lane % RET_DK) < (RET_DK // 2)
    q = proj(2)
    kk = proj(3)
    for i in range(n_pos):
        rs = slice(i * n_seq, (i + 1) * n_seq)
        cos = cos_ref[i:i + 1, :]
        sin = sin_ref[i:i + 1, :]
        for p in range(N_PAIRS):
            sl = slice(p * PAIR_LANES, (p + 1) * PAIR_LANES)
            q_ref[rs, sl] = _rotary_pair(q[rs, sl], cos, sin, first_half)
            k_ref[rs, sl] = _rotary_pair(kk[rs, sl], cos, sin, first_half)
    v_ref[...] = proj(4)
    g_ref[...] = proj(5)


def _proj_sample(xs, mod1, g_mix, w_in, cache_t, conv_w, conv_b, cln_g, cln_b, rope):
    n_pos, n_seq, d = xs.shape
    n_tok = n_pos * n_seq
    cos, sin = rope
    row = lambda a: a.reshape(1, -1)
    tok = lambda dt: jax.ShapeDtypeStruct((n_tok, RET_W), dt)
    return pl.pallas_call(
        functools.partial(_proj_sample_kernel, n_pos, n_seq),
        out_shape=(jax.ShapeDtypeStruct((n_pos, n_seq, CONV_CH), F32), tok(BF16),
                   tok(F32), tok(F32), tok(F32), tok(F32)),
        compiler_params=pltpu.CompilerParams(vmem_limit_bytes=VMEM_LIMIT),
        name="proj_sample",
    )(xs, mod1, row(g_mix), w_in, cache_t, conv_w, row(conv_b), row(cln_g), row(cln_b), cos, sin)


RET_LANE_BLOCK = 2048


def _ret_sample_kernel(n_pos, n_seq, q_ref, k_ref, v_ref, g_ref, s_ref, pw_ref, rlng_ref,
                       rlnb_ref, out_ref, snew_ref):
    lane = lax.broadcasted_iota(jnp.int32, (1, PAIR_LANES), 1)
    lo = lane < RET_DK
    rows = lambda a, i: a[i * n_seq:(i + 1) * n_seq]
    pw = pw_ref[...]
    q = q_ref[...]
    k = k_ref[...]
    qb = [rows(q, i).astype(BF16) for i in range(n_pos)]
    kb = [rows(k, i).astype(BF16) for i in range(n_pos)]
    vb = [rows(v_ref[...], i).astype(BF16) for i in range(n_pos)]

    def head_bcast(x, a):
        other = pltpu.roll(x, RET_DK, 1)
        return jnp.where(lo, x, other) if a == 0 else jnp.where(lo, other, x)

    o = []
    for i in range(n_pos):
        acc = jnp.zeros((n_seq, PAIR_LANES), F32)
        for j in range(i + 1):
            prod = qb[i].astype(F32) * kb[j].astype(F32)
            s_lo = jnp.sum(jnp.where(lo, prod, 0.0), axis=-1, keepdims=True)
            s_hi = jnp.sum(jnp.where(lo, 0.0, prod), axis=-1, keepdims=True)
            score = jnp.where(lo, s_lo, s_hi) * pw[i - j:i - j + 1, :]
            acc = acc + score.astype(BF16).astype(F32) * vb[j].astype(F32)
        o.append(acc)

    qd = jnp.concatenate([(rows(q, i) * pw[i + 1:i + 2, :]).astype(BF16) for i in range(n_pos)], axis=0)
    kd = jnp.concatenate([(rows(k, j) * pw[n_pos - 1 - j:n_pos - j, :]).astype(BF16)
                          for j in range(n_pos)], axis=0)
    qk = jnp.concatenate([qd, kd], axis=0)
    head_lanes = RET_DK * RET_DV
    cross = [[jnp.zeros((n_seq, PAIR_LANES), F32) for _ in range(2)] for _ in range(n_pos)]
    for blk in range(2 * head_lanes // RET_LANE_BLOCK):
        c0 = blk * RET_LANE_BLOCK
        a = c0 // head_lanes
        col = lax.broadcasted_iota(jnp.int32, (PAIR_LANES, RET_LANE_BLOCK), 1) + c0
        r = lax.broadcasted_iota(jnp.int32, (PAIR_LANES, RET_LANE_BLOCK), 0)
        expand = jnp.where(col // RET_DV == r, 1.0, 0.0).astype(BF16)
        big = _dot(qk, expand)
        s_old = s_ref[:, c0:c0 + RET_LANE_BLOCK]
        s_b = s_old.astype(BF16).astype(F32)
        gam = pw[n_pos:n_pos + 1, a * RET_DK:a * RET_DK + 1]
        s_new = s_old * gam
        reps = RET_LANE_BLOCK // PAIR_LANES
        for i in range(n_pos):
            prod = rows(big, i) * s_b
            t = prod[:, 0:PAIR_LANES]
            for e in range(1, reps):
                t = t + prod[:, e * PAIR_LANES:(e + 1) * PAIR_LANES]
            cross[i][a] = cross[i][a] + t
            vrep = jnp.tile(head_bcast(vb[i].astype(F32), a), (1, reps))
            s_new = s_new + rows(big, n_pos + i) * vrep
        snew_ref[:, c0:c0 + RET_LANE_BLOCK] = s_new

    for i in range(n_pos):
        fold = [c + pltpu.roll(c, RET_DK, 1) for c in cross[i]]
        oi = o[i] + jnp.where(lo, fold[0], fold[1])
        on = _pair_groupnorm(oi, lo, rlng_ref[...], rlnb_ref[...])
        out_ref[i * n_seq:(i + 1) * n_seq, :] = (on * _silu(rows(g_ref[...], i))).astype(BF16)


def _ret_sample(q, k, v, g, state, pw, rln_g, rln_b, n_pos, n_seq):
    n_tok = n_pos * n_seq
    pair_state = 2 * RET_DK * RET_DV
    slab = pl.BlockSpec((n_tok, PAIR_LANES), lambda p: (0, p))
    return pl.pallas_call(
        functools.partial(_ret_sample_kernel, n_pos, n_seq),
        grid=(N_PAIRS,),
        in_specs=[slab, slab, slab, slab,
                  pl.BlockSpec((n_seq, pair_state), lambda p: (0, p)),
                  pl.BlockSpec((8, PAIR_LANES), lambda p: (0, p)),
                  pl.BlockSpec((1, PAIR_LANES), lambda p: (0, p)),
                  pl.BlockSpec((1, PAIR_LANES), lambda p: (0, p))],
        out_specs=[slab, pl.BlockSpec((n_seq, pair_state), lambda p: (0, p))],
        out_shape=(jax.ShapeDtypeStruct((n_tok, RET_W), BF16),
                   jax.ShapeDtypeStruct(state.shape, F32)),
        compiler_params=pltpu.CompilerParams(
            dimension_semantics=("arbitrary",), vmem_limit_bytes=VMEM_LIMIT),
        name="ret_sample",
    )(q, k, v, g, state, pw, rln_g.reshape(1, RET_W), rln_b.reshape(1, RET_W))


def _unpack_state(s2):
    blocks = []
    for p in range(N_PAIRS):
        for a in range(2):
            sl = slice(a * RET_DK, (a + 1) * RET_DK)
            blocks.append(s2[:, p, sl, sl])
    return jnp.stack(blocks, axis=1)


def _cast_weights(w_in, w_out, w_ff1, w_ff2):
    col_scale = jnp.ones((w_in.shape[1],), F32).at[2 * CONV_CH:2 * CONV_CH + RET_W].set(RET_DK ** -0.5)
    return ((w_in * col_scale).astype(BF16), w_out.astype(BF16), w_ff1.astype(BF16),
            w_ff2.astype(BF16))


def _prompt_path(x, mod, g_mix, w_in, conv_w, conv_b, cln_g, cln_b, rln_g, rln_b,
                 w_out, g_ffn, w_ff1, w_ff2, g_final, tile=PROMPT_TILE):
    seq = x.shape[1]
    rope = _rope_tables(seq, 0)
    decay = _decay_tables(RET_CHUNK)
    y, hist, s2 = _layer_prompt(x, mod, g_mix, w_in, conv_w, conv_b, cln_g, cln_b, rln_g, rln_b,
                                w_out, g_ffn, w_ff1, w_ff2, g_final, rope, decay, tile)
    return y, hist[:, HIST_ROWS - (CONV_K - 1):], _unpack_state(s2)


def _sample_path(x, mod, cache, state, pos0, g_mix, w_in, conv_w, conv_b, cln_g, cln_b,
                 rln_g, rln_b, w_out, g_ffn, w_ff1, w_ff2, g_final):
    n_seq, n_pos, d = x.shape
    xs = x.transpose(1, 0, 2)
    modt = mod.transpose(1, 0, 2)
    rope = _rope_tables(8, pos0)
    u, conv_out, q, k, v, g = _proj_sample(xs, modt[0:2], g_mix, w_in, cache.transpose(1, 0, 2),
                                           conv_w, conv_b, cln_g, cln_b, rope)
    ret_out, s_new = _ret_sample(q, k, v, g, state.reshape(n_seq, -1), _power_table(),
                                 rln_g, rln_b, n_pos, n_seq)
    y = _mlp_sample(xs, conv_out, ret_out, modt[2], w_out, modt[3:6], g_ffn, w_ff1, w_ff2,
                    g_final)
    new_cache = jnp.concatenate([cache[:, n_pos:], u.transpose(1, 0, 2)], axis=1)
    return y.transpose(1, 0, 2), new_cache, s_new.reshape(state.shape)


def kernel(x_prompt, x_sample, cache_conv, state_ret, c_prompt, c_sample, w_ada, b_ada, g_mix, w_in, conv_w, conv_b, conv_ln_g, conv_ln_b, ret_ln_g, ret_ln_b, w_out, g_ffn, w_ff1, w_ff2, g_final):
    bp = x_prompt.shape[0]
    d = x_prompt.shape[-1]
    w_in_b, w_out_b, w_ff1_b, w_ff2_b = _cast_weights(w_in[0], w_out[0], w_ff1[0], w_ff2[0])
    mod = _modulation(jnp.concatenate([c_prompt, c_sample], axis=0), w_ada[0], b_ada[0])
    mod = mod.reshape(-1, N_MOD, d)
    params = (g_mix[0], w_in_b, conv_w[0], conv_b[0], conv_ln_g[0], conv_ln_b[0], ret_ln_g[0],
              ret_ln_b[0], w_out_b, g_ffn[0], w_ff1_b, w_ff2_b, g_final)
    y_p, conv_p, ret_p = _prompt_path(x_prompt, mod[:bp], *params)
    y_s, conv_s, ret_s = _sample_path(x_sample, mod[bp:], cache_conv[0], state_ret[0], PAST_LEN,
                                      *params)
    return (y_p, y_s, conv_p[None], ret_p[None], conv_s[None], ret_s[None])
```

```python
import functools

import jax
import jax.numpy as jnp
from jax import lax
from jax.experimental import pallas as pl
from jax.experimental.pallas import tpu as pltpu

D_MODEL = 1024
CONV_CH = 512
CONV_K = 31
RET_HEADS = 8
RET_W = 512
RET_DK = 64
RET_DV = 64
D_FF = 4 * D_MODEL
RET_CHUNK = 128
ROPE_THETA = 10000.0
EPS = 1e-6
N_MOD = 6
PAST_LEN = 16384

SUBLANES = 8
HIST_ROWS = 32
CONV_ROW_BLOCK = 32
PROMPT_TILE = 256
FFN_CHUNK = 512
SAMPLE_FFN_CHUNK = 1024
LAYER_PROGRAM = ("c cf cf cf cf cf cf cf cf "
                 "rrrr rfrf rfrf r "
                 "rfrf rfrf r")
PAIR_LANES = 2 * RET_DK
N_PAIRS = RET_HEADS // 2
VMEM_LIMIT = 56 * 1024 * 1024

F32 = jnp.float32
BF16 = jnp.bfloat16


def _log_decay(head):
    return jnp.log1p(-jnp.exp2(-5.0 - head))


def _rope_kernel(pos0, cos_ref, sin_ref):
    rows = cos_ref.shape[0]
    lane = lax.broadcasted_iota(jnp.int32, (rows, PAIR_LANES), 1)
    row = lax.broadcasted_iota(jnp.int32, (rows, PAIR_LANES), 0)
    half = RET_DK // 2
    freq = (lane % half).astype(F32)
    inv = jnp.power(jnp.full_like(freq, ROPE_THETA), -freq / half)
    ang = (row + pos0).astype(F32) * inv
    first = (lane % RET_DK) < half
    cos_ref[...] = jnp.cos(ang)
    sin_ref[...] = jnp.where(first, -jnp.sin(ang), jnp.sin(ang))


ROPE_BLOCK = 16


def _rope_blocked_kernel(pos0, cos_ref, sin_ref):
    rows = cos_ref.shape[0]
    n_blk = rows // ROPE_BLOCK
    half = RET_DK // 2

    def angles(n, step, start):
        lane = lax.broadcasted_iota(jnp.int32, (n, PAIR_LANES), 1)
        row = lax.broadcasted_iota(jnp.int32, (n, PAIR_LANES), 0)
        freq = (lane % half).astype(F32)
        inv = jnp.power(jnp.full_like(freq, ROPE_THETA), -freq / half)
        return (row * step + start).astype(F32) * inv

    a = angles(n_blk, ROPE_BLOCK, pos0)
    b = angles(ROPE_BLOCK, 1, 0)
    lane = lax.broadcasted_iota(jnp.int32, (ROPE_BLOCK, PAIR_LANES), 1)
    sign = jnp.where((lane % RET_DK) < half, -1.0, 1.0)
    cos_a, sin_a = jnp.cos(a), jnp.sin(a)
    cos_b, sin_b = jnp.cos(b), jnp.sin(b)
    cos_bs, sin_bs = cos_b * sign, sin_b * sign
    for blk in range(n_blk):
        ca, sa = cos_a[blk:blk + 1, :], sin_a[blk:blk + 1, :]
        rs = slice(blk * ROPE_BLOCK, (blk + 1) * ROPE_BLOCK)
        cos_ref[rs, :] = ca * cos_b - sa * sin_b
        sin_ref[rs, :] = sa * cos_bs + ca * sin_bs


def _rope_tables(rows, pos0):
    blocked = rows % ROPE_BLOCK == 0 and rows // ROPE_BLOCK >= SUBLANES
    return pl.pallas_call(
        functools.partial(_rope_blocked_kernel if blocked else _rope_kernel, pos0),
        out_shape=(jax.ShapeDtypeStruct((rows, PAIR_LANES), F32),
                   jax.ShapeDtypeStruct((rows, PAIR_LANES), F32)),
        name="rope_tables",
    )()


def _decay_kernel(chunk, dec_ref, qdec_ref, kdec_ref, sdec_ref):
    c = chunk
    shape = (N_PAIRS, c, 2 * c)
    col = lax.broadcasted_iota(jnp.int32, shape, 2)
    h = (2 * lax.broadcasted_iota(jnp.int32, shape, 0) + col // c).astype(F32)
    i = lax.broadcasted_iota(jnp.int32, shape, 1).astype(F32)
    j = (col % c).astype(F32)
    diff = i - j
    dec_ref[...] = jnp.where(diff >= 0, jnp.exp(_log_decay(h) * jnp.maximum(diff, 0.0)), 0.0)
    lane_h = (lax.broadcasted_iota(jnp.int32, (c, RET_W), 1) // RET_DK).astype(F32)
    idx = lax.broadcasted_iota(jnp.int32, (c, RET_W), 0).astype(F32)
    lg = _log_decay(lane_h)
    qdec_ref[...] = jnp.exp(lg * (idx + 1.0))
    kdec_ref[...] = jnp.exp(lg * (c - 1.0 - idx))
    p = lax.broadcasted_iota(jnp.int32, (N_PAIRS, PAIR_LANES, PAIR_LANES), 0)
    r = lax.broadcasted_iota(jnp.int32, (N_PAIRS, PAIR_LANES, PAIR_LANES), 1) // RET_DK
    q = lax.broadcasted_iota(jnp.int32, (N_PAIRS, PAIR_LANES, PAIR_LANES), 2) // RET_DK
    hh = (2 * p + r).astype(F32)
    sdec_ref[...] = jnp.where(r == q, jnp.exp(_log_decay(hh) * float(c)), 0.0)


def _decay_tables(chunk):
    return pl.pallas_call(
        functools.partial(_decay_kernel, chunk),
        out_shape=(jax.ShapeDtypeStruct((N_PAIRS, chunk, 2 * chunk), F32),
                   jax.ShapeDtypeStruct((chunk, RET_W), F32),
                   jax.ShapeDtypeStruct((chunk, RET_W), F32),
                   jax.ShapeDtypeStruct((N_PAIRS, PAIR_LANES, PAIR_LANES), F32)),
        name="decay_tables",
    )()


def _mod_kernel(c_ref, w_ref, b_ref, o_ref):
    c = c_ref[...]
    s = c * jax.nn.sigmoid(c)
    o_ref[...] = jnp.dot(s.astype(BF16), w_ref[...].astype(BF16),
                         preferred_element_type=F32) + b_ref[...]


def _modulation(c, w_ada, b_ada):
    n, d = c.shape
    width = w_ada.shape[1]
    blk = D_MODEL
    return pl.pallas_call(
        _mod_kernel,
        grid=(width // blk,),
        in_specs=[pl.BlockSpec((n, d), lambda i: (0, 0)),
                  pl.BlockSpec((d, blk), lambda i: (0, i)),
                  pl.BlockSpec((1, blk), lambda i: (0, i))],
        out_specs=pl.BlockSpec((n, blk), lambda i: (0, i)),
        out_shape=jax.ShapeDtypeStruct((n, width), F32),
        compiler_params=pltpu.CompilerParams(vmem_limit_bytes=VMEM_LIMIT),
        name="adaln_mod",
    )(c, w_ada, b_ada.reshape(1, width))


def _rmsnorm_mod(x, g, scale, shift):
    y = x * lax.rsqrt(jnp.mean(x * x, axis=-1, keepdims=True) + EPS)
    return (y * g) * (1.0 + scale) + shift


def _layernorm(x, g, b):
    mu = jnp.mean(x, axis=-1, keepdims=True)
    d = x - mu
    var = jnp.mean(d * d, axis=-1, keepdims=True)
    return d * lax.rsqrt(var + EPS) * g + b


def _silu(x):
    return x * jax.nn.sigmoid(x)


def _rotary_pair(x, cos, sin_signed, first_half):
    partner = jnp.where(first_half, pltpu.roll(x, PAIR_LANES - RET_DK // 2, 1),
                        pltpu.roll(x, RET_DK // 2, 1))
    return x * cos + partner * sin_signed


def _pair_groupnorm(o, lo, g, b):
    inv_n = 1.0 / RET_DV
    s_lo = jnp.sum(jnp.where(lo, o, 0.0), axis=-1, keepdims=True)
    s_hi = jnp.sum(jnp.where(lo, 0.0, o), axis=-1, keepdims=True)
    d = o - jnp.where(lo, s_lo, s_hi) * inv_n
    d2 = d * d
    v_lo = jnp.sum(jnp.where(lo, d2, 0.0), axis=-1, keepdims=True)
    v_hi = jnp.sum(jnp.where(lo, 0.0, d2), axis=-1, keepdims=True)
    var = jnp.where(lo, v_lo, v_hi) * inv_n
    return d * lax.rsqrt(var + EPS) * g + b


def _dot(a, b):
    return jnp.dot(a, b, preferred_element_type=F32)


def _dot_nt(a, b):
    return lax.dot_general(a, b, (((1,), (1,)), ((), ())), preferred_element_type=F32)


def _dot_tn(a, b):
    return lax.dot_general(a, b, (((0,), (0,)), ((), ())), preferred_element_type=F32)


def _layer_prompt_kernel(tile, nj, x_ref, mod_ref, gmix_ref, win_ref, convw_ref, convb_ref,
                         clng_ref, clnb_ref, rlng_ref, rlnb_ref, wout_ref,
                         cos_ref, sin_ref, dec_ref, qdec_ref, kdec_ref, sdec_ref,
                         mod2_ref, gffn_ref, w1_ref, w2_ref, gfin_ref,
                         y_ref, hist_ref, state_ref,
                         ubuf, ushift, sbuf, mixbuf, x1buf):
    g = pl.program_id(0)
    total = pl.num_programs(0) - 1
    j = lax.rem(jnp.minimum(g, total - 1), nj)
    live = g < total

    @pl.when(g == 0)
    def _():
        x1buf[...] = jnp.zeros((tile, D_MODEL), F32)

    @pl.when(jnp.logical_and(j == 0, live))
    def _():
        ubuf[0:HIST_ROWS, :] = jnp.zeros((HIST_ROWS, CONV_CH), F32)
        sbuf[...] = jnp.zeros(sbuf.shape, F32)

    x = x_ref[0]
    b_mix = lax.div(jnp.minimum(g, total - 1), nj)
    b_mlp = lax.div(jnp.maximum(g - 1, 0), nj)
    mods = lambda ref, b: [ref[pl.ds(b, 1), k * D_MODEL:(k + 1) * D_MODEL] for k in range(3)]
    shift, scale, gate = mods(mod_ref, b_mix)

    def conv_stages(u):
        ubuf[HIST_ROWS:HIST_ROWS + tile, :] = u
        off = HIST_ROWS - (CONV_K - 1)
        for r in range(SUBLANES):
            n = tile + SUBLANES * ((CONV_K - 1 - r) // SUBLANES)
            ushift[r, 0:n, :] = ubuf[off + r:off + r + n, :]
        ubuf[0:HIST_ROWS, :] = ubuf[tile:tile + HIST_ROWS, :]
        yield
        for rb in range(tile // CONV_ROW_BLOCK):
            r0 = rb * CONV_ROW_BLOCK
            dw = jnp.broadcast_to(convb_ref[...], (CONV_ROW_BLOCK, CONV_CH))
            for k in range(CONV_K):
                s0 = r0 + SUBLANES * (k // SUBLANES)
                wk = jnp.concatenate([convw_ref[k]] * (CONV_ROW_BLOCK // SUBLANES), axis=0)
                dw = dw + wk * ushift[k % SUBLANES, s0:s0 + CONV_ROW_BLOCK, :]
            conv_out = _silu(_layernorm(dw, clng_ref[...], clnb_ref[...]))
            mixbuf[r0:r0 + CONV_ROW_BLOCK, 0:CONV_CH] = conv_out.astype(BF16)
            yield

    def ret_stages(proj):
        ret_proj = []
        for k in range(4):
            ret_proj.append(proj(2 + k))
            yield
        q, kk, v, gt = ret_proj
        lane = lax.broadcasted_iota(jnp.int32, (1, PAIR_LANES), 1)
        lo = lane < RET_DK
        first_half = (lane % RET_DK) < (RET_DK // 2)
        lo_b = lo.astype(BF16)
        hi_b = (~lo).astype(BF16)
        pairs = range(N_PAIRS)
        halves = (pairs[:N_PAIRS // 2], pairs[N_PAIRS // 2:])
        lanes = [slice(p * PAIR_LANES, (p + 1) * PAIR_LANES) for p in pairs]
        for c in range(tile // RET_CHUNK):
            rs = slice(c * RET_CHUNK, (c + 1) * RET_CHUNK)
            cos = cos_ref[rs, :]
            sin = sin_ref[rs, :]
            qc, kc, vc, scores, outs = {}, {}, {}, {}, {}
            for group in halves:
                for p in group:
                    qc[p] = _rotary_pair(q[rs, lanes[p]], cos, sin, first_half)
                    kc[p] = _rotary_pair(kk[rs, lanes[p]], cos, sin, first_half)
                    vc[p] = v[rs, lanes[p]].astype(BF16)
                    kb = kc[p].astype(BF16)
                    k_cat = jnp.concatenate([kb * lo_b, kb * hi_b], axis=0)
                    scores[p] = _dot_nt(qc[p].astype(BF16), k_cat)
                yield
            for group in halves:
                for p in group:
                    state = sbuf[p]
                    lhs = jnp.concatenate([(scores[p] * dec_ref[p]).astype(BF16),
                                           (qc[p] * qdec_ref[:, lanes[p]]).astype(BF16)], axis=1)
                    rhs = jnp.concatenate([vc[p] * lo_b, vc[p] * hi_b, state.astype(BF16)],
                                          axis=0)
                    outs[p] = _dot(lhs, rhs)
                    kd = (kc[p] * kdec_ref[:, lanes[p]]).astype(BF16)
                    sbuf[p] = (state * sdec_ref[p]
                               + jnp.where(sdec_ref[p] > 0.0, _dot_tn(kd, vc[p]), 0.0))
                yield
            for p in pairs:
                on = _pair_groupnorm(outs[p], lo, rlng_ref[:, lanes[p]], rlnb_ref[:, lanes[p]])
                mixbuf[rs, CONV_CH + p * PAIR_LANES:CONV_CH + (p + 1) * PAIR_LANES] = (
                    on * _silu(gt[rs, lanes[p]])).astype(BF16)
            yield

    ffn = _ffn_stages(x1buf[...], mods(mod2_ref, b_mlp), gffn_ref, w1_ref, w2_ref, gfin_ref, y_ref)
    next(ffn)
    hb = _rmsnorm_mod(x, gmix_ref[...], scale, shift).astype(BF16)

    def proj(k):
        return _dot(hb, win_ref[:, k * CONV_CH:(k + 1) * CONV_CH])

    streams = {"f": ffn, "c": conv_stages(proj(0) * jax.nn.sigmoid(proj(1))),
               "r": ret_stages(proj)}
    for name in LAYER_PROGRAM.replace(" ", ""):
        next(streams[name], None)
    for stream in (streams["c"], streams["r"], streams["f"]):
        for _ in stream:
            pass

    mix = _dot(mixbuf[...], wout_ref[...])
    x1buf[...] = x + gate * mix

    @pl.when(jnp.logical_and(j == nj - 1, live))
    def _():
        hist_ref[0] = ubuf[0:HIST_ROWS, :]
        state_ref[0] = sbuf[...]


def _layer_prompt(x, mod, mod_row0, g_mix, w_in, conv_w, conv_b, cln_g, cln_b, rln_g, rln_b,
                  w_out, g_ffn, w1, w2, g_final, rope, decay, tile):
    bsz, seq, d = x.shape
    assert bsz % SUBLANES == 0 and mod_row0 % bsz == 0
    mod_blk = mod_row0 // bsz
    nj = seq // tile
    total = bsz * nj
    cos, sin = rope
    dec, qdec, kdec, sdec = decay
    mix_tile = lambda g: jnp.minimum(g, total - 1)
    ffn_tile = lambda g: jnp.maximum(g - 1, 0)
    resident = lambda shape: pl.BlockSpec(shape, lambda g: (0,) * len(shape),
                                          pipeline_mode=pl.Buffered(1))
    row = lambda a: a.reshape(1, -1)
    conv_w_rows = jnp.broadcast_to(conv_w[:, None, :], (CONV_K, SUBLANES, CONV_CH))
    rope_spec = pl.BlockSpec((tile, PAIR_LANES), lambda g: (mix_tile(g) % nj, 0))
    return pl.pallas_call(
        functools.partial(_layer_prompt_kernel, tile, nj),
        grid=(total + 1,),
        in_specs=[
            pl.BlockSpec((1, tile, d), lambda g: (mix_tile(g) // nj, mix_tile(g) % nj, 0)),
            pl.BlockSpec((bsz, 3 * d), lambda g: (mod_blk, 0), pipeline_mode=pl.Buffered(1)),
            resident((1, d)),
            resident(w_in.shape),
            resident((CONV_K, SUBLANES, CONV_CH)),
            resident((1, CONV_CH)),
            resident((1, CONV_CH)),
            resident((1, CONV_CH)),
            resident((1, RET_W)),
            resident((1, RET_W)),
            resident(w_out.shape),
            rope_spec,
            rope_spec,
            resident(dec.shape),
            resident(qdec.shape),
            resident(kdec.shape),
            resident(sdec.shape),
            pl.BlockSpec((bsz, 3 * d), lambda g: (mod_blk, 1), pipeline_mode=pl.Buffered(1)),
            resident((1, d)),
            resident(w1.shape),
            resident(w2.shape),
            resident((1, d)),
        ],
        out_specs=[
            pl.BlockSpec((1, tile, d), lambda g: (ffn_tile(g) // nj, ffn_tile(g) % nj, 0)),
            pl.BlockSpec((1, HIST_ROWS, CONV_CH), lambda g: (mix_tile(g) // nj, 0, 0)),
            pl.BlockSpec((1, N_PAIRS, PAIR_LANES, PAIR_LANES),
                         lambda g: (mix_tile(g) // nj, 0, 0, 0)),
        ],
        out_shape=(
            jax.ShapeDtypeStruct((bsz, seq, d), F32),
            jax.ShapeDtypeStruct((bsz, HIST_ROWS, CONV_CH), F32),
            jax.ShapeDtypeStruct((bsz, N_PAIRS, PAIR_LANES, PAIR_LANES), F32),
        ),
        scratch_shapes=[
            pltpu.VMEM((HIST_ROWS + tile, CONV_CH), F32),
            pltpu.VMEM((SUBLANES, HIST_ROWS + tile, CONV_CH), F32),
            pltpu.VMEM((N_PAIRS, PAIR_LANES, PAIR_LANES), F32),
            pltpu.VMEM((tile, D_MODEL), BF16),
            pltpu.VMEM((tile, D_MODEL), F32),
        ],
        compiler_params=pltpu.CompilerParams(
            dimension_semantics=("arbitrary",), vmem_limit_bytes=VMEM_LIMIT),
        name="layer_prompt",
    )(x, mod, row(g_mix), w_in, conv_w_rows, row(conv_b), row(cln_g), row(cln_b),
      row(rln_g), row(rln_b), w_out, cos, sin, dec, qdec, kdec, sdec,
      mod, row(g_ffn), w1, w2, row(g_final))


def _ffn_stages(x, mods, gffn_ref, w1_ref, w2_ref, gfin_ref, y_ref):
    shift, scale, gate = mods
    hb = _rmsnorm_mod(x, gffn_ref[...], scale, shift).astype(BF16)
    yield
    half = D_MODEL // 2
    ff = [None, None]
    for c in range(D_FF // (2 * FFN_CHUNK)):
        acts = []
        for h in range(2):
            c0 = (2 * c + h) * FFN_CHUNK
            a = jnp.maximum(_dot(hb, w1_ref[:, c0:c0 + FFN_CHUNK]), 0.0)
            acts.append((a * a).astype(BF16))
            yield
        act = jnp.concatenate(acts, axis=1)
        for n in range(2):
            part = _dot(act, w2_ref[2 * c * FFN_CHUNK:2 * (c + 1) * FFN_CHUNK,
                                    n * half:(n + 1) * half])
            ff[n] = part if ff[n] is None else ff[n] + part
            yield
    z = x + gate * jnp.concatenate(ff, axis=1)
    y_ref[0] = z * lax.rsqrt(jnp.mean(z * z, axis=-1, keepdims=True) + EPS) * gfin_ref[...]


def _mod_cols(mod_ref, n_seq, k):
    return mod_ref[0:n_seq, k * D_MODEL:(k + 1) * D_MODEL]


def _mlp_sample_kernel(n_pos, n_seq, x_ref, conv_ref, ret_ref, wout_ref, mod_ref,
                       gffn_ref, w1_ref, w2_ref, gfin_ref, y_ref, x1buf, hbuf, acc):
    c = pl.program_id(0)
    rows = lambda i: slice(i * n_seq, (i + 1) * n_seq)

    @pl.when(c == 0)
    def _():
        mix = (_dot(conv_ref[...], wout_ref[0:CONV_CH, :])
               + _dot(ret_ref[...], wout_ref[CONV_CH:CONV_CH + RET_W, :]))
        for i in range(n_pos):
            x1 = x_ref[i] + _mod_cols(mod_ref, n_seq, 2) * mix[rows(i)]
            x1buf[rows(i), :] = x1
            hbuf[rows(i), :] = _rmsnorm_mod(x1, gffn_ref[...], _mod_cols(mod_ref, n_seq, 4),
                                            _mod_cols(mod_ref, n_seq, 3)).astype(BF16)

    a = jnp.maximum(_dot(hbuf[...], w1_ref[...]), 0.0)
    part = _dot((a * a).astype(BF16), w2_ref[...])

    @pl.when(c == 0)
    def _():
        acc[...] = part

    @pl.when(c > 0)
    def _():
        acc[...] += part

    @pl.when(c == pl.num_programs(0) - 1)
    def _():
        for i in range(n_pos):
            z = x1buf[rows(i), :] + _mod_cols(mod_ref, n_seq, 5) * acc[rows(i), :]
            y_ref[i] = (z * lax.rsqrt(jnp.mean(z * z, axis=-1, keepdims=True) + EPS)
                        * gfin_ref[...])


def _mlp_sample(x, conv_out, ret_out, w_out, mod, g_ffn, w1, w2, g_final):
    n_pos, n_seq, d = x.shape
    n_tok = n_pos * n_seq
    n_chunks = D_FF // SAMPLE_FFN_CHUNK
    whole = lambda shape: pl.BlockSpec(shape, lambda c: (0,) * len(shape),
                                       pipeline_mode=pl.Buffered(1))
    return pl.pallas_call(
        functools.partial(_mlp_sample_kernel, n_pos, n_seq),
        grid=(n_chunks,),
        in_specs=[whole(x.shape), whole(conv_out.shape), whole(ret_out.shape),
                  whole(w_out.shape), whole(mod.shape), whole((1, d)),
                  pl.BlockSpec((d, SAMPLE_FFN_CHUNK), lambda c: (0, c)),
                  pl.BlockSpec((SAMPLE_FFN_CHUNK, d), lambda c: (c, 0)),
                  whole((1, d))],
        out_specs=pl.BlockSpec(x.shape, lambda c: (0, 0, 0)),
        out_shape=jax.ShapeDtypeStruct(x.shape, F32),
        scratch_shapes=[pltpu.VMEM((n_tok, d), F32), pltpu.VMEM((n_tok, d), BF16),
                        pltpu.VMEM((n_tok, d), F32)],
        compiler_params=pltpu.CompilerParams(
            dimension_semantics=("arbitrary",), vmem_limit_bytes=VMEM_LIMIT),
        name="mlp_sample",
    )(x, conv_out, ret_out, w_out, mod, g_ffn.reshape(1, d), w1, w2, g_final.reshape(1, d))


def _power_table_kernel(o_ref):
    lane_h = (lax.broadcasted_iota(jnp.int32, o_ref.shape, 1) // RET_DK).astype(F32)
    n = lax.broadcasted_iota(jnp.int32, o_ref.shape, 0).astype(F32)
    o_ref[...] = jnp.exp(_log_decay(lane_h) * n)


def _power_table():
    return pl.pallas_call(_power_table_kernel,
                          out_shape=jax.ShapeDtypeStruct((8, RET_W), F32),
                          name="decay_powers")()


def _proj_sample_kernel(n_pos, n_seq, x_ref, mod_ref, gmix_ref, win_ref, cache_ref, convw_ref,
                        convb_ref, clng_ref, clnb_ref, cos_ref, sin_ref,
                        u_ref, conv_ref, q_ref, k_ref, v_ref, g_ref):
    shift, scale = _mod_cols(mod_ref, n_seq, 0), _mod_cols(mod_ref, n_seq, 1)
    hb = jnp.concatenate(
        [_rmsnorm_mod(x_ref[i], gmix_ref[...], scale, shift).astype(BF16)
         for i in range(n_pos)], axis=0)

    def proj(k):
        return _dot(hb, win_ref[:, k * CONV_CH:(k + 1) * CONV_CH])

    u = proj(0) * jax.nn.sigmoid(proj(1))
    n_hist = CONV_K - 1
    rows = lambda a, i: a[i * n_seq:(i + 1) * n_seq]
    outs = []
    for i in range(n_pos):
        u_ref[i] = rows(u, i)
        dw = jnp.broadcast_to(convb_ref[...], (n_seq, CONV_CH))
        for k in range(CONV_K):
            m = i + k
            src = cache_ref[m] if m < n_hist else rows(u, m - n_hist)
            dw = dw + convw_ref[k:k + 1, :] * src
        outs.append(_silu(_layernorm(dw, clng_ref[...], clnb_ref[...])).astype(BF16))
    conv_ref[...] = jnp.concatenate(outs, axis=0)

    lane = lax.broadcasted_iota(jnp.int32, (1, PAIR_LANES), 1)
    first_half = (lane % RET_DK) < (RET_DK // 2)
    q = proj(2)
    kk = proj(3)
    for i in range(n_pos):
        rs = slice(i * n_seq, (i + 1) * n_seq)
        cos = cos_ref[i:i + 1, :]
        sin = sin_ref[i:i + 1, :]
        for p in range(N_PAIRS):
            sl = slice(p * PAIR_LANES, (p + 1) * PAIR_LANES)
            q_ref[rs, sl] = _rotary_pair(q[rs, sl], cos, sin, first_half)
            k_ref[rs, sl] = _rotary_pair(kk[rs, sl], cos, sin, first_half)
    v_ref[...] = proj(4)
    g_ref[...] = proj(5)


def _proj_sample(x, mod, g_mix, w_in, cache_t, conv_w, conv_b, cln_g, cln_b, rope):
    n_pos, n_seq, d = x.shape
    n_tok = n_pos * n_seq
    cos, sin = rope
    row = lambda a: a.reshape(1, -1)
    tok = lambda dt: jax.ShapeDtypeStruct((n_tok, RET_W), dt)
    return pl.pallas_call(
        functools.partial(_proj_sample_kernel, n_pos, n_seq),
        out_shape=(jax.ShapeDtypeStruct((n_pos, n_seq, CONV_CH), F32), tok(BF16),
                   tok(F32), tok(F32), tok(F32), tok(F32)),
        compiler_params=pltpu.CompilerParams(vmem_limit_bytes=VMEM_LIMIT),
        name="proj_sample",
    )(x, mod, row(g_mix), w_in, cache_t, conv_w, row(conv_b), row(cln_g), row(cln_b), cos, sin)


RET_LANE_BLOCK = 2048


def _ret_sample_kernel(n_pos, n_seq, q_ref, k_ref, v_ref, g_ref, s_ref, pw_ref, rlng_ref,
                       rlnb_ref, out_ref, snew_ref):
    lane = lax.broadcasted_iota(jnp.int32, (1, PAIR_LANES), 1)
    lo = lane < RET_DK
    rows = lambda a, i: a[i * n_seq:(i + 1) * n_seq]
    pw = pw_ref[...]
    q = q_ref[...]
    k = k_ref[...]
    qb = [rows(q, i).astype(BF16) for i in range(n_pos)]
    kb = [rows(k, i).astype(BF16) for i in range(n_pos)]
    vb = [rows(v_ref[...], i).astype(BF16) for i in range(n_pos)]

    def head_bcast(x, a):
        other = pltpu.roll(x, RET_DK, 1)
        return jnp.where(lo, x, other) if a == 0 else jnp.where(lo, other, x)

    o = []
    for i in range(n_pos):
        acc = jnp.zeros((n_seq, PAIR_LANES), F32)
        for j in range(i + 1):
            prod = qb[i].astype(F32) * kb[j].astype(F32)
            s_lo = jnp.sum(jnp.where(lo, prod, 0.0), axis=-1, keepdims=True)
            s_hi = jnp.sum(jnp.where(lo, 0.0, prod), axis=-1, keepdims=True)
            score = jnp.where(lo, s_lo, s_hi) * pw[i - j:i - j + 1, :]
            acc = acc + score.astype(BF16).astype(F32) * vb[j].astype(F32)
        o.append(acc)

    qd = jnp.concatenate([(rows(q, i) * pw[i + 1:i + 2, :]).astype(BF16) for i in range(n_pos)], axis=0)
    kd = jnp.concatenate([(rows(k, j) * pw[n_pos - 1 - j:n_pos - j, :]).astype(BF16)
                          for j in range(n_pos)], axis=0)
    qk = jnp.concatenate([qd, kd], axis=0)
    head_lanes = RET_DK * RET_DV
    cross = [[jnp.zeros((n_seq, PAIR_LANES), F32) for _ in range(2)] for _ in range(n_pos)]
    for blk in range(2 * head_lanes // RET_LANE_BLOCK):
        c0 = blk * RET_LANE_BLOCK
        a = c0 // head_lanes
        col = lax.broadcasted_iota(jnp.int32, (PAIR_LANES, RET_LANE_BLOCK), 1) + c0
        r = lax.broadcasted_iota(jnp.int32, (PAIR_LANES, RET_LANE_BLOCK), 0)
        expand = jnp.where(col // RET_DV == r, 1.0, 0.0).astype(BF16)
        big = _dot(qk, expand)
        s_old = s_ref[:, c0:c0 + RET_LANE_BLOCK]
        s_b = s_old.astype(BF16).astype(F32)
        gam = pw[n_pos:n_pos + 1, a * RET_DK:a * RET_DK + 1]
        s_new = s_old * gam
        reps = RET_LANE_BLOCK // PAIR_LANES
        for i in range(n_pos):
            prod = rows(big, i) * s_b
            t = prod[:, 0:PAIR_LANES]
            for e in range(1, reps):
                t = t + prod[:, e * PAIR_LANES:(e + 1) * PAIR_LANES]
            cross[i][a] = cross[i][a] + t
            vrep = jnp.tile(head_bcast(vb[i].astype(F32), a), (1, reps))
            s_new = s_new + rows(big, n_pos + i) * vrep
        snew_ref[:, c0:c0 + RET_LANE_BLOCK] = s_new

    for i in range(n_pos):
        fold = [c + pltpu.roll(c, RET_DK, 1) for c in cross[i]]
        oi = o[i] + jnp.where(lo, fold[0], fold[1])
        on = _pair_groupnorm(oi, lo, rlng_ref[...], rlnb_ref[...])
        out_ref[i * n_seq:(i + 1) * n_seq, :] = (on * _silu(rows(g_ref[...], i))).astype(BF16)


def _ret_sample(q, k, v, g, state, pw, rln_g, rln_b, n_pos, n_seq):
    n_tok = n_pos * n_seq
    pair_state = 2 * RET_DK * RET_DV
    slab = pl.BlockSpec((n_tok, PAIR_LANES), lambda p: (0, p))
    return pl.pallas_call(
        functools.partial(_ret_sample_kernel, n_pos, n_seq),
        grid=(N_PAIRS,),
        in_specs=[slab, slab, slab, slab,
                  pl.BlockSpec((n_seq, pair_state), lambda p: (0, p)),
                  pl.BlockSpec((8, PAIR_LANES), lambda p: (0, p)),
                  pl.BlockSpec((1, PAIR_LANES), lambda p: (0, p)),
                  pl.BlockSpec((1, PAIR_LANES), lambda p: (0, p))],
        out_specs=[slab, pl.BlockSpec((n_seq, pair_state), lambda p: (0, p))],
        out_shape=(jax.ShapeDtypeStruct((n_tok, RET_W), BF16),
                   jax.ShapeDtypeStruct(state.shape, F32)),
        compiler_params=pltpu.CompilerParams(
            dimension_semantics=("arbitrary",), vmem_limit_bytes=VMEM_LIMIT),
        name="ret_sample",
    )(q, k, v, g, state, pw, rln_g.reshape(1, RET_W), rln_b.reshape(1, RET_W))


def _unpack_state(s2):
    blocks = []
    for p in range(N_PAIRS):
        for a in range(2):
            sl = slice(a * RET_DK, (a + 1) * RET_DK)
            blocks.append(s2[:, p, sl, sl])
    return jnp.stack(blocks, axis=1)


def _cast_weights(w_in, w_out, w_ff1, w_ff2):
    col_scale = jnp.ones((w_in.shape[1],), F32).at[2 * CONV_CH:2 * CONV_CH + RET_W].set(RET_DK ** -0.5)
    return ((w_in * col_scale).astype(BF16), w_out.astype(BF16), w_ff1.astype(BF16),
            w_ff2.astype(BF16))


def _prompt_path(x, mod, mod_row0, g_mix, w_in, conv_w, conv_b, cln_g, cln_b, rln_g, rln_b,
                 w_out, g_ffn, w_ff1, w_ff2, g_final, tile=PROMPT_TILE):
    seq = x.shape[1]
    rope = _rope_tables(seq, 0)
    decay = _decay_tables(RET_CHUNK)
    y, hist, s2 = _layer_prompt(x, mod, mod_row0, g_mix, w_in, conv_w, conv_b, cln_g, cln_b,
                                rln_g, rln_b, w_out, g_ffn, w_ff1, w_ff2, g_final, rope, decay,
                                tile)
    return y, hist[:, HIST_ROWS - (CONV_K - 1):], _unpack_state(s2)


def _sample_path(x, mod, cache, state, pos0, g_mix, w_in, conv_w, conv_b, cln_g, cln_b,
                 rln_g, rln_b, w_out, g_ffn, w_ff1, w_ff2, g_final):
    n_seq, n_pos, d = x.shape
    xs = x.transpose(1, 0, 2)
    rope = _rope_tables(8, pos0)
    u, conv_out, q, k, v, g = _proj_sample(xs, mod, g_mix, w_in, cache.transpose(1, 0, 2),
                                           conv_w, conv_b, cln_g, cln_b, rope)
    ret_out, s_new = _ret_sample(q, k, v, g, state.reshape(n_seq, -1), _power_table(),
                                 rln_g, rln_b, n_pos, n_seq)
    y = _mlp_sample(xs, conv_out, ret_out, w_out, mod, g_ffn, w_ff1, w_ff2, g_final)
    new_cache = jnp.concatenate([cache[:, n_pos:], u.transpose(1, 0, 2)], axis=1)
    return y.transpose(1, 0, 2), new_cache, s_new.reshape(state.shape)


def kernel(x_prompt, x_sample, cache_conv, state_ret, c_prompt, c_sample, w_ada, b_ada, g_mix, w_in, conv_w, conv_b, conv_ln_g, conv_ln_b, ret_ln_g, ret_ln_b, w_out, g_ffn, w_ff1, w_ff2, g_final):
    w_in_b, w_out_b, w_ff1_b, w_ff2_b = _cast_weights(w_in[0], w_out[0], w_ff1[0], w_ff2[0])
    mod = _modulation(jnp.concatenate([c_sample, c_prompt], axis=0), w_ada[0], b_ada[0])
    params = (g_mix[0], w_in_b, conv_w[0], conv_b[0], conv_ln_g[0], conv_ln_b[0], ret_ln_g[0],
              ret_ln_b[0], w_out_b, g_ffn[0], w_ff1_b, w_ff2_b, g_final)
    y_p, conv_p, ret_p = _prompt_path(x_prompt, mod, x_sample.shape[0], *params)
    y_s, conv_s, ret_s = _sample_path(x_sample, mod, cache_conv[0], state_ret[0], PAST_LEN,
                                      *params)
    return (y_p, y_s, conv_p[None], ret_p[None], conv_s[None], ret_s[None])
```

```python
import functools

import jax
import jax.numpy as jnp
from jax import lax
from jax.experimental import pallas as pl
from jax.experimental.pallas import tpu as pltpu

D_MODEL = 1024
CONV_CH = 512
CONV_K = 31
RET_HEADS = 8
RET_W = 512
RET_DK = 64
RET_DV = 64
D_FF = 4 * D_MODEL
RET_CHUNK = 128
ROPE_THETA = 10000.0
EPS = 1e-6
N_MOD = 6
PAST_LEN = 16384

SUBLANES = 8
HIST_ROWS = 32
CONV_ROW_BLOCK = 32
PROMPT_TILE = 256
FFN_CHUNK = 512
SAMPLE_FFN_CHUNK = 1024
LAYER_PROGRAM = ("c cf cf cf cf cf cf cf cf "
                 "rrrr rfrf rfrf r "
                 "rfrf rfrf r")
PAIR_LANES = 2 * RET_DK
N_PAIRS = RET_HEADS // 2
VMEM_LIMIT = 56 * 1024 * 1024

F32 = jnp.float32
BF16 = jnp.bfloat16


def _log_decay(head):
    return jnp.log1p(-jnp.exp2(-5.0 - head))


def _rope_kernel(pos0, cos_ref, sin_ref):
    rows = cos_ref.shape[0]
    lane = lax.broadcasted_iota(jnp.int32, (rows, PAIR_LANES), 1)
    row = lax.broadcasted_iota(jnp.int32, (rows, PAIR_LANES), 0)
    half = RET_DK // 2
    freq = (lane % half).astype(F32)
    inv = jnp.power(jnp.full_like(freq, ROPE_THETA), -freq / half)
    ang = (row + pos0).astype(F32) * inv
    first = (lane % RET_DK) < half
    cos_ref[...] = jnp.cos(ang)
    sin_ref[...] = jnp.where(first, -jnp.sin(ang), jnp.sin(ang))


ROPE_BLOCK = 16


def _rope_blocked_kernel(pos0, cos_ref, sin_ref):
    rows = cos_ref.shape[0]
    n_blk = rows // ROPE_BLOCK
    half = RET_DK // 2

    def angles(n, step, start):
        lane = lax.broadcasted_iota(jnp.int32, (n, PAIR_LANES), 1)
        row = lax.broadcasted_iota(jnp.int32, (n, PAIR_LANES), 0)
        freq = (lane % half).astype(F32)
        inv = jnp.power(jnp.full_like(freq, ROPE_THETA), -freq / half)
        return (row * step + start).astype(F32) * inv

    a = angles(n_blk, ROPE_BLOCK, pos0)
    b = angles(ROPE_BLOCK, 1, 0)
    lane = lax.broadcasted_iota(jnp.int32, (ROPE_BLOCK, PAIR_LANES), 1)
    sign = jnp.where((lane % RET_DK) < half, -1.0, 1.0)
    cos_a, sin_a = jnp.cos(a), jnp.sin(a)
    cos_b, sin_b = jnp.cos(b), jnp.sin(b)
    cos_bs, sin_bs = cos_b * sign, sin_b * sign
    for blk in range(n_blk):
        ca, sa = cos_a[blk:blk + 1, :], sin_a[blk:blk + 1, :]
        rs = slice(blk * ROPE_BLOCK, (blk + 1) * ROPE_BLOCK)
        cos_ref[rs, :] = ca * cos_b - sa * sin_b
        sin_ref[rs, :] = sa * cos_bs + ca * sin_bs


def _rope_tables(rows, pos0):
    blocked = rows % ROPE_BLOCK == 0 and rows // ROPE_BLOCK >= SUBLANES
    return pl.pallas_call(
        functools.partial(_rope_blocked_kernel if blocked else _rope_kernel, pos0),
        out_shape=(jax.ShapeDtypeStruct((rows, PAIR_LANES), F32),
                   jax.ShapeDtypeStruct((rows, PAIR_LANES), F32)),
        name="rope_tables",
    )()


def _decay_kernel(chunk, dec_ref, qdec_ref, kdec_ref, sdec_ref):
    c = chunk
    shape = (N_PAIRS, c, 2 * c)
    col = lax.broadcasted_iota(jnp.int32, shape, 2)
    h = (2 * lax.broadcasted_iota(jnp.int32, shape, 0) + col // c).astype(F32)
    i = lax.broadcasted_iota(jnp.int32, shape, 1).astype(F32)
    j = (col % c).astype(F32)
    diff = i - j
    dec_ref[...] = jnp.where(diff >= 0, jnp.exp(_log_decay(h) * jnp.maximum(diff, 0.0)), 0.0)
    lane_h = (lax.broadcasted_iota(jnp.int32, (c, RET_W), 1) // RET_DK).astype(F32)
    idx = lax.broadcasted_iota(jnp.int32, (c, RET_W), 0).astype(F32)
    lg = _log_decay(lane_h)
    qdec_ref[...] = jnp.exp(lg * (idx + 1.0))
    kdec_ref[...] = jnp.exp(lg * (c - 1.0 - idx))
    p = lax.broadcasted_iota(jnp.int32, (N_PAIRS, PAIR_LANES, PAIR_LANES), 0)
    r = lax.broadcasted_iota(jnp.int32, (N_PAIRS, PAIR_LANES, PAIR_LANES), 1) // RET_DK
    q = lax.broadcasted_iota(jnp.int32, (N_PAIRS, PAIR_LANES, PAIR_LANES), 2) // RET_DK
    hh = (2 * p + r).astype(F32)
    sdec_ref[...] = jnp.where(r == q, jnp.exp(_log_decay(hh) * float(c)), 0.0)


def _decay_tables(chunk):
    return pl.pallas_call(
        functools.partial(_decay_kernel, chunk),
        out_shape=(jax.ShapeDtypeStruct((N_PAIRS, chunk, 2 * chunk), F32),
                   jax.ShapeDtypeStruct((chunk, RET_W), F32),
                   jax.ShapeDtypeStruct((chunk, RET_W), F32),
                   jax.ShapeDtypeStruct((N_PAIRS, PAIR_LANES, PAIR_LANES), F32)),
        name="decay_tables",
    )()


def _mod_kernel(c_ref, w_ref, b_ref, o_ref):
    c = c_ref[...]
    s = c * jax.nn.sigmoid(c)
    o_ref[...] = jnp.dot(s.astype(BF16), w_ref[...].astype(BF16),
                         preferred_element_type=F32) + b_ref[...]


def _modulation(c, w_ada, b_ada):
    n, d = c.shape
    width = w_ada.shape[1]
    blk = D_MODEL
    return pl.pallas_call(
        _mod_kernel,
        grid=(width // blk,),
        in_specs=[pl.BlockSpec((n, d), lambda i: (0, 0)),
                  pl.BlockSpec((d, blk), lambda i: (0, i)),
                  pl.BlockSpec((1, blk), lambda i: (0, i))],
        out_specs=pl.BlockSpec((n, blk), lambda i: (0, i)),
        out_shape=jax.ShapeDtypeStruct((n, width), F32),
        compiler_params=pltpu.CompilerParams(vmem_limit_bytes=VMEM_LIMIT),
        name="adaln_mod",
    )(c, w_ada, b_ada.reshape(1, width))


def _rmsnorm_mod(x, g, scale, shift):
    y = x * lax.rsqrt(jnp.mean(x * x, axis=-1, keepdims=True) + EPS)
    return (y * g) * (1.0 + scale) + shift


def _layernorm(x, g, b):
    mu = jnp.mean(x, axis=-1, keepdims=True)
    d = x - mu
    var = jnp.mean(d * d, axis=-1, keepdims=True)
    return d * lax.rsqrt(var + EPS) * g + b


def _silu(x):
    return x * jax.nn.sigmoid(x)


def _rotary_pair(x, cos, sin_signed, first_half):
    partner = jnp.where(first_half, pltpu.roll(x, PAIR_LANES - RET_DK // 2, 1),
                        pltpu.roll(x, RET_DK // 2, 1))
    return x * cos + partner * sin_signed


def _pair_groupnorm(o, lo, g, b):
    inv_n = 1.0 / RET_DV
    s_lo = jnp.sum(jnp.where(lo, o, 0.0), axis=-1, keepdims=True)
    s_hi = jnp.sum(jnp.where(lo, 0.0, o), axis=-1, keepdims=True)
    d = o - jnp.where(lo, s_lo, s_hi) * inv_n
    d2 = d * d
    v_lo = jnp.sum(jnp.where(lo, d2, 0.0), axis=-1, keepdims=True)
    v_hi = jnp.sum(jnp.where(lo, 0.0, d2), axis=-1, keepdims=True)
    var = jnp.where(lo, v_lo, v_hi) * inv_n
    return d * lax.rsqrt(var + EPS) * g + b


def _dot(a, b):
    return jnp.dot(a, b, preferred_element_type=F32)


def _dot_nt(a, b):
    return lax.dot_general(a, b, (((1,), (1,)), ((), ())), preferred_element_type=F32)


def _dot_tn(a, b):
    return lax.dot_general(a, b, (((0,), (0,)), ((), ())), preferred_element_type=F32)


def _layer_prompt_kernel(tile, nj, x_ref, mod_ref, gmix_ref, win_ref, convw_ref, convb_ref,
                         clng_ref, clnb_ref, rlng_ref, rlnb_ref, wout_ref,
                         cos_ref, sin_ref, dec_ref, qdec_ref, kdec_ref, sdec_ref,
                         mod2_ref, gffn_ref, w1_ref, w2_ref, gfin_ref,
                         y_ref, hist_ref, state_ref,
                         ubuf, ushift, sbuf, mixbuf, x1buf):
    g = pl.program_id(0)
    total = pl.num_programs(0) - 1
    j = lax.rem(jnp.minimum(g, total - 1), nj)
    live = g < total

    @pl.when(g == 0)
    def _():
        x1buf[...] = jnp.zeros((tile, D_MODEL), F32)

    @pl.when(jnp.logical_and(j == 0, live))
    def _():
        ubuf[0:HIST_ROWS, :] = jnp.zeros((HIST_ROWS, CONV_CH), F32)
        sbuf[...] = jnp.zeros(sbuf.shape, F32)

    x = x_ref[0]
    b_mix = lax.div(jnp.minimum(g, total - 1), nj)
    b_mlp = lax.div(jnp.maximum(g - 1, 0), nj)
    mods = lambda ref, b: [ref[pl.ds(b, 1), k * D_MODEL:(k + 1) * D_MODEL] for k in range(3)]
    shift, scale, gate = mods(mod_ref, b_mix)

    def conv_stages(u):
        ubuf[HIST_ROWS:HIST_ROWS + tile, :] = u
        off = HIST_ROWS - (CONV_K - 1)
        for r in range(SUBLANES):
            n = tile + SUBLANES * ((CONV_K - 1 - r) // SUBLANES)
            ushift[r, 0:n, :] = ubuf[off + r:off + r + n, :]
        ubuf[0:HIST_ROWS, :] = ubuf[tile:tile + HIST_ROWS, :]
        yield
        for rb in range(tile // CONV_ROW_BLOCK):
            r0 = rb * CONV_ROW_BLOCK
            dw = jnp.broadcast_to(convb_ref[...], (CONV_ROW_BLOCK, CONV_CH))
            for k in range(CONV_K):
                s0 = r0 + SUBLANES * (k // SUBLANES)
                wk = jnp.concatenate([convw_ref[k]] * (CONV_ROW_BLOCK // SUBLANES), axis=0)
                dw = dw + wk * ushift[k % SUBLANES, s0:s0 + CONV_ROW_BLOCK, :]
            conv_out = _silu(_layernorm(dw, clng_ref[...], clnb_ref[...]))
            mixbuf[r0:r0 + CONV_ROW_BLOCK, 0:CONV_CH] = conv_out.astype(BF16)
            yield

    def ret_stages(proj):
        ret_proj = []
        for k in range(4):
            ret_proj.append(proj(2 + k))
            yield
        q, kk, v, gt = ret_proj
        lane = lax.broadcasted_iota(jnp.int32, (1, PAIR_LANES), 1)
        lo = lane < RET_DK
        first_half = (lane % RET_DK) < (RET_DK // 2)
        lo_b = lo.astype(BF16)
        hi_b = (~lo).astype(BF16)
        pairs = range(N_PAIRS)
        halves = (pairs[:N_PAIRS // 2], pairs[N_PAIRS // 2:])
        lanes = [slice(p * PAIR_LANES, (p + 1) * PAIR_LANES) for p in pairs]
        for c in range(tile // RET_CHUNK):
            rs = slice(c * RET_CHUNK, (c + 1) * RET_CHUNK)
            cos = cos_ref[rs, :]
            sin = sin_ref[rs, :]
            qc, kc, vc, scores, outs = {}, {}, {}, {}, {}
            for group in halves:
                for p in group:
                    qc[p] = _rotary_pair(q[rs, lanes[p]], cos, sin, first_half)
                    kc[p] = _rotary_pair(kk[rs, lanes[p]], cos, sin, first_half)
                    vc[p] = v[rs, lanes[p]].astype(BF16)
                    kb = kc[p].astype(BF16)
                    k_cat = jnp.concatenate([kb * lo_b, kb * hi_b], axis=0)
                    scores[p] = _dot_nt(qc[p].astype(BF16), k_cat)
                yield
            for group in halves:
                for p in group:
                    state = sbuf[p]
                    lhs = jnp.concatenate([(scores[p] * dec_ref[p]).astype(BF16),
                                           (qc[p] * qdec_ref[:, lanes[p]]).astype(BF16)], axis=1)
                    rhs = jnp.concatenate([vc[p] * lo_b, vc[p] * hi_b, state.astype(BF16)],
                                          axis=0)
                    outs[p] = _dot(lhs, rhs)
                    kd = (kc[p] * kdec_ref[:, lanes[p]]).astype(BF16)
                    sbuf[p] = (state * sdec_ref[p]
                               + jnp.where(sdec_ref[p] > 0.0, _dot_tn(kd, vc[p]), 0.0))
                yield
            for p in pairs:
                on = _pair_groupnorm(outs[p], lo, rlng_ref[:, lanes[p]], rlnb_ref[:, lanes[p]])
                mixbuf[rs, CONV_CH + p * PAIR_LANES:CONV_CH + (p + 1) * PAIR_LANES] = (
                    on * _silu(gt[rs, lanes[p]])).astype(BF16)
            yield

    ffn = _ffn_stages(x1buf[...], mods(mod2_ref, b_mlp), gffn_ref, w1_ref, w2_ref, gfin_ref, y_ref)
    next(ffn)
    hb = _rmsnorm_mod(x, gmix_ref[...], scale, shift).astype(BF16)

    def proj(k):
        return _dot(hb, win_ref[:, k * CONV_CH:(k + 1) * CONV_CH])

    streams = {"f": ffn, "c": conv_stages(proj(0) * jax.nn.sigmoid(proj(1))),
               "r": ret_stages(proj)}
    for name in LAYER_PROGRAM.replace(" ", ""):
        next(streams[name], None)
    for stream in (streams["c"], streams["r"], streams["f"]):
        for _ in stream:
            pass

    mix = _dot(mixbuf[...], wout_ref[...])
    x1buf[...] = x + gate * mix

    @pl.when(jnp.logical_and(j == nj - 1, live))
    def _():
        hist_ref[0] = ubuf[0:HIST_ROWS, :]
        state_ref[0] = sbuf[...]


def _layer_prompt(x, mod, mod_row0, g_mix, w_in, conv_w, conv_b, cln_g, cln_b, rln_g, rln_b,
                  w_out, g_ffn, w1, w2, g_final, rope, decay, tile):
    bsz, seq, d = x.shape
    assert bsz % SUBLANES == 0 and mod_row0 % bsz == 0
    mod_blk = mod_row0 // bsz
    nj = seq // tile
    total = bsz * nj
    cos, sin = rope
    dec, qdec, kdec, sdec = decay
    mix_tile = lambda g: jnp.minimum(g, total - 1)
    ffn_tile = lambda g: jnp.maximum(g - 1, 0)
    resident = lambda shape: pl.BlockSpec(shape, lambda g: (0,) * len(shape),
                                          pipeline_mode=pl.Buffered(1))
    row = lambda a: a.reshape(1, -1)
    conv_w_rows = jnp.broadcast_to(conv_w[:, None, :], (CONV_K, SUBLANES, CONV_CH))
    rope_spec = pl.BlockSpec((tile, PAIR_LANES), lambda g: (mix_tile(g) % nj, 0))
    return pl.pallas_call(
        functools.partial(_layer_prompt_kernel, tile, nj),
        grid=(total + 1,),
        in_specs=[
            pl.BlockSpec((1, tile, d), lambda g: (mix_tile(g) // nj, mix_tile(g) % nj, 0)),
            pl.BlockSpec((bsz, 3 * d), lambda g: (mod_blk, 0), pipeline_mode=pl.Buffered(1)),
            resident((1, d)),
            resident(w_in.shape),
            resident((CONV_K, SUBLANES, CONV_CH)),
            resident((1, CONV_CH)),
            resident((1, CONV_CH)),
            resident((1, CONV_CH)),
            resident((1, RET_W)),
            resident((1, RET_W)),
            resident(w_out.shape),
            rope_spec,
            rope_spec,
            resident(dec.shape),
            resident(qdec.shape),
            resident(kdec.shape),
            resident(sdec.shape),
            pl.BlockSpec((bsz, 3 * d), lambda g: (mod_blk, 1), pipeline_mode=pl.Buffered(1)),
            resident((1, d)),
            resident(w1.shape),
            resident(w2.shape),
            resident((1, d)),
        ],
        out_specs=[
            pl.BlockSpec((1, tile, d), lambda g: (ffn_tile(g) // nj, ffn_tile(g) % nj, 0)),
            pl.BlockSpec((1, HIST_ROWS, CONV_CH), lambda g: (mix_tile(g) // nj, 0, 0)),
            pl.BlockSpec((1, N_PAIRS, PAIR_LANES, PAIR_LANES),
                         lambda g: (mix_tile(g) // nj, 0, 0, 0)),
        ],
        out_shape=(
            jax.ShapeDtypeStruct((bsz, seq, d), F32),
            jax.ShapeDtypeStruct((bsz, HIST_ROWS, CONV_CH), F32),
            jax.ShapeDtypeStruct((bsz, N_PAIRS, PAIR_LANES, PAIR_LANES), F32),
        ),
        scratch_shapes=[
            pltpu.VMEM((HIST_ROWS + tile, CONV_CH), F32),
            pltpu.VMEM((SUBLANES, HIST_ROWS + tile, CONV_CH), F32),
            pltpu.VMEM((N_PAIRS, PAIR_LANES, PAIR_LANES), F32),
            pltpu.VMEM((tile, D_MODEL), BF16),
            pltpu.VMEM((tile, D_MODEL), F32),
        ],
        compiler_params=pltpu.CompilerParams(
            dimension_semantics=("arbitrary",), vmem_limit_bytes=VMEM_LIMIT),
        name="layer_prompt",
    )(x, mod, row(g_mix), w_in, conv_w_rows, row(conv_b), row(cln_g), row(cln_b),
      row(rln_g), row(rln_b), w_out, cos, sin, dec, qdec, kdec, sdec,
      mod, row(g_ffn), w1, w2, row(g_final))


def _ffn_stages(x, mods, gffn_ref, w1_ref, w2_ref, gfin_ref, y_ref):
    shift, scale, gate = mods
    hb = _rmsnorm_mod(x, gffn_ref[...], scale, shift).astype(BF16)
    yield
    half = D_MODEL // 2
    ff = [None, None]
    for c in range(D_FF // (2 * FFN_CHUNK)):
        acts = []
        for h in range(2):
            c0 = (2 * c + h) * FFN_CHUNK
            a = jnp.maximum(_dot(hb, w1_ref[:, c0:c0 + FFN_CHUNK]), 0.0)
            acts.append((a * a).astype(BF16))
            yield
        act = jnp.concatenate(acts, axis=1)
        for n in range(2):
            part = _dot(act, w2_ref[2 * c * FFN_CHUNK:2 * (c + 1) * FFN_CHUNK,
                                    n * half:(n + 1) * half])
            ff[n] = part if ff[n] is None else ff[n] + part
            yield
    z = x + gate * jnp.concatenate(ff, axis=1)
    y_ref[0] = z * lax.rsqrt(jnp.mean(z * z, axis=-1, keepdims=True) + EPS) * gfin_ref[...]


def _mod_cols(mod_ref, n_seq, k):
    return mod_ref[0:n_seq, k * D_MODEL:(k + 1) * D_MODEL]


def _mlp_sample_kernel(n_pos, n_seq, x_ref, conv_ref, ret_ref, wout_ref, mod_ref,
                       gffn_ref, w1_ref, w2_ref, gfin_ref, y_ref, x1buf, hbuf, acc):
    c = pl.program_id(0)
    rows = lambda i: slice(i * n_seq, (i + 1) * n_seq)

    @pl.when(c == 0)
    def _():
        mix = (_dot(conv_ref[...], wout_ref[0:CONV_CH, :])
               + _dot(ret_ref[...], wout_ref[CONV_CH:CONV_CH + RET_W, :]))
        for i in range(n_pos):
            x1 = x_ref[i] + _mod_cols(mod_ref, n_seq, 2) * mix[rows(i)]
            x1buf[rows(i), :] = x1
            hbuf[rows(i), :] = _rmsnorm_mod(x1, gffn_ref[...], _mod_cols(mod_ref, n_seq, 4),
                                            _mod_cols(mod_ref, n_seq, 3)).astype(BF16)

    a = jnp.maximum(_dot(hbuf[...], w1_ref[...]), 0.0)
    part = _dot((a * a).astype(BF16), w2_ref[...])

    @pl.when(c == 0)
    def _():
        acc[...] = part

    @pl.when(c > 0)
    def _():
        acc[...] += part

    @pl.when(c == pl.num_programs(0) - 1)
    def _():
        for i in range(n_pos):
            z = x1buf[rows(i), :] + _mod_cols(mod_ref, n_seq, 5) * acc[rows(i), :]
            y_ref[i] = (z * lax.rsqrt(jnp.mean(z * z, axis=-1, keepdims=True) + EPS)
                        * gfin_ref[...])


def _mlp_sample(x, conv_out, ret_out, w_out, mod, g_ffn, w1, w2, g_final):
    n_pos, n_seq, d = x.shape
    n_tok = n_pos * n_seq
    n_chunks = D_FF // SAMPLE_FFN_CHUNK
    whole = lambda shape: pl.BlockSpec(shape, lambda c: (0,) * len(shape),
                                       pipeline_mode=pl.Buffered(1))
    return pl.pallas_call(
        functools.partial(_mlp_sample_kernel, n_pos, n_seq),
        grid=(n_chunks,),
        in_specs=[whole(x.shape), whole(conv_out.shape), whole(ret_out.shape),
                  whole(w_out.shape), whole(mod.shape), whole((1, d)),
                  pl.BlockSpec((d, SAMPLE_FFN_CHUNK), lambda c: (0, c)),
                  pl.BlockSpec((SAMPLE_FFN_CHUNK, d), lambda c: (c, 0)),
                  whole((1, d))],
        out_specs=pl.BlockSpec(x.shape, lambda c: (0, 0, 0)),
        out_shape=jax.ShapeDtypeStruct(x.shape, F32),
        scratch_shapes=[pltpu.VMEM((n_tok, d), F32), pltpu.VMEM((n_tok, d), BF16),
                        pltpu.VMEM((n_tok, d), F32)],
        compiler_params=pltpu.CompilerParams(
            dimension_semantics=("arbitrary",), vmem_limit_bytes=VMEM_LIMIT),
        name="mlp_sample",
    )(x, conv_out, ret_out, w_out, mod, g_ffn.reshape(1, d), w1, w2, g_final.reshape(1, d))


def _power_table_kernel(o_ref):
    lane_h = (lax.broadcasted_iota(jnp.int32, o_ref.shape, 1) // RET_DK).astype(F32)
    n = lax.broadcasted_iota(jnp.int32, o_ref.shape, 0).astype(F32)
    o_ref[...] = jnp.exp(_log_decay(lane_h) * n)


def _power_table():
    return pl.pallas_call(_power_table_kernel,
                          out_shape=jax.ShapeDtypeStruct((8, RET_W), F32),
                          name="decay_powers")()


def _proj_sample_kernel(n_pos, n_seq, x_ref, mod_ref, gmix_ref, win_ref, cache_ref, convw_ref,
                        convb_ref, clng_ref, clnb_ref, cos_ref, sin_ref,
                        hist_ref, conv_ref, q_ref, k_ref, v_ref, g_ref):
    shift, scale = _mod_cols(mod_ref, n_seq, 0), _mod_cols(mod_ref, n_seq, 1)
    hb = jnp.concatenate(
        [_rmsnorm_mod(x_ref[i], gmix_ref[...], scale, shift).astype(BF16)
         for i in range(n_pos)], axis=0)

    def proj(k):
        return _dot(hb, win_ref[:, k * CONV_CH:(k + 1) * CONV_CH])

    u = proj(0) * jax.nn.sigmoid(proj(1))
    n_hist = CONV_K - 1
    rows = lambda a, i: a[i * n_seq:(i + 1) * n_seq]
    outs = []
    for m in range(n_hist - n_pos):
        hist_ref[m] = cache_ref[m + n_pos]
    for i in range(n_pos):
        hist_ref[n_hist - n_pos + i] = rows(u, i)
        dw = jnp.broadcast_to(convb_ref[...], (n_seq, CONV_CH))
        for k in range(CONV_K):
            m = i + k
            src = cache_ref[m] if m < n_hist else rows(u, m - n_hist)
            dw = dw + convw_ref[k:k + 1, :] * src
        outs.append(_silu(_layernorm(dw, clng_ref[...], clnb_ref[...])).astype(BF16))
    conv_ref[...] = jnp.concatenate(outs, axis=0)

    lane = lax.broadcasted_iota(jnp.int32, (1, PAIR_LANES), 1)
    first_half = (lane % RET_DK) < (RET_DK // 2)
    q = proj(2)
    kk = proj(3)
    for i in range(n_pos):
        rs = slice(i * n_seq, (i + 1) * n_seq)
        cos = cos_ref[i:i + 1, :]
        sin = sin_ref[i:i + 1, :]
        for p in range(N_PAIRS):
            sl = slice(p * PAIR_LANES, (p + 1) * PAIR_LANES)
            q_ref[rs, sl] = _rotary_pair(q[rs, sl], cos, sin, first_half)
            k_ref[rs, sl] = _rotary_pair(kk[rs, sl], cos, sin, first_half)
    v_ref[...] = proj(4)
    g_ref[...] = proj(5)


def _proj_sample(x, mod, g_mix, w_in, cache_t, conv_w, conv_b, cln_g, cln_b, rope):
    n_pos, n_seq, d = x.shape
    n_tok = n_pos * n_seq
    cos, sin = rope
    row = lambda a: a.reshape(1, -1)
    tok = lambda dt: jax.ShapeDtypeStruct((n_tok, RET_W), dt)
    return pl.pallas_call(
        functools.partial(_proj_sample_kernel, n_pos, n_seq),
        out_shape=(jax.ShapeDtypeStruct(cache_t.shape, F32), tok(BF16),
                   tok(F32), tok(F32), tok(F32), tok(F32)),
        compiler_params=pltpu.CompilerParams(vmem_limit_bytes=VMEM_LIMIT),
        name="proj_sample",
    )(x, mod, row(g_mix), w_in, cache_t, conv_w, row(conv_b), row(cln_g), row(cln_b), cos, sin)


def _ret_sample_kernel(n_pos, n_seq, q_ref, k_ref, v_ref, g_ref, st_ref, pw_ref, rlng_ref,
                       rlnb_ref, out_ref, snew_ref, sb_buf, qd_buf, kd_buf, v_buf):
    lane = lax.broadcasted_iota(jnp.int32, (1, PAIR_LANES), 1)
    lo = lane < RET_DK
    rows = lambda a, i: a[i * n_seq:(i + 1) * n_seq]
    cols = lambda a, i: a[:, i * n_seq:(i + 1) * n_seq]
    pw = pw_ref[...]
    q = q_ref[...]
    k = k_ref[...]
    qb = [rows(q, i).astype(BF16) for i in range(n_pos)]
    kb = [rows(k, i).astype(BF16) for i in range(n_pos)]
    vb = [rows(v_ref[...], i).astype(BF16) for i in range(n_pos)]

    o = []
    for i in range(n_pos):
        acc = jnp.zeros((n_seq, PAIR_LANES), F32)
        for j in range(i + 1):
            prod = qb[i].astype(F32) * kb[j].astype(F32)
            s_lo = jnp.sum(jnp.where(lo, prod, 0.0), axis=-1, keepdims=True)
            s_hi = jnp.sum(jnp.where(lo, 0.0, prod), axis=-1, keepdims=True)
            score = jnp.where(lo, s_lo, s_hi) * pw[i - j:i - j + 1, :]
            acc = acc + score.astype(BF16).astype(F32) * vb[j].astype(F32)
        o.append(acc)

    rounded = lambda x: x.astype(BF16).astype(F32)
    qd_buf[...] = jnp.concatenate(
        [rounded(rows(q, i) * pw[i + 1:i + 2, :]) for i in range(n_pos)], axis=0).T
    kd_buf[...] = jnp.concatenate(
        [rounded(rows(k, j) * pw[n_pos - 1 - j:n_pos - j, :]) for j in range(n_pos)], axis=0).T
    v_buf[...] = jnp.concatenate([vb[j].astype(F32) for j in range(n_pos)], axis=0).T
    tile_at = lambda r: pl.ds(pl.multiple_of(r * RET_DV, RET_DV), RET_DV)
    pos = lambda i: slice(i * n_seq, (i + 1) * n_seq)

    for a in range(2):
        gam = pw[n_pos:n_pos + 1, a * RET_DK:a * RET_DK + 1]
        v_rows = slice(a * RET_DV, (a + 1) * RET_DV)

        def update(grp, carry, a=a, gam=gam, v_rows=v_rows):
            r0 = pl.multiple_of(a * RET_DK + grp * SUBLANES, SUBLANES)
            kd_rows = kd_buf[pl.ds(r0, SUBLANES), :]
            for s in range(SUBLANES):
                r = r0 + s
                s_old = st_ref[tile_at(r), :]
                sb_buf[tile_at(r), :] = rounded(s_old)
                s_new = s_old * gam
                for j in range(n_pos):
                    s_new = s_new + kd_rows[s:s + 1, pos(j)] * v_buf[v_rows, pos(j)]
                snew_ref[tile_at(r), :] = s_new
            return carry

        lax.fori_loop(0, RET_DK // SUBLANES, update, 0)

    for i in range(n_pos):
        heads = []
        for a in range(2):
            def cross(grp, acc, a=a, i=i):
                r0 = pl.multiple_of(a * RET_DK + grp * SUBLANES, SUBLANES)
                qd_rows = qd_buf[pl.ds(r0, SUBLANES), pos(i)]
                for s in range(SUBLANES):
                    acc = acc + qd_rows[s:s + 1, :] * sb_buf[tile_at(r0 + s), :]
                return acc

            heads.append(lax.fori_loop(0, RET_DK // SUBLANES, cross,
                                       jnp.zeros((RET_DV, n_seq), F32)))
        oi = o[i] + jnp.concatenate(heads, axis=0).T
        on = _pair_groupnorm(oi, lo, rlng_ref[...], rlnb_ref[...])
        out_ref[i * n_seq:(i + 1) * n_seq, :] = (on * _silu(rows(g_ref[...], i))).astype(BF16)


def _ret_sample(q, k, v, g, state_t, pw, rln_g, rln_b, n_pos, n_seq):
    n_tok = n_pos * n_seq
    pair_state = 2 * RET_DK * RET_DV
    slab = pl.BlockSpec((n_tok, PAIR_LANES), lambda p: (0, p))
    state_spec = pl.BlockSpec((pair_state, n_seq), lambda p: (p, 0))
    return pl.pallas_call(
        functools.partial(_ret_sample_kernel, n_pos, n_seq),
        grid=(N_PAIRS,),
        in_specs=[slab, slab, slab, slab, state_spec,
                  pl.BlockSpec((8, PAIR_LANES), lambda p: (0, p)),
                  pl.BlockSpec((1, PAIR_LANES), lambda p: (0, p)),
                  pl.BlockSpec((1, PAIR_LANES), lambda p: (0, p))],
        out_specs=[slab, state_spec],
        out_shape=(jax.ShapeDtypeStruct((n_tok, RET_W), BF16),
                   jax.ShapeDtypeStruct(state_t.shape, F32)),
        scratch_shapes=[pltpu.VMEM((pair_state, n_seq), F32)]
        + [pltpu.VMEM((PAIR_LANES, n_tok), F32)] * 3,
        compiler_params=pltpu.CompilerParams(
            dimension_semantics=("arbitrary",), vmem_limit_bytes=VMEM_LIMIT),
        name="ret_sample",
    )(q, k, v, g, state_t, pw, rln_g.reshape(1, RET_W), rln_b.reshape(1, RET_W))


def _unpack_state(s2):
    blocks = []
    for p in range(N_PAIRS):
        for a in range(2):
            sl = slice(a * RET_DK, (a + 1) * RET_DK)
            blocks.append(s2[:, p, sl, sl])
    return jnp.stack(blocks, axis=1)


def _cast_weights(w_in, w_out, w_ff1, w_ff2):
    col_scale = jnp.ones((w_in.shape[1],), F32).at[2 * CONV_CH:2 * CONV_CH + RET_W].set(RET_DK ** -0.5)
    return ((w_in * col_scale).astype(BF16), w_out.astype(BF16), w_ff1.astype(BF16),
            w_ff2.astype(BF16))


def _prompt_path(x, mod, mod_row0, g_mix, w_in, conv_w, conv_b, cln_g, cln_b, rln_g, rln_b,
                 w_out, g_ffn, w_ff1, w_ff2, g_final, tile=PROMPT_TILE):
    seq = x.shape[1]
    rope = _rope_tables(seq, 0)
    decay = _decay_tables(RET_CHUNK)
    y, hist, s2 = _layer_prompt(x, mod, mod_row0, g_mix, w_in, conv_w, conv_b, cln_g, cln_b,
                                rln_g, rln_b, w_out, g_ffn, w_ff1, w_ff2, g_final, rope, decay,
                                tile)
    return y, hist[:, HIST_ROWS - (CONV_K - 1):], _unpack_state(s2)


def _sample_path(x, mod, cache, state, pos0, g_mix, w_in, conv_w, conv_b, cln_g, cln_b,
                 rln_g, rln_b, w_out, g_ffn, w_ff1, w_ff2, g_final):
    n_seq, n_pos, d = x.shape
    xs = x.transpose(1, 0, 2)
    rope = _rope_tables(8, pos0)
    hist_t, conv_out, q, k, v, g = _proj_sample(xs, mod, g_mix, w_in, cache.transpose(1, 0, 2),
                                                conv_w, conv_b, cln_g, cln_b, rope)
    ret_out, s_new_t = _ret_sample(q, k, v, g, state.reshape(n_seq, -1).T, _power_table(),
                                   rln_g, rln_b, n_pos, n_seq)
    s_new = s_new_t.T
    y = _mlp_sample(xs, conv_out, ret_out, w_out, mod, g_ffn, w_ff1, w_ff2, g_final)
    return y.transpose(1, 0, 2), hist_t.transpose(1, 0, 2), s_new.reshape(state.shape)


def kernel(x_prompt, x_sample, cache_conv, state_ret, c_prompt, c_sample, w_ada, b_ada, g_mix, w_in, conv_w, conv_b, conv_ln_g, conv_ln_b, ret_ln_g, ret_ln_b, w_out, g_ffn, w_ff1, w_ff2, g_final):
    w_in_b, w_out_b, w_ff1_b, w_ff2_b = _cast_weights(w_in[0], w_out[0], w_ff1[0], w_ff2[0])
    mod = _modulation(jnp.concatenate([c_sample, c_prompt], axis=0), w_ada[0], b_ada[0])
    params = (g_mix[0], w_in_b, conv_w[0], conv_b[0], conv_ln_g[0], conv_ln_b[0], ret_ln_g[0],
              ret_ln_b[0], w_out_b, g_ffn[0], w_ff1_b, w_ff2_b, g_final)
    y_p, conv_p, ret_p = _prompt_path(x_prompt, mod, x_sample.shape[0], *params)
    y_s, conv_s, ret_s = _sample_path(x_sample, mod, cache_conv[0], state_ret[0], PAST_LEN,
                                      *params)
    return (y_p, y_s, conv_p[None], ret_p[None], conv_s[None], ret_s[None])
```

```python
import functools

import jax
import jax.numpy as jnp
from jax import lax
from jax.experimental import pallas as pl
from jax.experimental.pallas import tpu as pltpu

D_MODEL = 1024
CONV_CH = 512
CONV_K = 31
RET_HEADS = 8
RET_W = 512
RET_DK = 64
RET_DV = 64
D_FF = 4 * D_MODEL
RET_CHUNK = 128
ROPE_THETA = 10000.0
EPS = 1e-6
N_MOD = 6
PAST_LEN = 16384

SUBLANES = 8
HIST_ROWS = 32
CONV_ROW_BLOCK = 32
PROMPT_TILE = 256
FFN_CHUNK = 512
SAMPLE_FFN_CHUNK = 1024
WEIGHT_STAGE_ROWS = 256
LAYER_PROGRAM = ("c cf cf cf cf cf cf cf cf "
                 "rrrr rfrf rfrf r "
                 "rfrf rfrf r")
PAIR_LANES = 2 * RET_DK
N_PAIRS = RET_HEADS // 2
VMEM_LIMIT = 56 * 1024 * 1024

F32 = jnp.float32
BF16 = jnp.bfloat16


def _log_decay(head):
    return jnp.log1p(-jnp.exp2(-5.0 - head))


def _rope_kernel(pos0, cos_ref, sin_ref):
    rows = cos_ref.shape[0]
    lane = lax.broadcasted_iota(jnp.int32, (rows, PAIR_LANES), 1)
    row = lax.broadcasted_iota(jnp.int32, (rows, PAIR_LANES), 0)
    half = RET_DK // 2
    freq = (lane % half).astype(F32)
    inv = jnp.power(jnp.full_like(freq, ROPE_THETA), -freq / half)
    ang = (row + pos0).astype(F32) * inv
    first = (lane % RET_DK) < half
    cos_ref[...] = jnp.cos(ang)
    sin_ref[...] = jnp.where(first, -jnp.sin(ang), jnp.sin(ang))


ROPE_BLOCK = 16


def _rope_blocked_kernel(pos0, cos_ref, sin_ref):
    rows = cos_ref.shape[0]
    n_blk = rows // ROPE_BLOCK
    half = RET_DK // 2

    def angles(n, step, start):
        lane = lax.broadcasted_iota(jnp.int32, (n, PAIR_LANES), 1)
        row = lax.broadcasted_iota(jnp.int32, (n, PAIR_LANES), 0)
        freq = (lane % half).astype(F32)
        inv = jnp.power(jnp.full_like(freq, ROPE_THETA), -freq / half)
        return (row * step + start).astype(F32) * inv

    a = angles(n_blk, ROPE_BLOCK, pos0)
    b = angles(ROPE_BLOCK, 1, 0)
    lane = lax.broadcasted_iota(jnp.int32, (ROPE_BLOCK, PAIR_LANES), 1)
    sign = jnp.where((lane % RET_DK) < half, -1.0, 1.0)
    cos_a, sin_a = jnp.cos(a), jnp.sin(a)
    cos_b, sin_b = jnp.cos(b), jnp.sin(b)
    cos_bs, sin_bs = cos_b * sign, sin_b * sign
    for blk in range(n_blk):
        ca, sa = cos_a[blk:blk + 1, :], sin_a[blk:blk + 1, :]
        rs = slice(blk * ROPE_BLOCK, (blk + 1) * ROPE_BLOCK)
        cos_ref[rs, :] = ca * cos_b - sa * sin_b
        sin_ref[rs, :] = sa * cos_bs + ca * sin_bs


def _rope_tables(rows, pos0):
    blocked = rows % ROPE_BLOCK == 0 and rows // ROPE_BLOCK >= SUBLANES
    return pl.pallas_call(
        functools.partial(_rope_blocked_kernel if blocked else _rope_kernel, pos0),
        out_shape=(jax.ShapeDtypeStruct((rows, PAIR_LANES), F32),
                   jax.ShapeDtypeStruct((rows, PAIR_LANES), F32)),
        name="rope_tables",
    )()


def _decay_kernel(chunk, dec_ref, qdec_ref, kdec_ref, sdec_ref):
    c = chunk
    shape = (N_PAIRS, c, 2 * c)
    col = lax.broadcasted_iota(jnp.int32, shape, 2)
    h = (2 * lax.broadcasted_iota(jnp.int32, shape, 0) + col // c).astype(F32)
    i = lax.broadcasted_iota(jnp.int32, shape, 1).astype(F32)
    j = (col % c).astype(F32)
    diff = i - j
    dec_ref[...] = jnp.where(diff >= 0, jnp.exp(_log_decay(h) * jnp.maximum(diff, 0.0)), 0.0)
    lane_h = (lax.broadcasted_iota(jnp.int32, (c, RET_W), 1) // RET_DK).astype(F32)
    idx = lax.broadcasted_iota(jnp.int32, (c, RET_W), 0).astype(F32)
    lg = _log_decay(lane_h)
    qdec_ref[...] = jnp.exp(lg * (idx + 1.0))
    kdec_ref[...] = jnp.exp(lg * (c - 1.0 - idx))
    p = lax.broadcasted_iota(jnp.int32, (N_PAIRS, PAIR_LANES, PAIR_LANES), 0)
    r = lax.broadcasted_iota(jnp.int32, (N_PAIRS, PAIR_LANES, PAIR_LANES), 1) // RET_DK
    q = lax.broadcasted_iota(jnp.int32, (N_PAIRS, PAIR_LANES, PAIR_LANES), 2) // RET_DK
    hh = (2 * p + r).astype(F32)
    sdec_ref[...] = jnp.where(r == q, jnp.exp(_log_decay(hh) * float(c)), 0.0)


def _decay_tables(chunk):
    return pl.pallas_call(
        functools.partial(_decay_kernel, chunk),
        out_shape=(jax.ShapeDtypeStruct((N_PAIRS, chunk, 2 * chunk), F32),
                   jax.ShapeDtypeStruct((chunk, RET_W), F32),
                   jax.ShapeDtypeStruct((chunk, RET_W), F32),
                   jax.ShapeDtypeStruct((N_PAIRS, PAIR_LANES, PAIR_LANES), F32)),
        name="decay_tables",
    )()


def _mod_kernel(c_ref, w_ref, b_ref, o_ref):
    c = c_ref[...]
    s = c * jax.nn.sigmoid(c)
    o_ref[...] = jnp.dot(s.astype(BF16), w_ref[...].astype(BF16),
                         preferred_element_type=F32) + b_ref[...]


def _modulation(c, w_ada, b_ada):
    n, d = c.shape
    width = w_ada.shape[1]
    blk = D_MODEL
    return pl.pallas_call(
        _mod_kernel,
        grid=(width // blk,),
        in_specs=[pl.BlockSpec((n, d), lambda i: (0, 0)),
                  pl.BlockSpec((d, blk), lambda i: (0, i)),
                  pl.BlockSpec((1, blk), lambda i: (0, i))],
        out_specs=pl.BlockSpec((n, blk), lambda i: (0, i)),
        out_shape=jax.ShapeDtypeStruct((n, width), F32),
        compiler_params=pltpu.CompilerParams(vmem_limit_bytes=VMEM_LIMIT),
        name="adaln_mod",
    )(c, w_ada, b_ada.reshape(1, width))


def _rmsnorm_mod(x, g, scale, shift):
    y = x * lax.rsqrt(jnp.mean(x * x, axis=-1, keepdims=True) + EPS)
    return (y * g) * (1.0 + scale) + shift


def _layernorm(x, g, b):
    mu = jnp.mean(x, axis=-1, keepdims=True)
    d = x - mu
    var = jnp.mean(d * d, axis=-1, keepdims=True)
    return d * lax.rsqrt(var + EPS) * g + b


def _silu(x):
    return x * jax.nn.sigmoid(x)


def _rotary_pair(x, cos, sin_signed, first_half):
    partner = jnp.where(first_half, pltpu.roll(x, PAIR_LANES - RET_DK // 2, 1),
                        pltpu.roll(x, RET_DK // 2, 1))
    return x * cos + partner * sin_signed


def _pair_groupnorm(o, lo, g, b):
    inv_n = 1.0 / RET_DV
    s_lo = jnp.sum(jnp.where(lo, o, 0.0), axis=-1, keepdims=True)
    s_hi = jnp.sum(jnp.where(lo, 0.0, o), axis=-1, keepdims=True)
    d = o - jnp.where(lo, s_lo, s_hi) * inv_n
    d2 = d * d
    v_lo = jnp.sum(jnp.where(lo, d2, 0.0), axis=-1, keepdims=True)
    v_hi = jnp.sum(jnp.where(lo, 0.0, d2), axis=-1, keepdims=True)
    var = jnp.where(lo, v_lo, v_hi) * inv_n
    return d * lax.rsqrt(var + EPS) * g + b


def _dot(a, b):
    return jnp.dot(a, b, preferred_element_type=F32)


def _dot_nt(a, b):
    return lax.dot_general(a, b, (((1,), (1,)), ((), ())), preferred_element_type=F32)


def _dot_tn(a, b):
    return lax.dot_general(a, b, (((0,), (0,)), ((), ())), preferred_element_type=F32)


def _stream_cast_weights(pairs, stage, sems):
    rows = stage.shape[1]
    chunks = [(src, dst, r0) for src, dst in pairs for r0 in range(0, src.shape[0], rows)]

    def copy(c):
        src, _, r0 = chunks[c]
        slot = c % 2
        return pltpu.make_async_copy(src.at[pl.ds(r0, rows), :],
                                     stage.at[slot, :, pl.ds(0, src.shape[1])], sems.at[slot])

    copy(0).start()
    for c, (src, dst, r0) in enumerate(chunks):
        if c + 1 < len(chunks):
            copy(c + 1).start()
        copy(c).wait()
        dst[r0:r0 + rows, :] = stage[c % 2, :, 0:src.shape[1]].astype(BF16)


def _layer_prompt_kernel(tile, nj, x_ref, mod_ref, gmix_ref, win_hbm, convw_ref, convb_ref,
                         clng_ref, clnb_ref, rlng_ref, rlnb_ref, wout_hbm,
                         cos_ref, sin_ref, dec_ref, qdec_ref, kdec_ref, sdec_ref,
                         mod2_ref, gffn_ref, w1_hbm, w2_hbm, gfin_ref,
                         y_ref, hist_ref, state_ref,
                         ubuf, ushift, sbuf, mixbuf, x1buf,
                         win_ref, wout_ref, w1_ref, w2_ref, stage, sems):
    g = pl.program_id(0)
    total = pl.num_programs(0) - 1
    j = lax.rem(jnp.minimum(g, total - 1), nj)
    live = g < total

    @pl.when(g == 0)
    def _():
        _stream_cast_weights([(win_hbm, win_ref), (wout_hbm, wout_ref), (w1_hbm, w1_ref),
                              (w2_hbm, w2_ref)], stage, sems)
        x1buf[...] = jnp.zeros((tile, D_MODEL), F32)

    @pl.when(jnp.logical_and(j == 0, live))
    def _():
        ubuf[0:HIST_ROWS, :] = jnp.zeros((HIST_ROWS, CONV_CH), F32)
        sbuf[...] = jnp.zeros(sbuf.shape, F32)

    x = x_ref[0]
    b_mix = lax.div(jnp.minimum(g, total - 1), nj)
    b_mlp = lax.div(jnp.maximum(g - 1, 0), nj)
    mods = lambda ref, b: [ref[pl.ds(b, 1), k * D_MODEL:(k + 1) * D_MODEL] for k in range(3)]
    shift, scale, gate = mods(mod_ref, b_mix)

    def conv_stages(u):
        ubuf[HIST_ROWS:HIST_ROWS + tile, :] = u
        off = HIST_ROWS - (CONV_K - 1)
        for r in range(SUBLANES):
            n = tile + SUBLANES * ((CONV_K - 1 - r) // SUBLANES)
            ushift[r, 0:n, :] = ubuf[off + r:off + r + n, :]
        ubuf[0:HIST_ROWS, :] = ubuf[tile:tile + HIST_ROWS, :]
        yield
        for rb in range(tile // CONV_ROW_BLOCK):
            r0 = rb * CONV_ROW_BLOCK
            dw = jnp.broadcast_to(convb_ref[...], (CONV_ROW_BLOCK, CONV_CH))
            for k in range(CONV_K):
                s0 = r0 + SUBLANES * (k // SUBLANES)
                wk = jnp.concatenate([convw_ref[k]] * (CONV_ROW_BLOCK // SUBLANES), axis=0)
                dw = dw + wk * ushift[k % SUBLANES, s0:s0 + CONV_ROW_BLOCK, :]
            conv_out = _silu(_layernorm(dw, clng_ref[...], clnb_ref[...]))
            mixbuf[r0:r0 + CONV_ROW_BLOCK, 0:CONV_CH] = conv_out.astype(BF16)
            yield

    def ret_stages(proj):
        ret_proj = []
        for k in range(4):
            ret_proj.append(proj(2 + k))
            yield
        q, kk, v, gt = ret_proj
        q = q * (RET_DK ** -0.5)
        lane = lax.broadcasted_iota(jnp.int32, (1, PAIR_LANES), 1)
        lo = lane < RET_DK
        first_half = (lane % RET_DK) < (RET_DK // 2)
        lo_b = lo.astype(BF16)
        hi_b = (~lo).astype(BF16)
        pairs = range(N_PAIRS)
        halves = (pairs[:N_PAIRS // 2], pairs[N_PAIRS // 2:])
        lanes = [slice(p * PAIR_LANES, (p + 1) * PAIR_LANES) for p in pairs]
        for c in range(tile // RET_CHUNK):
            rs = slice(c * RET_CHUNK, (c + 1) * RET_CHUNK)
            cos = cos_ref[rs, :]
            sin = sin_ref[rs, :]
            qc, kc, vc, scores, outs = {}, {}, {}, {}, {}
            for group in halves:
                for p in group:
                    qc[p] = _rotary_pair(q[rs, lanes[p]], cos, sin, first_half)
                    kc[p] = _rotary_pair(kk[rs, lanes[p]], cos, sin, first_half)
                    vc[p] = v[rs, lanes[p]].astype(BF16)
                    kb = kc[p].astype(BF16)
                    k_cat = jnp.concatenate([kb * lo_b, kb * hi_b], axis=0)
                    scores[p] = _dot_nt(qc[p].astype(BF16), k_cat)
                yield
            for group in halves:
                for p in group:
                    state = sbuf[p]
                    lhs = jnp.concatenate([(scores[p] * dec_ref[p]).astype(BF16),
                                           (qc[p] * qdec_ref[:, lanes[p]]).astype(BF16)], axis=1)
                    rhs = jnp.concatenate([vc[p] * lo_b, vc[p] * hi_b, state.astype(BF16)],
                                          axis=0)
                    outs[p] = _dot(lhs, rhs)
                    kd = (kc[p] * kdec_ref[:, lanes[p]]).astype(BF16)
                    sbuf[p] = (state * sdec_ref[p]
                               + jnp.where(sdec_ref[p] > 0.0, _dot_tn(kd, vc[p]), 0.0))
                yield
            for p in pairs:
                on = _pair_groupnorm(outs[p], lo, rlng_ref[:, lanes[p]], rlnb_ref[:, lanes[p]])
                mixbuf[rs, CONV_CH + p * PAIR_LANES:CONV_CH + (p + 1) * PAIR_LANES] = (
                    on * _silu(gt[rs, lanes[p]])).astype(BF16)
            yield

    ffn = _ffn_stages(x1buf[...], mods(mod2_ref, b_mlp), gffn_ref, w1_ref, w2_ref, gfin_ref, y_ref)
    next(ffn)
    hb = _rmsnorm_mod(x, gmix_ref[...], scale, shift).astype(BF16)

    def proj(k):
        return _dot(hb, win_ref[:, k * CONV_CH:(k + 1) * CONV_CH])

    streams = {"f": ffn, "c": conv_stages(proj(0) * jax.nn.sigmoid(proj(1))),
               "r": ret_stages(proj)}
    for name in LAYER_PROGRAM.replace(" ", ""):
        next(streams[name], None)
    for stream in (streams["c"], streams["r"], streams["f"]):
        for _ in stream:
            pass

    mix = _dot(mixbuf[...], wout_ref[...])
    x1buf[...] = x + gate * mix

    @pl.when(jnp.logical_and(j == nj - 1, live))
    def _():
        hist_ref[0] = ubuf[0:HIST_ROWS, :]
        state_ref[0] = sbuf[...]


def _layer_prompt(x, mod, mod_row0, g_mix, w_in, conv_w, conv_b, cln_g, cln_b, rln_g, rln_b,
                  w_out, g_ffn, w1, w2, g_final, rope, decay, tile):
    bsz, seq, d = x.shape
    assert bsz % SUBLANES == 0 and mod_row0 % bsz == 0
    assert all(w.shape[0] % WEIGHT_STAGE_ROWS == 0 for w in (w_in, w_out, w1, w2))
    in_hbm = pl.BlockSpec(memory_space=pl.ANY)
    mod_blk = mod_row0 // bsz
    nj = seq // tile
    total = bsz * nj
    cos, sin = rope
    dec, qdec, kdec, sdec = decay
    mix_tile = lambda g: jnp.minimum(g, total - 1)
    ffn_tile = lambda g: jnp.maximum(g - 1, 0)
    resident = lambda shape: pl.BlockSpec(shape, lambda g: (0,) * len(shape),
                                          pipeline_mode=pl.Buffered(1))
    row = lambda a: a.reshape(1, -1)
    conv_w_rows = jnp.broadcast_to(conv_w[:, None, :], (CONV_K, SUBLANES, CONV_CH))
    rope_spec = pl.BlockSpec((tile, PAIR_LANES), lambda g: (mix_tile(g) % nj, 0))
    return pl.pallas_call(
        functools.partial(_layer_prompt_kernel, tile, nj),
        grid=(total + 1,),
        in_specs=[
            pl.BlockSpec((1, tile, d), lambda g: (mix_tile(g) // nj, mix_tile(g) % nj, 0)),
            pl.BlockSpec((bsz, 3 * d), lambda g: (mod_blk, 0), pipeline_mode=pl.Buffered(1)),
            resident((1, d)),
            in_hbm,
            resident((CONV_K, SUBLANES, CONV_CH)),
            resident((1, CONV_CH)),
            resident((1, CONV_CH)),
            resident((1, CONV_CH)),
            resident((1, RET_W)),
            resident((1, RET_W)),
            in_hbm,
            rope_spec,
            rope_spec,
            resident(dec.shape),
            resident(qdec.shape),
            resident(kdec.shape),
            resident(sdec.shape),
            pl.BlockSpec((bsz, 3 * d), lambda g: (mod_blk, 1), pipeline_mode=pl.Buffered(1)),
            resident((1, d)),
            in_hbm,
            in_hbm,
            resident((1, d)),
        ],
        out_specs=[
            pl.BlockSpec((1, tile, d), lambda g: (ffn_tile(g) // nj, ffn_tile(g) % nj, 0)),
            pl.BlockSpec((1, HIST_ROWS, CONV_CH), lambda g: (mix_tile(g) // nj, 0, 0)),
            pl.BlockSpec((1, N_PAIRS, PAIR_LANES, PAIR_LANES),
                         lambda g: (mix_tile(g) // nj, 0, 0, 0)),
        ],
        out_shape=(
            jax.ShapeDtypeStruct((bsz, seq, d), F32),
            jax.ShapeDtypeStruct((bsz, HIST_ROWS, CONV_CH), F32),
            jax.ShapeDtypeStruct((bsz, N_PAIRS, PAIR_LANES, PAIR_LANES), F32),
        ),
        scratch_shapes=[
            pltpu.VMEM((HIST_ROWS + tile, CONV_CH), F32),
            pltpu.VMEM((SUBLANES, HIST_ROWS + tile, CONV_CH), F32),
            pltpu.VMEM((N_PAIRS, PAIR_LANES, PAIR_LANES), F32),
            pltpu.VMEM((tile, D_MODEL), BF16),
            pltpu.VMEM((tile, D_MODEL), F32),
            pltpu.VMEM(w_in.shape, BF16),
            pltpu.VMEM(w_out.shape, BF16),
            pltpu.VMEM(w1.shape, BF16),
            pltpu.VMEM(w2.shape, BF16),
            pltpu.VMEM((2, WEIGHT_STAGE_ROWS, max(w.shape[1] for w in (w_in, w_out, w1, w2))),
                       F32),
            pltpu.SemaphoreType.DMA((2,)),
        ],
        compiler_params=pltpu.CompilerParams(
            dimension_semantics=("arbitrary",), vmem_limit_bytes=VMEM_LIMIT),
        name="layer_prompt",
    )(x, mod, row(g_mix), w_in, conv_w_rows, row(conv_b), row(cln_g), row(cln_b),
      row(rln_g), row(rln_b), w_out, cos, sin, dec, qdec, kdec, sdec,
      mod, row(g_ffn), w1, w2, row(g_final))


def _ffn_stages(x, mods, gffn_ref, w1_ref, w2_ref, gfin_ref, y_ref):
    shift, scale, gate = mods
    hb = _rmsnorm_mod(x, gffn_ref[...], scale, shift).astype(BF16)
    yield
    half = D_MODEL // 2
    ff = [None, None]
    for c in range(D_FF // (2 * FFN_CHUNK)):
        acts = []
        for h in range(2):
            c0 = (2 * c + h) * FFN_CHUNK
            a = jnp.maximum(_dot(hb, w1_ref[:, c0:c0 + FFN_CHUNK]), 0.0)
            acts.append((a * a).astype(BF16))
            yield
        act = jnp.concatenate(acts, axis=1)
        for n in range(2):
            part = _dot(act, w2_ref[2 * c * FFN_CHUNK:2 * (c + 1) * FFN_CHUNK,
                                    n * half:(n + 1) * half])
            ff[n] = part if ff[n] is None else ff[n] + part
            yield
    z = x + gate * jnp.concatenate(ff, axis=1)
    y_ref[0] = z * lax.rsqrt(jnp.mean(z * z, axis=-1, keepdims=True) + EPS) * gfin_ref[...]


def _mod_cols(mod_ref, n_seq, k):
    return mod_ref[0:n_seq, k * D_MODEL:(k + 1) * D_MODEL]


def _mlp_sample_kernel(n_pos, n_seq, x_ref, conv_ref, ret_ref, wout_ref, mod_ref,
                       gffn_ref, w1_ref, w2_ref, gfin_ref, y_ref, x1buf, hbuf, acc):
    c = pl.program_id(0)
    rows = lambda i: slice(i * n_seq, (i + 1) * n_seq)

    @pl.when(c == 0)
    def _():
        mix = (_dot(conv_ref[...], wout_ref[0:CONV_CH, :].astype(BF16))
               + _dot(ret_ref[...], wout_ref[CONV_CH:CONV_CH + RET_W, :].astype(BF16)))
        for i in range(n_pos):
            x1 = x_ref[i] + _mod_cols(mod_ref, n_seq, 2) * mix[rows(i)]
            x1buf[rows(i), :] = x1
            hbuf[rows(i), :] = _rmsnorm_mod(x1, gffn_ref[...], _mod_cols(mod_ref, n_seq, 4),
                                            _mod_cols(mod_ref, n_seq, 3)).astype(BF16)

    a = jnp.maximum(_dot(hbuf[...], w1_ref[...].astype(BF16)), 0.0)
    part = _dot((a * a).astype(BF16), w2_ref[...].astype(BF16))

    @pl.when(c == 0)
    def _():
        acc[...] = part

    @pl.when(c > 0)
    def _():
        acc[...] += part

    @pl.when(c == pl.num_programs(0) - 1)
    def _():
        for i in range(n_pos):
            z = x1buf[rows(i), :] + _mod_cols(mod_ref, n_seq, 5) * acc[rows(i), :]
            y_ref[i] = (z * lax.rsqrt(jnp.mean(z * z, axis=-1, keepdims=True) + EPS)
                        * gfin_ref[...])


def _mlp_sample(x, conv_out, ret_out, w_out, mod, g_ffn, w1, w2, g_final):
    n_pos, n_seq, d = x.shape
    n_tok = n_pos * n_seq
    n_chunks = D_FF // SAMPLE_FFN_CHUNK
    whole = lambda shape: pl.BlockSpec(shape, lambda c: (0,) * len(shape),
                                       pipeline_mode=pl.Buffered(1))
    return pl.pallas_call(
        functools.partial(_mlp_sample_kernel, n_pos, n_seq),
        grid=(n_chunks,),
        in_specs=[whole(x.shape), whole(conv_out.shape), whole(ret_out.shape),
                  whole(w_out.shape), whole(mod.shape), whole((1, d)),
                  pl.BlockSpec((d, SAMPLE_FFN_CHUNK), lambda c: (0, c)),
                  pl.BlockSpec((SAMPLE_FFN_CHUNK, d), lambda c: (c, 0)),
                  whole((1, d))],
        out_specs=pl.BlockSpec(x.shape, lambda c: (0, 0, 0)),
        out_shape=jax.ShapeDtypeStruct(x.shape, F32),
        scratch_shapes=[pltpu.VMEM((n_tok, d), F32), pltpu.VMEM((n_tok, d), BF16),
                        pltpu.VMEM((n_tok, d), F32)],
        compiler_params=pltpu.CompilerParams(
            dimension_semantics=("arbitrary",), vmem_limit_bytes=VMEM_LIMIT),
        name="mlp_sample",
    )(x, conv_out, ret_out, w_out, mod, g_ffn.reshape(1, d), w1, w2, g_final.reshape(1, d))


def _power_table_kernel(o_ref):
    lane_h = (lax.broadcasted_iota(jnp.int32, o_ref.shape, 1) // RET_DK).astype(F32)
    n = lax.broadcasted_iota(jnp.int32, o_ref.shape, 0).astype(F32)
    o_ref[...] = jnp.exp(_log_decay(lane_h) * n)


def _power_table():
    return pl.pallas_call(_power_table_kernel,
                          out_shape=jax.ShapeDtypeStruct((8, RET_W), F32),
                          name="decay_powers")()


def _proj_sample_kernel(n_pos, n_seq, x_ref, mod_ref, gmix_ref, win_ref, cache_ref, convw_ref,
                        convb_ref, clng_ref, clnb_ref, cos_ref, sin_ref,
                        hist_ref, conv_ref, q_ref, k_ref, v_ref, g_ref):
    shift, scale = _mod_cols(mod_ref, n_seq, 0), _mod_cols(mod_ref, n_seq, 1)
    hb = jnp.concatenate(
        [_rmsnorm_mod(x_ref[i], gmix_ref[...], scale, shift).astype(BF16)
         for i in range(n_pos)], axis=0)

    def proj(k):
        return _dot(hb, win_ref[:, k * CONV_CH:(k + 1) * CONV_CH].astype(BF16))

    u = proj(0) * jax.nn.sigmoid(proj(1))
    n_hist = CONV_K - 1
    rows = lambda a, i: a[i * n_seq:(i + 1) * n_seq]
    outs = []
    for m in range(n_hist - n_pos):
        hist_ref[m] = cache_ref[m + n_pos]
    for i in range(n_pos):
        hist_ref[n_hist - n_pos + i] = rows(u, i)
        dw = jnp.broadcast_to(convb_ref[...], (n_seq, CONV_CH))
        for k in range(CONV_K):
            m = i + k
            src = cache_ref[m] if m < n_hist else rows(u, m - n_hist)
            dw = dw + convw_ref[k:k + 1, :] * src
        outs.append(_silu(_layernorm(dw, clng_ref[...], clnb_ref[...])).astype(BF16))
    conv_ref[...] = jnp.concatenate(outs, axis=0)

    lane = lax.broadcasted_iota(jnp.int32, (1, PAIR_LANES), 1)
    first_half = (lane % RET_DK) < (RET_DK // 2)
    q = proj(2) * (RET_DK ** -0.5)
    kk = proj(3)
    for i in range(n_pos):
        rs = slice(i * n_seq, (i + 1) * n_seq)
        cos = cos_ref[i:i + 1, :]
        sin = sin_ref[i:i + 1, :]
        for p in range(N_PAIRS):
            sl = slice(p * PAIR_LANES, (p + 1) * PAIR_LANES)
            q_ref[rs, sl] = _rotary_pair(q[rs, sl], cos, sin, first_half)
            k_ref[rs, sl] = _rotary_pair(kk[rs, sl], cos, sin, first_half)
    v_ref[...] = proj(4)
    g_ref[...] = proj(5)


def _proj_sample(x, mod, g_mix, w_in, cache_t, conv_w, conv_b, cln_g, cln_b, rope):
    n_pos, n_seq, d = x.shape
    n_tok = n_pos * n_seq
    cos, sin = rope
    row = lambda a: a.reshape(1, -1)
    tok = lambda dt: jax.ShapeDtypeStruct((n_tok, RET_W), dt)
    return pl.pallas_call(
        functools.partial(_proj_sample_kernel, n_pos, n_seq),
        out_shape=(jax.ShapeDtypeStruct(cache_t.shape, F32), tok(BF16),
                   tok(F32), tok(F32), tok(F32), tok(F32)),
        compiler_params=pltpu.CompilerParams(vmem_limit_bytes=VMEM_LIMIT),
        name="proj_sample",
    )(x, mod, row(g_mix), w_in, cache_t, conv_w, row(conv_b), row(cln_g), row(cln_b), cos, sin)


def _ret_sample_kernel(n_pos, n_seq, q_ref, k_ref, v_ref, g_ref, st_ref, pw_ref, rlng_ref,
                       rlnb_ref, out_ref, snew_ref, sb_buf, qd_buf, kd_buf, v_buf):
    lane = lax.broadcasted_iota(jnp.int32, (1, PAIR_LANES), 1)
    lo = lane < RET_DK
    rows = lambda a, i: a[i * n_seq:(i + 1) * n_seq]
    cols = lambda a, i: a[:, i * n_seq:(i + 1) * n_seq]
    pw = pw_ref[...]
    q = q_ref[...]
    k = k_ref[...]
    qb = [rows(q, i).astype(BF16) for i in range(n_pos)]
    kb = [rows(k, i).astype(BF16) for i in range(n_pos)]
    vb = [rows(v_ref[...], i).astype(BF16) for i in range(n_pos)]

    o = []
    for i in range(n_pos):
        acc = jnp.zeros((n_seq, PAIR_LANES), F32)
        for j in range(i + 1):
            prod = qb[i].astype(F32) * kb[j].astype(F32)
            s_lo = jnp.sum(jnp.where(lo, prod, 0.0), axis=-1, keepdims=True)
            s_hi = jnp.sum(jnp.where(lo, 0.0, prod), axis=-1, keepdims=True)
            score = jnp.where(lo, s_lo, s_hi) * pw[i - j:i - j + 1, :]
            acc = acc + score.astype(BF16).astype(F32) * vb[j].astype(F32)
        o.append(acc)

    rounded = lambda x: x.astype(BF16).astype(F32)
    qd_buf[...] = jnp.concatenate(
        [rounded(rows(q, i) * pw[i + 1:i + 2, :]) for i in range(n_pos)], axis=0).T
    kd_buf[...] = jnp.concatenate(
        [rounded(rows(k, j) * pw[n_pos - 1 - j:n_pos - j, :]) for j in range(n_pos)], axis=0).T
    v_buf[...] = jnp.concatenate([vb[j].astype(F32) for j in range(n_pos)], axis=0).T
    tile_at = lambda r: pl.ds(pl.multiple_of(r * RET_DV, RET_DV), RET_DV)
    pos = lambda i: slice(i * n_seq, (i + 1) * n_seq)

    for a in range(2):
        gam = pw[n_pos:n_pos + 1, a * RET_DK:a * RET_DK + 1]
        v_rows = slice(a * RET_DV, (a + 1) * RET_DV)

        def update(grp, carry, a=a, gam=gam, v_rows=v_rows):
            r0 = pl.multiple_of(a * RET_DK + grp * SUBLANES, SUBLANES)
            kd_rows = kd_buf[pl.ds(r0, SUBLANES), :]
            for s in range(SUBLANES):
                r = r0 + s
                s_old = st_ref[tile_at(r), :]
                sb_buf[tile_at(r), :] = rounded(s_old)
                s_new = s_old * gam
                for j in range(n_pos):
                    s_new = s_new + kd_rows[s:s + 1, pos(j)] * v_buf[v_rows, pos(j)]
                snew_ref[tile_at(r), :] = s_new
            return carry

        lax.fori_loop(0, RET_DK // SUBLANES, update, 0)

    for i in range(n_pos):
        heads = []
        for a in range(2):
            def cross(grp, acc, a=a, i=i):
                r0 = pl.multiple_of(a * RET_DK + grp * SUBLANES, SUBLANES)
                qd_rows = qd_buf[pl.ds(r0, SUBLANES), pos(i)]
                for s in range(SUBLANES):
                    acc = acc + qd_rows[s:s + 1, :] * sb_buf[tile_at(r0 + s), :]
                return acc

            heads.append(lax.fori_loop(0, RET_DK // SUBLANES, cross,
                                       jnp.zeros((RET_DV, n_seq), F32)))
        oi = o[i] + jnp.concatenate(heads, axis=0).T
        on = _pair_groupnorm(oi, lo, rlng_ref[...], rlnb_ref[...])
        out_ref[i * n_seq:(i + 1) * n_seq, :] = (on * _silu(rows(g_ref[...], i))).astype(BF16)


def _ret_sample(q, k, v, g, state_t, pw, rln_g, rln_b, n_pos, n_seq):
    n_tok = n_pos * n_seq
    pair_state = 2 * RET_DK * RET_DV
    slab = pl.BlockSpec((n_tok, PAIR_LANES), lambda p: (0, p))
    state_spec = pl.BlockSpec((pair_state, n_seq), lambda p: (p, 0))
    return pl.pallas_call(
        functools.partial(_ret_sample_kernel, n_pos, n_seq),
        grid=(N_PAIRS,),
        in_specs=[slab, slab, slab, slab, state_spec,
                  pl.BlockSpec((8, PAIR_LANES), lambda p: (0, p)),
                  pl.BlockSpec((1, PAIR_LANES), lambda p: (0, p)),
                  pl.BlockSpec((1, PAIR_LANES), lambda p: (0, p))],
        out_specs=[slab, state_spec],
        out_shape=(jax.ShapeDtypeStruct((n_tok, RET_W), BF16),
                   jax.ShapeDtypeStruct(state_t.shape, F32)),
        scratch_shapes=[pltpu.VMEM((pair_state, n_seq), F32)]
        + [pltpu.VMEM((PAIR_LANES, n_tok), F32)] * 3,
        compiler_params=pltpu.CompilerParams(
            dimension_semantics=("arbitrary",), vmem_limit_bytes=VMEM_LIMIT),
        name="ret_sample",
    )(q, k, v, g, state_t, pw, rln_g.reshape(1, RET_W), rln_b.reshape(1, RET_W))


def _unpack_state(s2):
    blocks = []
    for p in range(N_PAIRS):
        for a in range(2):
            sl = slice(a * RET_DK, (a + 1) * RET_DK)
            blocks.append(s2[:, p, sl, sl])
    return jnp.stack(blocks, axis=1)


def _prompt_path(x, mod, mod_row0, g_mix, w_in, conv_w, conv_b, cln_g, cln_b, rln_g, rln_b,
                 w_out, g_ffn, w_ff1, w_ff2, g_final, tile=PROMPT_TILE):
    seq = x.shape[1]
    rope = _rope_tables(seq, 0)
    decay = _decay_tables(RET_CHUNK)
    y, hist, s2 = _layer_prompt(x, mod, mod_row0, g_mix, w_in, conv_w, conv_b, cln_g, cln_b,
                                rln_g, rln_b, w_out, g_ffn, w_ff1, w_ff2, g_final, rope, decay,
                                tile)
    return y, hist[:, HIST_ROWS - (CONV_K - 1):], _unpack_state(s2)


def _sample_path(x, mod, cache, state, pos0, g_mix, w_in, conv_w, conv_b, cln_g, cln_b,
                 rln_g, rln_b, w_out, g_ffn, w_ff1, w_ff2, g_final):
    n_seq, n_pos, d = x.shape
    xs = x.transpose(1, 0, 2)
    rope = _rope_tables(8, pos0)
    hist_t, conv_out, q, k, v, g = _proj_sample(xs, mod, g_mix, w_in, cache.transpose(1, 0, 2),
                                                conv_w, conv_b, cln_g, cln_b, rope)
    ret_out, s_new_t = _ret_sample(q, k, v, g, state.reshape(n_seq, -1).T, _power_table(),
                                   rln_g, rln_b, n_pos, n_seq)
    s_new = s_new_t.T
    y = _mlp_sample(xs, conv_out, ret_out, w_out, mod, g_ffn, w_ff1, w_ff2, g_final)
    return y.transpose(1, 0, 2), hist_t.transpose(1, 0, 2), s_new.reshape(state.shape)


def kernel(x_prompt, x_sample, cache_conv, state_ret, c_prompt, c_sample, w_ada, b_ada, g_mix, w_in, conv_w, conv_b, conv_ln_g, conv_ln_b, ret_ln_g, ret_ln_b, w_out, g_ffn, w_ff1, w_ff2, g_final):
    mod = _modulation(jnp.concatenate([c_sample, c_prompt], axis=0), w_ada[0], b_ada[0])
    params = (g_mix[0], w_in[0], conv_w[0], conv_b[0], conv_ln_g[0], conv_ln_b[0], ret_ln_g[0],
              ret_ln_b[0], w_out[0], g_ffn[0], w_ff1[0], w_ff2[0], g_final)
    y_p, conv_p, ret_p = _prompt_path(x_prompt, mod, x_sample.shape[0], *params)
    y_s, conv_s, ret_s = _sample_path(x_sample, mod, cache_conv[0], state_ret[0], PAST_LEN,
                                      *params)
    return (y_p, y_s, conv_p[None], ret_p[None], conv_s[None], ret_s[None])
```

```python
import functools

import jax
import jax.numpy as jnp
from jax import lax
from jax.experimental import pallas as pl
from jax.experimental.pallas import tpu as pltpu

D_MODEL = 1024
CONV_CH = 512
CONV_K = 31
RET_HEADS = 8
RET_W = 512
RET_DK = 64
RET_DV = 64
D_FF = 4 * D_MODEL
RET_CHUNK = 128
ROPE_THETA = 10000.0
EPS = 1e-6
N_MOD = 6
PAST_LEN = 16384

SUBLANES = 8
HIST_ROWS = 32
CONV_ROW_BLOCK = 32
PROMPT_TILE = 256
FFN_CHUNK = 512
SAMPLE_FFN_CHUNK = 1024
WEIGHT_STAGE_ROWS = 128
WEIGHT_STAGE_SLOTS = 4
LAYER_PROGRAM = ("c cf cf cf cf cf cf cf cf "
                 "rrrr rfrf rfrf r "
                 "rfrf rfrf r")
PAIR_LANES = 2 * RET_DK
N_PAIRS = RET_HEADS // 2
VMEM_LIMIT = 56 * 1024 * 1024

F32 = jnp.float32
BF16 = jnp.bfloat16


def _log_decay(head):
    return jnp.log1p(-jnp.exp2(-5.0 - head))


def _rope_kernel(pos0, cos_ref, sin_ref):
    rows = cos_ref.shape[0]
    lane = lax.broadcasted_iota(jnp.int32, (rows, PAIR_LANES), 1)
    row = lax.broadcasted_iota(jnp.int32, (rows, PAIR_LANES), 0)
    half = RET_DK // 2
    freq = (lane % half).astype(F32)
    inv = jnp.power(jnp.full_like(freq, ROPE_THETA), -freq / half)
    ang = (row + pos0).astype(F32) * inv
    first = (lane % RET_DK) < half
    cos_ref[...] = jnp.cos(ang)
    sin_ref[...] = jnp.where(first, -jnp.sin(ang), jnp.sin(ang))


ROPE_BLOCK = 16


def _rope_blocked_kernel(pos0, cos_ref, sin_ref):
    rows = cos_ref.shape[0]
    n_blk = rows // ROPE_BLOCK
    half = RET_DK // 2

    def angles(n, step, start):
        lane = lax.broadcasted_iota(jnp.int32, (n, PAIR_LANES), 1)
        row = lax.broadcasted_iota(jnp.int32, (n, PAIR_LANES), 0)
        freq = (lane % half).astype(F32)
        inv = jnp.power(jnp.full_like(freq, ROPE_THETA), -freq / half)
        return (row * step + start).astype(F32) * inv

    a = angles(n_blk, ROPE_BLOCK, pos0)
    b = angles(ROPE_BLOCK, 1, 0)
    lane = lax.broadcasted_iota(jnp.int32, (ROPE_BLOCK, PAIR_LANES), 1)
    sign = jnp.where((lane % RET_DK) < half, -1.0, 1.0)
    cos_a, sin_a = jnp.cos(a), jnp.sin(a)
    cos_b, sin_b = jnp.cos(b), jnp.sin(b)
    cos_bs, sin_bs = cos_b * sign, sin_b * sign
    for blk in range(n_blk):
        ca, sa = cos_a[blk:blk + 1, :], sin_a[blk:blk + 1, :]
        rs = slice(blk * ROPE_BLOCK, (blk + 1) * ROPE_BLOCK)
        cos_ref[rs, :] = ca * cos_b - sa * sin_b
        sin_ref[rs, :] = sa * cos_bs + ca * sin_bs


def _rope_tables(rows, pos0):
    blocked = rows % ROPE_BLOCK == 0 and rows // ROPE_BLOCK >= SUBLANES
    return pl.pallas_call(
        functools.partial(_rope_blocked_kernel if blocked else _rope_kernel, pos0),
        out_shape=(jax.ShapeDtypeStruct((rows, PAIR_LANES), F32),
                   jax.ShapeDtypeStruct((rows, PAIR_LANES), F32)),
        name="rope_tables",
    )()


def _decay_kernel(chunk, dec_ref, qdec_ref, kdec_ref, sdec_ref):
    c = chunk
    shape = (N_PAIRS, c, 2 * c)
    col = lax.broadcasted_iota(jnp.int32, shape, 2)
    h = (2 * lax.broadcasted_iota(jnp.int32, shape, 0) + col // c).astype(F32)
    i = lax.broadcasted_iota(jnp.int32, shape, 1).astype(F32)
    j = (col % c).astype(F32)
    diff = i - j
    dec_ref[...] = jnp.where(diff >= 0, jnp.exp(_log_decay(h) * jnp.maximum(diff, 0.0)), 0.0)
    lane_h = (lax.broadcasted_iota(jnp.int32, (c, RET_W), 1) // RET_DK).astype(F32)
    idx = lax.broadcasted_iota(jnp.int32, (c, RET_W), 0).astype(F32)
    lg = _log_decay(lane_h)
    qdec_ref[...] = jnp.exp(lg * (idx + 1.0))
    kdec_ref[...] = jnp.exp(lg * (c - 1.0 - idx))
    p = lax.broadcasted_iota(jnp.int32, (N_PAIRS, PAIR_LANES, PAIR_LANES), 0)
    r = lax.broadcasted_iota(jnp.int32, (N_PAIRS, PAIR_LANES, PAIR_LANES), 1) // RET_DK
    q = lax.broadcasted_iota(jnp.int32, (N_PAIRS, PAIR_LANES, PAIR_LANES), 2) // RET_DK
    hh = (2 * p + r).astype(F32)
    sdec_ref[...] = jnp.where(r == q, jnp.exp(_log_decay(hh) * float(c)), 0.0)


def _decay_tables(chunk):
    return pl.pallas_call(
        functools.partial(_decay_kernel, chunk),
        out_shape=(jax.ShapeDtypeStruct((N_PAIRS, chunk, 2 * chunk), F32),
                   jax.ShapeDtypeStruct((chunk, RET_W), F32),
                   jax.ShapeDtypeStruct((chunk, RET_W), F32),
                   jax.ShapeDtypeStruct((N_PAIRS, PAIR_LANES, PAIR_LANES), F32)),
        name="decay_tables",
    )()


def _mod_kernel(c_ref, w_lo_ref, w_hi_ref, b_ref, o_ref):
    c = c_ref[...]
    s = (c * jax.nn.sigmoid(c)).astype(BF16)
    half = w_lo_ref.shape[1]
    for k, w_ref in enumerate((w_lo_ref, w_hi_ref)):
        cs = slice(k * half, (k + 1) * half)
        o_ref[:, cs] = _dot(s, w_ref[...].astype(BF16)) + b_ref[:, cs]


def _modulation(c, w_ada, b_ada):
    n, d = c.shape
    width = w_ada.shape[1]
    blk = D_MODEL
    half = lambda k: pl.BlockSpec((d, blk // 2), lambda i: (0, 2 * i + k))
    return pl.pallas_call(
        _mod_kernel,
        grid=(width // blk,),
        in_specs=[pl.BlockSpec((n, d), lambda i: (0, 0)), half(0), half(1),
                  pl.BlockSpec((1, blk), lambda i: (0, i))],
        out_specs=pl.BlockSpec((n, blk), lambda i: (0, i)),
        out_shape=jax.ShapeDtypeStruct((n, width), F32),
        compiler_params=pltpu.CompilerParams(vmem_limit_bytes=VMEM_LIMIT),
        name="adaln_mod",
    )(c, w_ada, w_ada, b_ada.reshape(1, width))


def _rmsnorm_mod(x, g, scale, shift):
    y = x * lax.rsqrt(jnp.mean(x * x, axis=-1, keepdims=True) + EPS)
    return (y * g) * (1.0 + scale) + shift


def _layernorm(x, g, b):
    mu = jnp.mean(x, axis=-1, keepdims=True)
    d = x - mu
    var = jnp.mean(d * d, axis=-1, keepdims=True)
    return d * lax.rsqrt(var + EPS) * g + b


def _silu(x):
    return x * jax.nn.sigmoid(x)


def _rotary_pair(x, cos, sin_signed, first_half):
    partner = jnp.where(first_half, pltpu.roll(x, PAIR_LANES - RET_DK // 2, 1),
                        pltpu.roll(x, RET_DK // 2, 1))
    return x * cos + partner * sin_signed


def _pair_groupnorm(o, lo, g, b):
    inv_n = 1.0 / RET_DV
    s_lo = jnp.sum(jnp.where(lo, o, 0.0), axis=-1, keepdims=True)
    s_hi = jnp.sum(jnp.where(lo, 0.0, o), axis=-1, keepdims=True)
    d = o - jnp.where(lo, s_lo, s_hi) * inv_n
    d2 = d * d
    v_lo = jnp.sum(jnp.where(lo, d2, 0.0), axis=-1, keepdims=True)
    v_hi = jnp.sum(jnp.where(lo, 0.0, d2), axis=-1, keepdims=True)
    var = jnp.where(lo, v_lo, v_hi) * inv_n
    return d * lax.rsqrt(var + EPS) * g + b


def _dot(a, b):
    return jnp.dot(a, b, preferred_element_type=F32)


def _dot_nt(a, b):
    return lax.dot_general(a, b, (((1,), (1,)), ((), ())), preferred_element_type=F32)


def _dot_tn(a, b):
    return lax.dot_general(a, b, (((0,), (0,)), ((), ())), preferred_element_type=F32)


def _stream_cast_weights(pairs, stage, sems):
    n_slots, rows = stage.shape[0], stage.shape[1]
    chunks = [(src, dst, r0) for src, dst in pairs for r0 in range(0, src.shape[0], rows)]

    def copy(c):
        src, _, r0 = chunks[c]
        slot = c % n_slots
        return pltpu.make_async_copy(src.at[pl.ds(r0, rows), :],
                                     stage.at[slot, :, pl.ds(0, src.shape[1])], sems.at[slot])

    for c in range(min(n_slots - 1, len(chunks))):
        copy(c).start()
    for c, (src, dst, r0) in enumerate(chunks):
        if c + n_slots - 1 < len(chunks):
            copy(c + n_slots - 1).start()
        copy(c).wait()
        dst[r0:r0 + rows, :] = stage[c % n_slots, :, 0:src.shape[1]].astype(BF16)


def _layer_prompt_kernel(tile, nj, x_ref, mod_ref, gmix_ref, win_hbm, convw_ref, convb_ref,
                         clng_ref, clnb_ref, rlng_ref, rlnb_ref, wout_hbm,
                         cos_ref, sin_ref, dec_ref, qdec_ref, kdec_ref, sdec_ref,
                         mod2_ref, gffn_ref, w1_hbm, w2_hbm, gfin_ref,
                         y_ref, hist_ref, state_ref,
                         ubuf, ushift, sbuf, mixbuf, x1buf,
                         win_ref, wout_ref, w1_ref, w2_ref, stage, sems):
    g = pl.program_id(0)
    total = pl.num_programs(0) - 1
    j = lax.rem(jnp.minimum(g, total - 1), nj)
    live = g < total

    @pl.when(g == 0)
    def _():
        _stream_cast_weights([(win_hbm, win_ref), (wout_hbm, wout_ref), (w1_hbm, w1_ref),
                              (w2_hbm, w2_ref)], stage, sems)
        x1buf[...] = jnp.zeros((tile, D_MODEL), F32)

    @pl.when(jnp.logical_and(j == 0, live))
    def _():
        ubuf[0:HIST_ROWS, :] = jnp.zeros((HIST_ROWS, CONV_CH), F32)
        sbuf[...] = jnp.zeros(sbuf.shape, F32)

    x = x_ref[0]
    b_mix = lax.div(jnp.minimum(g, total - 1), nj)
    b_mlp = lax.div(jnp.maximum(g - 1, 0), nj)
    mods = lambda ref, b: [ref[pl.ds(b, 1), k * D_MODEL:(k + 1) * D_MODEL] for k in range(3)]
    shift, scale, gate = mods(mod_ref, b_mix)

    def conv_stages(u):
        ubuf[HIST_ROWS:HIST_ROWS + tile, :] = u
        off = HIST_ROWS - (CONV_K - 1)
        for r in range(SUBLANES):
            n = tile + SUBLANES * ((CONV_K - 1 - r) // SUBLANES)
            ushift[r, 0:n, :] = ubuf[off + r:off + r + n, :]
        ubuf[0:HIST_ROWS, :] = ubuf[tile:tile + HIST_ROWS, :]
        yield
        for rb in range(tile // CONV_ROW_BLOCK):
            r0 = rb * CONV_ROW_BLOCK
            dw = jnp.broadcast_to(convb_ref[...], (CONV_ROW_BLOCK, CONV_CH))
            for k in range(CONV_K):
                s0 = r0 + SUBLANES * (k // SUBLANES)
                wk = jnp.concatenate([convw_ref[k]] * (CONV_ROW_BLOCK // SUBLANES), axis=0)
                dw = dw + wk * ushift[k % SUBLANES, s0:s0 + CONV_ROW_BLOCK, :]
            conv_out = _silu(_layernorm(dw, clng_ref[...], clnb_ref[...]))
            mixbuf[r0:r0 + CONV_ROW_BLOCK, 0:CONV_CH] = conv_out.astype(BF16)
            yield

    def ret_stages(proj):
        ret_proj = []
        for k in range(4):
            ret_proj.append(proj(2 + k))
            yield
        q, kk, v, gt = ret_proj
        q = q * (RET_DK ** -0.5)
        lane = lax.broadcasted_iota(jnp.int32, (1, PAIR_LANES), 1)
        lo = lane < RET_DK
        first_half = (lane % RET_DK) < (RET_DK // 2)
        lo_b = lo.astype(BF16)
        hi_b = (~lo).astype(BF16)
        pairs = range(N_PAIRS)
        halves = (pairs[:N_PAIRS // 2], pairs[N_PAIRS // 2:])
        lanes = [slice(p * PAIR_LANES, (p + 1) * PAIR_LANES) for p in pairs]
        for c in range(tile // RET_CHUNK):
            rs = slice(c * RET_CHUNK, (c + 1) * RET_CHUNK)
            cos = cos_ref[rs, :]
            sin = sin_ref[rs, :]
            qc, kc, vc, scores, outs = {}, {}, {}, {}, {}
            for group in halves:
                for p in group:
                    qc[p] = _rotary_pair(q[rs, lanes[p]], cos, sin, first_half)
                    kc[p] = _rotary_pair(kk[rs, lanes[p]], cos, sin, first_half)
                    vc[p] = v[rs, lanes[p]].astype(BF16)
                    kb = kc[p].astype(BF16)
                    k_cat = jnp.concatenate([kb * lo_b, kb * hi_b], axis=0)
                    scores[p] = _dot_nt(qc[p].astype(BF16), k_cat)
                yield
            for group in halves:
                for p in group:
                    state = sbuf[p]
                    lhs = jnp.concatenate([(scores[p] * dec_ref[p]).astype(BF16),
                                           (qc[p] * qdec_ref[:, lanes[p]]).astype(BF16)], axis=1)
                    rhs = jnp.concatenate([vc[p] * lo_b, vc[p] * hi_b, state.astype(BF16)],
                                          axis=0)
                    outs[p] = _dot(lhs, rhs)
                    kd = (kc[p] * kdec_ref[:, lanes[p]]).astype(BF16)
                    sbuf[p] = (state * sdec_ref[p]
                               + jnp.where(sdec_ref[p] > 0.0, _dot_tn(kd, vc[p]), 0.0))
                yield
            for p in pairs:
                on = _pair_groupnorm(outs[p], lo, rlng_ref[:, lanes[p]], rlnb_ref[:, lanes[p]])
                mixbuf[rs, CONV_CH + p * PAIR_LANES:CONV_CH + (p + 1) * PAIR_LANES] = (
                    on * _silu(gt[rs, lanes[p]])).astype(BF16)
            yield

    ffn = _ffn_stages(x1buf[...], mods(mod2_ref, b_mlp), gffn_ref, w1_ref, w2_ref, gfin_ref, y_ref)
    next(ffn)
    hb = _rmsnorm_mod(x, gmix_ref[...], scale, shift).astype(BF16)

    def proj(k):
        return _dot(hb, win_ref[:, k * CONV_CH:(k + 1) * CONV_CH])

    streams = {"f": ffn, "c": conv_stages(proj(0) * jax.nn.sigmoid(proj(1))),
               "r": ret_stages(proj)}
    for name in LAYER_PROGRAM.replace(" ", ""):
        next(streams[name], None)
    for stream in (streams["c"], streams["r"], streams["f"]):
        for _ in stream:
            pass

    mix = _dot(mixbuf[...], wout_ref[...])
    x1buf[...] = x + gate * mix

    @pl.when(jnp.logical_and(j == nj - 1, live))
    def _():
        hist_ref[0] = ubuf[0:HIST_ROWS, :]
        state_ref[0] = sbuf[...]


def _layer_prompt(x, mod, mod_row0, g_mix, w_in, conv_w, conv_b, cln_g, cln_b, rln_g, rln_b,
                  w_out, g_ffn, w1, w2, g_final, rope, decay, tile):
    bsz, seq, d = x.shape
    assert bsz % SUBLANES == 0 and mod_row0 % bsz == 0
    assert all(w.shape[0] % WEIGHT_STAGE_ROWS == 0 for w in (w_in, w_out, w1, w2))
    in_hbm = pl.BlockSpec(memory_space=pl.ANY)
    mod_blk = mod_row0 // bsz
    nj = seq // tile
    total = bsz * nj
    cos, sin = rope
    dec, qdec, kdec, sdec = decay
    mix_tile = lambda g: jnp.minimum(g, total - 1)
    ffn_tile = lambda g: jnp.maximum(g - 1, 0)
    resident = lambda shape: pl.BlockSpec(shape, lambda g: (0,) * len(shape),
                                          pipeline_mode=pl.Buffered(1))
    row = lambda a: a.reshape(1, -1)
    conv_w_rows = jnp.broadcast_to(conv_w[:, None, :], (CONV_K, SUBLANES, CONV_CH))
    rope_spec = pl.BlockSpec((tile, PAIR_LANES), lambda g: (mix_tile(g) % nj, 0))
    return pl.pallas_call(
        functools.partial(_layer_prompt_kernel, tile, nj),
        grid=(total + 1,),
        in_specs=[
            pl.BlockSpec((1, tile, d), lambda g: (mix_tile(g) // nj, mix_tile(g) % nj, 0)),
            pl.BlockSpec((bsz, 3 * d), lambda g: (mod_blk, 0), pipeline_mode=pl.Buffered(1)),
            resident((1, d)),
            in_hbm,
            resident((CONV_K, SUBLANES, CONV_CH)),
            resident((1, CONV_CH)),
            resident((1, CONV_CH)),
            resident((1, CONV_CH)),
            resident((1, RET_W)),
            resident((1, RET_W)),
            in_hbm,
            rope_spec,
            rope_spec,
            resident(dec.shape),
            resident(qdec.shape),
            resident(kdec.shape),
            resident(sdec.shape),
            pl.BlockSpec((bsz, 3 * d), lambda g: (mod_blk, 1), pipeline_mode=pl.Buffered(1)),
            resident((1, d)),
            in_hbm,
            in_hbm,
            resident((1, d)),
        ],
        out_specs=[
            pl.BlockSpec((1, tile, d), lambda g: (ffn_tile(g) // nj, ffn_tile(g) % nj, 0)),
            pl.BlockSpec((1, HIST_ROWS, CONV_CH), lambda g: (mix_tile(g) // nj, 0, 0)),
            pl.BlockSpec((1, N_PAIRS, PAIR_LANES, PAIR_LANES),
                         lambda g: (mix_tile(g) // nj, 0, 0, 0)),
        ],
        out_shape=(
            jax.ShapeDtypeStruct((bsz, seq, d), F32),
            jax.ShapeDtypeStruct((bsz, HIST_ROWS, CONV_CH), F32),
            jax.ShapeDtypeStruct((bsz, N_PAIRS, PAIR_LANES, PAIR_LANES), F32),
        ),
        scratch_shapes=[
            pltpu.VMEM((HIST_ROWS + tile, CONV_CH), F32),
            pltpu.VMEM((SUBLANES, HIST_ROWS + tile, CONV_CH), F32),
            pltpu.VMEM((N_PAIRS, PAIR_LANES, PAIR_LANES), F32),
            pltpu.VMEM((tile, D_MODEL), BF16),
            pltpu.VMEM((tile, D_MODEL), F32),
            pltpu.VMEM(w_in.shape, BF16),
            pltpu.VMEM(w_out.shape, BF16),
            pltpu.VMEM(w1.shape, BF16),
            pltpu.VMEM(w2.shape, BF16),
            pltpu.VMEM((WEIGHT_STAGE_SLOTS, WEIGHT_STAGE_ROWS,
                        max(w.shape[1] for w in (w_in, w_out, w1, w2))), F32),
            pltpu.SemaphoreType.DMA((WEIGHT_STAGE_SLOTS,)),
        ],
        compiler_params=pltpu.CompilerParams(
            dimension_semantics=("arbitrary",), vmem_limit_bytes=VMEM_LIMIT),
        name="layer_prompt",
    )(x, mod, row(g_mix), w_in, conv_w_rows, row(conv_b), row(cln_g), row(cln_b),
      row(rln_g), row(rln_b), w_out, cos, sin, dec, qdec, kdec, sdec,
      mod, row(g_ffn), w1, w2, row(g_final))


def _ffn_stages(x, mods, gffn_ref, w1_ref, w2_ref, gfin_ref, y_ref):
    shift, scale, gate = mods
    hb = _rmsnorm_mod(x, gffn_ref[...], scale, shift).astype(BF16)
    yield
    half = D_MODEL // 2
    ff = [None, None]
    for c in range(D_FF // (2 * FFN_CHUNK)):
        acts = []
        for h in range(2):
            c0 = (2 * c + h) * FFN_CHUNK
            a = jnp.maximum(_dot(hb, w1_ref[:, c0:c0 + FFN_CHUNK]), 0.0)
            acts.append((a * a).astype(BF16))
            yield
        act = jnp.concatenate(acts, axis=1)
        for n in range(2):
            part = _dot(act, w2_ref[2 * c * FFN_CHUNK:2 * (c + 1) * FFN_CHUNK,
                                    n * half:(n + 1) * half])
            ff[n] = part if ff[n] is None else ff[n] + part
            yield
    z = x + gate * jnp.concatenate(ff, axis=1)
    y_ref[0] = z * lax.rsqrt(jnp.mean(z * z, axis=-1, keepdims=True) + EPS) * gfin_ref[...]


def _mod_cols(mod_ref, n_seq, k):
    return mod_ref[0:n_seq, k * D_MODEL:(k + 1) * D_MODEL]


def _mlp_sample_kernel(n_pos, n_seq, x_ref, conv_ref, ret_ref, wout_ref, mod_ref,
                       gffn_ref, w1_ref, w2_ref, gfin_ref, y_ref, x1buf, hbuf, acc):
    c = pl.program_id(0)
    rows = lambda i: slice(i * n_seq, (i + 1) * n_seq)

    @pl.when(c == 0)
    def _():
        mix = (_dot(conv_ref[...], wout_ref[0:CONV_CH, :].astype(BF16))
               + _dot(ret_ref[...], wout_ref[CONV_CH:CONV_CH + RET_W, :].astype(BF16)))
        for i in range(n_pos):
            x1 = x_ref[i] + _mod_cols(mod_ref, n_seq, 2) * mix[rows(i)]
            x1buf[rows(i), :] = x1
            hbuf[rows(i), :] = _rmsnorm_mod(x1, gffn_ref[...], _mod_cols(mod_ref, n_seq, 4),
                                            _mod_cols(mod_ref, n_seq, 3)).astype(BF16)

    a = jnp.maximum(_dot(hbuf[...], w1_ref[...].astype(BF16)), 0.0)
    part = _dot((a * a).astype(BF16), w2_ref[...].astype(BF16))

    @pl.when(c == 0)
    def _():
        acc[...] = part

    @pl.when(c > 0)
    def _():
        acc[...] += part

    @pl.when(c == pl.num_programs(0) - 1)
    def _():
        for i in range(n_pos):
            z = x1buf[rows(i), :] + _mod_cols(mod_ref, n_seq, 5) * acc[rows(i), :]
            y_ref[i] = (z * lax.rsqrt(jnp.mean(z * z, axis=-1, keepdims=True) + EPS)
                        * gfin_ref[...])


def _mlp_sample(x, conv_out, ret_out, w_out, mod, g_ffn, w1, w2, g_final):
    n_pos, n_seq, d = x.shape
    n_tok = n_pos * n_seq
    n_chunks = D_FF // SAMPLE_FFN_CHUNK
    whole = lambda shape: pl.BlockSpec(shape, lambda c: (0,) * len(shape),
                                       pipeline_mode=pl.Buffered(1))
    return pl.pallas_call(
        functools.partial(_mlp_sample_kernel, n_pos, n_seq),
        grid=(n_chunks,),
        in_specs=[whole(x.shape), whole(conv_out.shape), whole(ret_out.shape),
                  whole(w_out.shape), whole(mod.shape), whole((1, d)),
                  pl.BlockSpec((d, SAMPLE_FFN_CHUNK), lambda c: (0, c)),
                  pl.BlockSpec((SAMPLE_FFN_CHUNK, d), lambda c: (c, 0)),
                  whole((1, d))],
        out_specs=pl.BlockSpec(x.shape, lambda c: (0, 0, 0)),
        out_shape=jax.ShapeDtypeStruct(x.shape, F32),
        scratch_shapes=[pltpu.VMEM((n_tok, d), F32), pltpu.VMEM((n_tok, d), BF16),
                        pltpu.VMEM((n_tok, d), F32)],
        compiler_params=pltpu.CompilerParams(
            dimension_semantics=("arbitrary",), vmem_limit_bytes=VMEM_LIMIT),
        name="mlp_sample",
    )(x, conv_out, ret_out, w_out, mod, g_ffn.reshape(1, d), w1, w2, g_final.reshape(1, d))


def _power_table_kernel(o_ref):
    lane_h = (lax.broadcasted_iota(jnp.int32, o_ref.shape, 1) // RET_DK).astype(F32)
    n = lax.broadcasted_iota(jnp.int32, o_ref.shape, 0).astype(F32)
    o_ref[...] = jnp.exp(_log_decay(lane_h) * n)


def _power_table():
    return pl.pallas_call(_power_table_kernel,
                          out_shape=jax.ShapeDtypeStruct((8, RET_W), F32),
                          name="decay_powers")()


def _proj_sample_kernel(n_pos, n_seq, x_ref, mod_ref, gmix_ref, win_ref, cache_ref, convw_ref,
                        convb_ref, clng_ref, clnb_ref, cos_ref, sin_ref,
                        hist_ref, conv_ref, q_ref, k_ref, v_ref, g_ref):
    shift, scale = _mod_cols(mod_ref, n_seq, 0), _mod_cols(mod_ref, n_seq, 1)
    hb = jnp.concatenate(
        [_rmsnorm_mod(x_ref[i], gmix_ref[...], scale, shift).astype(BF16)
         for i in range(n_pos)], axis=0)

    def proj(k):
        return _dot(hb, win_ref[:, k * CONV_CH:(k + 1) * CONV_CH].astype(BF16))

    u = proj(0) * jax.nn.sigmoid(proj(1))
    n_hist = CONV_K - 1
    rows = lambda a, i: a[i * n_seq:(i + 1) * n_seq]
    outs = []
    for m in range(n_hist - n_pos):
        hist_ref[m] = cache_ref[m + n_pos]
    for i in range(n_pos):
        hist_ref[n_hist - n_pos + i] = rows(u, i)
        dw = jnp.broadcast_to(convb_ref[...], (n_seq, CONV_CH))
        for k in range(CONV_K):
            m = i + k
            src = cache_ref[m] if m < n_hist else rows(u, m - n_hist)
            dw = dw + convw_ref[k:k + 1, :] * src
        outs.append(_silu(_layernorm(dw, clng_ref[...], clnb_ref[...])).astype(BF16))
    conv_ref[...] = jnp.concatenate(outs, axis=0)

    lane = lax.broadcasted_iota(jnp.int32, (1, PAIR_LANES), 1)
    first_half = (lane % RET_DK) < (RET_DK // 2)
    q = proj(2) * (RET_DK ** -0.5)
    kk = proj(3)
    for i in range(n_pos):
        rs = slice(i * n_seq, (i + 1) * n_seq)
        cos = cos_ref[i:i + 1, :]
        sin = sin_ref[i:i + 1, :]
        for p in range(N_PAIRS):
            sl = slice(p * PAIR_LANES, (p + 1) * PAIR_LANES)
            q_ref[rs, sl] = _rotary_pair(q[rs, sl], cos, sin, first_half)
            k_ref[rs, sl] = _rotary_pair(kk[rs, sl], cos, sin, first_half)
    v_ref[...] = proj(4)
    g_ref[...] = proj(5)


def _proj_sample(x, mod, g_mix, w_in, cache_t, conv_w, conv_b, cln_g, cln_b, rope):
    n_pos, n_seq, d = x.shape
    n_tok = n_pos * n_seq
    cos, sin = rope
    row = lambda a: a.reshape(1, -1)
    tok = lambda dt: jax.ShapeDtypeStruct((n_tok, RET_W), dt)
    return pl.pallas_call(
        functools.partial(_proj_sample_kernel, n_pos, n_seq),
        out_shape=(jax.ShapeDtypeStruct(cache_t.shape, F32), tok(BF16),
                   tok(F32), tok(F32), tok(F32), tok(F32)),
        compiler_params=pltpu.CompilerParams(vmem_limit_bytes=VMEM_LIMIT),
        name="proj_sample",
    )(x, mod, row(g_mix), w_in, cache_t, conv_w, row(conv_b), row(cln_g), row(cln_b), cos, sin)


def _ret_sample_kernel(n_pos, n_seq, q_ref, k_ref, v_ref, g_ref, st_ref, pw_ref, rlng_ref,
                       rlnb_ref, out_ref, snew_ref, sb_buf, qd_buf, kd_buf, v_buf):
    lane = lax.broadcasted_iota(jnp.int32, (1, PAIR_LANES), 1)
    lo = lane < RET_DK
    rows = lambda a, i: a[i * n_seq:(i + 1) * n_seq]
    cols = lambda a, i: a[:, i * n_seq:(i + 1) * n_seq]
    pw = pw_ref[...]
    q = q_ref[...]
    k = k_ref[...]
    qb = [rows(q, i).astype(BF16) for i in range(n_pos)]
    kb = [rows(k, i).astype(BF16) for i in range(n_pos)]
    vb = [rows(v_ref[...], i).astype(BF16) for i in range(n_pos)]

    o = []
    for i in range(n_pos):
        acc = jnp.zeros((n_seq, PAIR_LANES), F32)
        for j in range(i + 1):
            prod = qb[i].astype(F32) * kb[j].astype(F32)
            s_lo = jnp.sum(jnp.where(lo, prod, 0.0), axis=-1, keepdims=True)
            s_hi = jnp.sum(jnp.where(lo, 0.0, prod), axis=-1, keepdims=True)
            score = jnp.where(lo, s_lo, s_hi) * pw[i - j:i - j + 1, :]
            acc = acc + score.astype(BF16).astype(F32) * vb[j].astype(F32)
        o.append(acc)

    rounded = lambda x: x.astype(BF16).astype(F32)
    qd_buf[...] = jnp.concatenate(
        [rounded(rows(q, i) * pw[i + 1:i + 2, :]) for i in range(n_pos)], axis=0).T
    kd_buf[...] = jnp.concatenate(
        [rounded(rows(k, j) * pw[n_pos - 1 - j:n_pos - j, :]) for j in range(n_pos)], axis=0).T
    v_buf[...] = jnp.concatenate([vb[j].astype(F32) for j in range(n_pos)], axis=0).T
    tile_at = lambda r: pl.ds(pl.multiple_of(r * RET_DV, RET_DV), RET_DV)
    pos = lambda i: slice(i * n_seq, (i + 1) * n_seq)

    for a in range(2):
        gam = pw[n_pos:n_pos + 1, a * RET_DK:a * RET_DK + 1]
        v_rows = slice(a * RET_DV, (a + 1) * RET_DV)

        def update(grp, carry, a=a, gam=gam, v_rows=v_rows):
            r0 = pl.multiple_of(a * RET_DK + grp * SUBLANES, SUBLANES)
            kd_rows = kd_buf[pl.ds(r0, SUBLANES), :]
            for s in range(SUBLANES):
                r = r0 + s
                s_old = st_ref[tile_at(r), :]
                sb_buf[tile_at(r), :] = rounded(s_old)
                s_new = s_old * gam
                for j in range(n_pos):
                    s_new = s_new + kd_rows[s:s + 1, pos(j)] * v_buf[v_rows, pos(j)]
                snew_ref[tile_at(r), :] = s_new
            return carry

        lax.fori_loop(0, RET_DK // SUBLANES, update, 0)

    for i in range(n_pos):
        heads = []
        for a in range(2):
            def cross(grp, acc, a=a, i=i):
                r0 = pl.multiple_of(a * RET_DK + grp * SUBLANES, SUBLANES)
                qd_rows = qd_buf[pl.ds(r0, SUBLANES), pos(i)]
                for s in range(SUBLANES):
                    acc = acc + qd_rows[s:s + 1, :] * sb_buf[tile_at(r0 + s), :]
                return acc

            heads.append(lax.fori_loop(0, RET_DK // SUBLANES, cross,
                                       jnp.zeros((RET_DV, n_seq), F32)))
        oi = o[i] + jnp.concatenate(heads, axis=0).T
        on = _pair_groupnorm(oi, lo, rlng_ref[...], rlnb_ref[...])
        out_ref[i * n_seq:(i + 1) * n_seq, :] = (on * _silu(rows(g_ref[...], i))).astype(BF16)


def _ret_sample(q, k, v, g, state_t, pw, rln_g, rln_b, n_pos, n_seq):
    n_tok = n_pos * n_seq
    pair_state = 2 * RET_DK * RET_DV
    slab = pl.BlockSpec((n_tok, PAIR_LANES), lambda p: (0, p))
    state_spec = pl.BlockSpec((pair_state, n_seq), lambda p: (p, 0))
    return pl.pallas_call(
        functools.partial(_ret_sample_kernel, n_pos, n_seq),
        grid=(N_PAIRS,),
        in_specs=[slab, slab, slab, slab, state_spec,
                  pl.BlockSpec((8, PAIR_LANES), lambda p: (0, p)),
                  pl.BlockSpec((1, PAIR_LANES), lambda p: (0, p)),
                  pl.BlockSpec((1, PAIR_LANES), lambda p: (0, p))],
        out_specs=[slab, state_spec],
        out_shape=(jax.ShapeDtypeStruct((n_tok, RET_W), BF16),
                   jax.ShapeDtypeStruct(state_t.shape, F32)),
        scratch_shapes=[pltpu.VMEM((pair_state, n_seq), F32)]
        + [pltpu.VMEM((PAIR_LANES, n_tok), F32)] * 3,
        compiler_params=pltpu.CompilerParams(
            dimension_semantics=("arbitrary",), vmem_limit_bytes=VMEM_LIMIT),
        name="ret_sample",
    )(q, k, v, g, state_t, pw, rln_g.reshape(1, RET_W), rln_b.reshape(1, RET_W))


def _unpack_state(s2):
    blocks = []
    for p in range(N_PAIRS):
        for a in range(2):
            sl = slice(a * RET_DK, (a + 1) * RET_DK)
            blocks.append(s2[:, p, sl, sl])
    return jnp.stack(blocks, axis=1)


def _prompt_path(x, mod, mod_row0, g_mix, w_in, conv_w, conv_b, cln_g, cln_b, rln_g, rln_b,
                 w_out, g_ffn, w_ff1, w_ff2, g_final, tile=PROMPT_TILE):
    seq = x.shape[1]
    rope = _rope_tables(seq, 0)
    decay = _decay_tables(RET_CHUNK)
    y, hist, s2 = _layer_prompt(x, mod, mod_row0, g_mix, w_in, conv_w, conv_b, cln_g, cln_b,
                                rln_g, rln_b, w_out, g_ffn, w_ff1, w_ff2, g_final, rope, decay,
                                tile)
    return y, hist[:, HIST_ROWS - (CONV_K - 1):], _unpack_state(s2)


def _sample_path(x, mod, cache, state, pos0, g_mix, w_in, conv_w, conv_b, cln_g, cln_b,
                 rln_g, rln_b, w_out, g_ffn, w_ff1, w_ff2, g_final):
    n_seq, n_pos, d = x.shape
    xs = x.transpose(1, 0, 2)
    rope = _rope_tables(8, pos0)
    hist_t, conv_out, q, k, v, g = _proj_sample(xs, mod, g_mix, w_in, cache.transpose(1, 0, 2),
                                                conv_w, conv_b, cln_g, cln_b, rope)
    ret_out, s_new_t = _ret_sample(q, k, v, g, state.reshape(n_seq, -1).T, _power_table(),
                                   rln_g, rln_b, n_pos, n_seq)
    s_new = s_new_t.T
    y = _mlp_sample(xs, conv_out, ret_out, w_out, mod, g_ffn, w_ff1, w_ff2, g_final)
    return y.transpose(1, 0, 2), hist_t.transpose(1, 0, 2), s_new.reshape(state.shape)


def kernel(x_prompt, x_sample, cache_conv, state_ret, c_prompt, c_sample, w_ada, b_ada, g_mix, w_in, conv_w, conv_b, conv_ln_g, conv_ln_b, ret_ln_g, ret_ln_b, w_out, g_ffn, w_ff1, w_ff2, g_final):
    mod = _modulation(jnp.concatenate([c_sample, c_prompt], axis=0), w_ada[0], b_ada[0])
    params = (g_mix[0], w_in[0], conv_w[0], conv_b[0], conv_ln_g[0], conv_ln_b[0], ret_ln_g[0],
              ret_ln_b[0], w_out[0], g_ffn[0], w_ff1[0], w_ff2[0], g_final)
    y_p, conv_p, ret_p = _prompt_path(x_prompt, mod, x_sample.shape[0], *params)
    y_s, conv_s, ret_s = _sample_path(x_sample, mod, cache_conv[0], state_ret[0], PAST_LEN,
                                      *params)
    return (y_p, y_s, conv_p[None], ret_p[None], conv_s[None], ret_s[None])
```

```python
import functools

import jax
import jax.numpy as jnp
from jax import lax
from jax.experimental import pallas as pl
from jax.experimental.pallas import tpu as pltpu

D_MODEL = 1024
CONV_CH = 512
CONV_K = 31
RET_HEADS = 8
RET_W = 512
RET_DK = 64
RET_DV = 64
D_FF = 4 * D_MODEL
RET_CHUNK = 128
ROPE_THETA = 10000.0
EPS = 1e-6
N_MOD = 6
PAST_LEN = 16384

SUBLANES = 8
HIST_ROWS = 32
CONV_ROW_BLOCK = 32
PROMPT_TILE = 256
FFN_CHUNK = 512
SAMPLE_FFN_CHUNK = 1024
SAMPLE_SEQ_BLOCK = 32
WEIGHT_STAGE_ELEMS = 512 * 1024
WEIGHT_STAGE_SLOTS = 3
LAYER_PROGRAM = ("c cf cf cf cf cf cf cf cf "
                 "rrrr rfrf rfrf r "
                 "rfrf rfrf r")
PAIR_LANES = 2 * RET_DK
N_PAIRS = RET_HEADS // 2
VMEM_LIMIT = 56 * 1024 * 1024

F32 = jnp.float32
BF16 = jnp.bfloat16


def _log_decay(head):
    return jnp.log1p(-jnp.exp2(-5.0 - head))


def _rope_kernel(pos0, cos_ref, sin_ref):
    rows = cos_ref.shape[0]
    lane = lax.broadcasted_iota(jnp.int32, (rows, PAIR_LANES), 1)
    row = lax.broadcasted_iota(jnp.int32, (rows, PAIR_LANES), 0)
    half = RET_DK // 2
    freq = (lane % half).astype(F32)
    inv = jnp.power(jnp.full_like(freq, ROPE_THETA), -freq / half)
    ang = (row + pos0).astype(F32) * inv
    first = (lane % RET_DK) < half
    cos_ref[...] = jnp.cos(ang)
    sin_ref[...] = jnp.where(first, -jnp.sin(ang), jnp.sin(ang))


ROPE_BLOCK = 16


def _rope_blocked_kernel(pos0, cos_ref, sin_ref):
    rows = cos_ref.shape[0]
    n_blk = rows // ROPE_BLOCK
    half = RET_DK // 2

    def angles(n, step, start):
        lane = lax.broadcasted_iota(jnp.int32, (n, PAIR_LANES), 1)
        row = lax.broadcasted_iota(jnp.int32, (n, PAIR_LANES), 0)
        freq = (lane % half).astype(F32)
        inv = jnp.power(jnp.full_like(freq, ROPE_THETA), -freq / half)
        return (row * step + start).astype(F32) * inv

    a = angles(n_blk, ROPE_BLOCK, pos0)
    b = angles(ROPE_BLOCK, 1, 0)
    lane = lax.broadcasted_iota(jnp.int32, (ROPE_BLOCK, PAIR_LANES), 1)
    sign = jnp.where((lane % RET_DK) < half, -1.0, 1.0)
    cos_a, sin_a = jnp.cos(a), jnp.sin(a)
    cos_b, sin_b = jnp.cos(b), jnp.sin(b)
    cos_bs, sin_bs = cos_b * sign, sin_b * sign
    for blk in range(n_blk):
        ca, sa = cos_a[blk:blk + 1, :], sin_a[blk:blk + 1, :]
        rs = slice(blk * ROPE_BLOCK, (blk + 1) * ROPE_BLOCK)
        cos_ref[rs, :] = ca * cos_b - sa * sin_b
        sin_ref[rs, :] = sa * cos_bs + ca * sin_bs


def _rope_tables(rows, pos0):
    blocked = rows % ROPE_BLOCK == 0 and rows // ROPE_BLOCK >= SUBLANES
    return pl.pallas_call(
        functools.partial(_rope_blocked_kernel if blocked else _rope_kernel, pos0),
        out_shape=(jax.ShapeDtypeStruct((rows, PAIR_LANES), F32),
                   jax.ShapeDtypeStruct((rows, PAIR_LANES), F32)),
        name="rope_tables",
    )()


def _decay_kernel(chunk, dec_ref, qdec_ref, kdec_ref, sdec_ref):
    c = chunk
    shape = (N_PAIRS, c, 2 * c)
    col = lax.broadcasted_iota(jnp.int32, shape, 2)
    h = (2 * lax.broadcasted_iota(jnp.int32, shape, 0) + col // c).astype(F32)
    i = lax.broadcasted_iota(jnp.int32, shape, 1).astype(F32)
    j = (col % c).astype(F32)
    diff = i - j
    dec_ref[...] = jnp.where(diff >= 0, jnp.exp(_log_decay(h) * jnp.maximum(diff, 0.0)), 0.0)
    lane_h = (lax.broadcasted_iota(jnp.int32, (c, RET_W), 1) // RET_DK).astype(F32)
    idx = lax.broadcasted_iota(jnp.int32, (c, RET_W), 0).astype(F32)
    lg = _log_decay(lane_h)
    qdec_ref[...] = jnp.exp(lg * (idx + 1.0))
    kdec_ref[...] = jnp.exp(lg * (c - 1.0 - idx))
    p = lax.broadcasted_iota(jnp.int32, (N_PAIRS, PAIR_LANES, PAIR_LANES), 0)
    r = lax.broadcasted_iota(jnp.int32, (N_PAIRS, PAIR_LANES, PAIR_LANES), 1) // RET_DK
    q = lax.broadcasted_iota(jnp.int32, (N_PAIRS, PAIR_LANES, PAIR_LANES), 2) // RET_DK
    hh = (2 * p + r).astype(F32)
    sdec_ref[...] = jnp.where(r == q, jnp.exp(_log_decay(hh) * float(c)), 0.0)


def _decay_tables(chunk):
    return pl.pallas_call(
        functools.partial(_decay_kernel, chunk),
        out_shape=(jax.ShapeDtypeStruct((N_PAIRS, chunk, 2 * chunk), F32),
                   jax.ShapeDtypeStruct((chunk, RET_W), F32),
                   jax.ShapeDtypeStruct((chunk, RET_W), F32),
                   jax.ShapeDtypeStruct((N_PAIRS, PAIR_LANES, PAIR_LANES), F32)),
        name="decay_tables",
    )()


def _mod_kernel(c_ref, w_lo_ref, w_hi_ref, b_ref, o_ref):
    c = c_ref[...]
    s = (c * jax.nn.sigmoid(c)).astype(BF16)
    half = w_lo_ref.shape[1]
    for k, w_ref in enumerate((w_lo_ref, w_hi_ref)):
        cs = slice(k * half, (k + 1) * half)
        o_ref[:, cs] = _dot(s, w_ref[...].astype(BF16)) + b_ref[:, cs]


def _modulation(c, w_ada, b_ada):
    n, d = c.shape
    width = w_ada.shape[1]
    blk = D_MODEL
    half = lambda k: pl.BlockSpec((d, blk // 2), lambda i: (0, 2 * i + k))
    return pl.pallas_call(
        _mod_kernel,
        grid=(width // blk,),
        in_specs=[pl.BlockSpec((n, d), lambda i: (0, 0)), half(0), half(1),
                  pl.BlockSpec((1, blk), lambda i: (0, i))],
        out_specs=pl.BlockSpec((n, blk), lambda i: (0, i)),
        out_shape=jax.ShapeDtypeStruct((n, width), F32),
        compiler_params=pltpu.CompilerParams(vmem_limit_bytes=VMEM_LIMIT),
        name="adaln_mod",
    )(c, w_ada, w_ada, b_ada.reshape(1, width))


def _rmsnorm_mod(x, g, scale, shift):
    y = x * lax.rsqrt(jnp.mean(x * x, axis=-1, keepdims=True) + EPS)
    return (y * g) * (1.0 + scale) + shift


def _layernorm(x, g, b):
    mu = jnp.mean(x, axis=-1, keepdims=True)
    d = x - mu
    var = jnp.mean(d * d, axis=-1, keepdims=True)
    return d * lax.rsqrt(var + EPS) * g + b


def _silu(x):
    return x * jax.nn.sigmoid(x)


def _rotary_pair(x, cos, sin_signed, first_half):
    partner = jnp.where(first_half, pltpu.roll(x, PAIR_LANES - RET_DK // 2, 1),
                        pltpu.roll(x, RET_DK // 2, 1))
    return x * cos + partner * sin_signed


def _pair_groupnorm(o, lo, g, b):
    inv_n = 1.0 / RET_DV
    s_lo = jnp.sum(jnp.where(lo, o, 0.0), axis=-1, keepdims=True)
    s_hi = jnp.sum(jnp.where(lo, 0.0, o), axis=-1, keepdims=True)
    d = o - jnp.where(lo, s_lo, s_hi) * inv_n
    d2 = d * d
    v_lo = jnp.sum(jnp.where(lo, d2, 0.0), axis=-1, keepdims=True)
    v_hi = jnp.sum(jnp.where(lo, 0.0, d2), axis=-1, keepdims=True)
    var = jnp.where(lo, v_lo, v_hi) * inv_n
    return d * lax.rsqrt(var + EPS) * g + b


def _dot(a, b):
    return jnp.dot(a, b, preferred_element_type=F32)


def _dot_nt(a, b):
    return lax.dot_general(a, b, (((1,), (1,)), ((), ())), preferred_element_type=F32)


def _dot_tn(a, b):
    return lax.dot_general(a, b, (((0,), (0,)), ((), ())), preferred_element_type=F32)


def _stream_cast_weights(pairs, stage, sems):
    n_slots, rows = stage.shape[0], stage.shape[1]
    chunks = [(src, dst, r0) for src, dst in pairs for r0 in range(0, src.shape[0], rows)]

    def copy(c):
        src, _, r0 = chunks[c]
        slot = c % n_slots
        return pltpu.make_async_copy(src.at[pl.ds(r0, rows), :],
                                     stage.at[slot, :, pl.ds(0, src.shape[1])], sems.at[slot])

    for c in range(min(n_slots - 1, len(chunks))):
        copy(c).start()
    for c, (src, dst, r0) in enumerate(chunks):
        if c + n_slots - 1 < len(chunks):
            copy(c + n_slots - 1).start()
        copy(c).wait()
        dst[r0:r0 + rows, :] = stage[c % n_slots, :, 0:src.shape[1]].astype(BF16)


def _layer_prompt_kernel(tile, nj, x_ref, mod_ref, gmix_ref, win_hbm, convw_ref, convb_ref,
                         clng_ref, clnb_ref, rlng_ref, rlnb_ref, wout_hbm,
                         cos_ref, sin_ref, dec_ref, qdec_ref, kdec_ref, sdec_ref,
                         mod2_ref, gffn_ref, w1_hbm, w2_hbm, gfin_ref,
                         y_ref, hist_ref, state_ref,
                         ubuf, ushift, sbuf, mixbuf, x1buf,
                         win_ref, wout_ref, w1_ref, w2_ref,
                         stage_wide, stage_tall, sems_wide, sems_tall):
    g = pl.program_id(0)
    total = pl.num_programs(0) - 1
    j = lax.rem(jnp.minimum(g, total - 1), nj)
    live = g < total

    @pl.when(g == 0)
    def _():
        _stream_cast_weights([(win_hbm, win_ref), (w1_hbm, w1_ref)], stage_wide, sems_wide)
        _stream_cast_weights([(wout_hbm, wout_ref), (w2_hbm, w2_ref)], stage_tall, sems_tall)
        x1buf[...] = jnp.zeros((tile, D_MODEL), F32)

    @pl.when(jnp.logical_and(j == 0, live))
    def _():
        ubuf[0:HIST_ROWS, :] = jnp.zeros((HIST_ROWS, CONV_CH), F32)
        sbuf[...] = jnp.zeros(sbuf.shape, F32)

    x = x_ref[0]
    b_mix = lax.div(jnp.minimum(g, total - 1), nj)
    b_mlp = lax.div(jnp.maximum(g - 1, 0), nj)
    mods = lambda ref, b: [ref[pl.ds(b, 1), k * D_MODEL:(k + 1) * D_MODEL] for k in range(3)]
    shift, scale, gate = mods(mod_ref, b_mix)

    def conv_stages(u):
        ubuf[HIST_ROWS:HIST_ROWS + tile, :] = u
        off = HIST_ROWS - (CONV_K - 1)
        for r in range(SUBLANES):
            n = tile + SUBLANES * ((CONV_K - 1 - r) // SUBLANES)
            ushift[r, 0:n, :] = ubuf[off + r:off + r + n, :]
        ubuf[0:HIST_ROWS, :] = ubuf[tile:tile + HIST_ROWS, :]
        yield
        for rb in range(tile // CONV_ROW_BLOCK):
            r0 = rb * CONV_ROW_BLOCK
            dw = jnp.broadcast_to(convb_ref[...], (CONV_ROW_BLOCK, CONV_CH))
            for k in range(CONV_K):
                s0 = r0 + SUBLANES * (k // SUBLANES)
                wk = jnp.concatenate([convw_ref[k]] * (CONV_ROW_BLOCK // SUBLANES), axis=0)
                dw = dw + wk * ushift[k % SUBLANES, s0:s0 + CONV_ROW_BLOCK, :]
            conv_out = _silu(_layernorm(dw, clng_ref[...], clnb_ref[...]))
            mixbuf[r0:r0 + CONV_ROW_BLOCK, 0:CONV_CH] = conv_out.astype(BF16)
            yield

    def ret_stages(proj):
        ret_proj = []
        for k in range(4):
            ret_proj.append(proj(2 + k))
            yield
        q, kk, v, gt = ret_proj
        q = q * (RET_DK ** -0.5)
        lane = lax.broadcasted_iota(jnp.int32, (1, PAIR_LANES), 1)
        lo = lane < RET_DK
        first_half = (lane % RET_DK) < (RET_DK // 2)
        lo_b = lo.astype(BF16)
        hi_b = (~lo).astype(BF16)
        pairs = range(N_PAIRS)
        halves = (pairs[:N_PAIRS // 2], pairs[N_PAIRS // 2:])
        lanes = [slice(p * PAIR_LANES, (p + 1) * PAIR_LANES) for p in pairs]
        for c in range(tile // RET_CHUNK):
            rs = slice(c * RET_CHUNK, (c + 1) * RET_CHUNK)
            cos = cos_ref[rs, :]
            sin = sin_ref[rs, :]
            qc, kc, vc, scores, outs = {}, {}, {}, {}, {}
            for group in halves:
                for p in group:
                    qc[p] = _rotary_pair(q[rs, lanes[p]], cos, sin, first_half)
                    kc[p] = _rotary_pair(kk[rs, lanes[p]], cos, sin, first_half)
                    vc[p] = v[rs, lanes[p]].astype(BF16)
                    kb = kc[p].astype(BF16)
                    k_cat = jnp.concatenate([kb * lo_b, kb * hi_b], axis=0)
                    scores[p] = _dot_nt(qc[p].astype(BF16), k_cat)
                yield
            for group in halves:
                for p in group:
                    state = sbuf[p]
                    lhs = jnp.concatenate([(scores[p] * dec_ref[p]).astype(BF16),
                                           (qc[p] * qdec_ref[:, lanes[p]]).astype(BF16)], axis=1)
                    rhs = jnp.concatenate([vc[p] * lo_b, vc[p] * hi_b, state.astype(BF16)],
                                          axis=0)
                    outs[p] = _dot(lhs, rhs)
                    kd = (kc[p] * kdec_ref[:, lanes[p]]).astype(BF16)
                    sbuf[p] = (state * sdec_ref[p]
                               + jnp.where(sdec_ref[p] > 0.0, _dot_tn(kd, vc[p]), 0.0))
                yield
            for p in pairs:
                on = _pair_groupnorm(outs[p], lo, rlng_ref[:, lanes[p]], rlnb_ref[:, lanes[p]])
                mixbuf[rs, CONV_CH + p * PAIR_LANES:CONV_CH + (p + 1) * PAIR_LANES] = (
                    on * _silu(gt[rs, lanes[p]])).astype(BF16)
            yield

    ffn = _ffn_stages(x1buf[...], mods(mod2_ref, b_mlp), gffn_ref, w1_ref, w2_ref, gfin_ref, y_ref)
    next(ffn)
    hb = _rmsnorm_mod(x, gmix_ref[...], scale, shift).astype(BF16)

    def proj(k):
        return _dot(hb, win_ref[:, k * CONV_CH:(k + 1) * CONV_CH])

    streams = {"f": ffn, "c": conv_stages(proj(0) * jax.nn.sigmoid(proj(1))),
               "r": ret_stages(proj)}
    for name in LAYER_PROGRAM.replace(" ", ""):
        next(streams[name], None)
    for stream in (streams["c"], streams["r"], streams["f"]):
        for _ in stream:
            pass

    mix = _dot(mixbuf[...], wout_ref[...])
    x1buf[...] = x + gate * mix

    @pl.when(jnp.logical_and(j == nj - 1, live))
    def _():
        hist_ref[0] = ubuf[0:HIST_ROWS, :]
        state_ref[0] = sbuf[...]


def _layer_prompt(x, mod, mod_row0, g_mix, w_in, conv_w, conv_b, cln_g, cln_b, rln_g, rln_b,
                  w_out, g_ffn, w1, w2, g_final, rope, decay, tile):
    bsz, seq, d = x.shape
    assert bsz % SUBLANES == 0 and mod_row0 % bsz == 0
    in_hbm = pl.BlockSpec(memory_space=pl.ANY)

    def stage_shape(*ws):
        width = max(w.shape[1] for w in ws)
        rows = WEIGHT_STAGE_ELEMS // width
        assert all(w.shape[0] % rows == 0 for w in ws)
        return pltpu.VMEM((WEIGHT_STAGE_SLOTS, rows, width), F32)

    mod_blk = mod_row0 // bsz
    nj = seq // tile
    total = bsz * nj
    cos, sin = rope
    dec, qdec, kdec, sdec = decay
    mix_tile = lambda g: jnp.minimum(g, total - 1)
    ffn_tile = lambda g: jnp.maximum(g - 1, 0)
    resident = lambda shape: pl.BlockSpec(shape, lambda g: (0,) * len(shape),
                                          pipeline_mode=pl.Buffered(1))
    row = lambda a: a.reshape(1, -1)
    conv_w_rows = jnp.broadcast_to(conv_w[:, None, :], (CONV_K, SUBLANES, CONV_CH))
    rope_spec = pl.BlockSpec((tile, PAIR_LANES), lambda g: (mix_tile(g) % nj, 0))
    return pl.pallas_call(
        functools.partial(_layer_prompt_kernel, tile, nj),
        grid=(total + 1,),
        in_specs=[
            pl.BlockSpec((1, tile, d), lambda g: (mix_tile(g) // nj, mix_tile(g) % nj, 0)),
            pl.BlockSpec((bsz, 3 * d), lambda g: (mod_blk, 0), pipeline_mode=pl.Buffered(1)),
            resident((1, d)),
            in_hbm,
            resident((CONV_K, SUBLANES, CONV_CH)),
            resident((1, CONV_CH)),
            resident((1, CONV_CH)),
            resident((1, CONV_CH)),
            resident((1, RET_W)),
            resident((1, RET_W)),
            in_hbm,
            rope_spec,
            rope_spec,
            resident(dec.shape),
            resident(qdec.shape),
            resident(kdec.shape),
            resident(sdec.shape),
            pl.BlockSpec((bsz, 3 * d), lambda g: (mod_blk, 1), pipeline_mode=pl.Buffered(1)),
            resident((1, d)),
            in_hbm,
            in_hbm,
            resident((1, d)),
        ],
        out_specs=[
            pl.BlockSpec((1, tile, d), lambda g: (ffn_tile(g) // nj, ffn_tile(g) % nj, 0)),
            pl.BlockSpec((1, HIST_ROWS, CONV_CH), lambda g: (mix_tile(g) // nj, 0, 0)),
            pl.BlockSpec((1, N_PAIRS, PAIR_LANES, PAIR_LANES),
                         lambda g: (mix_tile(g) // nj, 0, 0, 0)),
        ],
        out_shape=(
            jax.ShapeDtypeStruct((bsz, seq, d), F32),
            jax.ShapeDtypeStruct((bsz, HIST_ROWS, CONV_CH), F32),
            jax.ShapeDtypeStruct((bsz, N_PAIRS, PAIR_LANES, PAIR_LANES), F32),
        ),
        scratch_shapes=[
            pltpu.VMEM((HIST_ROWS + tile, CONV_CH), F32),
            pltpu.VMEM((SUBLANES, HIST_ROWS + tile, CONV_CH), F32),
            pltpu.VMEM((N_PAIRS, PAIR_LANES, PAIR_LANES), F32),
            pltpu.VMEM((tile, D_MODEL), BF16),
            pltpu.VMEM((tile, D_MODEL), F32),
            pltpu.VMEM(w_in.shape, BF16),
            pltpu.VMEM(w_out.shape, BF16),
            pltpu.VMEM(w1.shape, BF16),
            pltpu.VMEM(w2.shape, BF16),
            stage_shape(w_in, w1),
            stage_shape(w_out, w2),
            pltpu.SemaphoreType.DMA((WEIGHT_STAGE_SLOTS,)),
            pltpu.SemaphoreType.DMA((WEIGHT_STAGE_SLOTS,)),
        ],
        compiler_params=pltpu.CompilerParams(
            dimension_semantics=("arbitrary",), vmem_limit_bytes=VMEM_LIMIT),
        name="layer_prompt",
    )(x, mod, row(g_mix), w_in, conv_w_rows, row(conv_b), row(cln_g), row(cln_b),
      row(rln_g), row(rln_b), w_out, cos, sin, dec, qdec, kdec, sdec,
      mod, row(g_ffn), w1, w2, row(g_final))


def _ffn_stages(x, mods, gffn_ref, w1_ref, w2_ref, gfin_ref, y_ref):
    shift, scale, gate = mods
    hb = _rmsnorm_mod(x, gffn_ref[...], scale, shift).astype(BF16)
    yield
    half = D_MODEL // 2
    ff = [None, None]
    for c in range(D_FF // (2 * FFN_CHUNK)):
        acts = []
        for h in range(2):
            c0 = (2 * c + h) * FFN_CHUNK
            a = jnp.maximum(_dot(hb, w1_ref[:, c0:c0 + FFN_CHUNK]), 0.0)
            acts.append((a * a).astype(BF16))
            yield
        act = jnp.concatenate(acts, axis=1)
        for n in range(2):
            part = _dot(act, w2_ref[2 * c * FFN_CHUNK:2 * (c + 1) * FFN_CHUNK,
                                    n * half:(n + 1) * half])
            ff[n] = part if ff[n] is None else ff[n] + part
            yield
    z = x + gate * jnp.concatenate(ff, axis=1)
    y_ref[0] = z * lax.rsqrt(jnp.mean(z * z, axis=-1, keepdims=True) + EPS) * gfin_ref[...]


def _mod_cols(mod_ref, n_seq, k):
    return mod_ref[0:n_seq, k * D_MODEL:(k + 1) * D_MODEL]


def _mlp_sample_kernel(n_pos, n_seq, x_ref, conv_ref, ret_ref, wout_ref, mod_ref,
                       gffn_ref, w1_ref, w2_ref, gfin_ref, y_ref, x1buf, hbuf, acc):
    c = pl.program_id(0)
    rows = lambda i: slice(i * n_seq, (i + 1) * n_seq)

    @pl.when(c == 0)
    def _():
        mix = (_dot(conv_ref[...], wout_ref[0:CONV_CH, :].astype(BF16))
               + _dot(ret_ref[...], wout_ref[CONV_CH:CONV_CH + RET_W, :].astype(BF16)))
        for i in range(n_pos):
            x1 = x_ref[i] + _mod_cols(mod_ref, n_seq, 2) * mix[rows(i)]
            x1buf[rows(i), :] = x1
            hbuf[rows(i), :] = _rmsnorm_mod(x1, gffn_ref[...], _mod_cols(mod_ref, n_seq, 4),
                                            _mod_cols(mod_ref, n_seq, 3)).astype(BF16)

    a = jnp.maximum(_dot(hbuf[...], w1_ref[...].astype(BF16)), 0.0)
    part = _dot((a * a).astype(BF16), w2_ref[...].astype(BF16))

    @pl.when(c == 0)
    def _():
        acc[...] = part

    @pl.when(c > 0)
    def _():
        acc[...] += part

    @pl.when(c == pl.num_programs(0) - 1)
    def _():
        for i in range(n_pos):
            z = x1buf[rows(i), :] + _mod_cols(mod_ref, n_seq, 5) * acc[rows(i), :]
            y_ref[i] = (z * lax.rsqrt(jnp.mean(z * z, axis=-1, keepdims=True) + EPS)
                        * gfin_ref[...])


def _mlp_sample(x, conv_out, ret_out, w_out, mod, g_ffn, w1, w2, g_final):
    n_pos, n_seq, d = x.shape
    n_tok = n_pos * n_seq
    n_chunks = D_FF // SAMPLE_FFN_CHUNK
    whole = lambda shape: pl.BlockSpec(shape, lambda c: (0,) * len(shape),
                                       pipeline_mode=pl.Buffered(1))
    return pl.pallas_call(
        functools.partial(_mlp_sample_kernel, n_pos, n_seq),
        grid=(n_chunks,),
        in_specs=[whole(x.shape), whole(conv_out.shape), whole(ret_out.shape),
                  whole(w_out.shape), whole(mod.shape), whole((1, d)),
                  pl.BlockSpec((d, SAMPLE_FFN_CHUNK), lambda c: (0, c)),
                  pl.BlockSpec((SAMPLE_FFN_CHUNK, d), lambda c: (c, 0)),
                  whole((1, d))],
        out_specs=pl.BlockSpec(x.shape, lambda c: (0, 0, 0)),
        out_shape=jax.ShapeDtypeStruct(x.shape, F32),
        scratch_shapes=[pltpu.VMEM((n_tok, d), F32), pltpu.VMEM((n_tok, d), BF16),
                        pltpu.VMEM((n_tok, d), F32)],
        compiler_params=pltpu.CompilerParams(
            dimension_semantics=("arbitrary",), vmem_limit_bytes=VMEM_LIMIT),
        name="mlp_sample",
    )(x, conv_out, ret_out, w_out, mod, g_ffn.reshape(1, d), w1, w2, g_final.reshape(1, d))


def _power_table_kernel(o_ref):
    lane_h = (lax.broadcasted_iota(jnp.int32, o_ref.shape, 1) // RET_DK).astype(F32)
    n = lax.broadcasted_iota(jnp.int32, o_ref.shape, 0).astype(F32)
    o_ref[...] = jnp.exp(_log_decay(lane_h) * n)


def _power_table():
    return pl.pallas_call(_power_table_kernel,
                          out_shape=jax.ShapeDtypeStruct((8, RET_W), F32),
                          name="decay_powers")()


def _proj_sample_kernel(n_pos, n_seq, x_ref, mod_ref, gmix_ref, win_ref, cache_ref, convw_ref,
                        convb_ref, clng_ref, clnb_ref, cos_ref, sin_ref,
                        hist_ref, conv_ref, q_ref, k_ref, v_ref, g_ref):
    shift, scale = _mod_cols(mod_ref, n_seq, 0), _mod_cols(mod_ref, n_seq, 1)
    hb = jnp.concatenate(
        [_rmsnorm_mod(x_ref[i], gmix_ref[...], scale, shift).astype(BF16)
         for i in range(n_pos)], axis=0)

    def proj(k):
        return _dot(hb, win_ref[:, k * CONV_CH:(k + 1) * CONV_CH].astype(BF16))

    u = proj(0) * jax.nn.sigmoid(proj(1))
    n_hist = CONV_K - 1
    rows = lambda a, i: a[i * n_seq:(i + 1) * n_seq]
    for m in range(n_hist - n_pos):
        hist_ref[m] = cache_ref[m + n_pos]
    for i in range(n_pos):
        hist_ref[n_hist - n_pos + i] = rows(u, i)
        dw = jnp.broadcast_to(convb_ref[...], (n_seq, CONV_CH))
        for k in range(CONV_K):
            m = i + k
            src = cache_ref[m] if m < n_hist else rows(u, m - n_hist)
            dw = dw + convw_ref[k:k + 1, :] * src
        conv_ref[i] = _silu(_layernorm(dw, clng_ref[...], clnb_ref[...])).astype(BF16)

    lane = lax.broadcasted_iota(jnp.int32, (1, PAIR_LANES), 1)
    first_half = (lane % RET_DK) < (RET_DK // 2)
    q = proj(2) * (RET_DK ** -0.5)
    kk = proj(3)
    v = proj(4)
    gt = proj(5)
    for i in range(n_pos):
        cos = cos_ref[i:i + 1, :]
        sin = sin_ref[i:i + 1, :]
        for p in range(N_PAIRS):
            sl = slice(p * PAIR_LANES, (p + 1) * PAIR_LANES)
            q_ref[i, :, sl] = _rotary_pair(rows(q, i)[:, sl], cos, sin, first_half)
            k_ref[i, :, sl] = _rotary_pair(rows(kk, i)[:, sl], cos, sin, first_half)
        v_ref[i] = rows(v, i)
        g_ref[i] = rows(gt, i)


def _proj_sample(x, mod, g_mix, w_in, cache_t, conv_w, conv_b, cln_g, cln_b, rope):
    n_pos, n_seq, d = x.shape
    blk = SAMPLE_SEQ_BLOCK
    cos, sin = rope
    row = lambda a: a.reshape(1, -1)
    whole = lambda a: pl.BlockSpec(a.shape, lambda s: (0,) * a.ndim, pipeline_mode=pl.Buffered(1))
    seqs = lambda lead, width: pl.BlockSpec((lead, blk, width), lambda s: (0, s, 0))
    tok = lambda dt: jax.ShapeDtypeStruct((n_pos, n_seq, RET_W), dt)
    args = (x, mod, row(g_mix), w_in, cache_t, conv_w, row(conv_b), row(cln_g), row(cln_b),
            cos, sin)
    return pl.pallas_call(
        functools.partial(_proj_sample_kernel, n_pos, blk),
        grid=(n_seq // blk,),
        in_specs=[seqs(n_pos, d), pl.BlockSpec((blk, mod.shape[1]), lambda s: (s, 0)),
                  whole(args[2]), whole(w_in), seqs(cache_t.shape[0], CONV_CH)]
        + [whole(a) for a in args[5:]],
        out_specs=[seqs(cache_t.shape[0], CONV_CH)] + [seqs(n_pos, RET_W)] * 5,
        out_shape=(jax.ShapeDtypeStruct(cache_t.shape, F32), tok(BF16),
                   tok(F32), tok(F32), tok(F32), tok(F32)),
        compiler_params=pltpu.CompilerParams(
            dimension_semantics=("arbitrary",), vmem_limit_bytes=VMEM_LIMIT),
        name="proj_sample",
    )(*args)


def _ret_sample_kernel(n_pos, n_seq, q_ref, k_ref, v_ref, g_ref, st_ref, pw_ref, rlng_ref,
                       rlnb_ref, out_ref, snew_ref, sb_buf, qd_buf, kd_buf, v_buf):
    lane = lax.broadcasted_iota(jnp.int32, (1, PAIR_LANES), 1)
    lo = lane < RET_DK
    rows = lambda a, i: a[i * n_seq:(i + 1) * n_seq]
    cols = lambda a, i: a[:, i * n_seq:(i + 1) * n_seq]
    pw = pw_ref[...]
    q = q_ref[...]
    k = k_ref[...]
    qb = [rows(q, i).astype(BF16) for i in range(n_pos)]
    kb = [rows(k, i).astype(BF16) for i in range(n_pos)]
    vb = [rows(v_ref[...], i).astype(BF16) for i in range(n_pos)]

    o = []
    for i in range(n_pos):
        acc = jnp.zeros((n_seq, PAIR_LANES), F32)
        for j in range(i + 1):
            prod = qb[i].astype(F32) * kb[j].astype(F32)
            s_lo = jnp.sum(jnp.where(lo, prod, 0.0), axis=-1, keepdims=True)
            s_hi = jnp.sum(jnp.where(lo, 0.0, prod), axis=-1, keepdims=True)
            score = jnp.where(lo, s_lo, s_hi) * pw[i - j:i - j + 1, :]
            acc = acc + score.astype(BF16).astype(F32) * vb[j].astype(F32)
        o.append(acc)

    rounded = lambda x: x.astype(BF16).astype(F32)
    qd_buf[...] = jnp.concatenate(
        [rounded(rows(q, i) * pw[i + 1:i + 2, :]) for i in range(n_pos)], axis=0).T
    kd_buf[...] = jnp.concatenate(
        [rounded(rows(k, j) * pw[n_pos - 1 - j:n_pos - j, :]) for j in range(n_pos)], axis=0).T
    v_buf[...] = jnp.concatenate([vb[j].astype(F32) for j in range(n_pos)], axis=0).T
    tile_at = lambda r: pl.ds(pl.multiple_of(r * RET_DV, RET_DV), RET_DV)
    pos = lambda i: slice(i * n_seq, (i + 1) * n_seq)

    for a in range(2):
        gam = pw[n_pos:n_pos + 1, a * RET_DK:a * RET_DK + 1]
        v_rows = slice(a * RET_DV, (a + 1) * RET_DV)

        def update(grp, carry, a=a, gam=gam, v_rows=v_rows):
            r0 = pl.multiple_of(a * RET_DK + grp * SUBLANES, SUBLANES)
            kd_rows = kd_buf[pl.ds(r0, SUBLANES), :]
            for s in range(SUBLANES):
                r = r0 + s
                s_old = st_ref[tile_at(r), :]
                sb_buf[tile_at(r), :] = rounded(s_old)
                s_new = s_old * gam
                for j in range(n_pos):
                    s_new = s_new + kd_rows[s:s + 1, pos(j)] * v_buf[v_rows, pos(j)]
                snew_ref[tile_at(r), :] = s_new
            return carry

        lax.fori_loop(0, RET_DK // SUBLANES, update, 0)

    for i in range(n_pos):
        heads = []
        for a in range(2):
            def cross(grp, acc, a=a, i=i):
                r0 = pl.multiple_of(a * RET_DK + grp * SUBLANES, SUBLANES)
                qd_rows = qd_buf[pl.ds(r0, SUBLANES), pos(i)]
                for s in range(SUBLANES):
                    acc = acc + qd_rows[s:s + 1, :] * sb_buf[tile_at(r0 + s), :]
                return acc

            heads.append(lax.fori_loop(0, RET_DK // SUBLANES, cross,
                                       jnp.zeros((RET_DV, n_seq), F32)))
        oi = o[i] + jnp.concatenate(heads, axis=0).T
        on = _pair_groupnorm(oi, lo, rlng_ref[...], rlnb_ref[...])
        out_ref[i * n_seq:(i + 1) * n_seq, :] = (on * _silu(rows(g_ref[...], i))).astype(BF16)


def _ret_sample(q, k, v, g, state_t, pw, rln_g, rln_b, n_pos, n_seq):
    n_tok = n_pos * n_seq
    pair_state = 2 * RET_DK * RET_DV
    slab = pl.BlockSpec((n_tok, PAIR_LANES), lambda p: (0, p))
    state_spec = pl.BlockSpec((pair_state, n_seq), lambda p: (p, 0))
    return pl.pallas_call(
        functools.partial(_ret_sample_kernel, n_pos, n_seq),
        grid=(N_PAIRS,),
        in_specs=[slab, slab, slab, slab, state_spec,
                  pl.BlockSpec((8, PAIR_LANES), lambda p: (0, p)),
                  pl.BlockSpec((1, PAIR_LANES), lambda p: (0, p)),
                  pl.BlockSpec((1, PAIR_LANES), lambda p: (0, p))],
        out_specs=[slab, state_spec],
        out_shape=(jax.ShapeDtypeStruct((n_tok, RET_W), BF16),
                   jax.ShapeDtypeStruct(state_t.shape, F32)),
        scratch_shapes=[pltpu.VMEM((pair_state, n_seq), F32)]
        + [pltpu.VMEM((PAIR_LANES, n_tok), F32)] * 3,
        compiler_params=pltpu.CompilerParams(
            dimension_semantics=("arbitrary",), vmem_limit_bytes=VMEM_LIMIT),
        name="ret_sample",
    )(q, k, v, g, state_t, pw, rln_g.reshape(1, RET_W), rln_b.reshape(1, RET_W))


def _unpack_state(s2):
    blocks = []
    for p in range(N_PAIRS):
        for a in range(2):
            sl = slice(a * RET_DK, (a + 1) * RET_DK)
            blocks.append(s2[:, p, sl, sl])
    return jnp.stack(blocks, axis=1)


def _prompt_path(x, mod, mod_row0, g_mix, w_in, conv_w, conv_b, cln_g, cln_b, rln_g, rln_b,
                 w_out, g_ffn, w_ff1, w_ff2, g_final, tile=PROMPT_TILE):
    seq = x.shape[1]
    rope = _rope_tables(seq, 0)
    decay = _decay_tables(RET_CHUNK)
    y, hist, s2 = _layer_prompt(x, mod, mod_row0, g_mix, w_in, conv_w, conv_b, cln_g, cln_b,
                                rln_g, rln_b, w_out, g_ffn, w_ff1, w_ff2, g_final, rope, decay,
                                tile)
    return y, hist[:, HIST_ROWS - (CONV_K - 1):], _unpack_state(s2)


def _sample_path(x, mod, cache, state, pos0, g_mix, w_in, conv_w, conv_b, cln_g, cln_b,
                 rln_g, rln_b, w_out, g_ffn, w_ff1, w_ff2, g_final):
    n_seq, n_pos, d = x.shape
    xs = x.transpose(1, 0, 2)
    rope = _rope_tables(8, pos0)
    hist_t, *tok = _proj_sample(xs, mod, g_mix, w_in, cache.transpose(1, 0, 2),
                                conv_w, conv_b, cln_g, cln_b, rope)
    conv_out, q, k, v, g = [a.reshape(n_pos * n_seq, -1) for a in tok]
    ret_out, s_new_t = _ret_sample(q, k, v, g, state.reshape(n_seq, -1).T, _power_table(),
                                   rln_g, rln_b, n_pos, n_seq)
    s_new = s_new_t.T
    y = _mlp_sample(xs, conv_out, ret_out, w_out, mod, g_ffn, w_ff1, w_ff2, g_final)
    return y.transpose(1, 0, 2), hist_t.transpose(1, 0, 2), s_new.reshape(state.shape)


def kernel(x_prompt, x_sample, cache_conv, state_ret, c_prompt, c_sample, w_ada, b_ada, g_mix, w_in, conv_w, conv_b, conv_ln_g, conv_ln_b, ret_ln_g, ret_ln_b, w_out, g_ffn, w_ff1, w_ff2, g_final):
    mod = _modulation(jnp.concatenate([c_sample, c_prompt], axis=0), w_ada[0], b_ada[0])
    params = (g_mix[0], w_in[0], conv_w[0], conv_b[0], conv_ln_g[0], conv_ln_b[0], ret_ln_g[0],
              ret_ln_b[0], w_out[0], g_ffn[0], w_ff1[0], w_ff2[0], g_final)
    y_p, conv_p, ret_p = _prompt_path(x_prompt, mod, x_sample.shape[0], *params)
    y_s, conv_s, ret_s = _sample_path(x_sample, mod, cache_conv[0], state_ret[0], PAST_LEN,
                                      *params)
    return (y_p, y_s, conv_p[None], ret_p[None], conv_s[None], ret_s[None])
```

```python
import functools

import jax
import jax.numpy as jnp
from jax import lax
from jax.experimental import pallas as pl
from jax.experimental.pallas import tpu as pltpu

D_MODEL = 1024
CONV_CH = 512
CONV_K = 31
RET_HEADS = 8
RET_W = 512
RET_DK = 64
RET_DV = 64
D_FF = 4 * D_MODEL
RET_CHUNK = 128
ROPE_THETA = 10000.0
EPS = 1e-6
N_MOD = 6
PAST_LEN = 16384

SUBLANES = 8
HIST_ROWS = 32
CONV_ROW_BLOCK = 32
PROMPT_TILE = 256
FFN_CHUNK = 512
SAMPLE_FFN_CHUNK = 1024
SAMPLE_SEQ_BLOCK = 32
WEIGHT_STAGE_ELEMS = 512 * 1024
WEIGHT_STAGE_SLOTS = 3
LAYER_PROGRAM = ("c cf cf cf cf cf cf cf cf "
                 "rrrr rfrf rfrf r "
                 "rfrf rfrf r")
PAIR_LANES = 2 * RET_DK
N_PAIRS = RET_HEADS // 2
VMEM_LIMIT = 56 * 1024 * 1024

F32 = jnp.float32
BF16 = jnp.bfloat16


def _log_decay(head):
    return jnp.log1p(-jnp.exp2(-5.0 - head))


def _rope_kernel(pos0, cos_ref, sin_ref):
    rows = cos_ref.shape[0]
    lane = lax.broadcasted_iota(jnp.int32, (rows, PAIR_LANES), 1)
    row = lax.broadcasted_iota(jnp.int32, (rows, PAIR_LANES), 0)
    half = RET_DK // 2
    freq = (lane % half).astype(F32)
    inv = jnp.power(jnp.full_like(freq, ROPE_THETA), -freq / half)
    ang = (row + pos0).astype(F32) * inv
    first = (lane % RET_DK) < half
    cos_ref[...] = jnp.cos(ang)
    sin_ref[...] = jnp.where(first, -jnp.sin(ang), jnp.sin(ang))


ROPE_BLOCK = 16


def _rope_blocked_kernel(pos0, cos_ref, sin_ref):
    rows = cos_ref.shape[0]
    n_blk = rows // ROPE_BLOCK
    half = RET_DK // 2

    def angles(n, step, start):
        lane = lax.broadcasted_iota(jnp.int32, (n, PAIR_LANES), 1)
        row = lax.broadcasted_iota(jnp.int32, (n, PAIR_LANES), 0)
        freq = (lane % half).astype(F32)
        inv = jnp.power(jnp.full_like(freq, ROPE_THETA), -freq / half)
        return (row * step + start).astype(F32) * inv

    a = angles(n_blk, ROPE_BLOCK, pos0)
    b = angles(ROPE_BLOCK, 1, 0)
    lane = lax.broadcasted_iota(jnp.int32, (ROPE_BLOCK, PAIR_LANES), 1)
    sign = jnp.where((lane % RET_DK) < half, -1.0, 1.0)
    cos_a, sin_a = jnp.cos(a), jnp.sin(a)
    cos_b, sin_b = jnp.cos(b), jnp.sin(b)
    cos_bs, sin_bs = cos_b * sign, sin_b * sign
    for blk in range(n_blk):
        ca, sa = cos_a[blk:blk + 1, :], sin_a[blk:blk + 1, :]
        rs = slice(blk * ROPE_BLOCK, (blk + 1) * ROPE_BLOCK)
        cos_ref[rs, :] = ca * cos_b - sa * sin_b
        sin_ref[rs, :] = sa * cos_bs + ca * sin_bs


def _rope_tables(rows, pos0):
    blocked = rows % ROPE_BLOCK == 0 and rows // ROPE_BLOCK >= SUBLANES
    return pl.pallas_call(
        functools.partial(_rope_blocked_kernel if blocked else _rope_kernel, pos0),
        out_shape=(jax.ShapeDtypeStruct((rows, PAIR_LANES), F32),
                   jax.ShapeDtypeStruct((rows, PAIR_LANES), F32)),
        name="rope_tables",
    )()


def _decay_kernel(chunk, dec_ref, qdec_ref, kdec_ref, sdec_ref):
    c = chunk
    shape = (N_PAIRS, c, 2 * c)
    col = lax.broadcasted_iota(jnp.int32, shape, 2)
    h = (2 * lax.broadcasted_iota(jnp.int32, shape, 0) + col // c).astype(F32)
    i = lax.broadcasted_iota(jnp.int32, shape, 1).astype(F32)
    j = (col % c).astype(F32)
    diff = i - j
    dec_ref[...] = jnp.where(diff >= 0, jnp.exp(_log_decay(h) * jnp.maximum(diff, 0.0)), 0.0)
    lane_h = (lax.broadcasted_iota(jnp.int32, (c, RET_W), 1) // RET_DK).astype(F32)
    idx = lax.broadcasted_iota(jnp.int32, (c, RET_W), 0).astype(F32)
    lg = _log_decay(lane_h)
    qdec_ref[...] = jnp.exp(lg * (idx + 1.0))
    kdec_ref[...] = jnp.exp(lg * (c - 1.0 - idx))
    p = lax.broadcasted_iota(jnp.int32, (N_PAIRS, PAIR_LANES, PAIR_LANES), 0)
    r = lax.broadcasted_iota(jnp.int32, (N_PAIRS, PAIR_LANES, PAIR_LANES), 1) // RET_DK
    q = lax.broadcasted_iota(jnp.int32, (N_PAIRS, PAIR_LANES, PAIR_LANES), 2) // RET_DK
    hh = (2 * p + r).astype(F32)
    sdec_ref[...] = jnp.where(r == q, jnp.exp(_log_decay(hh) * float(c)), 0.0)


def _decay_tables(chunk):
    return pl.pallas_call(
        functools.partial(_decay_kernel, chunk),
        out_shape=(jax.ShapeDtypeStruct((N_PAIRS, chunk, 2 * chunk), F32),
                   jax.ShapeDtypeStruct((chunk, RET_W), F32),
                   jax.ShapeDtypeStruct((chunk, RET_W), F32),
                   jax.ShapeDtypeStruct((N_PAIRS, PAIR_LANES, PAIR_LANES), F32)),
        name="decay_tables",
    )()


ADALN_COPIES = 4


def _mod_kernel(c_ref, *refs):
    w_refs, b_ref, o_ref = refs[:ADALN_COPIES], refs[ADALN_COPIES], refs[ADALN_COPIES + 1]
    c = c_ref[...]
    s = (c * jax.nn.sigmoid(c)).astype(BF16)
    part = w_refs[0].shape[1]
    for k, w_ref in enumerate(w_refs):
        cs = slice(k * part, (k + 1) * part)
        o_ref[:, cs] = _dot(s, w_ref[...].astype(BF16)) + b_ref[:, cs]


def _modulation(c, w_ada, b_ada):
    n, d = c.shape
    width = w_ada.shape[1]
    blk = D_MODEL
    part = lambda k: pl.BlockSpec((d, blk // ADALN_COPIES), lambda i: (0, ADALN_COPIES * i + k))
    return pl.pallas_call(
        _mod_kernel,
        grid=(width // blk,),
        in_specs=[pl.BlockSpec((n, d), lambda i: (0, 0))]
        + [part(k) for k in range(ADALN_COPIES)]
        + [pl.BlockSpec((1, blk), lambda i: (0, i))],
        out_specs=pl.BlockSpec((n, blk), lambda i: (0, i)),
        out_shape=jax.ShapeDtypeStruct((n, width), F32),
        compiler_params=pltpu.CompilerParams(vmem_limit_bytes=VMEM_LIMIT),
        name="adaln_mod",
    )(c, *([w_ada] * ADALN_COPIES), b_ada.reshape(1, width))


def _rmsnorm_mod(x, g, scale, shift):
    y = x * lax.rsqrt(jnp.mean(x * x, axis=-1, keepdims=True) + EPS)
    return (y * g) * (1.0 + scale) + shift


def _layernorm(x, g, b):
    mu = jnp.mean(x, axis=-1, keepdims=True)
    d = x - mu
    var = jnp.mean(d * d, axis=-1, keepdims=True)
    return d * lax.rsqrt(var + EPS) * g + b


def _silu(x):
    return x * jax.nn.sigmoid(x)


def _rotary_pair(x, cos, sin_signed, first_half):
    partner = jnp.where(first_half, pltpu.roll(x, PAIR_LANES - RET_DK // 2, 1),
                        pltpu.roll(x, RET_DK // 2, 1))
    return x * cos + partner * sin_signed


def _pair_groupnorm(o, lo, g, b):
    inv_n = 1.0 / RET_DV
    s_lo = jnp.sum(jnp.where(lo, o, 0.0), axis=-1, keepdims=True)
    s_hi = jnp.sum(jnp.where(lo, 0.0, o), axis=-1, keepdims=True)
    d = o - jnp.where(lo, s_lo, s_hi) * inv_n
    d2 = d * d
    v_lo = jnp.sum(jnp.where(lo, d2, 0.0), axis=-1, keepdims=True)
    v_hi = jnp.sum(jnp.where(lo, 0.0, d2), axis=-1, keepdims=True)
    var = jnp.where(lo, v_lo, v_hi) * inv_n
    return d * lax.rsqrt(var + EPS) * g + b


def _dot(a, b):
    return jnp.dot(a, b, preferred_element_type=F32)


def _dot_nt(a, b):
    return lax.dot_general(a, b, (((1,), (1,)), ((), ())), preferred_element_type=F32)


def _dot_tn(a, b):
    return lax.dot_general(a, b, (((0,), (0,)), ((), ())), preferred_element_type=F32)


def _stream_cast_weights(pairs, stage, sems):
    n_slots, rows = stage.shape[0], stage.shape[1]
    chunks = [(src, dst, r0) for src, dst in pairs for r0 in range(0, src.shape[0], rows)]

    def copy(c):
        src, _, r0 = chunks[c]
        slot = c % n_slots
        return pltpu.make_async_copy(src.at[pl.ds(r0, rows), :],
                                     stage.at[slot, :, pl.ds(0, src.shape[1])], sems.at[slot])

    for c in range(min(n_slots - 1, len(chunks))):
        copy(c).start()
    for c, (src, dst, r0) in enumerate(chunks):
        if c + n_slots - 1 < len(chunks):
            copy(c + n_slots - 1).start()
        copy(c).wait()
        dst[r0:r0 + rows, :] = stage[c % n_slots, :, 0:src.shape[1]].astype(BF16)


def _layer_prompt_kernel(tile, nj, *refs):
    g = pl.program_id(0)
    total = pl.num_programs(0) - 1
    pl.when(g == 0)(lambda: _layer_step(tile, nj, *refs, stage_weights=True, do_mlp=False))
    pl.when(jnp.logical_and(g > 0, g < total))(lambda: _layer_step(tile, nj, *refs))
    pl.when(g == total)(lambda: _layer_step(tile, nj, *refs, do_mix=False))


def _layer_step(tile, nj, x_ref, mod_ref, gmix_ref, win_hbm, convw_ref, convb_ref,
                clng_ref, clnb_ref, rlng_ref, rlnb_ref, wout_hbm,
                cos_ref, sin_ref, dec_ref, qdec_ref, kdec_ref, sdec_ref,
                mod2_ref, gffn_ref, w1_hbm, w2_hbm, gfin_ref,
                y_ref, hist_ref, state_ref,
                ubuf, ushift, sbuf, mixbuf, x1buf,
                win_ref, wout_ref, w1_ref, w2_ref,
                stage_wide, stage_tall, sems_wide, sems_tall,
                stage_weights=False, do_mix=True, do_mlp=True):
    g = pl.program_id(0)
    total = pl.num_programs(0) - 1
    j = lax.rem(jnp.minimum(g, total - 1), nj)

    if stage_weights:
        _stream_cast_weights([(win_hbm, win_ref), (w1_hbm, w1_ref)], stage_wide, sems_wide)
        _stream_cast_weights([(wout_hbm, wout_ref), (w2_hbm, w2_ref)], stage_tall, sems_tall)

    if do_mix:
        @pl.when(j == 0)
        def _():
            ubuf[0:HIST_ROWS, :] = jnp.zeros((HIST_ROWS, CONV_CH), F32)
            sbuf[...] = jnp.zeros(sbuf.shape, F32)

    x = x_ref[0]
    b_mix = lax.div(jnp.minimum(g, total - 1), nj)
    b_mlp = lax.div(jnp.maximum(g - 1, 0), nj)
    mods = lambda ref, b: [ref[pl.ds(b, 1), k * D_MODEL:(k + 1) * D_MODEL] for k in range(3)]
    shift, scale, gate = mods(mod_ref, b_mix)

    def conv_stages(u):
        ubuf[HIST_ROWS:HIST_ROWS + tile, :] = u
        off = HIST_ROWS - (CONV_K - 1)
        for r in range(SUBLANES):
            n = tile + SUBLANES * ((CONV_K - 1 - r) // SUBLANES)
            ushift[r, 0:n, :] = ubuf[off + r:off + r + n, :]
        ubuf[0:HIST_ROWS, :] = ubuf[tile:tile + HIST_ROWS, :]
        yield
        for rb in range(tile // CONV_ROW_BLOCK):
            r0 = rb * CONV_ROW_BLOCK
            dw = jnp.broadcast_to(convb_ref[...], (CONV_ROW_BLOCK, CONV_CH))
            for k in range(CONV_K):
                s0 = r0 + SUBLANES * (k // SUBLANES)
                wk = jnp.concatenate([convw_ref[k]] * (CONV_ROW_BLOCK // SUBLANES), axis=0)
                dw = dw + wk * ushift[k % SUBLANES, s0:s0 + CONV_ROW_BLOCK, :]
            conv_out = _silu(_layernorm(dw, clng_ref[...], clnb_ref[...]))
            mixbuf[r0:r0 + CONV_ROW_BLOCK, 0:CONV_CH] = conv_out.astype(BF16)
            yield

    def ret_stages(proj):
        ret_proj = []
        for k in range(4):
            ret_proj.append(proj(2 + k))
            yield
        q, kk, v, gt = ret_proj
        q = q * (RET_DK ** -0.5)
        lane = lax.broadcasted_iota(jnp.int32, (1, PAIR_LANES), 1)
        lo = lane < RET_DK
        first_half = (lane % RET_DK) < (RET_DK // 2)
        lo_b = lo.astype(BF16)
        hi_b = (~lo).astype(BF16)
        pairs = range(N_PAIRS)
        halves = (pairs[:N_PAIRS // 2], pairs[N_PAIRS // 2:])
        lanes = [slice(p * PAIR_LANES, (p + 1) * PAIR_LANES) for p in pairs]
        for c in range(tile // RET_CHUNK):
            rs = slice(c * RET_CHUNK, (c + 1) * RET_CHUNK)
            cos = cos_ref[rs, :]
            sin = sin_ref[rs, :]
            qc, kc, vc, scores, outs = {}, {}, {}, {}, {}
            for group in halves:
                for p in group:
                    qc[p] = _rotary_pair(q[rs, lanes[p]], cos, sin, first_half)
                    kc[p] = _rotary_pair(kk[rs, lanes[p]], cos, sin, first_half)
                    vc[p] = v[rs, lanes[p]].astype(BF16)
                    kb = kc[p].astype(BF16)
                    k_cat = jnp.concatenate([kb * lo_b, kb * hi_b], axis=0)
                    scores[p] = _dot_nt(qc[p].astype(BF16), k_cat)
                yield
            for group in halves:
                for p in group:
                    state = sbuf[p]
                    lhs = jnp.concatenate([(scores[p] * dec_ref[p]).astype(BF16),
                                           (qc[p] * qdec_ref[:, lanes[p]]).astype(BF16)], axis=1)
                    rhs = jnp.concatenate([vc[p] * lo_b, vc[p] * hi_b, state.astype(BF16)],
                                          axis=0)
                    outs[p] = _dot(lhs, rhs)
                    kd = (kc[p] * kdec_ref[:, lanes[p]]).astype(BF16)
                    sbuf[p] = (state * sdec_ref[p]
                               + jnp.where(sdec_ref[p] > 0.0, _dot_tn(kd, vc[p]), 0.0))
                yield
            for p in pairs:
                on = _pair_groupnorm(outs[p], lo, rlng_ref[:, lanes[p]], rlnb_ref[:, lanes[p]])
                mixbuf[rs, CONV_CH + p * PAIR_LANES:CONV_CH + (p + 1) * PAIR_LANES] = (
                    on * _silu(gt[rs, lanes[p]])).astype(BF16)
            yield

    streams = {}
    if do_mlp:
        streams["f"] = _ffn_stages(x1buf[...], mods(mod2_ref, b_mlp), gffn_ref, w1_ref, w2_ref,
                                   gfin_ref, y_ref)
        next(streams["f"])
    if do_mix:
        hb = _rmsnorm_mod(x, gmix_ref[...], scale, shift).astype(BF16)

        def proj(k):
            return _dot(hb, win_ref[:, k * CONV_CH:(k + 1) * CONV_CH])

        streams["c"] = conv_stages(proj(0) * jax.nn.sigmoid(proj(1)))
        streams["r"] = ret_stages(proj)
    for name in LAYER_PROGRAM.replace(" ", ""):
        next(streams.get(name, iter(())), None)
    for stream in [streams[name] for name in "crf" if name in streams]:
        for _ in stream:
            pass

    if do_mix:
        mix = _dot(mixbuf[...], wout_ref[...])
        x1buf[...] = x + gate * mix

        @pl.when(j == nj - 1)
        def _():
            hist_ref[0] = ubuf[0:HIST_ROWS, :]
            state_ref[0] = sbuf[...]


def _layer_prompt(x, mod, mod_row0, g_mix, w_in, conv_w, conv_b, cln_g, cln_b, rln_g, rln_b,
                  w_out, g_ffn, w1, w2, g_final, rope, decay, tile):
    bsz, seq, d = x.shape
    assert bsz % SUBLANES == 0 and mod_row0 % bsz == 0
    in_hbm = pl.BlockSpec(memory_space=pl.ANY)

    def stage_shape(*ws):
        width = max(w.shape[1] for w in ws)
        rows = WEIGHT_STAGE_ELEMS // width
        assert all(w.shape[0] % rows == 0 for w in ws)
        return pltpu.VMEM((WEIGHT_STAGE_SLOTS, rows, width), F32)

    mod_blk = mod_row0 // bsz
    nj = seq // tile
    total = bsz * nj
    cos, sin = rope
    dec, qdec, kdec, sdec = decay
    mix_tile = lambda g: jnp.minimum(g, total - 1)
    ffn_tile = lambda g: jnp.maximum(g - 1, 0)
    resident = lambda shape: pl.BlockSpec(shape, lambda g: (0,) * len(shape),
                                          pipeline_mode=pl.Buffered(1))
    row = lambda a: a.reshape(1, -1)
    conv_w_rows = jnp.broadcast_to(conv_w[:, None, :], (CONV_K, SUBLANES, CONV_CH))
    rope_spec = pl.BlockSpec((tile, PAIR_LANES), lambda g: (mix_tile(g) % nj, 0))
    return pl.pallas_call(
        functools.partial(_layer_prompt_kernel, tile, nj),
        grid=(total + 1,),
        in_specs=[
            pl.BlockSpec((1, tile, d), lambda g: (mix_tile(g) // nj, mix_tile(g) % nj, 0)),
            pl.BlockSpec((bsz, 3 * d), lambda g: (mod_blk, 0), pipeline_mode=pl.Buffered(1)),
            resident((1, d)),
            in_hbm,
            resident((CONV_K, SUBLANES, CONV_CH)),
            resident((1, CONV_CH)),
            resident((1, CONV_CH)),
            resident((1, CONV_CH)),
            resident((1, RET_W)),
            resident((1, RET_W)),
            in_hbm,
            rope_spec,
            rope_spec,
            resident(dec.shape),
            resident(qdec.shape),
            resident(kdec.shape),
            resident(sdec.shape),
            pl.BlockSpec((bsz, 3 * d), lambda g: (mod_blk, 1), pipeline_mode=pl.Buffered(1)),
            resident((1, d)),
            in_hbm,
            in_hbm,
            resident((1, d)),
        ],
        out_specs=[
            pl.BlockSpec((1, tile, d), lambda g: (ffn_tile(g) // nj, ffn_tile(g) % nj, 0)),
            pl.BlockSpec((1, HIST_ROWS, CONV_CH), lambda g: (mix_tile(g) // nj, 0, 0)),
            pl.BlockSpec((1, N_PAIRS, PAIR_LANES, PAIR_LANES),
                         lambda g: (mix_tile(g) // nj, 0, 0, 0)),
        ],
        out_shape=(
            jax.ShapeDtypeStruct((bsz, seq, d), F32),
            jax.ShapeDtypeStruct((bsz, HIST_ROWS, CONV_CH), F32),
            jax.ShapeDtypeStruct((bsz, N_PAIRS, PAIR_LANES, PAIR_LANES), F32),
        ),
        scratch_shapes=[
            pltpu.VMEM((HIST_ROWS + tile, CONV_CH), F32),
            pltpu.VMEM((SUBLANES, HIST_ROWS + tile, CONV_CH), F32),
            pltpu.VMEM((N_PAIRS, PAIR_LANES, PAIR_LANES), F32),
            pltpu.VMEM((tile, D_MODEL), BF16),
            pltpu.VMEM((tile, D_MODEL), F32),
            pltpu.VMEM(w_in.shape, BF16),
            pltpu.VMEM(w_out.shape, BF16),
            pltpu.VMEM(w1.shape, BF16),
            pltpu.VMEM(w2.shape, BF16),
            stage_shape(w_in, w1),
            stage_shape(w_out, w2),
            pltpu.SemaphoreType.DMA((WEIGHT_STAGE_SLOTS,)),
            pltpu.SemaphoreType.DMA((WEIGHT_STAGE_SLOTS,)),
        ],
        compiler_params=pltpu.CompilerParams(
            dimension_semantics=("arbitrary",), vmem_limit_bytes=VMEM_LIMIT),
        name="layer_prompt",
    )(x, mod, row(g_mix), w_in, conv_w_rows, row(conv_b), row(cln_g), row(cln_b),
      row(rln_g), row(rln_b), w_out, cos, sin, dec, qdec, kdec, sdec,
      mod, row(g_ffn), w1, w2, row(g_final))


def _ffn_stages(x, mods, gffn_ref, w1_ref, w2_ref, gfin_ref, y_ref):
    shift, scale, gate = mods
    hb = _rmsnorm_mod(x, gffn_ref[...], scale, shift).astype(BF16)
    yield
    half = D_MODEL // 2
    ff = [None, None]
    for c in range(D_FF // (2 * FFN_CHUNK)):
        acts = []
        for h in range(2):
            c0 = (2 * c + h) * FFN_CHUNK
            a = jnp.maximum(_dot(hb, w1_ref[:, c0:c0 + FFN_CHUNK]), 0.0)
            acts.append((a * a).astype(BF16))
            yield
        act = jnp.concatenate(acts, axis=1)
        for n in range(2):
            part = _dot(act, w2_ref[2 * c * FFN_CHUNK:2 * (c + 1) * FFN_CHUNK,
                                    n * half:(n + 1) * half])
            ff[n] = part if ff[n] is None else ff[n] + part
            yield
    z = x + gate * jnp.concatenate(ff, axis=1)
    y_ref[0] = z * lax.rsqrt(jnp.mean(z * z, axis=-1, keepdims=True) + EPS) * gfin_ref[...]


def _mod_cols(mod_ref, n_seq, k):
    return mod_ref[0:n_seq, k * D_MODEL:(k + 1) * D_MODEL]


def _mlp_sample_kernel(n_pos, n_seq, x_ref, conv_ref, ret_ref, wout_ref, mod_ref,
                       gffn_ref, w1_ref, w2_ref, gfin_ref, y_ref, x1buf, hbuf, acc):
    c = pl.program_id(0)
    rows = lambda i: slice(i * n_seq, (i + 1) * n_seq)

    @pl.when(c == 0)
    def _():
        mix = (_dot(conv_ref[...], wout_ref[0:CONV_CH, :].astype(BF16))
               + _dot(ret_ref[...], wout_ref[CONV_CH:CONV_CH + RET_W, :].astype(BF16)))
        for i in range(n_pos):
            x1 = x_ref[i] + _mod_cols(mod_ref, n_seq, 2) * mix[rows(i)]
            x1buf[rows(i), :] = x1
            hbuf[rows(i), :] = _rmsnorm_mod(x1, gffn_ref[...], _mod_cols(mod_ref, n_seq, 4),
                                            _mod_cols(mod_ref, n_seq, 3)).astype(BF16)

    a = jnp.maximum(_dot(hbuf[...], w1_ref[...].astype(BF16)), 0.0)
    part = _dot((a * a).astype(BF16), w2_ref[...].astype(BF16))

    @pl.when(c == 0)
    def _():
        acc[...] = part

    @pl.when(c > 0)
    def _():
        acc[...] += part

    @pl.when(c == pl.num_programs(0) - 1)
    def _():
        for i in range(n_pos):
            z = x1buf[rows(i), :] + _mod_cols(mod_ref, n_seq, 5) * acc[rows(i), :]
            y_ref[i] = (z * lax.rsqrt(jnp.mean(z * z, axis=-1, keepdims=True) + EPS)
                        * gfin_ref[...])


def _mlp_sample(x, conv_out, ret_out, w_out, mod, g_ffn, w1, w2, g_final):
    n_pos, n_seq, d = x.shape
    n_tok = n_pos * n_seq
    n_chunks = D_FF // SAMPLE_FFN_CHUNK
    whole = lambda shape: pl.BlockSpec(shape, lambda c: (0,) * len(shape),
                                       pipeline_mode=pl.Buffered(1))
    return pl.pallas_call(
        functools.partial(_mlp_sample_kernel, n_pos, n_seq),
        grid=(n_chunks,),
        in_specs=[whole(x.shape), whole(conv_out.shape), whole(ret_out.shape),
                  whole(w_out.shape), whole(mod.shape), whole((1, d)),
                  pl.BlockSpec((d, SAMPLE_FFN_CHUNK), lambda c: (0, c)),
                  pl.BlockSpec((SAMPLE_FFN_CHUNK, d), lambda c: (c, 0)),
                  whole((1, d))],
        out_specs=pl.BlockSpec(x.shape, lambda c: (0, 0, 0)),
        out_shape=jax.ShapeDtypeStruct(x.shape, F32),
        scratch_shapes=[pltpu.VMEM((n_tok, d), F32), pltpu.VMEM((n_tok, d), BF16),
                        pltpu.VMEM((n_tok, d), F32)],
        compiler_params=pltpu.CompilerParams(
            dimension_semantics=("arbitrary",), vmem_limit_bytes=VMEM_LIMIT),
        name="mlp_sample",
    )(x, conv_out, ret_out, w_out, mod, g_ffn.reshape(1, d), w1, w2, g_final.reshape(1, d))


def _power_table_kernel(o_ref):
    lane_h = (lax.broadcasted_iota(jnp.int32, o_ref.shape, 1) // RET_DK).astype(F32)
    n = lax.broadcasted_iota(jnp.int32, o_ref.shape, 0).astype(F32)
    o_ref[...] = jnp.exp(_log_decay(lane_h) * n)


def _power_table():
    return pl.pallas_call(_power_table_kernel,
                          out_shape=jax.ShapeDtypeStruct((8, RET_W), F32),
                          name="decay_powers")()


def _proj_sample_kernel(n_pos, n_seq, x_ref, mod_ref, gmix_ref, win_ref, cache_ref, convw_ref,
                        convb_ref, clng_ref, clnb_ref, cos_ref, sin_ref,
                        hist_ref, conv_ref, q_ref, k_ref, v_ref, g_ref):
    shift, scale = _mod_cols(mod_ref, n_seq, 0), _mod_cols(mod_ref, n_seq, 1)
    hb = jnp.concatenate(
        [_rmsnorm_mod(x_ref[i], gmix_ref[...], scale, shift).astype(BF16)
         for i in range(n_pos)], axis=0)

    def proj(k):
        return _dot(hb, win_ref[:, k * CONV_CH:(k + 1) * CONV_CH].astype(BF16))

    u = proj(0) * jax.nn.sigmoid(proj(1))
    n_hist = CONV_K - 1
    rows = lambda a, i: a[i * n_seq:(i + 1) * n_seq]
    for m in range(n_hist - n_pos):
        hist_ref[m] = cache_ref[m + n_pos]
    for i in range(n_pos):
        hist_ref[n_hist - n_pos + i] = rows(u, i)
        dw = jnp.broadcast_to(convb_ref[...], (n_seq, CONV_CH))
        for k in range(CONV_K):
            m = i + k
            src = cache_ref[m] if m < n_hist else rows(u, m - n_hist)
            dw = dw + convw_ref[k:k + 1, :] * src
        conv_ref[i] = _silu(_layernorm(dw, clng_ref[...], clnb_ref[...])).astype(BF16)

    lane = lax.broadcasted_iota(jnp.int32, (1, PAIR_LANES), 1)
    first_half = (lane % RET_DK) < (RET_DK // 2)
    q = proj(2) * (RET_DK ** -0.5)
    kk = proj(3)
    v = proj(4)
    gt = proj(5)
    for i in range(n_pos):
        cos = cos_ref[i:i + 1, :]
        sin = sin_ref[i:i + 1, :]
        for p in range(N_PAIRS):
            sl = slice(p * PAIR_LANES, (p + 1) * PAIR_LANES)
            q_ref[i, :, sl] = _rotary_pair(rows(q, i)[:, sl], cos, sin, first_half)
            k_ref[i, :, sl] = _rotary_pair(rows(kk, i)[:, sl], cos, sin, first_half)
        v_ref[i] = rows(v, i)
        g_ref[i] = rows(gt, i)


def _proj_sample(x, mod, g_mix, w_in, cache_t, conv_w, conv_b, cln_g, cln_b, rope):
    n_pos, n_seq, d = x.shape
    blk = SAMPLE_SEQ_BLOCK
    cos, sin = rope
    row = lambda a: a.reshape(1, -1)
    whole = lambda a: pl.BlockSpec(a.shape, lambda s: (0,) * a.ndim, pipeline_mode=pl.Buffered(1))
    seqs = lambda lead, width: pl.BlockSpec((lead, blk, width), lambda s: (0, s, 0))
    tok = lambda dt: jax.ShapeDtypeStruct((n_pos, n_seq, RET_W), dt)
    args = (x, mod, row(g_mix), w_in, cache_t, conv_w, row(conv_b), row(cln_g), row(cln_b),
            cos, sin)
    return pl.pallas_call(
        functools.partial(_proj_sample_kernel, n_pos, blk),
        grid=(n_seq // blk,),
        in_specs=[seqs(n_pos, d), pl.BlockSpec((blk, mod.shape[1]), lambda s: (s, 0)),
                  whole(args[2]), whole(w_in), seqs(cache_t.shape[0], CONV_CH)]
        + [whole(a) for a in args[5:]],
        out_specs=[seqs(cache_t.shape[0], CONV_CH)] + [seqs(n_pos, RET_W)] * 5,
        out_shape=(jax.ShapeDtypeStruct(cache_t.shape, F32), tok(BF16),
                   tok(F32), tok(F32), tok(F32), tok(F32)),
        compiler_params=pltpu.CompilerParams(
            dimension_semantics=("arbitrary",), vmem_limit_bytes=VMEM_LIMIT),
        name="proj_sample",
    )(*args)


def _ret_sample_kernel(n_pos, n_seq, q_ref, k_ref, v_ref, g_ref, st_ref, pw_ref, rlng_ref,
                       rlnb_ref, out_ref, snew_ref, sb_buf, qd_buf, kd_buf, v_buf):
    lane = lax.broadcasted_iota(jnp.int32, (1, PAIR_LANES), 1)
    lo = lane < RET_DK
    rows = lambda a, i: a[i * n_seq:(i + 1) * n_seq]
    cols = lambda a, i: a[:, i * n_seq:(i + 1) * n_seq]
    pw = pw_ref[...]
    q = q_ref[...]
    k = k_ref[...]
    qb = [rows(q, i).astype(BF16) for i in range(n_pos)]
    kb = [rows(k, i).astype(BF16) for i in range(n_pos)]
    vb = [rows(v_ref[...], i).astype(BF16) for i in range(n_pos)]

    o = []
    for i in range(n_pos):
        acc = jnp.zeros((n_seq, PAIR_LANES), F32)
        for j in range(i + 1):
            prod = qb[i].astype(F32) * kb[j].astype(F32)
            s_lo = jnp.sum(jnp.where(lo, prod, 0.0), axis=-1, keepdims=True)
            s_hi = jnp.sum(jnp.where(lo, 0.0, prod), axis=-1, keepdims=True)
            score = jnp.where(lo, s_lo, s_hi) * pw[i - j:i - j + 1, :]
            acc = acc + score.astype(BF16).astype(F32) * vb[j].astype(F32)
        o.append(acc)

    rounded = lambda x: x.astype(BF16).astype(F32)
    qd_buf[...] = jnp.concatenate(
        [rounded(rows(q, i) * pw[i + 1:i + 2, :]) for i in range(n_pos)], axis=0).T
    kd_buf[...] = jnp.concatenate(
        [rounded(rows(k, j) * pw[n_pos - 1 - j:n_pos - j, :]) for j in range(n_pos)], axis=0).T
    v_buf[...] = jnp.concatenate([vb[j].astype(F32) for j in range(n_pos)], axis=0).T
    tile_at = lambda r: pl.ds(pl.multiple_of(r * RET_DV, RET_DV), RET_DV)
    pos = lambda i: slice(i * n_seq, (i + 1) * n_seq)

    for a in range(2):
        gam = pw[n_pos:n_pos + 1, a * RET_DK:a * RET_DK + 1]
        v_rows = slice(a * RET_DV, (a + 1) * RET_DV)

        def update(grp, carry, a=a, gam=gam, v_rows=v_rows):
            r0 = pl.multiple_of(a * RET_DK + grp * SUBLANES, SUBLANES)
            kd_rows = kd_buf[pl.ds(r0, SUBLANES), :]
            for s in range(SUBLANES):
                r = r0 + s
                s_old = st_ref[tile_at(r), :]
                sb_buf[tile_at(r), :] = rounded(s_old)
                s_new = s_old * gam
                for j in range(n_pos):
                    s_new = s_new + kd_rows[s:s + 1, pos(j)] * v_buf[v_rows, pos(j)]
                snew_ref[tile_at(r), :] = s_new
            return carry

        lax.fori_loop(0, RET_DK // SUBLANES, update, 0)

    for i in range(n_pos):
        heads = []
        for a in range(2):
            def cross(grp, acc, a=a, i=i):
                r0 = pl.multiple_of(a * RET_DK + grp * SUBLANES, SUBLANES)
                qd_rows = qd_buf[pl.ds(r0, SUBLANES), pos(i)]
                for s in range(SUBLANES):
                    acc = acc + qd_rows[s:s + 1, :] * sb_buf[tile_at(r0 + s), :]
                return acc

            heads.append(lax.fori_loop(0, RET_DK // SUBLANES, cross,
                                       jnp.zeros((RET_DV, n_seq), F32)))
        oi = o[i] + jnp.concatenate(heads, axis=0).T
        on = _pair_groupnorm(oi, lo, rlng_ref[...], rlnb_ref[...])
        out_ref[i * n_seq:(i + 1) * n_seq, :] = (on * _silu(rows(g_ref[...], i))).astype(BF16)


def _ret_sample(q, k, v, g, state_t, pw, rln_g, rln_b, n_pos, n_seq):
    n_tok = n_pos * n_seq
    pair_state = 2 * RET_DK * RET_DV
    slab = pl.BlockSpec((n_tok, PAIR_LANES), lambda p: (0, p))
    state_spec = pl.BlockSpec((pair_state, n_seq), lambda p: (p, 0))
    return pl.pallas_call(
        functools.partial(_ret_sample_kernel, n_pos, n_seq),
        grid=(N_PAIRS,),
        in_specs=[slab, slab, slab, slab, state_spec,
                  pl.BlockSpec((8, PAIR_LANES), lambda p: (0, p)),
                  pl.BlockSpec((1, PAIR_LANES), lambda p: (0, p)),
                  pl.BlockSpec((1, PAIR_LANES), lambda p: (0, p))],
        out_specs=[slab, state_spec],
        out_shape=(jax.ShapeDtypeStruct((n_tok, RET_W), BF16),
                   jax.ShapeDtypeStruct(state_t.shape, F32)),
        scratch_shapes=[pltpu.VMEM((pair_state, n_seq), F32)]
        + [pltpu.VMEM((PAIR_LANES, n_tok), F32)] * 3,
        compiler_params=pltpu.CompilerParams(
            dimension_semantics=("arbitrary",), vmem_limit_bytes=VMEM_LIMIT),
        name="ret_sample",
    )(q, k, v, g, state_t, pw, rln_g.reshape(1, RET_W), rln_b.reshape(1, RET_W))


def _unpack_state(s2):
    blocks = []
    for p in range(N_PAIRS):
        for a in range(2):
            sl = slice(a * RET_DK, (a + 1) * RET_DK)
            blocks.append(s2[:, p, sl, sl])
    return jnp.stack(blocks, axis=1)


def _prompt_path(x, mod, mod_row0, g_mix, w_in, conv_w, conv_b, cln_g, cln_b, rln_g, rln_b,
                 w_out, g_ffn, w_ff1, w_ff2, g_final, tile=PROMPT_TILE):
    seq = x.shape[1]
    rope = _rope_tables(seq, 0)
    decay = _decay_tables(RET_CHUNK)
    y, hist, s2 = _layer_prompt(x, mod, mod_row0, g_mix, w_in, conv_w, conv_b, cln_g, cln_b,
                                rln_g, rln_b, w_out, g_ffn, w_ff1, w_ff2, g_final, rope, decay,
                                tile)
    return y, hist[:, HIST_ROWS - (CONV_K - 1):], _unpack_state(s2)


def _sample_path(x, mod, cache, state, pos0, g_mix, w_in, conv_w, conv_b, cln_g, cln_b,
                 rln_g, rln_b, w_out, g_ffn, w_ff1, w_ff2, g_final):
    n_seq, n_pos, d = x.shape
    xs = x.transpose(1, 0, 2)
    rope = _rope_tables(8, pos0)
    hist_t, *tok = _proj_sample(xs, mod, g_mix, w_in, cache.transpose(1, 0, 2),
                                conv_w, conv_b, cln_g, cln_b, rope)
    conv_out, q, k, v, g = [a.reshape(n_pos * n_seq, -1) for a in tok]
    ret_out, s_new_t = _ret_sample(q, k, v, g, state.reshape(n_seq, -1).T, _power_table(),
                                   rln_g, rln_b, n_pos, n_seq)
    s_new = s_new_t.T
    y = _mlp_sample(xs, conv_out, ret_out, w_out, mod, g_ffn, w_ff1, w_ff2, g_final)
    return y.transpose(1, 0, 2), hist_t.transpose(1, 0, 2), s_new.reshape(state.shape)


def kernel(x_prompt, x_sample, cache_conv, state_ret, c_prompt, c_sample, w_ada, b_ada, g_mix, w_in, conv_w, conv_b, conv_ln_g, conv_ln_b, ret_ln_g, ret_ln_b, w_out, g_ffn, w_ff1, w_ff2, g_final):
    mod = _modulation(jnp.concatenate([c_sample, c_prompt], axis=0), w_ada[0], b_ada[0])
    params = (g_mix[0], w_in[0], conv_w[0], conv_b[0], conv_ln_g[0], conv_ln_b[0], ret_ln_g[0],
              ret_ln_b[0], w_out[0], g_ffn[0], w_ff1[0], w_ff2[0], g_final)
    y_p, conv_p, ret_p = _prompt_path(x_prompt, mod, x_sample.shape[0], *params)
    y_s, conv_s, ret_s = _sample_path(x_sample, mod, cache_conv[0], state_ret[0], PAST_LEN,
                                      *params)
    return (y_p, y_s, conv_p[None], ret_p[None], conv_s[None], ret_s[None])
```

```python
import functools

import jax
import jax.numpy as jnp
from jax import lax
from jax.experimental import pallas as pl
from jax.experimental.pallas import tpu as pltpu

D_MODEL = 1024
CONV_CH = 512
CONV_K = 31
RET_HEADS = 8
RET_W = 512
RET_DK = 64
RET_DV = 64
D_FF = 4 * D_MODEL
RET_CHUNK = 128
ROPE_THETA = 10000.0
EPS = 1e-6
N_MOD = 6
PAST_LEN = 16384

SUBLANES = 8
HIST_ROWS = 32
CONV_ROW_BLOCK = 32
PROMPT_TILE = 256
FFN_CHUNK = 512
SAMPLE_FFN_CHUNK = 1024
SAMPLE_SEQ_BLOCK = 32
WEIGHT_STAGE_ELEMS = 512 * 1024
WEIGHT_STAGE_SLOTS = 3
LAYER_PROGRAM = ("c cf cf cf cf cf cf cf cf "
                 "rrrr rfrf rfrf r "
                 "rfrf rfrf r")
PAIR_LANES = 2 * RET_DK
N_PAIRS = RET_HEADS // 2
VMEM_LIMIT = 56 * 1024 * 1024

F32 = jnp.float32
BF16 = jnp.bfloat16


def _log_decay(head):
    return jnp.log1p(-jnp.exp2(-5.0 - head))


def _rope_kernel(pos0, cos_ref, sin_ref):
    rows = cos_ref.shape[0]
    lane = lax.broadcasted_iota(jnp.int32, (rows, PAIR_LANES), 1)
    row = lax.broadcasted_iota(jnp.int32, (rows, PAIR_LANES), 0)
    half = RET_DK // 2
    freq = (lane % half).astype(F32)
    inv = jnp.power(jnp.full_like(freq, ROPE_THETA), -freq / half)
    ang = (row + pos0).astype(F32) * inv
    first = (lane % RET_DK) < half
    cos_ref[...] = jnp.cos(ang)
    sin_ref[...] = jnp.where(first, -jnp.sin(ang), jnp.sin(ang))


ROPE_BLOCK = 16


def _rope_blocked_kernel(pos0, cos_ref, sin_ref):
    rows = cos_ref.shape[0]
    n_blk = rows // ROPE_BLOCK
    half = RET_DK // 2

    def angles(n, step, start):
        lane = lax.broadcasted_iota(jnp.int32, (n, PAIR_LANES), 1)
        row = lax.broadcasted_iota(jnp.int32, (n, PAIR_LANES), 0)
        freq = (lane % half).astype(F32)
        inv = jnp.power(jnp.full_like(freq, ROPE_THETA), -freq / half)
        return (row * step + start).astype(F32) * inv

    a = angles(n_blk, ROPE_BLOCK, pos0)
    b = angles(ROPE_BLOCK, 1, 0)
    lane = lax.broadcasted_iota(jnp.int32, (ROPE_BLOCK, PAIR_LANES), 1)
    sign = jnp.where((lane % RET_DK) < half, -1.0, 1.0)
    cos_a, sin_a = jnp.cos(a), jnp.sin(a)
    cos_b, sin_b = jnp.cos(b), jnp.sin(b)
    cos_bs, sin_bs = cos_b * sign, sin_b * sign
    for blk in range(n_blk):
        ca, sa = cos_a[blk:blk + 1, :], sin_a[blk:blk + 1, :]
        rs = slice(blk * ROPE_BLOCK, (blk + 1) * ROPE_BLOCK)
        cos_ref[rs, :] = ca * cos_b - sa * sin_b
        sin_ref[rs, :] = sa * cos_bs + ca * sin_bs


def _rope_tables(rows, pos0):
    blocked = rows % ROPE_BLOCK == 0 and rows // ROPE_BLOCK >= SUBLANES
    return pl.pallas_call(
        functools.partial(_rope_blocked_kernel if blocked else _rope_kernel, pos0),
        out_shape=(jax.ShapeDtypeStruct((rows, PAIR_LANES), F32),
                   jax.ShapeDtypeStruct((rows, PAIR_LANES), F32)),
        name="rope_tables",
    )()


def _decay_kernel(chunk, dec_ref, qdec_ref, kdec_ref, sdec_ref):
    c = chunk
    shape = (N_PAIRS, c, 2 * c)
    col = lax.broadcasted_iota(jnp.int32, shape, 2)
    h = (2 * lax.broadcasted_iota(jnp.int32, shape, 0) + col // c).astype(F32)
    i = lax.broadcasted_iota(jnp.int32, shape, 1).astype(F32)
    j = (col % c).astype(F32)
    diff = i - j
    dec_ref[...] = jnp.where(diff >= 0, jnp.exp(_log_decay(h) * jnp.maximum(diff, 0.0)), 0.0)
    lane_h = (lax.broadcasted_iota(jnp.int32, (c, RET_W), 1) // RET_DK).astype(F32)
    idx = lax.broadcasted_iota(jnp.int32, (c, RET_W), 0).astype(F32)
    lg = _log_decay(lane_h)
    qdec_ref[...] = jnp.exp(lg * (idx + 1.0))
    kdec_ref[...] = jnp.exp(lg * (c - 1.0 - idx))
    p = lax.broadcasted_iota(jnp.int32, (N_PAIRS, PAIR_LANES, PAIR_LANES), 0)
    r = lax.broadcasted_iota(jnp.int32, (N_PAIRS, PAIR_LANES, PAIR_LANES), 1) // RET_DK
    q = lax.broadcasted_iota(jnp.int32, (N_PAIRS, PAIR_LANES, PAIR_LANES), 2) // RET_DK
    hh = (2 * p + r).astype(F32)
    sdec_ref[...] = jnp.where(r == q, jnp.exp(_log_decay(hh) * float(c)), 0.0)


def _decay_tables(chunk):
    return pl.pallas_call(
        functools.partial(_decay_kernel, chunk),
        out_shape=(jax.ShapeDtypeStruct((N_PAIRS, chunk, 2 * chunk), F32),
                   jax.ShapeDtypeStruct((chunk, RET_W), F32),
                   jax.ShapeDtypeStruct((chunk, RET_W), F32),
                   jax.ShapeDtypeStruct((N_PAIRS, PAIR_LANES, PAIR_LANES), F32)),
        name="decay_tables",
    )()


def _mod_kernel(c_ref, w_lo_ref, w_hi_ref, b_ref, o_ref):
    c = c_ref[...]
    s = (c * jax.nn.sigmoid(c)).astype(BF16)
    half = w_lo_ref.shape[1]
    for k, w_ref in enumerate((w_lo_ref, w_hi_ref)):
        cs = slice(k * half, (k + 1) * half)
        o_ref[:, cs] = _dot(s, w_ref[...].astype(BF16)) + b_ref[:, cs]


def _modulation(c, w_ada, b_ada):
    n, d = c.shape
    width = w_ada.shape[1]
    blk = D_MODEL
    half = lambda k: pl.BlockSpec((d, blk // 2), lambda i: (0, 2 * i + k))
    return pl.pallas_call(
        _mod_kernel,
        grid=(width // blk,),
        in_specs=[pl.BlockSpec((n, d), lambda i: (0, 0)), half(0), half(1),
                  pl.BlockSpec((1, blk), lambda i: (0, i))],
        out_specs=pl.BlockSpec((n, blk), lambda i: (0, i)),
        out_shape=jax.ShapeDtypeStruct((n, width), F32),
        compiler_params=pltpu.CompilerParams(vmem_limit_bytes=VMEM_LIMIT),
        name="adaln_mod",
    )(c, w_ada, w_ada, b_ada.reshape(1, width))


def _rmsnorm_mod(x, g, scale, shift):
    y = x * lax.rsqrt(jnp.mean(x * x, axis=-1, keepdims=True) + EPS)
    return (y * g) * (1.0 + scale) + shift


def _layernorm(x, g, b):
    mu = jnp.mean(x, axis=-1, keepdims=True)
    d = x - mu
    var = jnp.mean(d * d, axis=-1, keepdims=True)
    return d * lax.rsqrt(var + EPS) * g + b


def _silu(x):
    return x * jax.nn.sigmoid(x)


def _rotary_pair(x, cos, sin_signed, first_half):
    partner = jnp.where(first_half, pltpu.roll(x, PAIR_LANES - RET_DK // 2, 1),
                        pltpu.roll(x, RET_DK // 2, 1))
    return x * cos + partner * sin_signed


def _pair_groupnorm(o, lo, g, b):
    inv_n = 1.0 / RET_DV
    s_lo = jnp.sum(jnp.where(lo, o, 0.0), axis=-1, keepdims=True)
    s_hi = jnp.sum(jnp.where(lo, 0.0, o), axis=-1, keepdims=True)
    d = o - jnp.where(lo, s_lo, s_hi) * inv_n
    d2 = d * d
    v_lo = jnp.sum(jnp.where(lo, d2, 0.0), axis=-1, keepdims=True)
    v_hi = jnp.sum(jnp.where(lo, 0.0, d2), axis=-1, keepdims=True)
    var = jnp.where(lo, v_lo, v_hi) * inv_n
    return d * lax.rsqrt(var + EPS) * g + b


def _dot(a, b):
    return jnp.dot(a, b, preferred_element_type=F32)


def _dot_nt(a, b):
    return lax.dot_general(a, b, (((1,), (1,)), ((), ())), preferred_element_type=F32)


def _dot_tn(a, b):
    return lax.dot_general(a, b, (((0,), (0,)), ((), ())), preferred_element_type=F32)


def _stream_cast_weights(pairs, stage, sems):
    n_slots, rows = stage.shape[0], stage.shape[1]
    chunks = [(src, dst, r0) for src, dst in pairs for r0 in range(0, src.shape[0], rows)]

    def copy(c):
        src, _, r0 = chunks[c]
        slot = c % n_slots
        return pltpu.make_async_copy(src.at[pl.ds(r0, rows), :],
                                     stage.at[slot, :, pl.ds(0, src.shape[1])], sems.at[slot])

    for c in range(min(n_slots - 1, len(chunks))):
        copy(c).start()
    for c, (src, dst, r0) in enumerate(chunks):
        if c + n_slots - 1 < len(chunks):
            copy(c + n_slots - 1).start()
        copy(c).wait()
        dst[r0:r0 + rows, :] = stage[c % n_slots, :, 0:src.shape[1]].astype(BF16)


def _layer_prompt_kernel(tile, nj, x_ref, mod_ref, gmix_ref, win_hbm, convw_ref, convb_ref,
                         clng_ref, clnb_ref, rlng_ref, rlnb_ref, wout_hbm,
                         cos_ref, sin_ref, dec_ref, qdec_ref, kdec_ref, sdec_ref,
                         mod2_ref, gffn_ref, w1_hbm, w2_hbm, gfin_ref,
                         y_ref, hist_ref, state_ref,
                         ubuf, ushift, sbuf, mixbuf, x1buf,
                         win_ref, wout_ref, w1_ref, w2_ref,
                         stage_wide, stage_tall, sems_wide, sems_tall):
    g = pl.program_id(0)
    total = pl.num_programs(0) - 1
    j = lax.rem(jnp.minimum(g, total - 1), nj)
    live = g < total

    @pl.when(g == 0)
    def _():
        _stream_cast_weights([(win_hbm, win_ref), (w1_hbm, w1_ref)], stage_wide, sems_wide)
        _stream_cast_weights([(wout_hbm, wout_ref), (w2_hbm, w2_ref)], stage_tall, sems_tall)
        x1buf[...] = jnp.zeros((tile, D_MODEL), F32)

    @pl.when(jnp.logical_and(j == 0, live))
    def _():
        ubuf[0:HIST_ROWS, :] = jnp.zeros((HIST_ROWS, CONV_CH), F32)
        sbuf[...] = jnp.zeros(sbuf.shape, F32)

    x = x_ref[0]
    b_mix = lax.div(jnp.minimum(g, total - 1), nj)
    b_mlp = lax.div(jnp.maximum(g - 1, 0), nj)
    mods = lambda ref, b: [ref[pl.ds(b, 1), k * D_MODEL:(k + 1) * D_MODEL] for k in range(3)]
    shift, scale, gate = mods(mod_ref, b_mix)

    def conv_stages(u):
        ubuf[HIST_ROWS:HIST_ROWS + tile, :] = u
        off = HIST_ROWS - (CONV_K - 1)
        for r in range(SUBLANES):
            n = tile + SUBLANES * ((CONV_K - 1 - r) // SUBLANES)
            ushift[r, 0:n, :] = ubuf[off + r:off + r + n, :]
        ubuf[0:HIST_ROWS, :] = ubuf[tile:tile + HIST_ROWS, :]
        yield
        for rb in range(tile // CONV_ROW_BLOCK):
            r0 = rb * CONV_ROW_BLOCK
            dw = jnp.broadcast_to(convb_ref[...], (CONV_ROW_BLOCK, CONV_CH))
            for k in range(CONV_K):
                s0 = r0 + SUBLANES * (k // SUBLANES)
                wk = jnp.concatenate([convw_ref[k]] * (CONV_ROW_BLOCK // SUBLANES), axis=0)
                dw = dw + wk * ushift[k % SUBLANES, s0:s0 + CONV_ROW_BLOCK, :]
            conv_out = _silu(_layernorm(dw, clng_ref[...], clnb_ref[...]))
            mixbuf[r0:r0 + CONV_ROW_BLOCK, 0:CONV_CH] = conv_out.astype(BF16)
            yield

    def ret_stages(proj):
        ret_proj = []
        for k in range(4):
            ret_proj.append(proj(2 + k))
            yield
        q, kk, v, gt = ret_proj
        q = q * (RET_DK ** -0.5)
        lane = lax.broadcasted_iota(jnp.int32, (1, PAIR_LANES), 1)
        lo = lane < RET_DK
        first_half = (lane % RET_DK) < (RET_DK // 2)
        lo_b = lo.astype(BF16)
        hi_b = (~lo).astype(BF16)
        pairs = range(N_PAIRS)
        halves = (pairs[:N_PAIRS // 2], pairs[N_PAIRS // 2:])
        lanes = [slice(p * PAIR_LANES, (p + 1) * PAIR_LANES) for p in pairs]
        for c in range(tile // RET_CHUNK):
            rs = slice(c * RET_CHUNK, (c + 1) * RET_CHUNK)
            cos = cos_ref[rs, :]
            sin = sin_ref[rs, :]
            qc, kc, vc, scores, outs = {}, {}, {}, {}, {}
            for group in halves:
                for p in group:
                    qc[p] = _rotary_pair(q[rs, lanes[p]], cos, sin, first_half)
                    kc[p] = _rotary_pair(kk[rs, lanes[p]], cos, sin, first_half)
                    vc[p] = v[rs, lanes[p]].astype(BF16)
                    kb = kc[p].astype(BF16)
                    k_cat = jnp.concatenate([kb * lo_b, kb * hi_b], axis=0)
                    scores[p] = _dot_nt(qc[p].astype(BF16), k_cat)
                yield
            for group in halves:
                for p in group:
                    state = sbuf[p]
                    lhs = jnp.concatenate([(scores[p] * dec_ref[p]).astype(BF16),
                                           (qc[p] * qdec_ref[:, lanes[p]]).astype(BF16)], axis=1)
                    rhs = jnp.concatenate([vc[p] * lo_b, vc[p] * hi_b, state.astype(BF16)],
                                          axis=0)
                    outs[p] = _dot(lhs, rhs)
                    kd = (kc[p] * kdec_ref[:, lanes[p]]).astype(BF16)
                    sbuf[p] = (state * sdec_ref[p]
                               + jnp.where(sdec_ref[p] > 0.0, _dot_tn(kd, vc[p]), 0.0))
                yield
            for p in pairs:
                on = _pair_groupnorm(outs[p], lo, rlng_ref[:, lanes[p]], rlnb_ref[:, lanes[p]])
                mixbuf[rs, CONV_CH + p * PAIR_LANES:CONV_CH + (p + 1) * PAIR_LANES] = (
                    on * _silu(gt[rs, lanes[p]])).astype(BF16)
            yield

    ffn = _ffn_stages(x1buf[...], mods(mod2_ref, b_mlp), gffn_ref, w1_ref, w2_ref, gfin_ref, y_ref)
    next(ffn)
    hb = _rmsnorm_mod(x, gmix_ref[...], scale, shift).astype(BF16)

    def proj(k):
        return _dot(hb, win_ref[:, k * CONV_CH:(k + 1) * CONV_CH])

    streams = {"f": ffn, "c": conv_stages(proj(0) * jax.nn.sigmoid(proj(1))),
               "r": ret_stages(proj)}
    for name in LAYER_PROGRAM.replace(" ", ""):
        next(streams[name], None)
    for stream in (streams["c"], streams["r"], streams["f"]):
        for _ in stream:
            pass

    mix = _dot(mixbuf[...], wout_ref[...])
    x1buf[...] = x + gate * mix

    @pl.when(jnp.logical_and(j == nj - 1, live))
    def _():
        hist_ref[0] = ubuf[0:HIST_ROWS, :]
        state_ref[0] = sbuf[...]


def _layer_prompt(x, mod, mod_row0, g_mix, w_in, conv_w, conv_b, cln_g, cln_b, rln_g, rln_b,
                  w_out, g_ffn, w1, w2, g_final, rope, decay, tile):
    bsz, seq, d = x.shape
    assert bsz % SUBLANES == 0 and mod_row0 % bsz == 0
    in_hbm = pl.BlockSpec(memory_space=pl.ANY)

    def stage_shape(*ws):
        width = max(w.shape[1] for w in ws)
        rows = WEIGHT_STAGE_ELEMS // width
        assert all(w.shape[0] % rows == 0 for w in ws)
        return pltpu.VMEM((WEIGHT_STAGE_SLOTS, rows, width), F32)

    mod_blk = mod_row0 // bsz
    nj = seq // tile
    total = bsz * nj
    cos, sin = rope
    dec, qdec, kdec, sdec = decay
    mix_tile = lambda g: jnp.minimum(g, total - 1)
    ffn_tile = lambda g: jnp.maximum(g - 1, 0)
    resident = lambda shape: pl.BlockSpec(shape, lambda g: (0,) * len(shape),
                                          pipeline_mode=pl.Buffered(1))
    row = lambda a: a.reshape(1, -1)
    conv_w_rows = jnp.broadcast_to(conv_w[:, None, :], (CONV_K, SUBLANES, CONV_CH))
    rope_spec = pl.BlockSpec((tile, PAIR_LANES), lambda g: (mix_tile(g) % nj, 0))
    return pl.pallas_call(
        functools.partial(_layer_prompt_kernel, tile, nj),
        grid=(total + 1,),
        in_specs=[
            pl.BlockSpec((1, tile, d), lambda g: (mix_tile(g) // nj, mix_tile(g) % nj, 0)),
            pl.BlockSpec((bsz, 3 * d), lambda g: (mod_blk, 0), pipeline_mode=pl.Buffered(1)),
            resident((1, d)),
            in_hbm,
            resident((CONV_K, SUBLANES, CONV_CH)),
            resident((1, CONV_CH)),
            resident((1, CONV_CH)),
            resident((1, CONV_CH)),
            resident((1, RET_W)),
            resident((1, RET_W)),
            in_hbm,
            rope_spec,
            rope_spec,
            resident(dec.shape),
            resident(qdec.shape),
            resident(kdec.shape),
            resident(sdec.shape),
            pl.BlockSpec((bsz, 3 * d), lambda g: (mod_blk, 1), pipeline_mode=pl.Buffered(1)),
            resident((1, d)),
            in_hbm,
            in_hbm,
            resident((1, d)),
        ],
        out_specs=[
            pl.BlockSpec((1, tile, d), lambda g: (ffn_tile(g) // nj, ffn_tile(g) % nj, 0)),
            pl.BlockSpec((1, HIST_ROWS, CONV_CH), lambda g: (mix_tile(g) // nj, 0, 0)),
            pl.BlockSpec((1, N_PAIRS, PAIR_LANES, PAIR_LANES),
                         lambda g: (mix_tile(g) // nj, 0, 0, 0)),
        ],
        out_shape=(
            jax.ShapeDtypeStruct((bsz, seq, d), F32),
            jax.ShapeDtypeStruct((bsz, HIST_ROWS, CONV_CH), F32),
            jax.ShapeDtypeStruct((bsz, N_PAIRS, PAIR_LANES, PAIR_LANES), F32),
        ),
        scratch_shapes=[
            pltpu.VMEM((HIST_ROWS + tile, CONV_CH), F32),
            pltpu.VMEM((SUBLANES, HIST_ROWS + tile, CONV_CH), F32),
            pltpu.VMEM((N_PAIRS, PAIR_LANES, PAIR_LANES), F32),
            pltpu.VMEM((tile, D_MODEL), BF16),
            pltpu.VMEM((tile, D_MODEL), F32),
            pltpu.VMEM(w_in.shape, BF16),
            pltpu.VMEM(w_out.shape, BF16),
            pltpu.VMEM(w1.shape, BF16),
            pltpu.VMEM(w2.shape, BF16),
            stage_shape(w_in, w1),
            stage_shape(w_out, w2),
            pltpu.SemaphoreType.DMA((WEIGHT_STAGE_SLOTS,)),
            pltpu.SemaphoreType.DMA((WEIGHT_STAGE_SLOTS,)),
        ],
        compiler_params=pltpu.CompilerParams(
            dimension_semantics=("arbitrary",), vmem_limit_bytes=VMEM_LIMIT),
        name="layer_prompt",
    )(x, mod, row(g_mix), w_in, conv_w_rows, row(conv_b), row(cln_g), row(cln_b),
      row(rln_g), row(rln_b), w_out, cos, sin, dec, qdec, kdec, sdec,
      mod, row(g_ffn), w1, w2, row(g_final))


def _ffn_stages(x, mods, gffn_ref, w1_ref, w2_ref, gfin_ref, y_ref):
    shift, scale, gate = mods
    hb = _rmsnorm_mod(x, gffn_ref[...], scale, shift).astype(BF16)
    yield
    half = D_MODEL // 2
    ff = [None, None]
    for c in range(D_FF // (2 * FFN_CHUNK)):
        acts = []
        for h in range(2):
            c0 = (2 * c + h) * FFN_CHUNK
            a = jnp.maximum(_dot(hb, w1_ref[:, c0:c0 + FFN_CHUNK]), 0.0)
            acts.append((a * a).astype(BF16))
            yield
        act = jnp.concatenate(acts, axis=1)
        for n in range(2):
            part = _dot(act, w2_ref[2 * c * FFN_CHUNK:2 * (c + 1) * FFN_CHUNK,
                                    n * half:(n + 1) * half])
            ff[n] = part if ff[n] is None else ff[n] + part
            yield
    z = x + gate * jnp.concatenate(ff, axis=1)
    y_ref[0] = z * lax.rsqrt(jnp.mean(z * z, axis=-1, keepdims=True) + EPS) * gfin_ref[...]


def _pos_cols(x_ref, i):
    return x_ref[:, i * D_MODEL:(i + 1) * D_MODEL]


def _mod_cols(mod_ref, n_seq, k):
    return mod_ref[0:n_seq, k * D_MODEL:(k + 1) * D_MODEL]


def _mlp_sample_kernel(n_pos, n_seq, x_ref, conv_ref, ret_ref, wout_ref, mod_ref,
                       gffn_ref, w1_ref, w2_ref, gfin_ref, y_ref, x1buf, hbuf, acc):
    c = pl.program_id(0)
    rows = lambda i: slice(i * n_seq, (i + 1) * n_seq)

    @pl.when(c == 0)
    def _():
        mix = (_dot(conv_ref[...], wout_ref[0:CONV_CH, :].astype(BF16))
               + _dot(ret_ref[...], wout_ref[CONV_CH:CONV_CH + RET_W, :].astype(BF16)))
        for i in range(n_pos):
            x1 = _pos_cols(x_ref, i) + _mod_cols(mod_ref, n_seq, 2) * mix[rows(i)]
            x1buf[rows(i), :] = x1
            hbuf[rows(i), :] = _rmsnorm_mod(x1, gffn_ref[...], _mod_cols(mod_ref, n_seq, 4),
                                            _mod_cols(mod_ref, n_seq, 3)).astype(BF16)

    a = jnp.maximum(_dot(hbuf[...], w1_ref[...].astype(BF16)), 0.0)
    part = _dot((a * a).astype(BF16), w2_ref[...].astype(BF16))

    @pl.when(c == 0)
    def _():
        acc[...] = part

    @pl.when(c > 0)
    def _():
        acc[...] += part

    @pl.when(c == pl.num_programs(0) - 1)
    def _():
        for i in range(n_pos):
            z = x1buf[rows(i), :] + _mod_cols(mod_ref, n_seq, 5) * acc[rows(i), :]
            y_ref[:, i * D_MODEL:(i + 1) * D_MODEL] = (
                z * lax.rsqrt(jnp.mean(z * z, axis=-1, keepdims=True) + EPS) * gfin_ref[...])


def _mlp_sample(x, conv_out, ret_out, w_out, mod, g_ffn, w1, w2, g_final):
    n_seq, d = x.shape[0], D_MODEL
    n_pos = x.shape[1] // d
    n_tok = n_pos * n_seq
    n_chunks = D_FF // SAMPLE_FFN_CHUNK
    whole = lambda shape: pl.BlockSpec(shape, lambda c: (0,) * len(shape),
                                       pipeline_mode=pl.Buffered(1))
    return pl.pallas_call(
        functools.partial(_mlp_sample_kernel, n_pos, n_seq),
        grid=(n_chunks,),
        in_specs=[whole(x.shape), whole(conv_out.shape), whole(ret_out.shape),
                  whole(w_out.shape), whole(mod.shape), whole((1, d)),
                  pl.BlockSpec((d, SAMPLE_FFN_CHUNK), lambda c: (0, c)),
                  pl.BlockSpec((SAMPLE_FFN_CHUNK, d), lambda c: (c, 0)),
                  whole((1, d))],
        out_specs=pl.BlockSpec(x.shape, lambda c: (0, 0)),
        out_shape=jax.ShapeDtypeStruct(x.shape, F32),
        scratch_shapes=[pltpu.VMEM((n_tok, d), F32), pltpu.VMEM((n_tok, d), BF16),
                        pltpu.VMEM((n_tok, d), F32)],
        compiler_params=pltpu.CompilerParams(
            dimension_semantics=("arbitrary",), vmem_limit_bytes=VMEM_LIMIT),
        name="mlp_sample",
    )(x, conv_out, ret_out, w_out, mod, g_ffn.reshape(1, d), w1, w2, g_final.reshape(1, d))


def _power_table_kernel(o_ref):
    lane_h = (lax.broadcasted_iota(jnp.int32, o_ref.shape, 1) // RET_DK).astype(F32)
    n = lax.broadcasted_iota(jnp.int32, o_ref.shape, 0).astype(F32)
    o_ref[...] = jnp.exp(_log_decay(lane_h) * n)


def _power_table():
    return pl.pallas_call(_power_table_kernel,
                          out_shape=jax.ShapeDtypeStruct((8, RET_W), F32),
                          name="decay_powers")()


def _proj_sample_kernel(n_pos, n_seq, x_ref, mod_ref, gmix_ref, win_ref, cache_ref, convw_ref,
                        convb_ref, clng_ref, clnb_ref, cos_ref, sin_ref,
                        hist_ref, conv_ref, q_ref, k_ref, v_ref, g_ref):
    shift, scale = _mod_cols(mod_ref, n_seq, 0), _mod_cols(mod_ref, n_seq, 1)
    hb = jnp.concatenate(
        [_rmsnorm_mod(_pos_cols(x_ref, i), gmix_ref[...], scale, shift).astype(BF16)
         for i in range(n_pos)], axis=0)

    def proj(k):
        return _dot(hb, win_ref[:, k * CONV_CH:(k + 1) * CONV_CH].astype(BF16))

    u = proj(0) * jax.nn.sigmoid(proj(1))
    n_hist = CONV_K - 1
    rows = lambda a, i: a[i * n_seq:(i + 1) * n_seq]
    for m in range(n_hist - n_pos):
        hist_ref[m] = cache_ref[m + n_pos]
    for i in range(n_pos):
        hist_ref[n_hist - n_pos + i] = rows(u, i)
        dw = jnp.broadcast_to(convb_ref[...], (n_seq, CONV_CH))
        for k in range(CONV_K):
            m = i + k
            src = cache_ref[m] if m < n_hist else rows(u, m - n_hist)
            dw = dw + convw_ref[k:k + 1, :] * src
        conv_ref[i] = _silu(_layernorm(dw, clng_ref[...], clnb_ref[...])).astype(BF16)

    lane = lax.broadcasted_iota(jnp.int32, (1, PAIR_LANES), 1)
    first_half = (lane % RET_DK) < (RET_DK // 2)
    q = proj(2) * (RET_DK ** -0.5)
    kk = proj(3)
    v = proj(4)
    gt = proj(5)
    for i in range(n_pos):
        cos = cos_ref[i:i + 1, :]
        sin = sin_ref[i:i + 1, :]
        for p in range(N_PAIRS):
            sl = slice(p * PAIR_LANES, (p + 1) * PAIR_LANES)
            q_ref[i, :, sl] = _rotary_pair(rows(q, i)[:, sl], cos, sin, first_half)
            k_ref[i, :, sl] = _rotary_pair(rows(kk, i)[:, sl], cos, sin, first_half)
        v_ref[i] = rows(v, i)
        g_ref[i] = rows(gt, i)


def _proj_sample(x, mod, g_mix, w_in, cache_t, conv_w, conv_b, cln_g, cln_b, rope):
    n_seq, d = x.shape[0], D_MODEL
    n_pos = x.shape[1] // d
    blk = SAMPLE_SEQ_BLOCK
    cos, sin = rope
    row = lambda a: a.reshape(1, -1)
    whole = lambda a: pl.BlockSpec(a.shape, lambda s: (0,) * a.ndim, pipeline_mode=pl.Buffered(1))
    seqs = lambda lead, width: pl.BlockSpec((lead, blk, width), lambda s: (0, s, 0))
    tok = lambda dt: jax.ShapeDtypeStruct((n_pos, n_seq, RET_W), dt)
    args = (x, mod, row(g_mix), w_in, cache_t, conv_w, row(conv_b), row(cln_g), row(cln_b),
            cos, sin)
    return pl.pallas_call(
        functools.partial(_proj_sample_kernel, n_pos, blk),
        grid=(n_seq // blk,),
        in_specs=[pl.BlockSpec((blk, n_pos * d), lambda s: (s, 0)),
                  pl.BlockSpec((blk, mod.shape[1]), lambda s: (s, 0)),
                  whole(args[2]), whole(w_in), seqs(cache_t.shape[0], CONV_CH)]
        + [whole(a) for a in args[5:]],
        out_specs=[seqs(cache_t.shape[0], CONV_CH)] + [seqs(n_pos, RET_W)] * 5,
        out_shape=(jax.ShapeDtypeStruct(cache_t.shape, F32), tok(BF16),
                   tok(F32), tok(F32), tok(F32), tok(F32)),
        compiler_params=pltpu.CompilerParams(
            dimension_semantics=("arbitrary",), vmem_limit_bytes=VMEM_LIMIT),
        name="proj_sample",
    )(*args)


def _ret_sample_kernel(n_pos, n_seq, q_ref, k_ref, v_ref, g_ref, st_ref, pw_ref, rlng_ref,
                       rlnb_ref, out_ref, snew_ref, sb_buf, qd_buf, kd_buf, v_buf):
    lane = lax.broadcasted_iota(jnp.int32, (1, PAIR_LANES), 1)
    lo = lane < RET_DK
    rows = lambda a, i: a[i * n_seq:(i + 1) * n_seq]
    cols = lambda a, i: a[:, i * n_seq:(i + 1) * n_seq]
    pw = pw_ref[...]
    q = q_ref[...]
    k = k_ref[...]
    qb = [rows(q, i).astype(BF16) for i in range(n_pos)]
    kb = [rows(k, i).astype(BF16) for i in range(n_pos)]
    vb = [rows(v_ref[...], i).astype(BF16) for i in range(n_pos)]

    o = []
    for i in range(n_pos):
        acc = jnp.zeros((n_seq, PAIR_LANES), F32)
        for j in range(i + 1):
            prod = qb[i].astype(F32) * kb[j].astype(F32)
            s_lo = jnp.sum(jnp.where(lo, prod, 0.0), axis=-1, keepdims=True)
            s_hi = jnp.sum(jnp.where(lo, 0.0, prod), axis=-1, keepdims=True)
            score = jnp.where(lo, s_lo, s_hi) * pw[i - j:i - j + 1, :]
            acc = acc + score.astype(BF16).astype(F32) * vb[j].astype(F32)
        o.append(acc)

    rounded = lambda x: x.astype(BF16).astype(F32)
    qd_buf[...] = jnp.concatenate(
        [rounded(rows(q, i) * pw[i + 1:i + 2, :]) for i in range(n_pos)], axis=0).T
    kd_buf[...] = jnp.concatenate(
        [rounded(rows(k, j) * pw[n_pos - 1 - j:n_pos - j, :]) for j in range(n_pos)], axis=0).T
    v_buf[...] = jnp.concatenate([vb[j].astype(F32) for j in range(n_pos)], axis=0).T
    tile_at = lambda r: pl.ds(pl.multiple_of(r * RET_DV, RET_DV), RET_DV)
    pos = lambda i: slice(i * n_seq, (i + 1) * n_seq)

    for a in range(2):
        gam = pw[n_pos:n_pos + 1, a * RET_DK:a * RET_DK + 1]
        v_rows = slice(a * RET_DV, (a + 1) * RET_DV)

        def update(grp, carry, a=a, gam=gam, v_rows=v_rows):
            r0 = pl.multiple_of(a * RET_DK + grp * SUBLANES, SUBLANES)
            kd_rows = kd_buf[pl.ds(r0, SUBLANES), :]
            for s in range(SUBLANES):
                r = r0 + s
                s_old = st_ref[tile_at(r), :]
                sb_buf[tile_at(r), :] = rounded(s_old)
                s_new = s_old * gam
                for j in range(n_pos):
                    s_new = s_new + kd_rows[s:s + 1, pos(j)] * v_buf[v_rows, pos(j)]
                snew_ref[tile_at(r), :] = s_new
            return carry

        lax.fori_loop(0, RET_DK // SUBLANES, update, 0)

    for i in range(n_pos):
        heads = []
        for a in range(2):
            def cross(grp, acc, a=a, i=i):
                r0 = pl.multiple_of(a * RET_DK + grp * SUBLANES, SUBLANES)
                qd_rows = qd_buf[pl.ds(r0, SUBLANES), pos(i)]
                for s in range(SUBLANES):
                    acc = acc + qd_rows[s:s + 1, :] * sb_buf[tile_at(r0 + s), :]
                return acc

            heads.append(lax.fori_loop(0, RET_DK // SUBLANES, cross,
                                       jnp.zeros((RET_DV, n_seq), F32)))
        oi = o[i] + jnp.concatenate(heads, axis=0).T
        on = _pair_groupnorm(oi, lo, rlng_ref[...], rlnb_ref[...])
        out_ref[i * n_seq:(i + 1) * n_seq, :] = (on * _silu(rows(g_ref[...], i))).astype(BF16)


def _ret_sample(q, k, v, g, state_t, pw, rln_g, rln_b, n_pos, n_seq):
    n_tok = n_pos * n_seq
    pair_state = 2 * RET_DK * RET_DV
    slab = pl.BlockSpec((n_tok, PAIR_LANES), lambda p: (0, p))
    state_spec = pl.BlockSpec((pair_state, n_seq), lambda p: (p, 0))
    return pl.pallas_call(
        functools.partial(_ret_sample_kernel, n_pos, n_seq),
        grid=(N_PAIRS,),
        in_specs=[slab, slab, slab, slab, state_spec,
                  pl.BlockSpec((8, PAIR_LANES), lambda p: (0, p)),
                  pl.BlockSpec((1, PAIR_LANES), lambda p: (0, p)),
                  pl.BlockSpec((1, PAIR_LANES), lambda p: (0, p))],
        out_specs=[slab, state_spec],
        out_shape=(jax.ShapeDtypeStruct((n_tok, RET_W), BF16),
                   jax.ShapeDtypeStruct(state_t.shape, F32)),
        scratch_shapes=[pltpu.VMEM((pair_state, n_seq), F32)]
        + [pltpu.VMEM((PAIR_LANES, n_tok), F32)] * 3,
        compiler_params=pltpu.CompilerParams(
            dimension_semantics=("arbitrary",), vmem_limit_bytes=VMEM_LIMIT),
        name="ret_sample",
    )(q, k, v, g, state_t, pw, rln_g.reshape(1, RET_W), rln_b.reshape(1, RET_W))


def _unpack_state(s2):
    blocks = []
    for p in range(N_PAIRS):
        for a in range(2):
            sl = slice(a * RET_DK, (a + 1) * RET_DK)
            blocks.append(s2[:, p, sl, sl])
    return jnp.stack(blocks, axis=1)


def _prompt_path(x, mod, mod_row0, g_mix, w_in, conv_w, conv_b, cln_g, cln_b, rln_g, rln_b,
                 w_out, g_ffn, w_ff1, w_ff2, g_final, tile=PROMPT_TILE):
    seq = x.shape[1]
    rope = _rope_tables(seq, 0)
    decay = _decay_tables(RET_CHUNK)
    y, hist, s2 = _layer_prompt(x, mod, mod_row0, g_mix, w_in, conv_w, conv_b, cln_g, cln_b,
                                rln_g, rln_b, w_out, g_ffn, w_ff1, w_ff2, g_final, rope, decay,
                                tile)
    return y, hist[:, HIST_ROWS - (CONV_K - 1):], _unpack_state(s2)


def _sample_path(x, mod, cache, state, pos0, g_mix, w_in, conv_w, conv_b, cln_g, cln_b,
                 rln_g, rln_b, w_out, g_ffn, w_ff1, w_ff2, g_final):
    n_seq, n_pos, d = x.shape
    xs = x.reshape(n_seq, n_pos * d)
    rope = _rope_tables(8, pos0)
    hist_t, *tok = _proj_sample(xs, mod, g_mix, w_in, cache.transpose(1, 0, 2),
                                conv_w, conv_b, cln_g, cln_b, rope)
    conv_out, q, k, v, g = [a.reshape(n_pos * n_seq, -1) for a in tok]
    ret_out, s_new_t = _ret_sample(q, k, v, g, state.reshape(n_seq, -1).T, _power_table(),
                                   rln_g, rln_b, n_pos, n_seq)
    s_new = s_new_t.T
    y = _mlp_sample(xs, conv_out, ret_out, w_out, mod, g_ffn, w_ff1, w_ff2, g_final)
    return y.reshape(x.shape), hist_t.transpose(1, 0, 2), s_new.reshape(state.shape)


def kernel(x_prompt, x_sample, cache_conv, state_ret, c_prompt, c_sample, w_ada, b_ada, g_mix, w_in, conv_w, conv_b, conv_ln_g, conv_ln_b, ret_ln_g, ret_ln_b, w_out, g_ffn, w_ff1, w_ff2, g_final):
    mod = _modulation(jnp.concatenate([c_sample, c_prompt], axis=0), w_ada[0], b_ada[0])
    params = (g_mix[0], w_in[0], conv_w[0], conv_b[0], conv_ln_g[0], conv_ln_b[0], ret_ln_g[0],
              ret_ln_b[0], w_out[0], g_ffn[0], w_ff1[0], w_ff2[0], g_final)
    y_p, conv_p, ret_p = _prompt_path(x_prompt, mod, x_sample.shape[0], *params)
    y_s, conv_s, ret_s = _sample_path(x_sample, mod, cache_conv[0], state_ret[0], PAST_LEN,
                                      *params)
    return (y_p, y_s, conv_p[None], ret_p[None], conv_s[None], ret_s[None])
```

```python
import functools

import jax
import jax.numpy as jnp
from jax import lax
from jax.experimental import pallas as pl
from jax.experimental.pallas import tpu as pltpu

D_MODEL = 1024
CONV_CH = 512
CONV_K = 31
RET_HEADS = 8
RET_W = 512
RET_DK = 64
RET_DV = 64
D_FF = 4 * D_MODEL
RET_CHUNK = 128
ROPE_THETA = 10000.0
EPS = 1e-6
N_MOD = 6
PAST_LEN = 16384

SUBLANES = 8
HIST_ROWS = 32
CONV_ROW_BLOCK = 32
PROMPT_TILE = 256
FFN_CHUNK = 512
SAMPLE_FFN_CHUNK = 1024
SAMPLE_SEQ_BLOCK = 32
WEIGHT_STAGE_ELEMS = 512 * 1024
WEIGHT_STAGE_SLOTS = 3
LAYER_PROGRAM = ("c cf cf cf cf cf cf cf cf "
                 "rrrr rfrf rfrf r "
                 "rfrf rfrf r")
PAIR_LANES = 2 * RET_DK
N_PAIRS = RET_HEADS // 2
VMEM_LIMIT = 56 * 1024 * 1024

F32 = jnp.float32
BF16 = jnp.bfloat16


def _log_decay(head):
    return jnp.log1p(-jnp.exp2(-5.0 - head))


def _rope_kernel(pos0, cos_ref, sin_ref):
    rows = cos_ref.shape[0]
    lane = lax.broadcasted_iota(jnp.int32, (rows, PAIR_LANES), 1)
    row = lax.broadcasted_iota(jnp.int32, (rows, PAIR_LANES), 0)
    half = RET_DK // 2
    freq = (lane % half).astype(F32)
    inv = jnp.power(jnp.full_like(freq, ROPE_THETA), -freq / half)
    ang = (row + pos0).astype(F32) * inv
    first = (lane % RET_DK) < half
    cos_ref[...] = jnp.cos(ang)
    sin_ref[...] = jnp.where(first, -jnp.sin(ang), jnp.sin(ang))


ROPE_BLOCK = 16


def _rope_blocked_kernel(pos0, cos_ref, sin_ref):
    rows = cos_ref.shape[0]
    n_blk = rows // ROPE_BLOCK
    half = RET_DK // 2

    def angles(n, step, start):
        lane = lax.broadcasted_iota(jnp.int32, (n, PAIR_LANES), 1)
        row = lax.broadcasted_iota(jnp.int32, (n, PAIR_LANES), 0)
        freq = (lane % half).astype(F32)
        inv = jnp.power(jnp.full_like(freq, ROPE_THETA), -freq / half)
        return (row * step + start).astype(F32) * inv

    a = angles(n_blk, ROPE_BLOCK, pos0)
    b = angles(ROPE_BLOCK, 1, 0)
    lane = lax.broadcasted_iota(jnp.int32, (ROPE_BLOCK, PAIR_LANES), 1)
    sign = jnp.where((lane % RET_DK) < half, -1.0, 1.0)
    cos_a, sin_a = jnp.cos(a), jnp.sin(a)
    cos_b, sin_b = jnp.cos(b), jnp.sin(b)
    cos_bs, sin_bs = cos_b * sign, sin_b * sign
    for blk in range(n_blk):
        ca, sa = cos_a[blk:blk + 1, :], sin_a[blk:blk + 1, :]
        rs = slice(blk * ROPE_BLOCK, (blk + 1) * ROPE_BLOCK)
        cos_ref[rs, :] = ca * cos_b - sa * sin_b
        sin_ref[rs, :] = sa * cos_bs + ca * sin_bs


def _rope_tables(rows, pos0):
    blocked = rows % ROPE_BLOCK == 0 and rows // ROPE_BLOCK >= SUBLANES
    return pl.pallas_call(
        functools.partial(_rope_blocked_kernel if blocked else _rope_kernel, pos0),
        out_shape=(jax.ShapeDtypeStruct((rows, PAIR_LANES), F32),
                   jax.ShapeDtypeStruct((rows, PAIR_LANES), F32)),
        name="rope_tables",
    )()


def _decay_kernel(chunk, dec_ref, qdec_ref, kdec_ref, sdec_ref):
    c = chunk
    shape = (N_PAIRS, c, 2 * c)
    col = lax.broadcasted_iota(jnp.int32, shape, 2)
    h = (2 * lax.broadcasted_iota(jnp.int32, shape, 0) + col // c).astype(F32)
    i = lax.broadcasted_iota(jnp.int32, shape, 1).astype(F32)
    j = (col % c).astype(F32)
    diff = i - j
    dec_ref[...] = jnp.where(diff >= 0, jnp.exp(_log_decay(h) * jnp.maximum(diff, 0.0)), 0.0)
    lane_h = (lax.broadcasted_iota(jnp.int32, (c, RET_W), 1) // RET_DK).astype(F32)
    idx = lax.broadcasted_iota(jnp.int32, (c, RET_W), 0).astype(F32)
    lg = _log_decay(lane_h)
    qdec_ref[...] = jnp.exp(lg * (idx + 1.0))
    kdec_ref[...] = jnp.exp(lg * (c - 1.0 - idx))
    p = lax.broadcasted_iota(jnp.int32, (N_PAIRS, PAIR_LANES, PAIR_LANES), 0)
    r = lax.broadcasted_iota(jnp.int32, (N_PAIRS, PAIR_LANES, PAIR_LANES), 1) // RET_DK
    q = lax.broadcasted_iota(jnp.int32, (N_PAIRS, PAIR_LANES, PAIR_LANES), 2) // RET_DK
    hh = (2 * p + r).astype(F32)
    sdec_ref[...] = jnp.where(r == q, jnp.exp(_log_decay(hh) * float(c)), 0.0)


def _decay_tables(chunk):
    return pl.pallas_call(
        functools.partial(_decay_kernel, chunk),
        out_shape=(jax.ShapeDtypeStruct((N_PAIRS, chunk, 2 * chunk), F32),
                   jax.ShapeDtypeStruct((chunk, RET_W), F32),
                   jax.ShapeDtypeStruct((chunk, RET_W), F32),
                   jax.ShapeDtypeStruct((N_PAIRS, PAIR_LANES, PAIR_LANES), F32)),
        name="decay_tables",
    )()


def _mod_kernel(c_ref, w_lo_ref, w_hi_ref, b_ref, o_ref):
    c = c_ref[...]
    s = (c * jax.nn.sigmoid(c)).astype(BF16)
    half = w_lo_ref.shape[1]
    for k, w_ref in enumerate((w_lo_ref, w_hi_ref)):
        cs = slice(k * half, (k + 1) * half)
        o_ref[:, cs] = _dot(s, w_ref[...].astype(BF16)) + b_ref[:, cs]


def _modulation(c, w_ada, b_ada):
    n, d = c.shape
    width = w_ada.shape[1]
    blk = D_MODEL
    half = lambda k: pl.BlockSpec((d, blk // 2), lambda i: (0, 2 * i + k))
    return pl.pallas_call(
        _mod_kernel,
        grid=(width // blk,),
        in_specs=[pl.BlockSpec((n, d), lambda i: (0, 0)), half(0), half(1),
                  pl.BlockSpec((1, blk), lambda i: (0, i))],
        out_specs=pl.BlockSpec((n, blk), lambda i: (0, i)),
        out_shape=jax.ShapeDtypeStruct((n, width), F32),
        compiler_params=pltpu.CompilerParams(vmem_limit_bytes=VMEM_LIMIT),
        name="adaln_mod",
    )(c, w_ada, w_ada, b_ada.reshape(1, width))


def _rmsnorm_mod(x, g, scale, shift):
    y = x * lax.rsqrt(jnp.mean(x * x, axis=-1, keepdims=True) + EPS)
    return (y * g) * (1.0 + scale) + shift


def _layernorm(x, g, b):
    mu = jnp.mean(x, axis=-1, keepdims=True)
    d = x - mu
    var = jnp.mean(d * d, axis=-1, keepdims=True)
    return d * lax.rsqrt(var + EPS) * g + b


def _silu(x):
    return x * jax.nn.sigmoid(x)


def _rotary_pair(x, cos, sin_signed, first_half):
    partner = jnp.where(first_half, pltpu.roll(x, PAIR_LANES - RET_DK // 2, 1),
                        pltpu.roll(x, RET_DK // 2, 1))
    return x * cos + partner * sin_signed


def _pair_groupnorm(o, lo, g, b):
    inv_n = 1.0 / RET_DV
    s_lo = jnp.sum(jnp.where(lo, o, 0.0), axis=-1, keepdims=True)
    s_hi = jnp.sum(jnp.where(lo, 0.0, o), axis=-1, keepdims=True)
    d = o - jnp.where(lo, s_lo, s_hi) * inv_n
    d2 = d * d
    v_lo = jnp.sum(jnp.where(lo, d2, 0.0), axis=-1, keepdims=True)
    v_hi = jnp.sum(jnp.where(lo, 0.0, d2), axis=-1, keepdims=True)
    var = jnp.where(lo, v_lo, v_hi) * inv_n
    return d * lax.rsqrt(var + EPS) * g + b


def _dot(a, b):
    return jnp.dot(a, b, preferred_element_type=F32)


def _dot_nt(a, b):
    return lax.dot_general(a, b, (((1,), (1,)), ((), ())), preferred_element_type=F32)


def _dot_tn(a, b):
    return lax.dot_general(a, b, (((0,), (0,)), ((), ())), preferred_element_type=F32)


def _stream_cast_weights(rings):
    def ring_chunks(pairs, stage):
        rows = stage.shape[1]
        return [(src, dst, r0) for src, dst in pairs for r0 in range(0, src.shape[0], rows)]

    chunks = [ring_chunks(pairs, stage) for pairs, stage, _ in rings]

    def copy(k, c):
        _, stage, sems = rings[k]
        src, _, r0 = chunks[k][c]
        slot = c % stage.shape[0]
        return pltpu.make_async_copy(src.at[pl.ds(r0, stage.shape[1]), :],
                                     stage.at[slot, :, pl.ds(0, src.shape[1])], sems.at[slot])

    for k, (_, stage, _) in enumerate(rings):
        for c in range(min(stage.shape[0] - 1, len(chunks[k]))):
            copy(k, c).start()
    for c in range(max(len(ch) for ch in chunks)):
        for k, (_, stage, _) in enumerate(rings):
            if c >= len(chunks[k]):
                continue
            n_slots, rows = stage.shape[0], stage.shape[1]
            if c + n_slots - 1 < len(chunks[k]):
                copy(k, c + n_slots - 1).start()
            copy(k, c).wait()
            src, dst, r0 = chunks[k][c]
            dst[r0:r0 + rows, :] = stage[c % n_slots, :, 0:src.shape[1]].astype(BF16)


def _layer_prompt_kernel(tile, nj, x_ref, mod_ref, gmix_ref, win_hbm, convw_ref, convb_ref,
                         clng_ref, clnb_ref, rlng_ref, rlnb_ref, wout_hbm,
                         cos_ref, sin_ref, dec_ref, qdec_ref, kdec_ref, sdec_ref,
                         mod2_ref, gffn_ref, w1_hbm, w2_hbm, gfin_ref,
                         y_ref, hist_ref, state_ref,
                         ubuf, ushift, sbuf, mixbuf, x1buf,
                         win_ref, wout_ref, w1_ref, w2_ref,
                         stage_wide, stage_tall, sems_wide, sems_tall):
    g = pl.program_id(0)
    total = pl.num_programs(0) - 1
    j = lax.rem(jnp.minimum(g, total - 1), nj)
    live = g < total

    @pl.when(g == 0)
    def _():
        _stream_cast_weights([
            ([(win_hbm, win_ref), (w1_hbm, w1_ref)], stage_wide, sems_wide),
            ([(wout_hbm, wout_ref), (w2_hbm, w2_ref)], stage_tall, sems_tall)])
        x1buf[...] = jnp.zeros((tile, D_MODEL), F32)

    @pl.when(jnp.logical_and(j == 0, live))
    def _():
        ubuf[0:HIST_ROWS, :] = jnp.zeros((HIST_ROWS, CONV_CH), F32)
        sbuf[...] = jnp.zeros(sbuf.shape, F32)

    x = x_ref[0]
    b_mix = lax.div(jnp.minimum(g, total - 1), nj)
    b_mlp = lax.div(jnp.maximum(g - 1, 0), nj)
    mods = lambda ref, b: [ref[pl.ds(b, 1), k * D_MODEL:(k + 1) * D_MODEL] for k in range(3)]
    shift, scale, gate = mods(mod_ref, b_mix)

    def conv_stages(u):
        ubuf[HIST_ROWS:HIST_ROWS + tile, :] = u
        off = HIST_ROWS - (CONV_K - 1)
        for r in range(SUBLANES):
            n = tile + SUBLANES * ((CONV_K - 1 - r) // SUBLANES)
            ushift[r, 0:n, :] = ubuf[off + r:off + r + n, :]
        ubuf[0:HIST_ROWS, :] = ubuf[tile:tile + HIST_ROWS, :]
        yield
        for rb in range(tile // CONV_ROW_BLOCK):
            r0 = rb * CONV_ROW_BLOCK
            dw = jnp.broadcast_to(convb_ref[...], (CONV_ROW_BLOCK, CONV_CH))
            for k in range(CONV_K):
                s0 = r0 + SUBLANES * (k // SUBLANES)
                wk = jnp.concatenate([convw_ref[k]] * (CONV_ROW_BLOCK // SUBLANES), axis=0)
                dw = dw + wk * ushift[k % SUBLANES, s0:s0 + CONV_ROW_BLOCK, :]
            conv_out = _silu(_layernorm(dw, clng_ref[...], clnb_ref[...]))
            mixbuf[r0:r0 + CONV_ROW_BLOCK, 0:CONV_CH] = conv_out.astype(BF16)
            yield

    def ret_stages(proj):
        ret_proj = []
        for k in range(4):
            ret_proj.append(proj(2 + k))
            yield
        q, kk, v, gt = ret_proj
        q = q * (RET_DK ** -0.5)
        lane = lax.broadcasted_iota(jnp.int32, (1, PAIR_LANES), 1)
        lo = lane < RET_DK
        first_half = (lane % RET_DK) < (RET_DK // 2)
        lo_b = lo.astype(BF16)
        hi_b = (~lo).astype(BF16)
        pairs = range(N_PAIRS)
        halves = (pairs[:N_PAIRS // 2], pairs[N_PAIRS // 2:])
        lanes = [slice(p * PAIR_LANES, (p + 1) * PAIR_LANES) for p in pairs]
        for c in range(tile // RET_CHUNK):
            rs = slice(c * RET_CHUNK, (c + 1) * RET_CHUNK)
            cos = cos_ref[rs, :]
            sin = sin_ref[rs, :]
            qc, kc, vc, scores, outs = {}, {}, {}, {}, {}
            for group in halves:
                for p in group:
                    qc[p] = _rotary_pair(q[rs, lanes[p]], cos, sin, first_half)
                    kc[p] = _rotary_pair(kk[rs, lanes[p]], cos, sin, first_half)
                    vc[p] = v[rs, lanes[p]].astype(BF16)
                    kb = kc[p].astype(BF16)
                    k_cat = jnp.concatenate([kb * lo_b, kb * hi_b], axis=0)
                    scores[p] = _dot_nt(qc[p].astype(BF16), k_cat)
                yield
            for group in halves:
                for p in group:
                    state = sbuf[p]
                    lhs = jnp.concatenate([(scores[p] * dec_ref[p]).astype(BF16),
                                           (qc[p] * qdec_ref[:, lanes[p]]).astype(BF16)], axis=1)
                    rhs = jnp.concatenate([vc[p] * lo_b, vc[p] * hi_b, state.astype(BF16)],
                                          axis=0)
                    outs[p] = _dot(lhs, rhs)
                    kd = (kc[p] * kdec_ref[:, lanes[p]]).astype(BF16)
                    sbuf[p] = (state * sdec_ref[p]
                               + jnp.where(sdec_ref[p] > 0.0, _dot_tn(kd, vc[p]), 0.0))
                yield
            for p in pairs:
                on = _pair_groupnorm(outs[p], lo, rlng_ref[:, lanes[p]], rlnb_ref[:, lanes[p]])
                mixbuf[rs, CONV_CH + p * PAIR_LANES:CONV_CH + (p + 1) * PAIR_LANES] = (
                    on * _silu(gt[rs, lanes[p]])).astype(BF16)
            yield

    ffn = _ffn_stages(x1buf[...], mods(mod2_ref, b_mlp), gffn_ref, w1_ref, w2_ref, gfin_ref, y_ref)
    next(ffn)
    hb = _rmsnorm_mod(x, gmix_ref[...], scale, shift).astype(BF16)

    def proj(k):
        return _dot(hb, win_ref[:, k * CONV_CH:(k + 1) * CONV_CH])

    streams = {"f": ffn, "c": conv_stages(proj(0) * jax.nn.sigmoid(proj(1))),
               "r": ret_stages(proj)}
    for name in LAYER_PROGRAM.replace(" ", ""):
        next(streams[name], None)
    for stream in (streams["c"], streams["r"], streams["f"]):
        for _ in stream:
            pass

    mix = _dot(mixbuf[...], wout_ref[...])
    x1buf[...] = x + gate * mix

    @pl.when(jnp.logical_and(j == nj - 1, live))
    def _():
        hist_ref[0] = ubuf[0:HIST_ROWS, :]
        state_ref[0] = sbuf[...]


def _layer_prompt(x, mod, mod_row0, g_mix, w_in, conv_w, conv_b, cln_g, cln_b, rln_g, rln_b,
                  w_out, g_ffn, w1, w2, g_final, rope, decay, tile):
    bsz, seq, d = x.shape
    assert bsz % SUBLANES == 0 and mod_row0 % bsz == 0
    in_hbm = pl.BlockSpec(memory_space=pl.ANY)

    def stage_shape(*ws):
        width = max(w.shape[1] for w in ws)
        rows = WEIGHT_STAGE_ELEMS // width
        assert all(w.shape[0] % rows == 0 for w in ws)
        return pltpu.VMEM((WEIGHT_STAGE_SLOTS, rows, width), F32)

    mod_blk = mod_row0 // bsz
    nj = seq // tile
    total = bsz * nj
    cos, sin = rope
    dec, qdec, kdec, sdec = decay
    mix_tile = lambda g: jnp.minimum(g, total - 1)
    ffn_tile = lambda g: jnp.maximum(g - 1, 0)
    resident = lambda shape: pl.BlockSpec(shape, lambda g: (0,) * len(shape),
                                          pipeline_mode=pl.Buffered(1))
    row = lambda a: a.reshape(1, -1)
    conv_w_rows = jnp.broadcast_to(conv_w[:, None, :], (CONV_K, SUBLANES, CONV_CH))
    rope_spec = pl.BlockSpec((tile, PAIR_LANES), lambda g: (mix_tile(g) % nj, 0))
    return pl.pallas_call(
        functools.partial(_layer_prompt_kernel, tile, nj),
        grid=(total + 1,),
        in_specs=[
            pl.BlockSpec((1, tile, d), lambda g: (mix_tile(g) // nj, mix_tile(g) % nj, 0)),
            pl.BlockSpec((bsz, 3 * d), lambda g: (mod_blk, 0), pipeline_mode=pl.Buffered(1)),
            resident((1, d)),
            in_hbm,
            resident((CONV_K, SUBLANES, CONV_CH)),
            resident((1, CONV_CH)),
            resident((1, CONV_CH)),
            resident((1, CONV_CH)),
            resident((1, RET_W)),
            resident((1, RET_W)),
            in_hbm,
            rope_spec,
            rope_spec,
            resident(dec.shape),
            resident(qdec.shape),
            resident(kdec.shape),
            resident(sdec.shape),
            pl.BlockSpec((bsz, 3 * d), lambda g: (mod_blk, 1), pipeline_mode=pl.Buffered(1)),
            resident((1, d)),
            in_hbm,
            in_hbm,
            resident((1, d)),
        ],
        out_specs=[
            pl.BlockSpec((1, tile, d), lambda g: (ffn_tile(g) // nj, ffn_tile(g) % nj, 0)),
            pl.BlockSpec((1, HIST_ROWS, CONV_CH), lambda g: (mix_tile(g) // nj, 0, 0)),
            pl.BlockSpec((1, N_PAIRS, PAIR_LANES, PAIR_LANES),
                         lambda g: (mix_tile(g) // nj, 0, 0, 0)),
        ],
        out_shape=(
            jax.ShapeDtypeStruct((bsz, seq, d), F32),
            jax.ShapeDtypeStruct((bsz, HIST_ROWS, CONV_CH), F32),
            jax.ShapeDtypeStruct((bsz, N_PAIRS, PAIR_LANES, PAIR_LANES), F32),
        ),
        scratch_shapes=[
            pltpu.VMEM((HIST_ROWS + tile, CONV_CH), F32),
            pltpu.VMEM((SUBLANES, HIST_ROWS + tile, CONV_CH), F32),
            pltpu.VMEM((N_PAIRS, PAIR_LANES, PAIR_LANES), F32),
            pltpu.VMEM((tile, D_MODEL), BF16),
            pltpu.VMEM((tile, D_MODEL), F32),
            pltpu.VMEM(w_in.shape, BF16),
            pltpu.VMEM(w_out.shape, BF16),
            pltpu.VMEM(w1.shape, BF16),
            pltpu.VMEM(w2.shape, BF16),
            stage_shape(w_in, w1),
            stage_shape(w_out, w2),
            pltpu.SemaphoreType.DMA((WEIGHT_STAGE_SLOTS,)),
            pltpu.SemaphoreType.DMA((WEIGHT_STAGE_SLOTS,)),
        ],
        compiler_params=pltpu.CompilerParams(
            dimension_semantics=("arbitrary",), vmem_limit_bytes=VMEM_LIMIT),
        name="layer_prompt",
    )(x, mod, row(g_mix), w_in, conv_w_rows, row(conv_b), row(cln_g), row(cln_b),
      row(rln_g), row(rln_b), w_out, cos, sin, dec, qdec, kdec, sdec,
      mod, row(g_ffn), w1, w2, row(g_final))


def _ffn_stages(x, mods, gffn_ref, w1_ref, w2_ref, gfin_ref, y_ref):
    shift, scale, gate = mods
    hb = _rmsnorm_mod(x, gffn_ref[...], scale, shift).astype(BF16)
    yield
    half = D_MODEL // 2
    ff = [None, None]
    for c in range(D_FF // (2 * FFN_CHUNK)):
        acts = []
        for h in range(2):
            c0 = (2 * c + h) * FFN_CHUNK
            a = jnp.maximum(_dot(hb, w1_ref[:, c0:c0 + FFN_CHUNK]), 0.0)
            acts.append((a * a).astype(BF16))
            yield
        act = jnp.concatenate(acts, axis=1)
        for n in range(2):
            part = _dot(act, w2_ref[2 * c * FFN_CHUNK:2 * (c + 1) * FFN_CHUNK,
                                    n * half:(n + 1) * half])
            ff[n] = part if ff[n] is None else ff[n] + part
            yield
    z = x + gate * jnp.concatenate(ff, axis=1)
    y_ref[0] = z * lax.rsqrt(jnp.mean(z * z, axis=-1, keepdims=True) + EPS) * gfin_ref[...]


def _pos_cols(x_ref, i):
    return x_ref[:, i * D_MODEL:(i + 1) * D_MODEL]


def _mod_cols(mod_ref, n_seq, k):
    return mod_ref[0:n_seq, k * D_MODEL:(k + 1) * D_MODEL]


def _mlp_sample_kernel(n_pos, n_seq, x_ref, conv_ref, ret_ref, wout_ref, mod_ref,
                       gffn_ref, w1_ref, w2_ref, gfin_ref, y_ref, x1buf, hbuf, acc):
    c = pl.program_id(0)
    rows = lambda i: slice(i * n_seq, (i + 1) * n_seq)

    @pl.when(c == 0)
    def _():
        mix = (_dot(conv_ref[...], wout_ref[0:CONV_CH, :].astype(BF16))
               + _dot(ret_ref[...], wout_ref[CONV_CH:CONV_CH + RET_W, :].astype(BF16)))
        for i in range(n_pos):
            x1 = _pos_cols(x_ref, i) + _mod_cols(mod_ref, n_seq, 2) * mix[rows(i)]
            x1buf[rows(i), :] = x1
            hbuf[rows(i), :] = _rmsnorm_mod(x1, gffn_ref[...], _mod_cols(mod_ref, n_seq, 4),
                                            _mod_cols(mod_ref, n_seq, 3)).astype(BF16)

    a = jnp.maximum(_dot(hbuf[...], w1_ref[...].astype(BF16)), 0.0)
    part = _dot((a * a).astype(BF16), w2_ref[...].astype(BF16))

    @pl.when(c == 0)
    def _():
        acc[...] = part

    @pl.when(c > 0)
    def _():
        acc[...] += part

    @pl.when(c == pl.num_programs(0) - 1)
    def _():
        for i in range(n_pos):
            z = x1buf[rows(i), :] + _mod_cols(mod_ref, n_seq, 5) * acc[rows(i), :]
            y_ref[:, i * D_MODEL:(i + 1) * D_MODEL] = (
                z * lax.rsqrt(jnp.mean(z * z, axis=-1, keepdims=True) + EPS) * gfin_ref[...])


def _mlp_sample(x, conv_out, ret_out, w_out, mod, g_ffn, w1, w2, g_final):
    n_seq, d = x.shape[0], D_MODEL
    n_pos = x.shape[1] // d
    n_tok = n_pos * n_seq
    n_chunks = D_FF // SAMPLE_FFN_CHUNK
    whole = lambda shape: pl.BlockSpec(shape, lambda c: (0,) * len(shape),
                                       pipeline_mode=pl.Buffered(1))
    return pl.pallas_call(
        functools.partial(_mlp_sample_kernel, n_pos, n_seq),
        grid=(n_chunks,),
        in_specs=[whole(x.shape), whole(conv_out.shape), whole(ret_out.shape),
                  whole(w_out.shape), whole(mod.shape), whole((1, d)),
                  pl.BlockSpec((d, SAMPLE_FFN_CHUNK), lambda c: (0, c)),
                  pl.BlockSpec((SAMPLE_FFN_CHUNK, d), lambda c: (c, 0)),
                  whole((1, d))],
        out_specs=pl.BlockSpec(x.shape, lambda c: (0, 0)),
        out_shape=jax.ShapeDtypeStruct(x.shape, F32),
        scratch_shapes=[pltpu.VMEM((n_tok, d), F32), pltpu.VMEM((n_tok, d), BF16),
                        pltpu.VMEM((n_tok, d), F32)],
        compiler_params=pltpu.CompilerParams(
            dimension_semantics=("arbitrary",), vmem_limit_bytes=VMEM_LIMIT),
        name="mlp_sample",
    )(x, conv_out, ret_out, w_out, mod, g_ffn.reshape(1, d), w1, w2, g_final.reshape(1, d))


def _power_table_kernel(o_ref):
    lane_h = (lax.broadcasted_iota(jnp.int32, o_ref.shape, 1) // RET_DK).astype(F32)
    n = lax.broadcasted_iota(jnp.int32, o_ref.shape, 0).astype(F32)
    o_ref[...] = jnp.exp(_log_decay(lane_h) * n)


def _power_table():
    return pl.pallas_call(_power_table_kernel,
                          out_shape=jax.ShapeDtypeStruct((8, RET_W), F32),
                          name="decay_powers")()


def _proj_sample_kernel(n_pos, n_seq, x_ref, mod_ref, gmix_ref, win_ref, cache_ref, convw_ref,
                        convb_ref, clng_ref, clnb_ref, cos_ref, sin_ref,
                        hist_ref, conv_ref, q_ref, k_ref, v_ref, g_ref):
    shift, scale = _mod_cols(mod_ref, n_seq, 0), _mod_cols(mod_ref, n_seq, 1)
    hb = jnp.concatenate(
        [_rmsnorm_mod(_pos_cols(x_ref, i), gmix_ref[...], scale, shift).astype(BF16)
         for i in range(n_pos)], axis=0)

    def proj(k):
        return _dot(hb, win_ref[:, k * CONV_CH:(k + 1) * CONV_CH].astype(BF16))

    u = proj(0) * jax.nn.sigmoid(proj(1))
    n_hist = CONV_K - 1
    rows = lambda a, i: a[i * n_seq:(i + 1) * n_seq]
    for m in range(n_hist - n_pos):
        hist_ref[m] = cache_ref[m + n_pos]
    for i in range(n_pos):
        hist_ref[n_hist - n_pos + i] = rows(u, i)
        dw = jnp.broadcast_to(convb_ref[...], (n_seq, CONV_CH))
        for k in range(CONV_K):
            m = i + k
            src = cache_ref[m] if m < n_hist else rows(u, m - n_hist)
            dw = dw + convw_ref[k:k + 1, :] * src
        conv_ref[i] = _silu(_layernorm(dw, clng_ref[...], clnb_ref[...])).astype(BF16)

    lane = lax.broadcasted_iota(jnp.int32, (1, PAIR_LANES), 1)
    first_half = (lane % RET_DK) < (RET_DK // 2)
    q = proj(2) * (RET_DK ** -0.5)
    kk = proj(3)
    v = proj(4)
    gt = proj(5)
    for i in range(n_pos):
        cos = cos_ref[i:i + 1, :]
        sin = sin_ref[i:i + 1, :]
        for p in range(N_PAIRS):
            sl = slice(p * PAIR_LANES, (p + 1) * PAIR_LANES)
            q_ref[i, :, sl] = _rotary_pair(rows(q, i)[:, sl], cos, sin, first_half)
            k_ref[i, :, sl] = _rotary_pair(rows(kk, i)[:, sl], cos, sin, first_half)
        v_ref[i] = rows(v, i)
        g_ref[i] = rows(gt, i)


def _proj_sample(x, mod, g_mix, w_in, cache_t, conv_w, conv_b, cln_g, cln_b, rope):
    n_seq, d = x.shape[0], D_MODEL
    n_pos = x.shape[1] // d
    blk = SAMPLE_SEQ_BLOCK
    cos, sin = rope
    row = lambda a: a.reshape(1, -1)
    whole = lambda a: pl.BlockSpec(a.shape, lambda s: (0,) * a.ndim, pipeline_mode=pl.Buffered(1))
    seqs = lambda lead, width: pl.BlockSpec((lead, blk, width), lambda s: (0, s, 0))
    tok = lambda dt: jax.ShapeDtypeStruct((n_pos, n_seq, RET_W), dt)
    args = (x, mod, row(g_mix), w_in, cache_t, conv_w, row(conv_b), row(cln_g), row(cln_b),
            cos, sin)
    return pl.pallas_call(
        functools.partial(_proj_sample_kernel, n_pos, blk),
        grid=(n_seq // blk,),
        in_specs=[pl.BlockSpec((blk, n_pos * d), lambda s: (s, 0)),
                  pl.BlockSpec((blk, mod.shape[1]), lambda s: (s, 0)),
                  whole(args[2]), whole(w_in), seqs(cache_t.shape[0], CONV_CH)]
        + [whole(a) for a in args[5:]],
        out_specs=[seqs(cache_t.shape[0], CONV_CH)] + [seqs(n_pos, RET_W)] * 5,
        out_shape=(jax.ShapeDtypeStruct(cache_t.shape, F32), tok(BF16),
                   tok(F32), tok(F32), tok(F32), tok(F32)),
        compiler_params=pltpu.CompilerParams(
            dimension_semantics=("arbitrary",), vmem_limit_bytes=VMEM_LIMIT),
        name="proj_sample",
    )(*args)


def _ret_sample_kernel(n_pos, n_seq, q_ref, k_ref, v_ref, g_ref, st_ref, pw_ref, rlng_ref,
                       rlnb_ref, out_ref, snew_ref, sb_buf, qd_buf, kd_buf, v_buf):
    lane = lax.broadcasted_iota(jnp.int32, (1, PAIR_LANES), 1)
    lo = lane < RET_DK
    rows = lambda a, i: a[i * n_seq:(i + 1) * n_seq]
    cols = lambda a, i: a[:, i * n_seq:(i + 1) * n_seq]
    pw = pw_ref[...]
    q = q_ref[...]
    k = k_ref[...]
    qb = [rows(q, i).astype(BF16) for i in range(n_pos)]
    kb = [rows(k, i).astype(BF16) for i in range(n_pos)]
    vb = [rows(v_ref[...], i).astype(BF16) for i in range(n_pos)]

    o = []
    for i in range(n_pos):
        acc = jnp.zeros((n_seq, PAIR_LANES), F32)
        for j in range(i + 1):
            prod = qb[i].astype(F32) * kb[j].astype(F32)
            s_lo = jnp.sum(jnp.where(lo, prod, 0.0), axis=-1, keepdims=True)
            s_hi = jnp.sum(jnp.where(lo, 0.0, prod), axis=-1, keepdims=True)
            score = jnp.where(lo, s_lo, s_hi) * pw[i - j:i - j + 1, :]
            acc = acc + score.astype(BF16).astype(F32) * vb[j].astype(F32)
        o.append(acc)

    rounded = lambda x: x.astype(BF16).astype(F32)
    qd_buf[...] = jnp.concatenate(
        [rounded(rows(q, i) * pw[i + 1:i + 2, :]) for i in range(n_pos)], axis=0).T
    kd_buf[...] = jnp.concatenate(
        [rounded(rows(k, j) * pw[n_pos - 1 - j:n_pos - j, :]) for j in range(n_pos)], axis=0).T
    v_buf[...] = jnp.concatenate([vb[j].astype(F32) for j in range(n_pos)], axis=0).T
    tile_at = lambda r: pl.ds(pl.multiple_of(r * RET_DV, RET_DV), RET_DV)
    pos = lambda i: slice(i * n_seq, (i + 1) * n_seq)

    for a in range(2):
        gam = pw[n_pos:n_pos + 1, a * RET_DK:a * RET_DK + 1]
        v_rows = slice(a * RET_DV, (a + 1) * RET_DV)

        def update(grp, carry, a=a, gam=gam, v_rows=v_rows):
            r0 = pl.multiple_of(a * RET_DK + grp * SUBLANES, SUBLANES)
            kd_rows = kd_buf[pl.ds(r0, SUBLANES), :]
            for s in range(SUBLANES):
                r = r0 + s
                s_old = st_ref[tile_at(r), :]
                sb_buf[tile_at(r), :] = rounded(s_old)
                s_new = s_old * gam
                for j in range(n_pos):
                    s_new = s_new + kd_rows[s:s + 1, pos(j)] * v_buf[v_rows, pos(j)]
                snew_ref[tile_at(r), :] = s_new
            return carry

        lax.fori_loop(0, RET_DK // SUBLANES, update, 0)

    for i in range(n_pos):
        heads = []
        for a in range(2):
            def cross(grp, acc, a=a, i=i):
                r0 = pl.multiple_of(a * RET_DK + grp * SUBLANES, SUBLANES)
                qd_rows = qd_buf[pl.ds(r0, SUBLANES), pos(i)]
                for s in range(SUBLANES):
                    acc = acc + qd_rows[s:s + 1, :] * sb_buf[tile_at(r0 + s), :]
                return acc

            heads.append(lax.fori_loop(0, RET_DK // SUBLANES, cross,
                                       jnp.zeros((RET_DV, n_seq), F32)))
        oi = o[i] + jnp.concatenate(heads, axis=0).T
        on = _pair_groupnorm(oi, lo, rlng_ref[...], rlnb_ref[...])
        out_ref[i * n_seq:(i + 1) * n_seq, :] = (on * _silu(rows(g_ref[...], i))).astype(BF16)


def _ret_sample(q, k, v, g, state_t, pw, rln_g, rln_b, n_pos, n_seq):
    n_tok = n_pos * n_seq
    pair_state = 2 * RET_DK * RET_DV
    slab = pl.BlockSpec((n_tok, PAIR_LANES), lambda p: (0, p))
    state_spec = pl.BlockSpec((pair_state, n_seq), lambda p: (p, 0))
    return pl.pallas_call(
        functools.partial(_ret_sample_kernel, n_pos, n_seq),
        grid=(N_PAIRS,),
        in_specs=[slab, slab, slab, slab, state_spec,
                  pl.BlockSpec((8, PAIR_LANES), lambda p: (0, p)),
                  pl.BlockSpec((1, PAIR_LANES), lambda p: (0, p)),
                  pl.BlockSpec((1, PAIR_LANES), lambda p: (0, p))],
        out_specs=[slab, state_spec],
        out_shape=(jax.ShapeDtypeStruct((n_tok, RET_W), BF16),
                   jax.ShapeDtypeStruct(state_t.shape, F32)),
        scratch_shapes=[pltpu.VMEM((pair_state, n_seq), F32)]
        + [pltpu.VMEM((PAIR_LANES, n_tok), F32)] * 3,
        compiler_params=pltpu.CompilerParams(
            dimension_semantics=("arbitrary",), vmem_limit_bytes=VMEM_LIMIT),
        name="ret_sample",
    )(q, k, v, g, state_t, pw, rln_g.reshape(1, RET_W), rln_b.reshape(1, RET_W))


def _unpack_state(s2):
    blocks = []
    for p in range(N_PAIRS):
        for a in range(2):
            sl = slice(a * RET_DK, (a + 1) * RET_DK)
            blocks.append(s2[:, p, sl, sl])
    return jnp.stack(blocks, axis=1)


def _prompt_path(x, mod, mod_row0, g_mix, w_in, conv_w, conv_b, cln_g, cln_b, rln_g, rln_b,
                 w_out, g_ffn, w_ff1, w_ff2, g_final, tile=PROMPT_TILE):
    seq = x.shape[1]
    rope = _rope_tables(seq, 0)
    decay = _decay_tables(RET_CHUNK)
    y, hist, s2 = _layer_prompt(x, mod, mod_row0, g_mix, w_in, conv_w, conv_b, cln_g, cln_b,
                                rln_g, rln_b, w_out, g_ffn, w_ff1, w_ff2, g_final, rope, decay,
                                tile)
    return y, hist[:, HIST_ROWS - (CONV_K - 1):], _unpack_state(s2)


def _sample_path(x, mod, cache, state, pos0, g_mix, w_in, conv_w, conv_b, cln_g, cln_b,
                 rln_g, rln_b, w_out, g_ffn, w_ff1, w_ff2, g_final):
    n_seq, n_pos, d = x.shape
    xs = x.reshape(n_seq, n_pos * d)
    rope = _rope_tables(8, pos0)
    hist_t, *tok = _proj_sample(xs, mod, g_mix, w_in, cache.transpose(1, 0, 2),
                                conv_w, conv_b, cln_g, cln_b, rope)
    conv_out, q, k, v, g = [a.reshape(n_pos * n_seq, -1) for a in tok]
    ret_out, s_new_t = _ret_sample(q, k, v, g, state.reshape(n_seq, -1).T, _power_table(),
                                   rln_g, rln_b, n_pos, n_seq)
    s_new = s_new_t.T
    y = _mlp_sample(xs, conv_out, ret_out, w_out, mod, g_ffn, w_ff1, w_ff2, g_final)
    return y.reshape(x.shape), hist_t.transpose(1, 0, 2), s_new.reshape(state.shape)


def kernel(x_prompt, x_sample, cache_conv, state_ret, c_prompt, c_sample, w_ada, b_ada, g_mix, w_in, conv_w, conv_b, conv_ln_g, conv_ln_b, ret_ln_g, ret_ln_b, w_out, g_ffn, w_ff1, w_ff2, g_final):
    mod = _modulation(jnp.concatenate([c_sample, c_prompt], axis=0), w_ada[0], b_ada[0])
    params = (g_mix[0], w_in[0], conv_w[0], conv_b[0], conv_ln_g[0], conv_ln_b[0], ret_ln_g[0],
              ret_ln_b[0], w_out[0], g_ffn[0], w_ff1[0], w_ff2[0], g_final)
    y_p, conv_p, ret_p = _prompt_path(x_prompt, mod, x_sample.shape[0], *params)
    y_s, conv_s, ret_s = _sample_path(x_sample, mod, cache_conv[0], state_ret[0], PAST_LEN,
                                      *params)
    return (y_p, y_s, conv_p[None], ret_p[None], conv_s[None], ret_s[None])
```

```python
import functools

import jax
import jax.numpy as jnp
from jax import lax
from jax.experimental import pallas as pl
from jax.experimental.pallas import tpu as pltpu

D_MODEL = 1024
CONV_CH = 512
CONV_K = 31
RET_HEADS = 8
RET_W = 512
RET_DK = 64
RET_DV = 64
D_FF = 4 * D_MODEL
RET_CHUNK = 128
ROPE_THETA = 10000.0
EPS = 1e-6
N_MOD = 6
PAST_LEN = 16384

SUBLANES = 8
HIST_ROWS = 32
CONV_ROW_BLOCK = 32
PROMPT_TILE = 256
FFN_CHUNK = 512
SAMPLE_FFN_CHUNK = 1024
SAMPLE_SEQ_BLOCK = 32
WEIGHT_STAGE_ELEMS = 512 * 1024
WEIGHT_STAGE_SLOTS = 3
LAYER_PROGRAM = ("c cf cf cf cf cf cf cf cf "
                 "rrrr rfrf rfrf r "
                 "rfrf rfrf r")
PAIR_LANES = 2 * RET_DK
N_PAIRS = RET_HEADS // 2
VMEM_LIMIT = 56 * 1024 * 1024

F32 = jnp.float32
BF16 = jnp.bfloat16


def _log_decay(head):
    return jnp.log1p(-jnp.exp2(-5.0 - head))


def _rope_kernel(pos0, cos_ref, sin_ref):
    rows = cos_ref.shape[0]
    lane = lax.broadcasted_iota(jnp.int32, (rows, PAIR_LANES), 1)
    row = lax.broadcasted_iota(jnp.int32, (rows, PAIR_LANES), 0)
    half = RET_DK // 2
    freq = (lane % half).astype(F32)
    inv = jnp.power(jnp.full_like(freq, ROPE_THETA), -freq / half)
    ang = (row + pos0).astype(F32) * inv
    first = (lane % RET_DK) < half
    cos_ref[...] = jnp.cos(ang)
    sin_ref[...] = jnp.where(first, -jnp.sin(ang), jnp.sin(ang))


ROPE_BLOCK = 16


def _rope_blocked_kernel(pos0, cos_ref, sin_ref):
    rows = cos_ref.shape[0]
    n_blk = rows // ROPE_BLOCK
    half = RET_DK // 2

    def angles(n, step, start):
        lane = lax.broadcasted_iota(jnp.int32, (n, PAIR_LANES), 1)
        row = lax.broadcasted_iota(jnp.int32, (n, PAIR_LANES), 0)
        freq = (lane % half).astype(F32)
        inv = jnp.power(jnp.full_like(freq, ROPE_THETA), -freq / half)
        return (row * step + start).astype(F32) * inv

    a = angles(n_blk, ROPE_BLOCK, pos0)
    b = angles(ROPE_BLOCK, 1, 0)
    lane = lax.broadcasted_iota(jnp.int32, (ROPE_BLOCK, PAIR_LANES), 1)
    sign = jnp.where((lane % RET_DK) < half, -1.0, 1.0)
    cos_a, sin_a = jnp.cos(a), jnp.sin(a)
    cos_b, sin_b = jnp.cos(b), jnp.sin(b)
    cos_bs, sin_bs = cos_b * sign, sin_b * sign
    for blk in range(n_blk):
        ca, sa = cos_a[blk:blk + 1, :], sin_a[blk:blk + 1, :]
        rs = slice(blk * ROPE_BLOCK, (blk + 1) * ROPE_BLOCK)
        cos_ref[rs, :] = ca * cos_b - sa * sin_b
        sin_ref[rs, :] = sa * cos_bs + ca * sin_bs


def _rope_tables(rows, pos0):
    blocked = rows % ROPE_BLOCK == 0 and rows // ROPE_BLOCK >= SUBLANES
    return pl.pallas_call(
        functools.partial(_rope_blocked_kernel if blocked else _rope_kernel, pos0),
        out_shape=(jax.ShapeDtypeStruct((rows, PAIR_LANES), F32),
                   jax.ShapeDtypeStruct((rows, PAIR_LANES), F32)),
        name="rope_tables",
    )()


def _decay_kernel(chunk, dec_ref, qdec_ref, kdec_ref, sdec_ref):
    c = chunk
    shape = (N_PAIRS, c, 2 * c)
    col = lax.broadcasted_iota(jnp.int32, shape, 2)
    h = (2 * lax.broadcasted_iota(jnp.int32, shape, 0) + col // c).astype(F32)
    i = lax.broadcasted_iota(jnp.int32, shape, 1).astype(F32)
    j = (col % c).astype(F32)
    diff = i - j
    dec_ref[...] = jnp.where(diff >= 0, jnp.exp(_log_decay(h) * jnp.maximum(diff, 0.0)), 0.0)
    lane_h = (lax.broadcasted_iota(jnp.int32, (c, RET_W), 1) // RET_DK).astype(F32)
    idx = lax.broadcasted_iota(jnp.int32, (c, RET_W), 0).astype(F32)
    lg = _log_decay(lane_h)
    qdec_ref[...] = jnp.exp(lg * (idx + 1.0))
    kdec_ref[...] = jnp.exp(lg * (c - 1.0 - idx))
    p = lax.broadcasted_iota(jnp.int32, (N_PAIRS, PAIR_LANES, PAIR_LANES), 0)
    r = lax.broadcasted_iota(jnp.int32, (N_PAIRS, PAIR_LANES, PAIR_LANES), 1) // RET_DK
    q = lax.broadcasted_iota(jnp.int32, (N_PAIRS, PAIR_LANES, PAIR_LANES), 2) // RET_DK
    hh = (2 * p + r).astype(F32)
    sdec_ref[...] = jnp.where(r == q, jnp.exp(_log_decay(hh) * float(c)), 0.0)


def _decay_tables(chunk):
    return pl.pallas_call(
        functools.partial(_decay_kernel, chunk),
        out_shape=(jax.ShapeDtypeStruct((N_PAIRS, chunk, 2 * chunk), F32),
                   jax.ShapeDtypeStruct((chunk, RET_W), F32),
                   jax.ShapeDtypeStruct((chunk, RET_W), F32),
                   jax.ShapeDtypeStruct((N_PAIRS, PAIR_LANES, PAIR_LANES), F32)),
        name="decay_tables",
    )()


ADALN_ROWS = 128


def _mod_kernel(c_ref, w_ref, b_ref, o_ref):
    k = pl.program_id(0)
    c = c_ref[...]
    part = _dot((c * jax.nn.sigmoid(c)).astype(BF16), w_ref[...].astype(BF16))

    @pl.when(k == 0)
    def _():
        o_ref[...] = part + b_ref[...]

    @pl.when(k > 0)
    def _():
        o_ref[...] += part


def _modulation(c, w_ada, b_ada):
    n, d = c.shape
    width = w_ada.shape[1]
    return pl.pallas_call(
        _mod_kernel,
        grid=(d // ADALN_ROWS,),
        in_specs=[pl.BlockSpec((n, ADALN_ROWS), lambda k: (0, k)),
                  pl.BlockSpec((ADALN_ROWS, width), lambda k: (k, 0)),
                  pl.BlockSpec((1, width), lambda k: (0, 0))],
        out_specs=pl.BlockSpec((n, width), lambda k: (0, 0)),
        out_shape=jax.ShapeDtypeStruct((n, width), F32),
        compiler_params=pltpu.CompilerParams(
            dimension_semantics=("arbitrary",), vmem_limit_bytes=VMEM_LIMIT),
        name="adaln_mod",
    )(c, w_ada, b_ada.reshape(1, width))


def _rmsnorm_mod(x, g, scale, shift):
    y = x * lax.rsqrt(jnp.mean(x * x, axis=-1, keepdims=True) + EPS)
    return (y * g) * (1.0 + scale) + shift


def _layernorm(x, g, b):
    mu = jnp.mean(x, axis=-1, keepdims=True)
    d = x - mu
    var = jnp.mean(d * d, axis=-1, keepdims=True)
    return d * lax.rsqrt(var + EPS) * g + b


def _silu(x):
    return x * jax.nn.sigmoid(x)


def _rotary_pair(x, cos, sin_signed, first_half):
    partner = jnp.where(first_half, pltpu.roll(x, PAIR_LANES - RET_DK // 2, 1),
                        pltpu.roll(x, RET_DK // 2, 1))
    return x * cos + partner * sin_signed


def _pair_groupnorm(o, lo, g, b):
    inv_n = 1.0 / RET_DV
    s_lo = jnp.sum(jnp.where(lo, o, 0.0), axis=-1, keepdims=True)
    s_hi = jnp.sum(jnp.where(lo, 0.0, o), axis=-1, keepdims=True)
    d = o - jnp.where(lo, s_lo, s_hi) * inv_n
    d2 = d * d
    v_lo = jnp.sum(jnp.where(lo, d2, 0.0), axis=-1, keepdims=True)
    v_hi = jnp.sum(jnp.where(lo, 0.0, d2), axis=-1, keepdims=True)
    var = jnp.where(lo, v_lo, v_hi) * inv_n
    return d * lax.rsqrt(var + EPS) * g + b


def _dot(a, b):
    return jnp.dot(a, b, preferred_element_type=F32)


def _dot_nt(a, b):
    return lax.dot_general(a, b, (((1,), (1,)), ((), ())), preferred_element_type=F32)


def _dot_tn(a, b):
    return lax.dot_general(a, b, (((0,), (0,)), ((), ())), preferred_element_type=F32)


def _stream_cast_weights(rings):
    def ring_chunks(pairs, stage):
        rows = stage.shape[1]
        return [(src, dst, r0) for src, dst in pairs for r0 in range(0, src.shape[0], rows)]

    chunks = [ring_chunks(pairs, stage) for pairs, stage, _ in rings]

    def copy(k, c):
        _, stage, sems = rings[k]
        src, _, r0 = chunks[k][c]
        slot = c % stage.shape[0]
        return pltpu.make_async_copy(src.at[pl.ds(r0, stage.shape[1]), :],
                                     stage.at[slot, :, pl.ds(0, src.shape[1])], sems.at[slot])

    for k, (_, stage, _) in enumerate(rings):
        for c in range(min(stage.shape[0] - 1, len(chunks[k]))):
            copy(k, c).start()
    for c in range(max(len(ch) for ch in chunks)):
        for k, (_, stage, _) in enumerate(rings):
            if c >= len(chunks[k]):
                continue
            n_slots, rows = stage.shape[0], stage.shape[1]
            if c + n_slots - 1 < len(chunks[k]):
                copy(k, c + n_slots - 1).start()
            copy(k, c).wait()
            src, dst, r0 = chunks[k][c]
            dst[r0:r0 + rows, :] = stage[c % n_slots, :, 0:src.shape[1]].astype(BF16)


def _layer_prompt_kernel(tile, nj, x_ref, mod_ref, gmix_ref, win_hbm, convw_ref, convb_ref,
                         clng_ref, clnb_ref, rlng_ref, rlnb_ref, wout_hbm,
                         cos_ref, sin_ref, dec_ref, qdec_ref, kdec_ref, sdec_ref,
                         mod2_ref, gffn_ref, w1_hbm, w2_hbm, gfin_ref,
                         y_ref, hist_ref, state_ref,
                         ubuf, ushift, sbuf, mixbuf, x1buf,
                         win_ref, wout_ref, w1_ref, w2_ref,
                         stage_wide, stage_tall, sems_wide, sems_tall):
    g = pl.program_id(0)
    total = pl.num_programs(0) - 1
    j = lax.rem(jnp.minimum(g, total - 1), nj)
    live = g < total

    @pl.when(g == 0)
    def _():
        _stream_cast_weights([
            ([(win_hbm, win_ref), (w1_hbm, w1_ref)], stage_wide, sems_wide),
            ([(wout_hbm, wout_ref), (w2_hbm, w2_ref)], stage_tall, sems_tall)])
        x1buf[...] = jnp.zeros((tile, D_MODEL), F32)

    @pl.when(jnp.logical_and(j == 0, live))
    def _():
        ubuf[0:HIST_ROWS, :] = jnp.zeros((HIST_ROWS, CONV_CH), F32)
        sbuf[...] = jnp.zeros(sbuf.shape, F32)

    x = x_ref[0]
    b_mix = lax.div(jnp.minimum(g, total - 1), nj)
    b_mlp = lax.div(jnp.maximum(g - 1, 0), nj)
    mods = lambda ref, b: [ref[pl.ds(b, 1), k * D_MODEL:(k + 1) * D_MODEL] for k in range(3)]
    shift, scale, gate = mods(mod_ref, b_mix)

    def conv_stages(u):
        ubuf[HIST_ROWS:HIST_ROWS + tile, :] = u
        off = HIST_ROWS - (CONV_K - 1)
        for r in range(SUBLANES):
            n = tile + SUBLANES * ((CONV_K - 1 - r) // SUBLANES)
            ushift[r, 0:n, :] = ubuf[off + r:off + r + n, :]
        ubuf[0:HIST_ROWS, :] = ubuf[tile:tile + HIST_ROWS, :]
        yield
        for rb in range(tile // CONV_ROW_BLOCK):
            r0 = rb * CONV_ROW_BLOCK
            dw = jnp.broadcast_to(convb_ref[...], (CONV_ROW_BLOCK, CONV_CH))
            for k in range(CONV_K):
                s0 = r0 + SUBLANES * (k // SUBLANES)
                wk = jnp.concatenate([convw_ref[k]] * (CONV_ROW_BLOCK // SUBLANES), axis=0)
                dw = dw + wk * ushift[k % SUBLANES, s0:s0 + CONV_ROW_BLOCK, :]
            conv_out = _silu(_layernorm(dw, clng_ref[...], clnb_ref[...]))
            mixbuf[r0:r0 + CONV_ROW_BLOCK, 0:CONV_CH] = conv_out.astype(BF16)
            yield

    def ret_stages(proj):
        ret_proj = []
        for k in range(4):
            ret_proj.append(proj(2 + k))
            yield
        q, kk, v, gt = ret_proj
        q = q * (RET_DK ** -0.5)
        lane = lax.broadcasted_iota(jnp.int32, (1, PAIR_LANES), 1)
        lo = lane < RET_DK
        first_half = (lane % RET_DK) < (RET_DK // 2)
        lo_b = lo.astype(BF16)
        hi_b = (~lo).astype(BF16)
        pairs = range(N_PAIRS)
        halves = (pairs[:N_PAIRS // 2], pairs[N_PAIRS // 2:])
        lanes = [slice(p * PAIR_LANES, (p + 1) * PAIR_LANES) for p in pairs]
        for c in range(tile // RET_CHUNK):
            rs = slice(c * RET_CHUNK, (c + 1) * RET_CHUNK)
            cos = cos_ref[rs, :]
            sin = sin_ref[rs, :]
            qc, kc, vc, scores, outs = {}, {}, {}, {}, {}
            for group in halves:
                for p in group:
                    qc[p] = _rotary_pair(q[rs, lanes[p]], cos, sin, first_half)
                    kc[p] = _rotary_pair(kk[rs, lanes[p]], cos, sin, first_half)
                    vc[p] = v[rs, lanes[p]].astype(BF16)
                    kb = kc[p].astype(BF16)
                    k_cat = jnp.concatenate([kb * lo_b, kb * hi_b], axis=0)
                    scores[p] = _dot_nt(qc[p].astype(BF16), k_cat)
                yield
            for group in halves:
                for p in group:
                    state = sbuf[p]
                    lhs = jnp.concatenate([(scores[p] * dec_ref[p]).astype(BF16),
                                           (qc[p] * qdec_ref[:, lanes[p]]).astype(BF16)], axis=1)
                    rhs = jnp.concatenate([vc[p] * lo_b, vc[p] * hi_b, state.astype(BF16)],
                                          axis=0)
                    outs[p] = _dot(lhs, rhs)
                    kd = (kc[p] * kdec_ref[:, lanes[p]]).astype(BF16)
                    sbuf[p] = (state * sdec_ref[p]
                               + jnp.where(sdec_ref[p] > 0.0, _dot_tn(kd, vc[p]), 0.0))
                yield
            for p in pairs:
                on = _pair_groupnorm(outs[p], lo, rlng_ref[:, lanes[p]], rlnb_ref[:, lanes[p]])
                mixbuf[rs, CONV_CH + p * PAIR_LANES:CONV_CH + (p + 1) * PAIR_LANES] = (
                    on * _silu(gt[rs, lanes[p]])).astype(BF16)
            yield

    ffn = _ffn_stages(x1buf[...], mods(mod2_ref, b_mlp), gffn_ref, w1_ref, w2_ref, gfin_ref, y_ref)
    next(ffn)
    hb = _rmsnorm_mod(x, gmix_ref[...], scale, shift).astype(BF16)

    def proj(k):
        return _dot(hb, win_ref[:, k * CONV_CH:(k + 1) * CONV_CH])

    streams = {"f": ffn, "c": conv_stages(proj(0) * jax.nn.sigmoid(proj(1))),
               "r": ret_stages(proj)}
    for name in LAYER_PROGRAM.replace(" ", ""):
        next(streams[name], None)
    for stream in (streams["c"], streams["r"], streams["f"]):
        for _ in stream:
            pass

    mix = _dot(mixbuf[...], wout_ref[...])
    x1buf[...] = x + gate * mix

    @pl.when(jnp.logical_and(j == nj - 1, live))
    def _():
        hist_ref[0] = ubuf[0:HIST_ROWS, :]
        state_ref[0] = sbuf[...]


def _layer_prompt(x, mod, mod_row0, g_mix, w_in, conv_w, conv_b, cln_g, cln_b, rln_g, rln_b,
                  w_out, g_ffn, w1, w2, g_final, rope, decay, tile):
    bsz, seq, d = x.shape
    assert bsz % SUBLANES == 0 and mod_row0 % bsz == 0
    in_hbm = pl.BlockSpec(memory_space=pl.ANY)

    def stage_shape(*ws):
        width = max(w.shape[1] for w in ws)
        rows = WEIGHT_STAGE_ELEMS // width
        assert all(w.shape[0] % rows == 0 for w in ws)
        return pltpu.VMEM((WEIGHT_STAGE_SLOTS, rows, width), F32)

    mod_blk = mod_row0 // bsz
    nj = seq // tile
    total = bsz * nj
    cos, sin = rope
    dec, qdec, kdec, sdec = decay
    mix_tile = lambda g: jnp.minimum(g, total - 1)
    ffn_tile = lambda g: jnp.maximum(g - 1, 0)
    resident = lambda shape: pl.BlockSpec(shape, lambda g: (0,) * len(shape),
                                          pipeline_mode=pl.Buffered(1))
    row = lambda a: a.reshape(1, -1)
    conv_w_rows = jnp.broadcast_to(conv_w[:, None, :], (CONV_K, SUBLANES, CONV_CH))
    rope_spec = pl.BlockSpec((tile, PAIR_LANES), lambda g: (mix_tile(g) % nj, 0))
    return pl.pallas_call(
        functools.partial(_layer_prompt_kernel, tile, nj),
        grid=(total + 1,),
        in_specs=[
            pl.BlockSpec((1, tile, d), lambda g: (mix_tile(g) // nj, mix_tile(g) % nj, 0)),
            pl.BlockSpec((bsz, 3 * d), lambda g: (mod_blk, 0), pipeline_mode=pl.Buffered(1)),
            resident((1, d)),
            in_hbm,
            resident((CONV_K, SUBLANES, CONV_CH)),
            resident((1, CONV_CH)),
            resident((1, CONV_CH)),
            resident((1, CONV_CH)),
            resident((1, RET_W)),
            resident((1, RET_W)),
            in_hbm,
            rope_spec,
            rope_spec,
            resident(dec.shape),
            resident(qdec.shape),
            resident(kdec.shape),
            resident(sdec.shape),
            pl.BlockSpec((bsz, 3 * d), lambda g: (mod_blk, 1), pipeline_mode=pl.Buffered(1)),
            resident((1, d)),
            in_hbm,
            in_hbm,
            resident((1, d)),
        ],
        out_specs=[
            pl.BlockSpec((1, tile, d), lambda g: (ffn_tile(g) // nj, ffn_tile(g) % nj, 0)),
            pl.BlockSpec((1, HIST_ROWS, CONV_CH), lambda g: (mix_tile(g) // nj, 0, 0)),
            pl.BlockSpec((1, N_PAIRS, PAIR_LANES, PAIR_LANES),
                         lambda g: (mix_tile(g) // nj, 0, 0, 0)),
        ],
        out_shape=(
            jax.ShapeDtypeStruct((bsz, seq, d), F32),
            jax.ShapeDtypeStruct((bsz, HIST_ROWS, CONV_CH), F32),
            jax.ShapeDtypeStruct((bsz, N_PAIRS, PAIR_LANES, PAIR_LANES), F32),
        ),
        scratch_shapes=[
            pltpu.VMEM((HIST_ROWS + tile, CONV_CH), F32),
            pltpu.VMEM((SUBLANES, HIST_ROWS + tile, CONV_CH), F32),
            pltpu.VMEM((N_PAIRS, PAIR_LANES, PAIR_LANES), F32),
            pltpu.VMEM((tile, D_MODEL), BF16),
            pltpu.VMEM((tile, D_MODEL), F32),
            pltpu.VMEM(w_in.shape, BF16),
            pltpu.VMEM(w_out.shape, BF16),
            pltpu.VMEM(w1.shape, BF16),
            pltpu.VMEM(w2.shape, BF16),
            stage_shape(w_in, w1),
            stage_shape(w_out, w2),
            pltpu.SemaphoreType.DMA((WEIGHT_STAGE_SLOTS,)),
            pltpu.SemaphoreType.DMA((WEIGHT_STAGE_SLOTS,)),
        ],
        compiler_params=pltpu.CompilerParams(
            dimension_semantics=("arbitrary",), vmem_limit_bytes=VMEM_LIMIT),
        name="layer_prompt",
    )(x, mod, row(g_mix), w_in, conv_w_rows, row(conv_b), row(cln_g), row(cln_b),
      row(rln_g), row(rln_b), w_out, cos, sin, dec, qdec, kdec, sdec,
      mod, row(g_ffn), w1, w2, row(g_final))


def _ffn_stages(x, mods, gffn_ref, w1_ref, w2_ref, gfin_ref, y_ref):
    shift, scale, gate = mods
    hb = _rmsnorm_mod(x, gffn_ref[...], scale, shift).astype(BF16)
    yield
    half = D_MODEL // 2
    ff = [None, None]
    for c in range(D_FF // (2 * FFN_CHUNK)):
        acts = []
        for h in range(2):
            c0 = (2 * c + h) * FFN_CHUNK
            a = jnp.maximum(_dot(hb, w1_ref[:, c0:c0 + FFN_CHUNK]), 0.0)
            acts.append((a * a).astype(BF16))
            yield
        act = jnp.concatenate(acts, axis=1)
        for n in range(2):
            part = _dot(act, w2_ref[2 * c * FFN_CHUNK:2 * (c + 1) * FFN_CHUNK,
                                    n * half:(n + 1) * half])
            ff[n] = part if ff[n] is None else ff[n] + part
            yield
    z = x + gate * jnp.concatenate(ff, axis=1)
    y_ref[0] = z * lax.rsqrt(jnp.mean(z * z, axis=-1, keepdims=True) + EPS) * gfin_ref[...]


def _pos_cols(x_ref, i):
    return x_ref[:, i * D_MODEL:(i + 1) * D_MODEL]


def _mod_cols(mod_ref, n_seq, k):
    return mod_ref[0:n_seq, k * D_MODEL:(k + 1) * D_MODEL]


def _mlp_sample_kernel(n_pos, n_seq, x_ref, conv_ref, ret_ref, wout_ref, mod_ref,
                       gffn_ref, w1_ref, w2_ref, gfin_ref, y_ref, x1buf, hbuf, acc):
    c = pl.program_id(0)
    rows = lambda i: slice(i * n_seq, (i + 1) * n_seq)

    @pl.when(c == 0)
    def _():
        mix = (_dot(conv_ref[...], wout_ref[0:CONV_CH, :].astype(BF16))
               + _dot(ret_ref[...], wout_ref[CONV_CH:CONV_CH + RET_W, :].astype(BF16)))
        for i in range(n_pos):
            x1 = _pos_cols(x_ref, i) + _mod_cols(mod_ref, n_seq, 2) * mix[rows(i)]
            x1buf[rows(i), :] = x1
            hbuf[rows(i), :] = _rmsnorm_mod(x1, gffn_ref[...], _mod_cols(mod_ref, n_seq, 4),
                                            _mod_cols(mod_ref, n_seq, 3)).astype(BF16)

    a = jnp.maximum(_dot(hbuf[...], w1_ref[...].astype(BF16)), 0.0)
    part = _dot((a * a).astype(BF16), w2_ref[...].astype(BF16))

    @pl.when(c == 0)
    def _():
        acc[...] = part

    @pl.when(c > 0)
    def _():
        acc[...] += part

    @pl.when(c == pl.num_programs(0) - 1)
    def _():
        for i in range(n_pos):
            z = x1buf[rows(i), :] + _mod_cols(mod_ref, n_seq, 5) * acc[rows(i), :]
            y_ref[:, i * D_MODEL:(i + 1) * D_MODEL] = (
                z * lax.rsqrt(jnp.mean(z * z, axis=-1, keepdims=True) + EPS) * gfin_ref[...])


def _mlp_sample(x, conv_out, ret_out, w_out, mod, g_ffn, w1, w2, g_final):
    n_seq, d = x.shape[0], D_MODEL
    n_pos = x.shape[1] // d
    n_tok = n_pos * n_seq
    n_chunks = D_FF // SAMPLE_FFN_CHUNK
    whole = lambda shape: pl.BlockSpec(shape, lambda c: (0,) * len(shape),
                                       pipeline_mode=pl.Buffered(1))
    return pl.pallas_call(
        functools.partial(_mlp_sample_kernel, n_pos, n_seq),
        grid=(n_chunks,),
        in_specs=[whole(x.shape), whole(conv_out.shape), whole(ret_out.shape),
                  whole(w_out.shape), whole(mod.shape), whole((1, d)),
                  pl.BlockSpec((d, SAMPLE_FFN_CHUNK), lambda c: (0, c)),
                  pl.BlockSpec((SAMPLE_FFN_CHUNK, d), lambda c: (c, 0)),
                  whole((1, d))],
        out_specs=pl.BlockSpec(x.shape, lambda c: (0, 0)),
        out_shape=jax.ShapeDtypeStruct(x.shape, F32),
        scratch_shapes=[pltpu.VMEM((n_tok, d), F32), pltpu.VMEM((n_tok, d), BF16),
                        pltpu.VMEM((n_tok, d), F32)],
        compiler_params=pltpu.CompilerParams(
            dimension_semantics=("arbitrary",), vmem_limit_bytes=VMEM_LIMIT),
        name="mlp_sample",
    )(x, conv_out, ret_out, w_out, mod, g_ffn.reshape(1, d), w1, w2, g_final.reshape(1, d))


def _power_table_kernel(o_ref):
    lane_h = (lax.broadcasted_iota(jnp.int32, o_ref.shape, 1) // RET_DK).astype(F32)
    n = lax.broadcasted_iota(jnp.int32, o_ref.shape, 0).astype(F32)
    o_ref[...] = jnp.exp(_log_decay(lane_h) * n)


def _power_table():
    return pl.pallas_call(_power_table_kernel,
                          out_shape=jax.ShapeDtypeStruct((8, RET_W), F32),
                          name="decay_powers")()


def _proj_sample_kernel(n_pos, n_seq, x_ref, mod_ref, gmix_ref, win_ref, cache_ref, convw_ref,
                        convb_ref, clng_ref, clnb_ref, cos_ref, sin_ref,
                        hist_ref, conv_ref, q_ref, k_ref, v_ref, g_ref, win_b):
    @pl.when(pl.program_id(0) == 0)
    def _():
        win_b[...] = win_ref[...].astype(BF16)

    shift, scale = _mod_cols(mod_ref, n_seq, 0), _mod_cols(mod_ref, n_seq, 1)
    hb = jnp.concatenate(
        [_rmsnorm_mod(_pos_cols(x_ref, i), gmix_ref[...], scale, shift).astype(BF16)
         for i in range(n_pos)], axis=0)

    def proj(k):
        return _dot(hb, win_b[:, k * CONV_CH:(k + 1) * CONV_CH])

    u = proj(0) * jax.nn.sigmoid(proj(1))
    n_hist = CONV_K - 1
    rows = lambda a, i: a[i * n_seq:(i + 1) * n_seq]
    for m in range(n_hist - n_pos):
        hist_ref[m] = cache_ref[m + n_pos]
    for i in range(n_pos):
        hist_ref[n_hist - n_pos + i] = rows(u, i)
        dw = jnp.broadcast_to(convb_ref[...], (n_seq, CONV_CH))
        for k in range(CONV_K):
            m = i + k
            src = cache_ref[m] if m < n_hist else rows(u, m - n_hist)
            dw = dw + convw_ref[k:k + 1, :] * src
        conv_ref[i] = _silu(_layernorm(dw, clng_ref[...], clnb_ref[...])).astype(BF16)

    lane = lax.broadcasted_iota(jnp.int32, (1, PAIR_LANES), 1)
    first_half = (lane % RET_DK) < (RET_DK // 2)
    q = proj(2) * (RET_DK ** -0.5)
    kk = proj(3)
    v = proj(4)
    gt = proj(5)
    for i in range(n_pos):
        cos = cos_ref[i:i + 1, :]
        sin = sin_ref[i:i + 1, :]
        for p in range(N_PAIRS):
            sl = slice(p * PAIR_LANES, (p + 1) * PAIR_LANES)
            q_ref[i, :, sl] = _rotary_pair(rows(q, i)[:, sl], cos, sin, first_half)
            k_ref[i, :, sl] = _rotary_pair(rows(kk, i)[:, sl], cos, sin, first_half)
        v_ref[i] = rows(v, i)
        g_ref[i] = rows(gt, i)


def _proj_sample(x, mod, g_mix, w_in, cache_t, conv_w, conv_b, cln_g, cln_b, rope):
    n_seq, d = x.shape[0], D_MODEL
    n_pos = x.shape[1] // d
    blk = SAMPLE_SEQ_BLOCK
    cos, sin = rope
    row = lambda a: a.reshape(1, -1)
    whole = lambda a: pl.BlockSpec(a.shape, lambda s: (0,) * a.ndim, pipeline_mode=pl.Buffered(1))
    seqs = lambda lead, width: pl.BlockSpec((lead, blk, width), lambda s: (0, s, 0))
    tok = lambda dt: jax.ShapeDtypeStruct((n_pos, n_seq, RET_W), dt)
    args = (x, mod, row(g_mix), w_in, cache_t, conv_w, row(conv_b), row(cln_g), row(cln_b),
            cos, sin)
    return pl.pallas_call(
        functools.partial(_proj_sample_kernel, n_pos, blk),
        grid=(n_seq // blk,),
        in_specs=[pl.BlockSpec((blk, n_pos * d), lambda s: (s, 0)),
                  pl.BlockSpec((blk, mod.shape[1]), lambda s: (s, 0)),
                  whole(args[2]), whole(w_in), seqs(cache_t.shape[0], CONV_CH)]
        + [whole(a) for a in args[5:]],
        out_specs=[seqs(cache_t.shape[0], CONV_CH)] + [seqs(n_pos, RET_W)] * 5,
        out_shape=(jax.ShapeDtypeStruct(cache_t.shape, F32), tok(BF16),
                   tok(F32), tok(F32), tok(F32), tok(F32)),
        scratch_shapes=[pltpu.VMEM(w_in.shape, BF16)],
        compiler_params=pltpu.CompilerParams(
            dimension_semantics=("arbitrary",), vmem_limit_bytes=VMEM_LIMIT),
        name="proj_sample",
    )(*args)


def _ret_sample_kernel(n_pos, n_seq, q_ref, k_ref, v_ref, g_ref, st_ref, pw_ref, rlng_ref,
                       rlnb_ref, out_ref, snew_ref, sb_buf, qd_buf, kd_buf, v_buf):
    lane = lax.broadcasted_iota(jnp.int32, (1, PAIR_LANES), 1)
    lo = lane < RET_DK
    rows = lambda a, i: a[i * n_seq:(i + 1) * n_seq]
    cols = lambda a, i: a[:, i * n_seq:(i + 1) * n_seq]
    pw = pw_ref[...]
    q = q_ref[...]
    k = k_ref[...]
    qb = [rows(q, i).astype(BF16) for i in range(n_pos)]
    kb = [rows(k, i).astype(BF16) for i in range(n_pos)]
    vb = [rows(v_ref[...], i).astype(BF16) for i in range(n_pos)]

    o = []
    for i in range(n_pos):
        acc = jnp.zeros((n_seq, PAIR_LANES), F32)
        for j in range(i + 1):
            prod = qb[i].astype(F32) * kb[j].astype(F32)
            s_lo = jnp.sum(jnp.where(lo, prod, 0.0), axis=-1, keepdims=True)
            s_hi = jnp.sum(jnp.where(lo, 0.0, prod), axis=-1, keepdims=True)
            score = jnp.where(lo, s_lo, s_hi) * pw[i - j:i - j + 1, :]
            acc = acc + score.astype(BF16).astype(F32) * vb[j].astype(F32)
        o.append(acc)

    rounded = lambda x: x.astype(BF16).astype(F32)
    qd_buf[...] = jnp.concatenate(
        [rounded(rows(q, i) * pw[i + 1:i + 2, :]) for i in range(n_pos)], axis=0).T
    kd_buf[...] = jnp.concatenate(
        [rounded(rows(k, j) * pw[n_pos - 1 - j:n_pos - j, :]) for j in range(n_pos)], axis=0).T
    v_buf[...] = jnp.concatenate([vb[j].astype(F32) for j in range(n_pos)], axis=0).T
    tile_at = lambda r: pl.ds(pl.multiple_of(r * RET_DV, RET_DV), RET_DV)
    pos = lambda i: slice(i * n_seq, (i + 1) * n_seq)

    for a in range(2):
        gam = pw[n_pos:n_pos + 1, a * RET_DK:a * RET_DK + 1]
        v_rows = slice(a * RET_DV, (a + 1) * RET_DV)

        def update(grp, carry, a=a, gam=gam, v_rows=v_rows):
            r0 = pl.multiple_of(a * RET_DK + grp * SUBLANES, SUBLANES)
            kd_rows = kd_buf[pl.ds(r0, SUBLANES), :]
            for s in range(SUBLANES):
                r = r0 + s
                s_old = st_ref[tile_at(r), :]
                sb_buf[tile_at(r), :] = rounded(s_old)
                s_new = s_old * gam
                for j in range(n_pos):
                    s_new = s_new + kd_rows[s:s + 1, pos(j)] * v_buf[v_rows, pos(j)]
                snew_ref[tile_at(r), :] = s_new
            return carry

        lax.fori_loop(0, RET_DK // SUBLANES, update, 0)

    for i in range(n_pos):
        heads = []
        for a in range(2):
            def cross(grp, acc, a=a, i=i):
                r0 = pl.multiple_of(a * RET_DK + grp * SUBLANES, SUBLANES)
                qd_rows = qd_buf[pl.ds(r0, SUBLANES), pos(i)]
                for s in range(SUBLANES):
                    acc = acc + qd_rows[s:s + 1, :] * sb_buf[tile_at(r0 + s), :]
                return acc

            heads.append(lax.fori_loop(0, RET_DK // SUBLANES, cross,
                                       jnp.zeros((RET_DV, n_seq), F32)))
        oi = o[i] + jnp.concatenate(heads, axis=0).T
        on = _pair_groupnorm(oi, lo, rlng_ref[...], rlnb_ref[...])
        out_ref[i * n_seq:(i + 1) * n_seq, :] = (on * _silu(rows(g_ref[...], i))).astype(BF16)


def _ret_sample(q, k, v, g, state_t, pw, rln_g, rln_b, n_pos, n_seq):
    n_tok = n_pos * n_seq
    pair_state = 2 * RET_DK * RET_DV
    slab = pl.BlockSpec((n_tok, PAIR_LANES), lambda p: (0, p))
    state_spec = pl.BlockSpec((pair_state, n_seq), lambda p: (p, 0))
    return pl.pallas_call(
        functools.partial(_ret_sample_kernel, n_pos, n_seq),
        grid=(N_PAIRS,),
        in_specs=[slab, slab, slab, slab, state_spec,
                  pl.BlockSpec((8, PAIR_LANES), lambda p: (0, p)),
                  pl.BlockSpec((1, PAIR_LANES), lambda p: (0, p)),
                  pl.BlockSpec((1, PAIR_LANES), lambda p: (0, p))],
        out_specs=[slab, state_spec],
        out_shape=(jax.ShapeDtypeStruct((n_tok, RET_W), BF16),
                   jax.ShapeDtypeStruct(state_t.shape, F32)),
        scratch_shapes=[pltpu.VMEM((pair_state, n_seq), F32)]
        + [pltpu.VMEM((PAIR_LANES, n_tok), F32)] * 3,
        compiler_params=pltpu.CompilerParams(
            dimension_semantics=("arbitrary",), vmem_limit_bytes=VMEM_LIMIT),
        name="ret_sample",
    )(q, k, v, g, state_t, pw, rln_g.reshape(1, RET_W), rln_b.reshape(1, RET_W))


def _unpack_state(s2):
    blocks = []
    for p in range(N_PAIRS):
        for a in range(2):
            sl = slice(a * RET_DK, (a + 1) * RET_DK)
            blocks.append(s2[:, p, sl, sl])
    return jnp.stack(blocks, axis=1)


def _prompt_path(x, mod, mod_row0, g_mix, w_in, conv_w, conv_b, cln_g, cln_b, rln_g, rln_b,
                 w_out, g_ffn, w_ff1, w_ff2, g_final, tile=PROMPT_TILE):
    seq = x.shape[1]
    rope = _rope_tables(seq, 0)
    decay = _decay_tables(RET_CHUNK)
    y, hist, s2 = _layer_prompt(x, mod, mod_row0, g_mix, w_in, conv_w, conv_b, cln_g, cln_b,
                                rln_g, rln_b, w_out, g_ffn, w_ff1, w_ff2, g_final, rope, decay,
                                tile)
    return y, hist[:, HIST_ROWS - (CONV_K - 1):], _unpack_state(s2)


def _sample_path(x, mod, cache, state, pos0, g_mix, w_in, conv_w, conv_b, cln_g, cln_b,
                 rln_g, rln_b, w_out, g_ffn, w_ff1, w_ff2, g_final):
    n_seq, n_pos, d = x.shape
    xs = x.reshape(n_seq, n_pos * d)
    rope = _rope_tables(8, pos0)
    hist_t, *tok = _proj_sample(xs, mod, g_mix, w_in, cache.transpose(1, 0, 2),
                                conv_w, conv_b, cln_g, cln_b, rope)
    conv_out, q, k, v, g = [a.reshape(n_pos * n_seq, -1) for a in tok]
    ret_out, s_new_t = _ret_sample(q, k, v, g, state.reshape(n_seq, -1).T, _power_table(),
                                   rln_g, rln_b, n_pos, n_seq)
    s_new = s_new_t.T
    y = _mlp_sample(xs, conv_out, ret_out, w_out, mod, g_ffn, w_ff1, w_ff2, g_final)
    return y.reshape(x.shape), hist_t.transpose(1, 0, 2), s_new.reshape(state.shape)


def kernel(x_prompt, x_sample, cache_conv, state_ret, c_prompt, c_sample, w_ada, b_ada, g_mix, w_in, conv_w, conv_b, conv_ln_g, conv_ln_b, ret_ln_g, ret_ln_b, w_out, g_ffn, w_ff1, w_ff2, g_final):
    mod = _modulation(jnp.concatenate([c_sample, c_prompt], axis=0), w_ada[0], b_ada[0])
    params = (g_mix[0], w_in[0], conv_w[0], conv_b[0], conv_ln_g[0], conv_ln_b[0], ret_ln_g[0],
              ret_ln_b[0], w_out[0], g_ffn[0], w_ff1[0], w_ff2[0], g_final)
    y_p, conv_p, ret_p = _prompt_path(x_prompt, mod, x_sample.shape[0], *params)
    y_s, conv_s, ret_s = _sample_path(x_sample, mod, cache_conv[0], state_ret[0], PAST_LEN,
                                      *params)
    return (y_p, y_s, conv_p[None], ret_p[None], conv_s[None], ret_s[None])
```

```python
import functools

import jax
import jax.numpy as jnp
from jax import lax
from jax.experimental import pallas as pl
from jax.experimental.pallas import tpu as pltpu

D_MODEL = 1024
CONV_CH = 512
CONV_K = 31
RET_HEADS = 8
RET_W = 512
RET_DK = 64
RET_DV = 64
D_FF = 4 * D_MODEL
RET_CHUNK = 128
ROPE_THETA = 10000.0
EPS = 1e-6
N_MOD = 6
PAST_LEN = 16384

SUBLANES = 8
HIST_ROWS = 32
CONV_ROW_BLOCK = 32
PROMPT_TILE = 256
FFN_CHUNK = 512
SAMPLE_FFN_CHUNK = 1024
SAMPLE_SEQ_BLOCK = 32
WEIGHT_STAGE_ELEMS = 512 * 1024
WEIGHT_STAGE_SLOTS = 3
LAYER_PROGRAM = ("c cf cf cf cf cf cf cf cf "
                 "rrrr rfrf rfrf r "
                 "rfrf rfrf r")
PAIR_LANES = 2 * RET_DK
N_PAIRS = RET_HEADS // 2
VMEM_LIMIT = 56 * 1024 * 1024

F32 = jnp.float32
BF16 = jnp.bfloat16


def _log_decay(head):
    return jnp.log1p(-jnp.exp2(-5.0 - head))


def _rope_kernel(pos0, cos_ref, sin_ref):
    rows = cos_ref.shape[0]
    lane = lax.broadcasted_iota(jnp.int32, (rows, PAIR_LANES), 1)
    row = lax.broadcasted_iota(jnp.int32, (rows, PAIR_LANES), 0)
    half = RET_DK // 2
    freq = (lane % half).astype(F32)
    inv = jnp.power(jnp.full_like(freq, ROPE_THETA), -freq / half)
    ang = (row + pos0).astype(F32) * inv
    first = (lane % RET_DK) < half
    cos_ref[...] = jnp.cos(ang)
    sin_ref[...] = jnp.where(first, -jnp.sin(ang), jnp.sin(ang))


ROPE_BLOCK = 16


def _rope_blocked_kernel(pos0, cos_ref, sin_ref):
    rows = cos_ref.shape[0]
    n_blk = rows // ROPE_BLOCK
    half = RET_DK // 2

    def angles(n, step, start):
        lane = lax.broadcasted_iota(jnp.int32, (n, PAIR_LANES), 1)
        row = lax.broadcasted_iota(jnp.int32, (n, PAIR_LANES), 0)
        freq = (lane % half).astype(F32)
        inv = jnp.power(jnp.full_like(freq, ROPE_THETA), -freq / half)
        return (row * step + start).astype(F32) * inv

    a = angles(n_blk, ROPE_BLOCK, pos0)
    b = angles(ROPE_BLOCK, 1, 0)
    lane = lax.broadcasted_iota(jnp.int32, (ROPE_BLOCK, PAIR_LANES), 1)
    sign = jnp.where((lane % RET_DK) < half, -1.0, 1.0)
    cos_a, sin_a = jnp.cos(a), jnp.sin(a)
    cos_b, sin_b = jnp.cos(b), jnp.sin(b)
    cos_bs, sin_bs = cos_b * sign, sin_b * sign
    for blk in range(n_blk):
        ca, sa = cos_a[blk:blk + 1, :], sin_a[blk:blk + 1, :]
        rs = slice(blk * ROPE_BLOCK, (blk + 1) * ROPE_BLOCK)
        cos_ref[rs, :] = ca * cos_b - sa * sin_b
        sin_ref[rs, :] = sa * cos_bs + ca * sin_bs


def _rope_tables(rows, pos0):
    blocked = rows % ROPE_BLOCK == 0 and rows // ROPE_BLOCK >= SUBLANES
    return pl.pallas_call(
        functools.partial(_rope_blocked_kernel if blocked else _rope_kernel, pos0),
        out_shape=(jax.ShapeDtypeStruct((rows, PAIR_LANES), F32),
                   jax.ShapeDtypeStruct((rows, PAIR_LANES), F32)),
        name="rope_tables",
    )()


def _decay_kernel(chunk, dec_ref, qdec_ref, kdec_ref, sdec_ref):
    c = chunk
    shape = (N_PAIRS, c, 2 * c)
    col = lax.broadcasted_iota(jnp.int32, shape, 2)
    h = (2 * lax.broadcasted_iota(jnp.int32, shape, 0) + col // c).astype(F32)
    i = lax.broadcasted_iota(jnp.int32, shape, 1).astype(F32)
    j = (col % c).astype(F32)
    diff = i - j
    dec_ref[...] = jnp.where(diff >= 0, jnp.exp(_log_decay(h) * jnp.maximum(diff, 0.0)), 0.0)
    lane_h = (lax.broadcasted_iota(jnp.int32, (c, RET_W), 1) // RET_DK).astype(F32)
    idx = lax.broadcasted_iota(jnp.int32, (c, RET_W), 0).astype(F32)
    lg = _log_decay(lane_h)
    qdec_ref[...] = jnp.exp(lg * (idx + 1.0))
    kdec_ref[...] = jnp.exp(lg * (c - 1.0 - idx))
    p = lax.broadcasted_iota(jnp.int32, (N_PAIRS, PAIR_LANES, PAIR_LANES), 0)
    r = lax.broadcasted_iota(jnp.int32, (N_PAIRS, PAIR_LANES, PAIR_LANES), 1) // RET_DK
    q = lax.broadcasted_iota(jnp.int32, (N_PAIRS, PAIR_LANES, PAIR_LANES), 2) // RET_DK
    hh = (2 * p + r).astype(F32)
    sdec_ref[...] = jnp.where(r == q, jnp.exp(_log_decay(hh) * float(c)), 0.0)


def _decay_tables(chunk):
    return pl.pallas_call(
        functools.partial(_decay_kernel, chunk),
        out_shape=(jax.ShapeDtypeStruct((N_PAIRS, chunk, 2 * chunk), F32),
                   jax.ShapeDtypeStruct((chunk, RET_W), F32),
                   jax.ShapeDtypeStruct((chunk, RET_W), F32),
                   jax.ShapeDtypeStruct((N_PAIRS, PAIR_LANES, PAIR_LANES), F32)),
        name="decay_tables",
    )()


def _mod_kernel(c_ref, w_lo_ref, w_hi_ref, b_ref, o_ref):
    c = c_ref[...]
    s = (c * jax.nn.sigmoid(c)).astype(BF16)
    half = w_lo_ref.shape[1]
    for k, w_ref in enumerate((w_lo_ref, w_hi_ref)):
        cs = slice(k * half, (k + 1) * half)
        o_ref[:, cs] = _dot(s, w_ref[...].astype(BF16)) + b_ref[:, cs]


def _modulation(c, w_ada, b_ada):
    n, d = c.shape
    width = w_ada.shape[1]
    blk = D_MODEL
    half = lambda k: pl.BlockSpec((d, blk // 2), lambda i: (0, 2 * i + k))
    return pl.pallas_call(
        _mod_kernel,
        grid=(width // blk,),
        in_specs=[pl.BlockSpec((n, d), lambda i: (0, 0)), half(0), half(1),
                  pl.BlockSpec((1, blk), lambda i: (0, i))],
        out_specs=pl.BlockSpec((n, blk), lambda i: (0, i)),
        out_shape=jax.ShapeDtypeStruct((n, width), F32),
        compiler_params=pltpu.CompilerParams(vmem_limit_bytes=VMEM_LIMIT),
        name="adaln_mod",
    )(c, w_ada, w_ada, b_ada.reshape(1, width))


def _rmsnorm_mod(x, g, scale, shift):
    y = x * lax.rsqrt(jnp.mean(x * x, axis=-1, keepdims=True) + EPS)
    return (y * g) * (1.0 + scale) + shift


def _layernorm(x, g, b):
    mu = jnp.mean(x, axis=-1, keepdims=True)
    d = x - mu
    var = jnp.mean(d * d, axis=-1, keepdims=True)
    return d * lax.rsqrt(var + EPS) * g + b


def _silu(x):
    return x * jax.nn.sigmoid(x)


def _rotary_pair(x, cos, sin_signed, first_half):
    partner = jnp.where(first_half, pltpu.roll(x, PAIR_LANES - RET_DK // 2, 1),
                        pltpu.roll(x, RET_DK // 2, 1))
    return x * cos + partner * sin_signed


def _pair_groupnorm(o, lo, g, b):
    inv_n = 1.0 / RET_DV
    s_lo = jnp.sum(jnp.where(lo, o, 0.0), axis=-1, keepdims=True)
    s_hi = jnp.sum(jnp.where(lo, 0.0, o), axis=-1, keepdims=True)
    d = o - jnp.where(lo, s_lo, s_hi) * inv_n
    d2 = d * d
    v_lo = jnp.sum(jnp.where(lo, d2, 0.0), axis=-1, keepdims=True)
    v_hi = jnp.sum(jnp.where(lo, 0.0, d2), axis=-1, keepdims=True)
    var = jnp.where(lo, v_lo, v_hi) * inv_n
    return d * lax.rsqrt(var + EPS) * g + b


def _dot(a, b):
    return jnp.dot(a, b, preferred_element_type=F32)


def _dot_nt(a, b):
    return lax.dot_general(a, b, (((1,), (1,)), ((), ())), preferred_element_type=F32)


def _dot_tn(a, b):
    return lax.dot_general(a, b, (((0,), (0,)), ((), ())), preferred_element_type=F32)


def _stream_cast_weights(rings):
    def ring_chunks(pairs, stage):
        rows = stage.shape[1]
        return [(src, dst, r0) for src, dst in pairs for r0 in range(0, src.shape[0], rows)]

    chunks = [ring_chunks(pairs, stage) for pairs, stage, _ in rings]

    def copy(k, c):
        _, stage, sems = rings[k]
        src, _, r0 = chunks[k][c]
        slot = c % stage.shape[0]
        return pltpu.make_async_copy(src.at[pl.ds(r0, stage.shape[1]), :],
                                     stage.at[slot, :, pl.ds(0, src.shape[1])], sems.at[slot])

    for k, (_, stage, _) in enumerate(rings):
        for c in range(min(stage.shape[0] - 1, len(chunks[k]))):
            copy(k, c).start()
    for c in range(max(len(ch) for ch in chunks)):
        for k, (_, stage, _) in enumerate(rings):
            if c >= len(chunks[k]):
                continue
            n_slots, rows = stage.shape[0], stage.shape[1]
            if c + n_slots - 1 < len(chunks[k]):
                copy(k, c + n_slots - 1).start()
            copy(k, c).wait()
            src, dst, r0 = chunks[k][c]
            dst[r0:r0 + rows, :] = stage[c % n_slots, :, 0:src.shape[1]].astype(BF16)


def _layer_prompt_kernel(tile, nj, x_ref, mod_ref, gmix_ref, win_hbm, convw_ref, convb_ref,
                         clng_ref, clnb_ref, rlng_ref, rlnb_ref, wout_hbm,
                         cos_ref, sin_ref, dec_ref, qdec_ref, kdec_ref, sdec_ref,
                         mod2_ref, gffn_ref, w1_hbm, w2_hbm, gfin_ref,
                         y_ref, hist_ref, state_ref, win_out, wout_out, w1_out, w2_out,
                         ubuf, ushift, sbuf, mixbuf, x1buf,
                         win_ref, wout_ref, w1_ref, w2_ref,
                         stage_wide, stage_tall, sems_wide, sems_tall, sems_out):
    g = pl.program_id(0)
    total = pl.num_programs(0) - 1
    j = lax.rem(jnp.minimum(g, total - 1), nj)
    live = g < total
    export = [pltpu.make_async_copy(src, dst, sems_out.at[k]) for k, (src, dst) in enumerate(
        [(win_ref, win_out), (wout_ref, wout_out), (w1_ref, w1_out), (w2_ref, w2_out)])]

    @pl.when(g == 0)
    def _():
        _stream_cast_weights([
            ([(win_hbm, win_ref), (w1_hbm, w1_ref)], stage_wide, sems_wide),
            ([(wout_hbm, wout_ref), (w2_hbm, w2_ref)], stage_tall, sems_tall)])
        for cp in export:
            cp.start()
        x1buf[...] = jnp.zeros((tile, D_MODEL), F32)

    @pl.when(jnp.logical_and(j == 0, live))
    def _():
        ubuf[0:HIST_ROWS, :] = jnp.zeros((HIST_ROWS, CONV_CH), F32)
        sbuf[...] = jnp.zeros(sbuf.shape, F32)

    x = x_ref[0]
    b_mix = lax.div(jnp.minimum(g, total - 1), nj)
    b_mlp = lax.div(jnp.maximum(g - 1, 0), nj)
    mods = lambda ref, b: [ref[pl.ds(b, 1), k * D_MODEL:(k + 1) * D_MODEL] for k in range(3)]
    shift, scale, gate = mods(mod_ref, b_mix)

    def conv_stages(u):
        ubuf[HIST_ROWS:HIST_ROWS + tile, :] = u
        off = HIST_ROWS - (CONV_K - 1)
        for r in range(SUBLANES):
            n = tile + SUBLANES * ((CONV_K - 1 - r) // SUBLANES)
            ushift[r, 0:n, :] = ubuf[off + r:off + r + n, :]
        ubuf[0:HIST_ROWS, :] = ubuf[tile:tile + HIST_ROWS, :]
        yield
        for rb in range(tile // CONV_ROW_BLOCK):
            r0 = rb * CONV_ROW_BLOCK
            dw = jnp.broadcast_to(convb_ref[...], (CONV_ROW_BLOCK, CONV_CH))
            for k in range(CONV_K):
                s0 = r0 + SUBLANES * (k // SUBLANES)
                wk = jnp.concatenate([convw_ref[k]] * (CONV_ROW_BLOCK // SUBLANES), axis=0)
                dw = dw + wk * ushift[k % SUBLANES, s0:s0 + CONV_ROW_BLOCK, :]
            conv_out = _silu(_layernorm(dw, clng_ref[...], clnb_ref[...]))
            mixbuf[r0:r0 + CONV_ROW_BLOCK, 0:CONV_CH] = conv_out.astype(BF16)
            yield

    def ret_stages(proj):
        ret_proj = []
        for k in range(4):
            ret_proj.append(proj(2 + k))
            yield
        q, kk, v, gt = ret_proj
        q = q * (RET_DK ** -0.5)
        lane = lax.broadcasted_iota(jnp.int32, (1, PAIR_LANES), 1)
        lo = lane < RET_DK
        first_half = (lane % RET_DK) < (RET_DK // 2)
        lo_b = lo.astype(BF16)
        hi_b = (~lo).astype(BF16)
        pairs = range(N_PAIRS)
        halves = (pairs[:N_PAIRS // 2], pairs[N_PAIRS // 2:])
        lanes = [slice(p * PAIR_LANES, (p + 1) * PAIR_LANES) for p in pairs]
        for c in range(tile // RET_CHUNK):
            rs = slice(c * RET_CHUNK, (c + 1) * RET_CHUNK)
            cos = cos_ref[rs, :]
            sin = sin_ref[rs, :]
            qc, kc, vc, scores, outs = {}, {}, {}, {}, {}
            for group in halves:
                for p in group:
                    qc[p] = _rotary_pair(q[rs, lanes[p]], cos, sin, first_half)
                    kc[p] = _rotary_pair(kk[rs, lanes[p]], cos, sin, first_half)
                    vc[p] = v[rs, lanes[p]].astype(BF16)
                    kb = kc[p].astype(BF16)
                    k_cat = jnp.concatenate([kb * lo_b, kb * hi_b], axis=0)
                    scores[p] = _dot_nt(qc[p].astype(BF16), k_cat)
                yield
            for group in halves:
                for p in group:
                    state = sbuf[p]
                    lhs = jnp.concatenate([(scores[p] * dec_ref[p]).astype(BF16),
                                           (qc[p] * qdec_ref[:, lanes[p]]).astype(BF16)], axis=1)
                    rhs = jnp.concatenate([vc[p] * lo_b, vc[p] * hi_b, state.astype(BF16)],
                                          axis=0)
                    outs[p] = _dot(lhs, rhs)
                    kd = (kc[p] * kdec_ref[:, lanes[p]]).astype(BF16)
                    sbuf[p] = (state * sdec_ref[p]
                               + jnp.where(sdec_ref[p] > 0.0, _dot_tn(kd, vc[p]), 0.0))
                yield
            for p in pairs:
                on = _pair_groupnorm(outs[p], lo, rlng_ref[:, lanes[p]], rlnb_ref[:, lanes[p]])
                mixbuf[rs, CONV_CH + p * PAIR_LANES:CONV_CH + (p + 1) * PAIR_LANES] = (
                    on * _silu(gt[rs, lanes[p]])).astype(BF16)
            yield

    ffn = _ffn_stages(x1buf[...], mods(mod2_ref, b_mlp), gffn_ref, w1_ref, w2_ref, gfin_ref, y_ref)
    next(ffn)
    hb = _rmsnorm_mod(x, gmix_ref[...], scale, shift).astype(BF16)

    def proj(k):
        return _dot(hb, win_ref[:, k * CONV_CH:(k + 1) * CONV_CH])

    streams = {"f": ffn, "c": conv_stages(proj(0) * jax.nn.sigmoid(proj(1))),
               "r": ret_stages(proj)}
    for name in LAYER_PROGRAM.replace(" ", ""):
        next(streams[name], None)
    for stream in (streams["c"], streams["r"], streams["f"]):
        for _ in stream:
            pass

    mix = _dot(mixbuf[...], wout_ref[...])
    x1buf[...] = x + gate * mix

    @pl.when(jnp.logical_and(j == nj - 1, live))
    def _():
        hist_ref[0] = ubuf[0:HIST_ROWS, :]
        state_ref[0] = sbuf[...]

    @pl.when(g == total)
    def _():
        for cp in export:
            cp.wait()


def _layer_prompt(x, mod, mod_row0, g_mix, w_in, conv_w, conv_b, cln_g, cln_b, rln_g, rln_b,
                  w_out, g_ffn, w1, w2, g_final, rope, decay, tile):
    bsz, seq, d = x.shape
    assert bsz % SUBLANES == 0 and mod_row0 % bsz == 0
    in_hbm = pl.BlockSpec(memory_space=pl.ANY)

    def stage_shape(*ws):
        width = max(w.shape[1] for w in ws)
        rows = WEIGHT_STAGE_ELEMS // width
        assert all(w.shape[0] % rows == 0 for w in ws)
        return pltpu.VMEM((WEIGHT_STAGE_SLOTS, rows, width), F32)

    mod_blk = mod_row0 // bsz
    nj = seq // tile
    total = bsz * nj
    cos, sin = rope
    dec, qdec, kdec, sdec = decay
    mix_tile = lambda g: jnp.minimum(g, total - 1)
    ffn_tile = lambda g: jnp.maximum(g - 1, 0)
    resident = lambda shape: pl.BlockSpec(shape, lambda g: (0,) * len(shape),
                                          pipeline_mode=pl.Buffered(1))
    row = lambda a: a.reshape(1, -1)
    conv_w_rows = jnp.broadcast_to(conv_w[:, None, :], (CONV_K, SUBLANES, CONV_CH))
    rope_spec = pl.BlockSpec((tile, PAIR_LANES), lambda g: (mix_tile(g) % nj, 0))
    return pl.pallas_call(
        functools.partial(_layer_prompt_kernel, tile, nj),
        grid=(total + 1,),
        in_specs=[
            pl.BlockSpec((1, tile, d), lambda g: (mix_tile(g) // nj, mix_tile(g) % nj, 0)),
            pl.BlockSpec((bsz, 3 * d), lambda g: (mod_blk, 0), pipeline_mode=pl.Buffered(1)),
            resident((1, d)),
            in_hbm,
            resident((CONV_K, SUBLANES, CONV_CH)),
            resident((1, CONV_CH)),
            resident((1, CONV_CH)),
            resident((1, CONV_CH)),
            resident((1, RET_W)),
            resident((1, RET_W)),
            in_hbm,
            rope_spec,
            rope_spec,
            resident(dec.shape),
            resident(qdec.shape),
            resident(kdec.shape),
            resident(sdec.shape),
            pl.BlockSpec((bsz, 3 * d), lambda g: (mod_blk, 1), pipeline_mode=pl.Buffered(1)),
            resident((1, d)),
            in_hbm,
            in_hbm,
            resident((1, d)),
        ],
        out_specs=[
            pl.BlockSpec((1, tile, d), lambda g: (ffn_tile(g) // nj, ffn_tile(g) % nj, 0)),
            pl.BlockSpec((1, HIST_ROWS, CONV_CH), lambda g: (mix_tile(g) // nj, 0, 0)),
            pl.BlockSpec((1, N_PAIRS, PAIR_LANES, PAIR_LANES),
                         lambda g: (mix_tile(g) // nj, 0, 0, 0)),
            in_hbm, in_hbm, in_hbm, in_hbm,
        ],
        out_shape=(
            jax.ShapeDtypeStruct((bsz, seq, d), F32),
            jax.ShapeDtypeStruct((bsz, HIST_ROWS, CONV_CH), F32),
            jax.ShapeDtypeStruct((bsz, N_PAIRS, PAIR_LANES, PAIR_LANES), F32),
        ) + tuple(jax.ShapeDtypeStruct(w.shape, BF16) for w in (w_in, w_out, w1, w2)),
        scratch_shapes=[
            pltpu.VMEM((HIST_ROWS + tile, CONV_CH), F32),
            pltpu.VMEM((SUBLANES, HIST_ROWS + tile, CONV_CH), F32),
            pltpu.VMEM((N_PAIRS, PAIR_LANES, PAIR_LANES), F32),
            pltpu.VMEM((tile, D_MODEL), BF16),
            pltpu.VMEM((tile, D_MODEL), F32),
            pltpu.VMEM(w_in.shape, BF16),
            pltpu.VMEM(w_out.shape, BF16),
            pltpu.VMEM(w1.shape, BF16),
            pltpu.VMEM(w2.shape, BF16),
            stage_shape(w_in, w1),
            stage_shape(w_out, w2),
            pltpu.SemaphoreType.DMA((WEIGHT_STAGE_SLOTS,)),
            pltpu.SemaphoreType.DMA((WEIGHT_STAGE_SLOTS,)),
            pltpu.SemaphoreType.DMA((4,)),
        ],
        compiler_params=pltpu.CompilerParams(
            dimension_semantics=("arbitrary",), vmem_limit_bytes=VMEM_LIMIT),
        name="layer_prompt",
    )(x, mod, row(g_mix), w_in, conv_w_rows, row(conv_b), row(cln_g), row(cln_b),
      row(rln_g), row(rln_b), w_out, cos, sin, dec, qdec, kdec, sdec,
      mod, row(g_ffn), w1, w2, row(g_final))


def _ffn_stages(x, mods, gffn_ref, w1_ref, w2_ref, gfin_ref, y_ref):
    shift, scale, gate = mods
    hb = _rmsnorm_mod(x, gffn_ref[...], scale, shift).astype(BF16)
    yield
    half = D_MODEL // 2
    ff = [None, None]
    for c in range(D_FF // (2 * FFN_CHUNK)):
        acts = []
        for h in range(2):
            c0 = (2 * c + h) * FFN_CHUNK
            a = jnp.maximum(_dot(hb, w1_ref[:, c0:c0 + FFN_CHUNK]), 0.0)
            acts.append((a * a).astype(BF16))
            yield
        act = jnp.concatenate(acts, axis=1)
        for n in range(2):
            part = _dot(act, w2_ref[2 * c * FFN_CHUNK:2 * (c + 1) * FFN_CHUNK,
                                    n * half:(n + 1) * half])
            ff[n] = part if ff[n] is None else ff[n] + part
            yield
    z = x + gate * jnp.concatenate(ff, axis=1)
    y_ref[0] = z * lax.rsqrt(jnp.mean(z * z, axis=-1, keepdims=True) + EPS) * gfin_ref[...]


def _pos_cols(x_ref, i):
    return x_ref[:, i * D_MODEL:(i + 1) * D_MODEL]


def _mod_cols(mod_ref, n_seq, k):
    return mod_ref[0:n_seq, k * D_MODEL:(k + 1) * D_MODEL]


def _mlp_sample_kernel(n_pos, n_seq, x_ref, conv_ref, ret_ref, wout_ref, mod_ref,
                       gffn_ref, w1_ref, w2_ref, gfin_ref, y_ref, x1buf, hbuf, acc):
    c = pl.program_id(0)
    rows = lambda i: slice(i * n_seq, (i + 1) * n_seq)

    @pl.when(c == 0)
    def _():
        mix = (_dot(conv_ref[...], wout_ref[0:CONV_CH, :].astype(BF16))
               + _dot(ret_ref[...], wout_ref[CONV_CH:CONV_CH + RET_W, :].astype(BF16)))
        for i in range(n_pos):
            x1 = _pos_cols(x_ref, i) + _mod_cols(mod_ref, n_seq, 2) * mix[rows(i)]
            x1buf[rows(i), :] = x1
            hbuf[rows(i), :] = _rmsnorm_mod(x1, gffn_ref[...], _mod_cols(mod_ref, n_seq, 4),
                                            _mod_cols(mod_ref, n_seq, 3)).astype(BF16)

    a = jnp.maximum(_dot(hbuf[...], w1_ref[...].astype(BF16)), 0.0)
    part = _dot((a * a).astype(BF16), w2_ref[...].astype(BF16))

    @pl.when(c == 0)
    def _():
        acc[...] = part

    @pl.when(c > 0)
    def _():
        acc[...] += part

    @pl.when(c == pl.num_programs(0) - 1)
    def _():
        for i in range(n_pos):
            z = x1buf[rows(i), :] + _mod_cols(mod_ref, n_seq, 5) * acc[rows(i), :]
            y_ref[:, i * D_MODEL:(i + 1) * D_MODEL] = (
                z * lax.rsqrt(jnp.mean(z * z, axis=-1, keepdims=True) + EPS) * gfin_ref[...])


def _mlp_sample(x, conv_out, ret_out, w_out, mod, g_ffn, w1, w2, g_final):
    n_seq, d = x.shape[0], D_MODEL
    n_pos = x.shape[1] // d
    n_tok = n_pos * n_seq
    n_chunks = D_FF // SAMPLE_FFN_CHUNK
    whole = lambda shape: pl.BlockSpec(shape, lambda c: (0,) * len(shape),
                                       pipeline_mode=pl.Buffered(1))
    return pl.pallas_call(
        functools.partial(_mlp_sample_kernel, n_pos, n_seq),
        grid=(n_chunks,),
        in_specs=[whole(x.shape), whole(conv_out.shape), whole(ret_out.shape),
                  whole(w_out.shape), whole(mod.shape), whole((1, d)),
                  pl.BlockSpec((d, SAMPLE_FFN_CHUNK), lambda c: (0, c)),
                  pl.BlockSpec((SAMPLE_FFN_CHUNK, d), lambda c: (c, 0)),
                  whole((1, d))],
        out_specs=pl.BlockSpec(x.shape, lambda c: (0, 0)),
        out_shape=jax.ShapeDtypeStruct(x.shape, F32),
        scratch_shapes=[pltpu.VMEM((n_tok, d), F32), pltpu.VMEM((n_tok, d), BF16),
                        pltpu.VMEM((n_tok, d), F32)],
        compiler_params=pltpu.CompilerParams(
            dimension_semantics=("arbitrary",), vmem_limit_bytes=VMEM_LIMIT),
        name="mlp_sample",
    )(x, conv_out, ret_out, w_out, mod, g_ffn.reshape(1, d), w1, w2, g_final.reshape(1, d))


def _power_table_kernel(o_ref):
    lane_h = (lax.broadcasted_iota(jnp.int32, o_ref.shape, 1) // RET_DK).astype(F32)
    n = lax.broadcasted_iota(jnp.int32, o_ref.shape, 0).astype(F32)
    o_ref[...] = jnp.exp(_log_decay(lane_h) * n)


def _power_table():
    return pl.pallas_call(_power_table_kernel,
                          out_shape=jax.ShapeDtypeStruct((8, RET_W), F32),
                          name="decay_powers")()


def _proj_sample_kernel(n_pos, n_seq, x_ref, mod_ref, gmix_ref, win_ref, cache_ref, convw_ref,
                        convb_ref, clng_ref, clnb_ref, cos_ref, sin_ref,
                        hist_ref, conv_ref, q_ref, k_ref, v_ref, g_ref):
    shift, scale = _mod_cols(mod_ref, n_seq, 0), _mod_cols(mod_ref, n_seq, 1)
    hb = jnp.concatenate(
        [_rmsnorm_mod(_pos_cols(x_ref, i), gmix_ref[...], scale, shift).astype(BF16)
         for i in range(n_pos)], axis=0)

    def proj(k):
        return _dot(hb, win_ref[:, k * CONV_CH:(k + 1) * CONV_CH].astype(BF16))

    u = proj(0) * jax.nn.sigmoid(proj(1))
    n_hist = CONV_K - 1
    rows = lambda a, i: a[i * n_seq:(i + 1) * n_seq]
    for m in range(n_hist - n_pos):
        hist_ref[m] = cache_ref[m + n_pos]
    for i in range(n_pos):
        hist_ref[n_hist - n_pos + i] = rows(u, i)
        dw = jnp.broadcast_to(convb_ref[...], (n_seq, CONV_CH))
        for k in range(CONV_K):
            m = i + k
            src = cache_ref[m] if m < n_hist else rows(u, m - n_hist)
            dw = dw + convw_ref[k:k + 1, :] * src
        conv_ref[i] = _silu(_layernorm(dw, clng_ref[...], clnb_ref[...])).astype(BF16)

    lane = lax.broadcasted_iota(jnp.int32, (1, PAIR_LANES), 1)
    first_half = (lane % RET_DK) < (RET_DK // 2)
    q = proj(2) * (RET_DK ** -0.5)
    kk = proj(3)
    v = proj(4)
    gt = proj(5)
    for i in range(n_pos):
        cos = cos_ref[i:i + 1, :]
        sin = sin_ref[i:i + 1, :]
        for p in range(N_PAIRS):
            sl = slice(p * PAIR_LANES, (p + 1) * PAIR_LANES)
            q_ref[i, :, sl] = _rotary_pair(rows(q, i)[:, sl], cos, sin, first_half)
            k_ref[i, :, sl] = _rotary_pair(rows(kk, i)[:, sl], cos, sin, first_half)
        v_ref[i] = rows(v, i)
        g_ref[i] = rows(gt, i)


def _proj_sample(x, mod, g_mix, w_in, cache_t, conv_w, conv_b, cln_g, cln_b, rope):
    n_seq, d = x.shape[0], D_MODEL
    n_pos = x.shape[1] // d
    blk = SAMPLE_SEQ_BLOCK
    cos, sin = rope
    row = lambda a: a.reshape(1, -1)
    whole = lambda a: pl.BlockSpec(a.shape, lambda s: (0,) * a.ndim, pipeline_mode=pl.Buffered(1))
    seqs = lambda lead, width: pl.BlockSpec((lead, blk, width), lambda s: (0, s, 0))
    tok = lambda dt: jax.ShapeDtypeStruct((n_pos, n_seq, RET_W), dt)
    args = (x, mod, row(g_mix), w_in, cache_t, conv_w, row(conv_b), row(cln_g), row(cln_b),
            cos, sin)
    return pl.pallas_call(
        functools.partial(_proj_sample_kernel, n_pos, blk),
        grid=(n_seq // blk,),
        in_specs=[pl.BlockSpec((blk, n_pos * d), lambda s: (s, 0)),
                  pl.BlockSpec((blk, mod.shape[1]), lambda s: (s, 0)),
                  whole(args[2]), whole(w_in), seqs(cache_t.shape[0], CONV_CH)]
        + [whole(a) for a in args[5:]],
        out_specs=[seqs(cache_t.shape[0], CONV_CH)] + [seqs(n_pos, RET_W)] * 5,
        out_shape=(jax.ShapeDtypeStruct(cache_t.shape, F32), tok(BF16),
                   tok(F32), tok(F32), tok(F32), tok(F32)),
        compiler_params=pltpu.CompilerParams(
            dimension_semantics=("arbitrary",), vmem_limit_bytes=VMEM_LIMIT),
        name="proj_sample",
    )(*args)


def _ret_sample_kernel(n_pos, n_seq, q_ref, k_ref, v_ref, g_ref, st_ref, pw_ref, rlng_ref,
                       rlnb_ref, out_ref, snew_ref, sb_buf, qd_buf, kd_buf, v_buf):
    lane = lax.broadcasted_iota(jnp.int32, (1, PAIR_LANES), 1)
    lo = lane < RET_DK
    rows = lambda a, i: a[i * n_seq:(i + 1) * n_seq]
    cols = lambda a, i: a[:, i * n_seq:(i + 1) * n_seq]
    pw = pw_ref[...]
    q = q_ref[...]
    k = k_ref[...]
    qb = [rows(q, i).astype(BF16) for i in range(n_pos)]
    kb = [rows(k, i).astype(BF16) for i in range(n_pos)]
    vb = [rows(v_ref[...], i).astype(BF16) for i in range(n_pos)]

    o = []
    for i in range(n_pos):
        acc = jnp.zeros((n_seq, PAIR_LANES), F32)
        for j in range(i + 1):
            prod = qb[i].astype(F32) * kb[j].astype(F32)
            s_lo = jnp.sum(jnp.where(lo, prod, 0.0), axis=-1, keepdims=True)
            s_hi = jnp.sum(jnp.where(lo, 0.0, prod), axis=-1, keepdims=True)
            score = jnp.where(lo, s_lo, s_hi) * pw[i - j:i - j + 1, :]
            acc = acc + score.astype(BF16).astype(F32) * vb[j].astype(F32)
        o.append(acc)

    rounded = lambda x: x.astype(BF16).astype(F32)
    qd_buf[...] = jnp.concatenate(
        [rounded(rows(q, i) * pw[i + 1:i + 2, :]) for i in range(n_pos)], axis=0).T
    kd_buf[...] = jnp.concatenate(
        [rounded(rows(k, j) * pw[n_pos - 1 - j:n_pos - j, :]) for j in range(n_pos)], axis=0).T
    v_buf[...] = jnp.concatenate([vb[j].astype(F32) for j in range(n_pos)], axis=0).T
    tile_at = lambda r: pl.ds(pl.multiple_of(r * RET_DV, RET_DV), RET_DV)
    pos = lambda i: slice(i * n_seq, (i + 1) * n_seq)

    for a in range(2):
        gam = pw[n_pos:n_pos + 1, a * RET_DK:a * RET_DK + 1]
        v_rows = slice(a * RET_DV, (a + 1) * RET_DV)

        def update(grp, carry, a=a, gam=gam, v_rows=v_rows):
            r0 = pl.multiple_of(a * RET_DK + grp * SUBLANES, SUBLANES)
            kd_rows = kd_buf[pl.ds(r0, SUBLANES), :]
            for s in range(SUBLANES):
                r = r0 + s
                s_old = st_ref[tile_at(r), :]
                sb_buf[tile_at(r), :] = rounded(s_old)
                s_new = s_old * gam
                for j in range(n_pos):
                    s_new = s_new + kd_rows[s:s + 1, pos(j)] * v_buf[v_rows, pos(j)]
                snew_ref[tile_at(r), :] = s_new
            return carry

        lax.fori_loop(0, RET_DK // SUBLANES, update, 0)

    for i in range(n_pos):
        heads = []
        for a in range(2):
            def cross(grp, acc, a=a, i=i):
                r0 = pl.multiple_of(a * RET_DK + grp * SUBLANES, SUBLANES)
                qd_rows = qd_buf[pl.ds(r0, SUBLANES), pos(i)]
                for s in range(SUBLANES):
                    acc = acc + qd_rows[s:s + 1, :] * sb_buf[tile_at(r0 + s), :]
                return acc

            heads.append(lax.fori_loop(0, RET_DK // SUBLANES, cross,
                                       jnp.zeros((RET_DV, n_seq), F32)))
        oi = o[i] + jnp.concatenate(heads, axis=0).T
        on = _pair_groupnorm(oi, lo, rlng_ref[...], rlnb_ref[...])
        out_ref[i * n_seq:(i + 1) * n_seq, :] = (on * _silu(rows(g_ref[...], i))).astype(BF16)


def _ret_sample(q, k, v, g, state_t, pw, rln_g, rln_b, n_pos, n_seq):
    n_tok = n_pos * n_seq
    pair_state = 2 * RET_DK * RET_DV
    slab = pl.BlockSpec((n_tok, PAIR_LANES), lambda p: (0, p))
    state_spec = pl.BlockSpec((pair_state, n_seq), lambda p: (p, 0))
    return pl.pallas_call(
        functools.partial(_ret_sample_kernel, n_pos, n_seq),
        grid=(N_PAIRS,),
        in_specs=[slab, slab, slab, slab, state_spec,
                  pl.BlockSpec((8, PAIR_LANES), lambda p: (0, p)),
                  pl.BlockSpec((1, PAIR_LANES), lambda p: (0, p)),
                  pl.BlockSpec((1, PAIR_LANES), lambda p: (0, p))],
        out_specs=[slab, state_spec],
        out_shape=(jax.ShapeDtypeStruct((n_tok, RET_W), BF16),
                   jax.ShapeDtypeStruct(state_t.shape, F32)),
        scratch_shapes=[pltpu.VMEM((pair_state, n_seq), F32)]
        + [pltpu.VMEM((PAIR_LANES, n_tok), F32)] * 3,
        compiler_params=pltpu.CompilerParams(
            dimension_semantics=("arbitrary",), vmem_limit_bytes=VMEM_LIMIT),
        name="ret_sample",
    )(q, k, v, g, state_t, pw, rln_g.reshape(1, RET_W), rln_b.reshape(1, RET_W))


def _unpack_state(s2):
    blocks = []
    for p in range(N_PAIRS):
        for a in range(2):
            sl = slice(a * RET_DK, (a + 1) * RET_DK)
            blocks.append(s2[:, p, sl, sl])
    return jnp.stack(blocks, axis=1)


def _prompt_path(x, mod, mod_row0, g_mix, w_in, conv_w, conv_b, cln_g, cln_b, rln_g, rln_b,
                 w_out, g_ffn, w_ff1, w_ff2, g_final, tile=PROMPT_TILE):
    seq = x.shape[1]
    rope = _rope_tables(seq, 0)
    decay = _decay_tables(RET_CHUNK)
    y, hist, s2, *w_bf16 = _layer_prompt(x, mod, mod_row0, g_mix, w_in, conv_w, conv_b, cln_g,
                                         cln_b, rln_g, rln_b, w_out, g_ffn, w_ff1, w_ff2, g_final,
                                         rope, decay, tile)
    return y, hist[:, HIST_ROWS - (CONV_K - 1):], _unpack_state(s2), w_bf16


def _sample_path(x, mod, cache, state, pos0, g_mix, w_in, conv_w, conv_b, cln_g, cln_b,
                 rln_g, rln_b, w_out, g_ffn, w_ff1, w_ff2, g_final):
    n_seq, n_pos, d = x.shape
    xs = x.reshape(n_seq, n_pos * d)
    rope = _rope_tables(8, pos0)
    hist_t, *tok = _proj_sample(xs, mod, g_mix, w_in, cache.transpose(1, 0, 2),
                                conv_w, conv_b, cln_g, cln_b, rope)
    conv_out, q, k, v, g = [a.reshape(n_pos * n_seq, -1) for a in tok]
    ret_out, s_new_t = _ret_sample(q, k, v, g, state.reshape(n_seq, -1).T, _power_table(),
                                   rln_g, rln_b, n_pos, n_seq)
    s_new = s_new_t.T
    y = _mlp_sample(xs, conv_out, ret_out, w_out, mod, g_ffn, w_ff1, w_ff2, g_final)
    return y.reshape(x.shape), hist_t.transpose(1, 0, 2), s_new.reshape(state.shape)


def kernel(x_prompt, x_sample, cache_conv, state_ret, c_prompt, c_sample, w_ada, b_ada, g_mix, w_in, conv_w, conv_b, conv_ln_g, conv_ln_b, ret_ln_g, ret_ln_b, w_out, g_ffn, w_ff1, w_ff2, g_final):
    mod = _modulation(jnp.concatenate([c_sample, c_prompt], axis=0), w_ada[0], b_ada[0])
    params = (g_mix[0], w_in[0], conv_w[0], conv_b[0], conv_ln_g[0], conv_ln_b[0], ret_ln_g[0],
              ret_ln_b[0], w_out[0], g_ffn[0], w_ff1[0], w_ff2[0], g_final)
    y_p, conv_p, ret_p, (w_in_b, w_out_b, w_ff1_b, w_ff2_b) = _prompt_path(
        x_prompt, mod, x_sample.shape[0], *params)
    params_b = (g_mix[0], w_in_b, conv_w[0], conv_b[0], conv_ln_g[0], conv_ln_b[0], ret_ln_g[0],
                ret_ln_b[0], w_out_b, g_ffn[0], w_ff1_b, w_ff2_b, g_final)
    y_s, conv_s, ret_s = _sample_path(x_sample, mod, cache_conv[0], state_ret[0], PAST_LEN,
                                      *params_b)
    return (y_p, y_s, conv_p[None], ret_p[None], conv_s[None], ret_s[None])
```

```python
import functools

import jax
import jax.numpy as jnp
from jax import lax
from jax.experimental import pallas as pl
from jax.experimental.pallas import tpu as pltpu

D_MODEL = 1024
CONV_CH = 512
CONV_K = 31
RET_HEADS = 8
RET_W = 512
RET_DK = 64
RET_DV = 64
D_FF = 4 * D_MODEL
RET_CHUNK = 128
ROPE_THETA = 10000.0
EPS = 1e-6
N_MOD = 6
PAST_LEN = 16384

SUBLANES = 8
HIST_ROWS = 32
CONV_ROW_BLOCK = 32
PROMPT_TILE = 256
FFN_CHUNK = 512
SAMPLE_FFN_CHUNK = 1024
SAMPLE_SEQ_BLOCK = 64
WEIGHT_STAGE_ELEMS = 512 * 1024
WEIGHT_STAGE_SLOTS = 3
LAYER_PROGRAM = ("c cf cf cf cf cf cf cf cf "
                 "rrrr rfrf rfrf r "
                 "rfrf rfrf r")
PAIR_LANES = 2 * RET_DK
N_PAIRS = RET_HEADS // 2
VMEM_LIMIT = 56 * 1024 * 1024

F32 = jnp.float32
BF16 = jnp.bfloat16


def _log_decay(head):
    return jnp.log1p(-jnp.exp2(-5.0 - head))


def _rope_kernel(pos0, cos_ref, sin_ref):
    rows = cos_ref.shape[0]
    lane = lax.broadcasted_iota(jnp.int32, (rows, PAIR_LANES), 1)
    row = lax.broadcasted_iota(jnp.int32, (rows, PAIR_LANES), 0)
    half = RET_DK // 2
    freq = (lane % half).astype(F32)
    inv = jnp.power(jnp.full_like(freq, ROPE_THETA), -freq / half)
    ang = (row + pos0).astype(F32) * inv
    first = (lane % RET_DK) < half
    cos_ref[...] = jnp.cos(ang)
    sin_ref[...] = jnp.where(first, -jnp.sin(ang), jnp.sin(ang))


ROPE_BLOCK = 16


def _rope_blocked_kernel(pos0, cos_ref, sin_ref):
    rows = cos_ref.shape[0]
    n_blk = rows // ROPE_BLOCK
    half = RET_DK // 2

    def angles(n, step, start):
        lane = lax.broadcasted_iota(jnp.int32, (n, PAIR_LANES), 1)
        row = lax.broadcasted_iota(jnp.int32, (n, PAIR_LANES), 0)
        freq = (lane % half).astype(F32)
        inv = jnp.power(jnp.full_like(freq, ROPE_THETA), -freq / half)
        return (row * step + start).astype(F32) * inv

    a = angles(n_blk, ROPE_BLOCK, pos0)
    b = angles(ROPE_BLOCK, 1, 0)
    lane = lax.broadcasted_iota(jnp.int32, (ROPE_BLOCK, PAIR_LANES), 1)
    sign = jnp.where((lane % RET_DK) < half, -1.0, 1.0)
    cos_a, sin_a = jnp.cos(a), jnp.sin(a)
    cos_b, sin_b = jnp.cos(b), jnp.sin(b)
    cos_bs, sin_bs = cos_b * sign, sin_b * sign
    for blk in range(n_blk):
        ca, sa = cos_a[blk:blk + 1, :], sin_a[blk:blk + 1, :]
        rs = slice(blk * ROPE_BLOCK, (blk + 1) * ROPE_BLOCK)
        cos_ref[rs, :] = ca * cos_b - sa * sin_b
        sin_ref[rs, :] = sa * cos_bs + ca * sin_bs


def _rope_tables(rows, pos0):
    blocked = rows % ROPE_BLOCK == 0 and rows // ROPE_BLOCK >= SUBLANES
    return pl.pallas_call(
        functools.partial(_rope_blocked_kernel if blocked else _rope_kernel, pos0),
        out_shape=(jax.ShapeDtypeStruct((rows, PAIR_LANES), F32),
                   jax.ShapeDtypeStruct((rows, PAIR_LANES), F32)),
        name="rope_tables",
    )()


def _decay_kernel(chunk, dec_ref, qdec_ref, kdec_ref, sdec_ref):
    c = chunk
    shape = (N_PAIRS, c, 2 * c)
    col = lax.broadcasted_iota(jnp.int32, shape, 2)
    h = (2 * lax.broadcasted_iota(jnp.int32, shape, 0) + col // c).astype(F32)
    i = lax.broadcasted_iota(jnp.int32, shape, 1).astype(F32)
    j = (col % c).astype(F32)
    diff = i - j
    dec_ref[...] = jnp.where(diff >= 0, jnp.exp(_log_decay(h) * jnp.maximum(diff, 0.0)), 0.0)
    lane_h = (lax.broadcasted_iota(jnp.int32, (c, RET_W), 1) // RET_DK).astype(F32)
    idx = lax.broadcasted_iota(jnp.int32, (c, RET_W), 0).astype(F32)
    lg = _log_decay(lane_h)
    qdec_ref[...] = jnp.exp(lg * (idx + 1.0))
    kdec_ref[...] = jnp.exp(lg * (c - 1.0 - idx))
    p = lax.broadcasted_iota(jnp.int32, (N_PAIRS, PAIR_LANES, PAIR_LANES), 0)
    r = lax.broadcasted_iota(jnp.int32, (N_PAIRS, PAIR_LANES, PAIR_LANES), 1) // RET_DK
    q = lax.broadcasted_iota(jnp.int32, (N_PAIRS, PAIR_LANES, PAIR_LANES), 2) // RET_DK
    hh = (2 * p + r).astype(F32)
    sdec_ref[...] = jnp.where(r == q, jnp.exp(_log_decay(hh) * float(c)), 0.0)


def _decay_tables(chunk):
    return pl.pallas_call(
        functools.partial(_decay_kernel, chunk),
        out_shape=(jax.ShapeDtypeStruct((N_PAIRS, chunk, 2 * chunk), F32),
                   jax.ShapeDtypeStruct((chunk, RET_W), F32),
                   jax.ShapeDtypeStruct((chunk, RET_W), F32),
                   jax.ShapeDtypeStruct((N_PAIRS, PAIR_LANES, PAIR_LANES), F32)),
        name="decay_tables",
    )()


def _mod_kernel(c_ref, w_lo_ref, w_hi_ref, b_ref, o_ref):
    c = c_ref[...]
    s = (c * jax.nn.sigmoid(c)).astype(BF16)
    half = w_lo_ref.shape[1]
    for k, w_ref in enumerate((w_lo_ref, w_hi_ref)):
        cs = slice(k * half, (k + 1) * half)
        o_ref[:, cs] = _dot(s, w_ref[...].astype(BF16)) + b_ref[:, cs]


def _modulation(c, w_ada, b_ada):
    n, d = c.shape
    width = w_ada.shape[1]
    blk = D_MODEL
    half = lambda k: pl.BlockSpec((d, blk // 2), lambda i: (0, 2 * i + k))
    return pl.pallas_call(
        _mod_kernel,
        grid=(width // blk,),
        in_specs=[pl.BlockSpec((n, d), lambda i: (0, 0)), half(0), half(1),
                  pl.BlockSpec((1, blk), lambda i: (0, i))],
        out_specs=pl.BlockSpec((n, blk), lambda i: (0, i)),
        out_shape=jax.ShapeDtypeStruct((n, width), F32),
        compiler_params=pltpu.CompilerParams(vmem_limit_bytes=VMEM_LIMIT),
        name="adaln_mod",
    )(c, w_ada, w_ada, b_ada.reshape(1, width))


def _rmsnorm_mod(x, g, scale, shift):
    y = x * lax.rsqrt(jnp.mean(x * x, axis=-1, keepdims=True) + EPS)
    return (y * g) * (1.0 + scale) + shift


def _layernorm(x, g, b):
    mu = jnp.mean(x, axis=-1, keepdims=True)
    d = x - mu
    var = jnp.mean(d * d, axis=-1, keepdims=True)
    return d * lax.rsqrt(var + EPS) * g + b


def _silu(x):
    return x * jax.nn.sigmoid(x)


def _rotary_pair(x, cos, sin_signed, first_half):
    partner = jnp.where(first_half, pltpu.roll(x, PAIR_LANES - RET_DK // 2, 1),
                        pltpu.roll(x, RET_DK // 2, 1))
    return x * cos + partner * sin_signed


def _pair_groupnorm(o, lo, g, b):
    inv_n = 1.0 / RET_DV
    s_lo = jnp.sum(jnp.where(lo, o, 0.0), axis=-1, keepdims=True)
    s_hi = jnp.sum(jnp.where(lo, 0.0, o), axis=-1, keepdims=True)
    d = o - jnp.where(lo, s_lo, s_hi) * inv_n
    d2 = d * d
    v_lo = jnp.sum(jnp.where(lo, d2, 0.0), axis=-1, keepdims=True)
    v_hi = jnp.sum(jnp.where(lo, 0.0, d2), axis=-1, keepdims=True)
    var = jnp.where(lo, v_lo, v_hi) * inv_n
    return d * lax.rsqrt(var + EPS) * g + b


def _dot(a, b):
    return jnp.dot(a, b, preferred_element_type=F32)


def _dot_nt(a, b):
    return lax.dot_general(a, b, (((1,), (1,)), ((), ())), preferred_element_type=F32)


def _dot_tn(a, b):
    return lax.dot_general(a, b, (((0,), (0,)), ((), ())), preferred_element_type=F32)


def _stream_cast_weights(rings):
    def ring_chunks(pairs, stage):
        rows = stage.shape[1]
        return [(src, dst, r0) for src, dst in pairs for r0 in range(0, src.shape[0], rows)]

    chunks = [ring_chunks(pairs, stage) for pairs, stage, _ in rings]

    def copy(k, c):
        _, stage, sems = rings[k]
        src, _, r0 = chunks[k][c]
        slot = c % stage.shape[0]
        return pltpu.make_async_copy(src.at[pl.ds(r0, stage.shape[1]), :],
                                     stage.at[slot, :, pl.ds(0, src.shape[1])], sems.at[slot])

    for k, (_, stage, _) in enumerate(rings):
        for c in range(min(stage.shape[0] - 1, len(chunks[k]))):
            copy(k, c).start()
    for c in range(max(len(ch) for ch in chunks)):
        for k, (_, stage, _) in enumerate(rings):
            if c >= len(chunks[k]):
                continue
            n_slots, rows = stage.shape[0], stage.shape[1]
            if c + n_slots - 1 < len(chunks[k]):
                copy(k, c + n_slots - 1).start()
            copy(k, c).wait()
            src, dst, r0 = chunks[k][c]
            dst[r0:r0 + rows, :] = stage[c % n_slots, :, 0:src.shape[1]].astype(BF16)


def _layer_prompt_kernel(tile, nj, x_ref, mod_ref, gmix_ref, win_hbm, convw_ref, convb_ref,
                         clng_ref, clnb_ref, rlng_ref, rlnb_ref, wout_hbm,
                         cos_ref, sin_ref, dec_ref, qdec_ref, kdec_ref, sdec_ref,
                         mod2_ref, gffn_ref, w1_hbm, w2_hbm, gfin_ref,
                         y_ref, hist_ref, state_ref, win_out, wout_out, w1_out, w2_out,
                         ubuf, ushift, sbuf, mixbuf, x1buf,
                         win_ref, wout_ref, w1_ref, w2_ref,
                         stage_wide, stage_tall, sems_wide, sems_tall, sems_out):
    g = pl.program_id(0)
    total = pl.num_programs(0) - 1
    j = lax.rem(jnp.minimum(g, total - 1), nj)
    live = g < total
    export = [pltpu.make_async_copy(src, dst, sems_out.at[k]) for k, (src, dst) in enumerate(
        [(win_ref, win_out), (wout_ref, wout_out), (w1_ref, w1_out), (w2_ref, w2_out)])]

    @pl.when(g == 0)
    def _():
        _stream_cast_weights([
            ([(win_hbm, win_ref), (w1_hbm, w1_ref)], stage_wide, sems_wide),
            ([(wout_hbm, wout_ref), (w2_hbm, w2_ref)], stage_tall, sems_tall)])
        for cp in export:
            cp.start()
        x1buf[...] = jnp.zeros((tile, D_MODEL), F32)

    @pl.when(jnp.logical_and(j == 0, live))
    def _():
        ubuf[0:HIST_ROWS, :] = jnp.zeros((HIST_ROWS, CONV_CH), F32)
        sbuf[...] = jnp.zeros(sbuf.shape, F32)

    x = x_ref[0]
    b_mix = lax.div(jnp.minimum(g, total - 1), nj)
    b_mlp = lax.div(jnp.maximum(g - 1, 0), nj)
    mods = lambda ref, b: [ref[pl.ds(b, 1), k * D_MODEL:(k + 1) * D_MODEL] for k in range(3)]
    shift, scale, gate = mods(mod_ref, b_mix)

    def conv_stages(u):
        ubuf[HIST_ROWS:HIST_ROWS + tile, :] = u
        off = HIST_ROWS - (CONV_K - 1)
        for r in range(SUBLANES):
            n = tile + SUBLANES * ((CONV_K - 1 - r) // SUBLANES)
            ushift[r, 0:n, :] = ubuf[off + r:off + r + n, :]
        ubuf[0:HIST_ROWS, :] = ubuf[tile:tile + HIST_ROWS, :]
        yield
        for rb in range(tile // CONV_ROW_BLOCK):
            r0 = rb * CONV_ROW_BLOCK
            dw = jnp.broadcast_to(convb_ref[...], (CONV_ROW_BLOCK, CONV_CH))
            for k in range(CONV_K):
                s0 = r0 + SUBLANES * (k // SUBLANES)
                wk = jnp.concatenate([convw_ref[k]] * (CONV_ROW_BLOCK // SUBLANES), axis=0)
                dw = dw + wk * ushift[k % SUBLANES, s0:s0 + CONV_ROW_BLOCK, :]
            conv_out = _silu(_layernorm(dw, clng_ref[...], clnb_ref[...]))
            mixbuf[r0:r0 + CONV_ROW_BLOCK, 0:CONV_CH] = conv_out.astype(BF16)
            yield

    def ret_stages(proj):
        ret_proj = []
        for k in range(4):
            ret_proj.append(proj(2 + k))
            yield
        q, kk, v, gt = ret_proj
        q = q * (RET_DK ** -0.5)
        lane = lax.broadcasted_iota(jnp.int32, (1, PAIR_LANES), 1)
        lo = lane < RET_DK
        first_half = (lane % RET_DK) < (RET_DK // 2)
        lo_b = lo.astype(BF16)
        hi_b = (~lo).astype(BF16)
        pairs = range(N_PAIRS)
        halves = (pairs[:N_PAIRS // 2], pairs[N_PAIRS // 2:])
        lanes = [slice(p * PAIR_LANES, (p + 1) * PAIR_LANES) for p in pairs]
        for c in range(tile // RET_CHUNK):
            rs = slice(c * RET_CHUNK, (c + 1) * RET_CHUNK)
            cos = cos_ref[rs, :]
            sin = sin_ref[rs, :]
            qc, kc, vc, scores, outs = {}, {}, {}, {}, {}
            for group in halves:
                for p in group:
                    qc[p] = _rotary_pair(q[rs, lanes[p]], cos, sin, first_half)
                    kc[p] = _rotary_pair(kk[rs, lanes[p]], cos, sin, first_half)
                    vc[p] = v[rs, lanes[p]].astype(BF16)
                    kb = kc[p].astype(BF16)
                    k_cat = jnp.concatenate([kb * lo_b, kb * hi_b], axis=0)
                    scores[p] = _dot_nt(qc[p].astype(BF16), k_cat)
                yield
            for group in halves:
                for p in group:
                    state = sbuf[p]
                    lhs = jnp.concatenate([(scores[p] * dec_ref[p]).astype(BF16),
                                           (qc[p] * qdec_ref[:, lanes[p]]).astype(BF16)], axis=1)
                    rhs = jnp.concatenate([vc[p] * lo_b, vc[p] * hi_b, state.astype(BF16)],
                                          axis=0)
                    outs[p] = _dot(lhs, rhs)
                    kd = (kc[p] * kdec_ref[:, lanes[p]]).astype(BF16)
                    sbuf[p] = (state * sdec_ref[p]
                               + jnp.where(sdec_ref[p] > 0.0, _dot_tn(kd, vc[p]), 0.0))
                yield
            for p in pairs:
                on = _pair_groupnorm(outs[p], lo, rlng_ref[:, lanes[p]], rlnb_ref[:, lanes[p]])
                mixbuf[rs, CONV_CH + p * PAIR_LANES:CONV_CH + (p + 1) * PAIR_LANES] = (
                    on * _silu(gt[rs, lanes[p]])).astype(BF16)
            yield

    ffn = _ffn_stages(x1buf[...], mods(mod2_ref, b_mlp), gffn_ref, w1_ref, w2_ref, gfin_ref, y_ref)
    next(ffn)
    hb = _rmsnorm_mod(x, gmix_ref[...], scale, shift).astype(BF16)

    def proj(k):
        return _dot(hb, win_ref[:, k * CONV_CH:(k + 1) * CONV_CH])

    streams = {"f": ffn, "c": conv_stages(proj(0) * jax.nn.sigmoid(proj(1))),
               "r": ret_stages(proj)}
    for name in LAYER_PROGRAM.replace(" ", ""):
        next(streams[name], None)
    for stream in (streams["c"], streams["r"], streams["f"]):
        for _ in stream:
            pass

    mix = _dot(mixbuf[...], wout_ref[...])
    x1buf[...] = x + gate * mix

    @pl.when(jnp.logical_and(j == nj - 1, live))
    def _():
        hist_ref[0] = ubuf[0:HIST_ROWS, :]
        state_ref[0] = sbuf[...]

    @pl.when(g == total)
    def _():
        for cp in export:
            cp.wait()


def _layer_prompt(x, mod, mod_row0, g_mix, w_in, conv_w, conv_b, cln_g, cln_b, rln_g, rln_b,
                  w_out, g_ffn, w1, w2, g_final, rope, decay, tile):
    bsz, seq, d = x.shape
    assert bsz % SUBLANES == 0 and mod_row0 % bsz == 0
    in_hbm = pl.BlockSpec(memory_space=pl.ANY)

    def stage_shape(*ws):
        width = max(w.shape[1] for w in ws)
        rows = WEIGHT_STAGE_ELEMS // width
        assert all(w.shape[0] % rows == 0 for w in ws)
        return pltpu.VMEM((WEIGHT_STAGE_SLOTS, rows, width), F32)

    mod_blk = mod_row0 // bsz
    nj = seq // tile
    total = bsz * nj
    cos, sin = rope
    dec, qdec, kdec, sdec = decay
    mix_tile = lambda g: jnp.minimum(g, total - 1)
    ffn_tile = lambda g: jnp.maximum(g - 1, 0)
    resident = lambda shape: pl.BlockSpec(shape, lambda g: (0,) * len(shape),
                                          pipeline_mode=pl.Buffered(1))
    row = lambda a: a.reshape(1, -1)
    conv_w_rows = jnp.broadcast_to(conv_w[:, None, :], (CONV_K, SUBLANES, CONV_CH))
    rope_spec = pl.BlockSpec((tile, PAIR_LANES), lambda g: (mix_tile(g) % nj, 0))
    return pl.pallas_call(
        functools.partial(_layer_prompt_kernel, tile, nj),
        grid=(total + 1,),
        in_specs=[
            pl.BlockSpec((1, tile, d), lambda g: (mix_tile(g) // nj, mix_tile(g) % nj, 0)),
            pl.BlockSpec((bsz, 3 * d), lambda g: (mod_blk, 0), pipeline_mode=pl.Buffered(1)),
            resident((1, d)),
            in_hbm,
            resident((CONV_K, SUBLANES, CONV_CH)),
            resident((1, CONV_CH)),
            resident((1, CONV_CH)),
            resident((1, CONV_CH)),
            resident((1, RET_W)),
            resident((1, RET_W)),
            in_hbm,
            rope_spec,
            rope_spec,
            resident(dec.shape),
            resident(qdec.shape),
            resident(kdec.shape),
            resident(sdec.shape),
            pl.BlockSpec((bsz, 3 * d), lambda g: (mod_blk, 1), pipeline_mode=pl.Buffered(1)),
            resident((1, d)),
            in_hbm,
            in_hbm,
            resident((1, d)),
        ],
        out_specs=[
            pl.BlockSpec((1, tile, d), lambda g: (ffn_tile(g) // nj, ffn_tile(g) % nj, 0)),
            pl.BlockSpec((1, HIST_ROWS, CONV_CH), lambda g: (mix_tile(g) // nj, 0, 0)),
            pl.BlockSpec((1, N_PAIRS, PAIR_LANES, PAIR_LANES),
                         lambda g: (mix_tile(g) // nj, 0, 0, 0)),
            in_hbm, in_hbm, in_hbm, in_hbm,
        ],
        out_shape=(
            jax.ShapeDtypeStruct((bsz, seq, d), F32),
            jax.ShapeDtypeStruct((bsz, HIST_ROWS, CONV_CH), F32),
            jax.ShapeDtypeStruct((bsz, N_PAIRS, PAIR_LANES, PAIR_LANES), F32),
        ) + tuple(jax.ShapeDtypeStruct(w.shape, BF16) for w in (w_in, w_out, w1, w2)),
        scratch_shapes=[
            pltpu.VMEM((HIST_ROWS + tile, CONV_CH), F32),
            pltpu.VMEM((SUBLANES, HIST_ROWS + tile, CONV_CH), F32),
            pltpu.VMEM((N_PAIRS, PAIR_LANES, PAIR_LANES), F32),
            pltpu.VMEM((tile, D_MODEL), BF16),
            pltpu.VMEM((tile, D_MODEL), F32),
            pltpu.VMEM(w_in.shape, BF16),
            pltpu.VMEM(w_out.shape, BF16),
            pltpu.VMEM(w1.shape, BF16),
            pltpu.VMEM(w2.shape, BF16),
            stage_shape(w_in, w1),
            stage_shape(w_out, w2),
            pltpu.SemaphoreType.DMA((WEIGHT_STAGE_SLOTS,)),
            pltpu.SemaphoreType.DMA((WEIGHT_STAGE_SLOTS,)),
            pltpu.SemaphoreType.DMA((4,)),
        ],
        compiler_params=pltpu.CompilerParams(
            dimension_semantics=("arbitrary",), vmem_limit_bytes=VMEM_LIMIT),
        name="layer_prompt",
    )(x, mod, row(g_mix), w_in, conv_w_rows, row(conv_b), row(cln_g), row(cln_b),
      row(rln_g), row(rln_b), w_out, cos, sin, dec, qdec, kdec, sdec,
      mod, row(g_ffn), w1, w2, row(g_final))


def _ffn_stages(x, mods, gffn_ref, w1_ref, w2_ref, gfin_ref, y_ref):
    shift, scale, gate = mods
    hb = _rmsnorm_mod(x, gffn_ref[...], scale, shift).astype(BF16)
    yield
    half = D_MODEL // 2
    ff = [None, None]
    for c in range(D_FF // (2 * FFN_CHUNK)):
        acts = []
        for h in range(2):
            c0 = (2 * c + h) * FFN_CHUNK
            a = jnp.maximum(_dot(hb, w1_ref[:, c0:c0 + FFN_CHUNK]), 0.0)
            acts.append((a * a).astype(BF16))
            yield
        act = jnp.concatenate(acts, axis=1)
        for n in range(2):
            part = _dot(act, w2_ref[2 * c * FFN_CHUNK:2 * (c + 1) * FFN_CHUNK,
                                    n * half:(n + 1) * half])
            ff[n] = part if ff[n] is None else ff[n] + part
            yield
    z = x + gate * jnp.concatenate(ff, axis=1)
    y_ref[0] = z * lax.rsqrt(jnp.mean(z * z, axis=-1, keepdims=True) + EPS) * gfin_ref[...]


def _pos_cols(x_ref, i):
    return x_ref[:, i * D_MODEL:(i + 1) * D_MODEL]


def _mod_cols(mod_ref, n_seq, k):
    return mod_ref[0:n_seq, k * D_MODEL:(k + 1) * D_MODEL]


def _mlp_sample_kernel(n_pos, n_seq, x_ref, conv_ref, ret_ref, wout_ref, mod_ref,
                       gffn_ref, w1_ref, w2_ref, gfin_ref, y_ref, x1buf, hbuf, acc):
    c = pl.program_id(0)
    rows = lambda i: slice(i * n_seq, (i + 1) * n_seq)

    @pl.when(c == 0)
    def _():
        mix = (_dot(conv_ref[...], wout_ref[0:CONV_CH, :].astype(BF16))
               + _dot(ret_ref[...], wout_ref[CONV_CH:CONV_CH + RET_W, :].astype(BF16)))
        for i in range(n_pos):
            x1 = _pos_cols(x_ref, i) + _mod_cols(mod_ref, n_seq, 2) * mix[rows(i)]
            x1buf[rows(i), :] = x1
            hbuf[rows(i), :] = _rmsnorm_mod(x1, gffn_ref[...], _mod_cols(mod_ref, n_seq, 4),
                                            _mod_cols(mod_ref, n_seq, 3)).astype(BF16)

    a = jnp.maximum(_dot(hbuf[...], w1_ref[...].astype(BF16)), 0.0)
    part = _dot((a * a).astype(BF16), w2_ref[...].astype(BF16))

    @pl.when(c == 0)
    def _():
        acc[...] = part

    @pl.when(c > 0)
    def _():
        acc[...] += part

    @pl.when(c == pl.num_programs(0) - 1)
    def _():
        for i in range(n_pos):
            z = x1buf[rows(i), :] + _mod_cols(mod_ref, n_seq, 5) * acc[rows(i), :]
            y_ref[:, i * D_MODEL:(i + 1) * D_MODEL] = (
                z * lax.rsqrt(jnp.mean(z * z, axis=-1, keepdims=True) + EPS) * gfin_ref[...])


def _mlp_sample(x, conv_out, ret_out, w_out, mod, g_ffn, w1, w2, g_final):
    n_seq, d = x.shape[0], D_MODEL
    n_pos = x.shape[1] // d
    n_tok = n_pos * n_seq
    n_chunks = D_FF // SAMPLE_FFN_CHUNK
    whole = lambda shape: pl.BlockSpec(shape, lambda c: (0,) * len(shape),
                                       pipeline_mode=pl.Buffered(1))
    return pl.pallas_call(
        functools.partial(_mlp_sample_kernel, n_pos, n_seq),
        grid=(n_chunks,),
        in_specs=[whole(x.shape), whole(conv_out.shape), whole(ret_out.shape),
                  whole(w_out.shape), whole(mod.shape), whole((1, d)),
                  pl.BlockSpec((d, SAMPLE_FFN_CHUNK), lambda c: (0, c)),
                  pl.BlockSpec((SAMPLE_FFN_CHUNK, d), lambda c: (c, 0)),
                  whole((1, d))],
        out_specs=pl.BlockSpec(x.shape, lambda c: (0, 0)),
        out_shape=jax.ShapeDtypeStruct(x.shape, F32),
        scratch_shapes=[pltpu.VMEM((n_tok, d), F32), pltpu.VMEM((n_tok, d), BF16),
                        pltpu.VMEM((n_tok, d), F32)],
        compiler_params=pltpu.CompilerParams(
            dimension_semantics=("arbitrary",), vmem_limit_bytes=VMEM_LIMIT),
        name="mlp_sample",
    )(x, conv_out, ret_out, w_out, mod, g_ffn.reshape(1, d), w1, w2, g_final.reshape(1, d))


def _power_table_kernel(o_ref):
    lane_h = (lax.broadcasted_iota(jnp.int32, o_ref.shape, 1) // RET_DK).astype(F32)
    n = lax.broadcasted_iota(jnp.int32, o_ref.shape, 0).astype(F32)
    o_ref[...] = jnp.exp(_log_decay(lane_h) * n)


def _power_table():
    return pl.pallas_call(_power_table_kernel,
                          out_shape=jax.ShapeDtypeStruct((8, RET_W), F32),
                          name="decay_powers")()


def _proj_sample_kernel(n_pos, n_seq, x_ref, mod_ref, gmix_ref, win_ref, cache_ref, convw_ref,
                        convb_ref, clng_ref, clnb_ref, cos_ref, sin_ref,
                        hist_ref, conv_ref, q_ref, k_ref, v_ref, g_ref):
    shift, scale = _mod_cols(mod_ref, n_seq, 0), _mod_cols(mod_ref, n_seq, 1)
    hb = jnp.concatenate(
        [_rmsnorm_mod(_pos_cols(x_ref, i), gmix_ref[...], scale, shift).astype(BF16)
         for i in range(n_pos)], axis=0)

    def proj(k):
        return _dot(hb, win_ref[:, k * CONV_CH:(k + 1) * CONV_CH].astype(BF16))

    u = proj(0) * jax.nn.sigmoid(proj(1))
    n_hist = CONV_K - 1
    rows = lambda a, i: a[i * n_seq:(i + 1) * n_seq]
    for m in range(n_hist - n_pos):
        hist_ref[m] = cache_ref[m + n_pos]
    for i in range(n_pos):
        hist_ref[n_hist - n_pos + i] = rows(u, i)
        dw = jnp.broadcast_to(convb_ref[...], (n_seq, CONV_CH))
        for k in range(CONV_K):
            m = i + k
            src = cache_ref[m] if m < n_hist else rows(u, m - n_hist)
            dw = dw + convw_ref[k:k + 1, :] * src
        conv_ref[i] = _silu(_layernorm(dw, clng_ref[...], clnb_ref[...])).astype(BF16)

    lane = lax.broadcasted_iota(jnp.int32, (1, PAIR_LANES), 1)
    first_half = (lane % RET_DK) < (RET_DK // 2)
    q = proj(2) * (RET_DK ** -0.5)
    kk = proj(3)
    v = proj(4)
    gt = proj(5)
    for i in range(n_pos):
        cos = cos_ref[i:i + 1, :]
        sin = sin_ref[i:i + 1, :]
        for p in range(N_PAIRS):
            sl = slice(p * PAIR_LANES, (p + 1) * PAIR_LANES)
            q_ref[i, :, sl] = _rotary_pair(rows(q, i)[:, sl], cos, sin, first_half)
            k_ref[i, :, sl] = _rotary_pair(rows(kk, i)[:, sl], cos, sin, first_half)
        v_ref[i] = rows(v, i)
        g_ref[i] = rows(gt, i)


def _proj_sample(x, mod, g_mix, w_in, cache_t, conv_w, conv_b, cln_g, cln_b, rope):
    n_seq, d = x.shape[0], D_MODEL
    n_pos = x.shape[1] // d
    blk = SAMPLE_SEQ_BLOCK
    cos, sin = rope
    row = lambda a: a.reshape(1, -1)
    whole = lambda a: pl.BlockSpec(a.shape, lambda s: (0,) * a.ndim, pipeline_mode=pl.Buffered(1))
    seqs = lambda lead, width: pl.BlockSpec((lead, blk, width), lambda s: (0, s, 0))
    tok = lambda dt: jax.ShapeDtypeStruct((n_pos, n_seq, RET_W), dt)
    args = (x, mod, row(g_mix), w_in, cache_t, conv_w, row(conv_b), row(cln_g), row(cln_b),
            cos, sin)
    return pl.pallas_call(
        functools.partial(_proj_sample_kernel, n_pos, blk),
        grid=(n_seq // blk,),
        in_specs=[pl.BlockSpec((blk, n_pos * d), lambda s: (s, 0)),
                  pl.BlockSpec((blk, mod.shape[1]), lambda s: (s, 0)),
                  whole(args[2]), whole(w_in), seqs(cache_t.shape[0], CONV_CH)]
        + [whole(a) for a in args[5:]],
        out_specs=[seqs(cache_t.shape[0], CONV_CH)] + [seqs(n_pos, RET_W)] * 5,
        out_shape=(jax.ShapeDtypeStruct(cache_t.shape, F32), tok(BF16),
                   tok(F32), tok(F32), tok(F32), tok(F32)),
        compiler_params=pltpu.CompilerParams(
            dimension_semantics=("arbitrary",), vmem_limit_bytes=VMEM_LIMIT),
        name="proj_sample",
    )(*args)


def _ret_sample_kernel(n_pos, n_seq, q_ref, k_ref, v_ref, g_ref, st_ref, pw_ref, rlng_ref,
                       rlnb_ref, out_ref, snew_ref, sb_buf, qd_buf, kd_buf, v_buf):
    lane = lax.broadcasted_iota(jnp.int32, (1, PAIR_LANES), 1)
    lo = lane < RET_DK
    rows = lambda a, i: a[i * n_seq:(i + 1) * n_seq]
    cols = lambda a, i: a[:, i * n_seq:(i + 1) * n_seq]
    pw = pw_ref[...]
    q = q_ref[...]
    k = k_ref[...]
    qb = [rows(q, i).astype(BF16) for i in range(n_pos)]
    kb = [rows(k, i).astype(BF16) for i in range(n_pos)]
    vb = [rows(v_ref[...], i).astype(BF16) for i in range(n_pos)]

    o = []
    for i in range(n_pos):
        acc = jnp.zeros((n_seq, PAIR_LANES), F32)
        for j in range(i + 1):
            prod = qb[i].astype(F32) * kb[j].astype(F32)
            s_lo = jnp.sum(jnp.where(lo, prod, 0.0), axis=-1, keepdims=True)
            s_hi = jnp.sum(jnp.where(lo, 0.0, prod), axis=-1, keepdims=True)
            score = jnp.where(lo, s_lo, s_hi) * pw[i - j:i - j + 1, :]
            acc = acc + score.astype(BF16).astype(F32) * vb[j].astype(F32)
        o.append(acc)

    rounded = lambda x: x.astype(BF16).astype(F32)
    qd_buf[...] = jnp.concatenate(
        [rounded(rows(q, i) * pw[i + 1:i + 2, :]) for i in range(n_pos)], axis=0).T
    kd_buf[...] = jnp.concatenate(
        [rounded(rows(k, j) * pw[n_pos - 1 - j:n_pos - j, :]) for j in range(n_pos)], axis=0).T
    v_buf[...] = jnp.concatenate([vb[j].astype(F32) for j in range(n_pos)], axis=0).T
    tile_at = lambda r: pl.ds(pl.multiple_of(r * RET_DV, RET_DV), RET_DV)
    pos = lambda i: slice(i * n_seq, (i + 1) * n_seq)

    for a in range(2):
        gam = pw[n_pos:n_pos + 1, a * RET_DK:a * RET_DK + 1]
        v_rows = slice(a * RET_DV, (a + 1) * RET_DV)

        def update(grp, carry, a=a, gam=gam, v_rows=v_rows):
            r0 = pl.multiple_of(a * RET_DK + grp * SUBLANES, SUBLANES)
            kd_rows = kd_buf[pl.ds(r0, SUBLANES), :]
            for s in range(SUBLANES):
                r = r0 + s
                s_old = st_ref[tile_at(r), :]
                sb_buf[tile_at(r), :] = rounded(s_old)
                s_new = s_old * gam
                for j in range(n_pos):
                    s_new = s_new + kd_rows[s:s + 1, pos(j)] * v_buf[v_rows, pos(j)]
                snew_ref[tile_at(r), :] = s_new
            return carry

        lax.fori_loop(0, RET_DK // SUBLANES, update, 0)

    for i in range(n_pos):
        heads = []
        for a in range(2):
            def cross(grp, acc, a=a, i=i):
                r0 = pl.multiple_of(a * RET_DK + grp * SUBLANES, SUBLANES)
                qd_rows = qd_buf[pl.ds(r0, SUBLANES), pos(i)]
                for s in range(SUBLANES):
                    acc = acc + qd_rows[s:s + 1, :] * sb_buf[tile_at(r0 + s), :]
                return acc

            heads.append(lax.fori_loop(0, RET_DK // SUBLANES, cross,
                                       jnp.zeros((RET_DV, n_seq), F32)))
        oi = o[i] + jnp.concatenate(heads, axis=0).T
        on = _pair_groupnorm(oi, lo, rlng_ref[...], rlnb_ref[...])
        out_ref[i * n_seq:(i + 1) * n_seq, :] = (on * _silu(rows(g_ref[...], i))).astype(BF16)


def _ret_sample(q, k, v, g, state_t, pw, rln_g, rln_b, n_pos, n_seq):
    n_tok = n_pos * n_seq
    pair_state = 2 * RET_DK * RET_DV
    slab = pl.BlockSpec((n_tok, PAIR_LANES), lambda p: (0, p))
    state_spec = pl.BlockSpec((pair_state, n_seq), lambda p: (p, 0))
    return pl.pallas_call(
        functools.partial(_ret_sample_kernel, n_pos, n_seq),
        grid=(N_PAIRS,),
        in_specs=[slab, slab, slab, slab, state_spec,
                  pl.BlockSpec((8, PAIR_LANES), lambda p: (0, p)),
                  pl.BlockSpec((1, PAIR_LANES), lambda p: (0, p)),
                  pl.BlockSpec((1, PAIR_LANES), lambda p: (0, p))],
        out_specs=[slab, state_spec],
        out_shape=(jax.ShapeDtypeStruct((n_tok, RET_W), BF16),
                   jax.ShapeDtypeStruct(state_t.shape, F32)),
        scratch_shapes=[pltpu.VMEM((pair_state, n_seq), F32)]
        + [pltpu.VMEM((PAIR_LANES, n_tok), F32)] * 3,
        compiler_params=pltpu.CompilerParams(
            dimension_semantics=("arbitrary",), vmem_limit_bytes=VMEM_LIMIT),
        name="ret_sample",
    )(q, k, v, g, state_t, pw, rln_g.reshape(1, RET_W), rln_b.reshape(1, RET_W))


def _unpack_state(s2):
    blocks = []
    for p in range(N_PAIRS):
        for a in range(2):
            sl = slice(a * RET_DK, (a + 1) * RET_DK)
            blocks.append(s2[:, p, sl, sl])
    return jnp.stack(blocks, axis=1)


def _prompt_path(x, mod, mod_row0, g_mix, w_in, conv_w, conv_b, cln_g, cln_b, rln_g, rln_b,
                 w_out, g_ffn, w_ff1, w_ff2, g_final, tile=PROMPT_TILE):
    seq = x.shape[1]
    rope = _rope_tables(seq, 0)
    decay = _decay_tables(RET_CHUNK)
    y, hist, s2, *w_bf16 = _layer_prompt(x, mod, mod_row0, g_mix, w_in, conv_w, conv_b, cln_g,
                                         cln_b, rln_g, rln_b, w_out, g_ffn, w_ff1, w_ff2, g_final,
                                         rope, decay, tile)
    return y, hist[:, HIST_ROWS - (CONV_K - 1):], _unpack_state(s2), w_bf16


def _sample_path(x, mod, cache, state, pos0, g_mix, w_in, conv_w, conv_b, cln_g, cln_b,
                 rln_g, rln_b, w_out, g_ffn, w_ff1, w_ff2, g_final):
    n_seq, n_pos, d = x.shape
    xs = x.reshape(n_seq, n_pos * d)
    rope = _rope_tables(8, pos0)
    hist_t, *tok = _proj_sample(xs, mod, g_mix, w_in, cache.transpose(1, 0, 2),
                                conv_w, conv_b, cln_g, cln_b, rope)
    conv_out, q, k, v, g = [a.reshape(n_pos * n_seq, -1) for a in tok]
    ret_out, s_new_t = _ret_sample(q, k, v, g, state.reshape(n_seq, -1).T, _power_table(),
                                   rln_g, rln_b, n_pos, n_seq)
    s_new = s_new_t.T
    y = _mlp_sample(xs, conv_out, ret_out, w_out, mod, g_ffn, w_ff1, w_ff2, g_final)
    return y.reshape(x.shape), hist_t.transpose(1, 0, 2), s_new.reshape(state.shape)


def kernel(x_prompt, x_sample, cache_conv, state_ret, c_prompt, c_sample, w_ada, b_ada, g_mix, w_in, conv_w, conv_b, conv_ln_g, conv_ln_b, ret_ln_g, ret_ln_b, w_out, g_ffn, w_ff1, w_ff2, g_final):
    mod = _modulation(jnp.concatenate([c_sample, c_prompt], axis=0), w_ada[0], b_ada[0])
    params = (g_mix[0], w_in[0], conv_w[0], conv_b[0], conv_ln_g[0], conv_ln_b[0], ret_ln_g[0],
              ret_ln_b[0], w_out[0], g_ffn[0], w_ff1[0], w_ff2[0], g_final)
    y_p, conv_p, ret_p, (w_in_b, w_out_b, w_ff1_b, w_ff2_b) = _prompt_path(
        x_prompt, mod, x_sample.shape[0], *params)
    params_b = (g_mix[0], w_in_b, conv_w[0], conv_b[0], conv_ln_g[0], conv_ln_b[0], ret_ln_g[0],
                ret_ln_b[0], w_out_b, g_ffn[0], w_ff1_b, w_ff2_b, g_final)
    y_s, conv_s, ret_s = _sample_path(x_sample, mod, cache_conv[0], state_ret[0], PAST_LEN,
                                      *params_b)
    return (y_p, y_s, conv_p[None], ret_p[None], conv_s[None], ret_s[None])
```

```python
import functools

import jax
import jax.numpy as jnp
from jax import lax
from jax.experimental import pallas as pl
from jax.experimental.pallas import tpu as pltpu

D_MODEL = 1024
CONV_CH = 512
CONV_K = 31
RET_HEADS = 8
RET_W = 512
RET_DK = 64
RET_DV = 64
D_FF = 4 * D_MODEL
RET_CHUNK = 128
ROPE_THETA = 10000.0
EPS = 1e-6
N_MOD = 6
PAST_LEN = 16384

SUBLANES = 8
HIST_ROWS = 32
CONV_ROW_BLOCK = 32
PROMPT_TILE = 256
FFN_CHUNK = 512
SAMPLE_FFN_CHUNK = 1024
SAMPLE_SEQ_BLOCK = 64
WEIGHT_STAGE_ELEMS = 512 * 1024
WEIGHT_STAGE_SLOTS = 4
LAYER_PROGRAM = ("c cf cf cf cf cf cf cf cf "
                 "rrrr rfrf rfrf r "
                 "rfrf rfrf r")
PAIR_LANES = 2 * RET_DK
N_PAIRS = RET_HEADS // 2
VMEM_LIMIT = 56 * 1024 * 1024

F32 = jnp.float32
BF16 = jnp.bfloat16


def _log_decay(head):
    return jnp.log1p(-jnp.exp2(-5.0 - head))


def _rope_kernel(pos0, cos_ref, sin_ref):
    rows = cos_ref.shape[0]
    lane = lax.broadcasted_iota(jnp.int32, (rows, PAIR_LANES), 1)
    row = lax.broadcasted_iota(jnp.int32, (rows, PAIR_LANES), 0)
    half = RET_DK // 2
    freq = (lane % half).astype(F32)
    inv = jnp.power(jnp.full_like(freq, ROPE_THETA), -freq / half)
    ang = (row + pos0).astype(F32) * inv
    first = (lane % RET_DK) < half
    cos_ref[...] = jnp.cos(ang)
    sin_ref[...] = jnp.where(first, -jnp.sin(ang), jnp.sin(ang))


ROPE_BLOCK = 16


def _rope_blocked_kernel(pos0, cos_ref, sin_ref):
    rows = cos_ref.shape[0]
    n_blk = rows // ROPE_BLOCK
    half = RET_DK // 2

    def angles(n, step, start):
        lane = lax.broadcasted_iota(jnp.int32, (n, PAIR_LANES), 1)
        row = lax.broadcasted_iota(jnp.int32, (n, PAIR_LANES), 0)
        freq = (lane % half).astype(F32)
        inv = jnp.power(jnp.full_like(freq, ROPE_THETA), -freq / half)
        return (row * step + start).astype(F32) * inv

    a = angles(n_blk, ROPE_BLOCK, pos0)
    b = angles(ROPE_BLOCK, 1, 0)
    lane = lax.broadcasted_iota(jnp.int32, (ROPE_BLOCK, PAIR_LANES), 1)
    sign = jnp.where((lane % RET_DK) < half, -1.0, 1.0)
    cos_a, sin_a = jnp.cos(a), jnp.sin(a)
    cos_b, sin_b = jnp.cos(b), jnp.sin(b)
    cos_bs, sin_bs = cos_b * sign, sin_b * sign
    for blk in range(n_blk):
        ca, sa = cos_a[blk:blk + 1, :], sin_a[blk:blk + 1, :]
        rs = slice(blk * ROPE_BLOCK, (blk + 1) * ROPE_BLOCK)
        cos_ref[rs, :] = ca * cos_b - sa * sin_b
        sin_ref[rs, :] = sa * cos_bs + ca * sin_bs


def _rope_tables(rows, pos0):
    blocked = rows % ROPE_BLOCK == 0 and rows // ROPE_BLOCK >= SUBLANES
    return pl.pallas_call(
        functools.partial(_rope_blocked_kernel if blocked else _rope_kernel, pos0),
        out_shape=(jax.ShapeDtypeStruct((rows, PAIR_LANES), F32),
                   jax.ShapeDtypeStruct((rows, PAIR_LANES), F32)),
        name="rope_tables",
    )()


def _decay_kernel(chunk, dec_ref, qdec_ref, kdec_ref, sdec_ref):
    c = chunk
    shape = (N_PAIRS, c, 2 * c)
    col = lax.broadcasted_iota(jnp.int32, shape, 2)
    h = (2 * lax.broadcasted_iota(jnp.int32, shape, 0) + col // c).astype(F32)
    i = lax.broadcasted_iota(jnp.int32, shape, 1).astype(F32)
    j = (col % c).astype(F32)
    diff = i - j
    dec_ref[...] = jnp.where(diff >= 0, jnp.exp(_log_decay(h) * jnp.maximum(diff, 0.0)), 0.0)
    lane_h = (lax.broadcasted_iota(jnp.int32, (c, RET_W), 1) // RET_DK).astype(F32)
    idx = lax.broadcasted_iota(jnp.int32, (c, RET_W), 0).astype(F32)
    lg = _log_decay(lane_h)
    qdec_ref[...] = jnp.exp(lg * (idx + 1.0))
    kdec_ref[...] = jnp.exp(lg * (c - 1.0 - idx))
    p = lax.broadcasted_iota(jnp.int32, (N_PAIRS, PAIR_LANES, PAIR_LANES), 0)
    r = lax.broadcasted_iota(jnp.int32, (N_PAIRS, PAIR_LANES, PAIR_LANES), 1) // RET_DK
    q = lax.broadcasted_iota(jnp.int32, (N_PAIRS, PAIR_LANES, PAIR_LANES), 2) // RET_DK
    hh = (2 * p + r).astype(F32)
    sdec_ref[...] = jnp.where(r == q, jnp.exp(_log_decay(hh) * float(c)), 0.0)


def _decay_tables(chunk):
    return pl.pallas_call(
        functools.partial(_decay_kernel, chunk),
        out_shape=(jax.ShapeDtypeStruct((N_PAIRS, chunk, 2 * chunk), F32),
                   jax.ShapeDtypeStruct((chunk, RET_W), F32),
                   jax.ShapeDtypeStruct((chunk, RET_W), F32),
                   jax.ShapeDtypeStruct((N_PAIRS, PAIR_LANES, PAIR_LANES), F32)),
        name="decay_tables",
    )()


def _mod_kernel(c_ref, w_lo_ref, w_hi_ref, b_ref, o_ref):
    c = c_ref[...]
    s = (c * jax.nn.sigmoid(c)).astype(BF16)
    half = w_lo_ref.shape[1]
    for k, w_ref in enumerate((w_lo_ref, w_hi_ref)):
        cs = slice(k * half, (k + 1) * half)
        o_ref[:, cs] = _dot(s, w_ref[...].astype(BF16)) + b_ref[:, cs]


def _modulation(c, w_ada, b_ada):
    n, d = c.shape
    width = w_ada.shape[1]
    blk = D_MODEL
    half = lambda k: pl.BlockSpec((d, blk // 2), lambda i: (0, 2 * i + k))
    return pl.pallas_call(
        _mod_kernel,
        grid=(width // blk,),
        in_specs=[pl.BlockSpec((n, d), lambda i: (0, 0)), half(0), half(1),
                  pl.BlockSpec((1, blk), lambda i: (0, i))],
        out_specs=pl.BlockSpec((n, blk), lambda i: (0, i)),
        out_shape=jax.ShapeDtypeStruct((n, width), F32),
        compiler_params=pltpu.CompilerParams(vmem_limit_bytes=VMEM_LIMIT),
        name="adaln_mod",
    )(c, w_ada, w_ada, b_ada.reshape(1, width))


def _rmsnorm_mod(x, g, scale, shift):
    y = x * lax.rsqrt(jnp.mean(x * x, axis=-1, keepdims=True) + EPS)
    return (y * g) * (1.0 + scale) + shift


def _layernorm(x, g, b):
    mu = jnp.mean(x, axis=-1, keepdims=True)
    d = x - mu
    var = jnp.mean(d * d, axis=-1, keepdims=True)
    return d * lax.rsqrt(var + EPS) * g + b


def _silu(x):
    return x * jax.nn.sigmoid(x)


def _rotary_pair(x, cos, sin_signed, first_half):
    partner = jnp.where(first_half, pltpu.roll(x, PAIR_LANES - RET_DK // 2, 1),
                        pltpu.roll(x, RET_DK // 2, 1))
    return x * cos + partner * sin_signed


def _pair_groupnorm(o, lo, g, b):
    inv_n = 1.0 / RET_DV
    s_lo = jnp.sum(jnp.where(lo, o, 0.0), axis=-1, keepdims=True)
    s_hi = jnp.sum(jnp.where(lo, 0.0, o), axis=-1, keepdims=True)
    d = o - jnp.where(lo, s_lo, s_hi) * inv_n
    d2 = d * d
    v_lo = jnp.sum(jnp.where(lo, d2, 0.0), axis=-1, keepdims=True)
    v_hi = jnp.sum(jnp.where(lo, 0.0, d2), axis=-1, keepdims=True)
    var = jnp.where(lo, v_lo, v_hi) * inv_n
    return d * lax.rsqrt(var + EPS) * g + b


def _dot(a, b):
    return jnp.dot(a, b, preferred_element_type=F32)


def _dot_nt(a, b):
    return lax.dot_general(a, b, (((1,), (1,)), ((), ())), preferred_element_type=F32)


def _dot_tn(a, b):
    return lax.dot_general(a, b, (((0,), (0,)), ((), ())), preferred_element_type=F32)


def _stream_cast_weights(rings):
    def ring_chunks(pairs, stage):
        rows = stage.shape[1]
        return [(src, dst, r0) for src, dst in pairs for r0 in range(0, src.shape[0], rows)]

    chunks = [ring_chunks(pairs, stage) for pairs, stage, _ in rings]

    def copy(k, c):
        _, stage, sems = rings[k]
        src, _, r0 = chunks[k][c]
        slot = c % stage.shape[0]
        return pltpu.make_async_copy(src.at[pl.ds(r0, stage.shape[1]), :],
                                     stage.at[slot, :, pl.ds(0, src.shape[1])], sems.at[slot])

    for k, (_, stage, _) in enumerate(rings):
        for c in range(min(stage.shape[0] - 1, len(chunks[k]))):
            copy(k, c).start()
    for c in range(max(len(ch) for ch in chunks)):
        for k, (_, stage, _) in enumerate(rings):
            if c >= len(chunks[k]):
                continue
            n_slots, rows = stage.shape[0], stage.shape[1]
            if c + n_slots - 1 < len(chunks[k]):
                copy(k, c + n_slots - 1).start()
            copy(k, c).wait()
            src, dst, r0 = chunks[k][c]
            dst[r0:r0 + rows, :] = stage[c % n_slots, :, 0:src.shape[1]].astype(BF16)


def _layer_prompt_kernel(tile, nj, x_ref, mod_ref, gmix_ref, win_hbm, convw_ref, convb_ref,
                         clng_ref, clnb_ref, rlng_ref, rlnb_ref, wout_hbm,
                         cos_ref, sin_ref, dec_ref, qdec_ref, kdec_ref, sdec_ref,
                         mod2_ref, gffn_ref, w1_hbm, w2_hbm, gfin_ref,
                         y_ref, hist_ref, state_ref, win_out, wout_out, w1_out, w2_out,
                         ubuf, ushift, sbuf, mixbuf, x1buf,
                         win_ref, wout_ref, w1_ref, w2_ref,
                         stage_wide, stage_tall, sems_wide, sems_tall, sems_out):
    g = pl.program_id(0)
    total = pl.num_programs(0) - 1
    j = lax.rem(jnp.minimum(g, total - 1), nj)
    live = g < total
    export = [pltpu.make_async_copy(src, dst, sems_out.at[k]) for k, (src, dst) in enumerate(
        [(win_ref, win_out), (wout_ref, wout_out), (w1_ref, w1_out), (w2_ref, w2_out)])]

    @pl.when(g == 0)
    def _():
        _stream_cast_weights([
            ([(win_hbm, win_ref), (w1_hbm, w1_ref)], stage_wide, sems_wide),
            ([(wout_hbm, wout_ref), (w2_hbm, w2_ref)], stage_tall, sems_tall)])
        for cp in export:
            cp.start()
        x1buf[...] = jnp.zeros((tile, D_MODEL), F32)

    @pl.when(jnp.logical_and(j == 0, live))
    def _():
        ubuf[0:HIST_ROWS, :] = jnp.zeros((HIST_ROWS, CONV_CH), F32)
        sbuf[...] = jnp.zeros(sbuf.shape, F32)

    x = x_ref[0]
    b_mix = lax.div(jnp.minimum(g, total - 1), nj)
    b_mlp = lax.div(jnp.maximum(g - 1, 0), nj)
    mods = lambda ref, b: [ref[pl.ds(b, 1), k * D_MODEL:(k + 1) * D_MODEL] for k in range(3)]
    shift, scale, gate = mods(mod_ref, b_mix)

    def conv_stages(u):
        ubuf[HIST_ROWS:HIST_ROWS + tile, :] = u
        off = HIST_ROWS - (CONV_K - 1)
        for r in range(SUBLANES):
            n = tile + SUBLANES * ((CONV_K - 1 - r) // SUBLANES)
            ushift[r, 0:n, :] = ubuf[off + r:off + r + n, :]
        ubuf[0:HIST_ROWS, :] = ubuf[tile:tile + HIST_ROWS, :]
        yield
        for rb in range(tile // CONV_ROW_BLOCK):
            r0 = rb * CONV_ROW_BLOCK
            dw = jnp.broadcast_to(convb_ref[...], (CONV_ROW_BLOCK, CONV_CH))
            for k in range(CONV_K):
                s0 = r0 + SUBLANES * (k // SUBLANES)
                wk = jnp.concatenate([convw_ref[k]] * (CONV_ROW_BLOCK // SUBLANES), axis=0)
                dw = dw + wk * ushift[k % SUBLANES, s0:s0 + CONV_ROW_BLOCK, :]
            conv_out = _silu(_layernorm(dw, clng_ref[...], clnb_ref[...]))
            mixbuf[r0:r0 + CONV_ROW_BLOCK, 0:CONV_CH] = conv_out.astype(BF16)
            yield

    def ret_stages(proj):
        ret_proj = []
        for k in range(4):
            ret_proj.append(proj(2 + k))
            yield
        q, kk, v, gt = ret_proj
        q = q * (RET_DK ** -0.5)
        lane = lax.broadcasted_iota(jnp.int32, (1, PAIR_LANES), 1)
        lo = lane < RET_DK
        first_half = (lane % RET_DK) < (RET_DK // 2)
        lo_b = lo.astype(BF16)
        hi_b = (~lo).astype(BF16)
        pairs = range(N_PAIRS)
        halves = (pairs[:N_PAIRS // 2], pairs[N_PAIRS // 2:])
        lanes = [slice(p * PAIR_LANES, (p + 1) * PAIR_LANES) for p in pairs]
        for c in range(tile // RET_CHUNK):
            rs = slice(c * RET_CHUNK, (c + 1) * RET_CHUNK)
            cos = cos_ref[rs, :]
            sin = sin_ref[rs, :]
            qc, kc, vc, scores, outs = {}, {}, {}, {}, {}
            for group in halves:
                for p in group:
                    qc[p] = _rotary_pair(q[rs, lanes[p]], cos, sin, first_half)
                    kc[p] = _rotary_pair(kk[rs, lanes[p]], cos, sin, first_half)
                    vc[p] = v[rs, lanes[p]].astype(BF16)
                    kb = kc[p].astype(BF16)
                    k_cat = jnp.concatenate([kb * lo_b, kb * hi_b], axis=0)
                    scores[p] = _dot_nt(qc[p].astype(BF16), k_cat)
                yield
            for group in halves:
                for p in group:
                    state = sbuf[p]
                    lhs = jnp.concatenate([(scores[p] * dec_ref[p]).astype(BF16),
                                           (qc[p] * qdec_ref[:, lanes[p]]).astype(BF16)], axis=1)
                    rhs = jnp.concatenate([vc[p] * lo_b, vc[p] * hi_b, state.astype(BF16)],
                                          axis=0)
                    outs[p] = _dot(lhs, rhs)
                    kd = (kc[p] * kdec_ref[:, lanes[p]]).astype(BF16)
                    sbuf[p] = (state * sdec_ref[p]
                               + jnp.where(sdec_ref[p] > 0.0, _dot_tn(kd, vc[p]), 0.0))
                yield
            for p in pairs:
                on = _pair_groupnorm(outs[p], lo, rlng_ref[:, lanes[p]], rlnb_ref[:, lanes[p]])
                mixbuf[rs, CONV_CH + p * PAIR_LANES:CONV_CH + (p + 1) * PAIR_LANES] = (
                    on * _silu(gt[rs, lanes[p]])).astype(BF16)
            yield

    ffn = _ffn_stages(x1buf[...], mods(mod2_ref, b_mlp), gffn_ref, w1_ref, w2_ref, gfin_ref, y_ref)
    next(ffn)
    hb = _rmsnorm_mod(x, gmix_ref[...], scale, shift).astype(BF16)

    def proj(k):
        return _dot(hb, win_ref[:, k * CONV_CH:(k + 1) * CONV_CH])

    streams = {"f": ffn, "c": conv_stages(proj(0) * jax.nn.sigmoid(proj(1))),
               "r": ret_stages(proj)}
    for name in LAYER_PROGRAM.replace(" ", ""):
        next(streams[name], None)
    for stream in (streams["c"], streams["r"], streams["f"]):
        for _ in stream:
            pass

    mix = _dot(mixbuf[...], wout_ref[...])
    x1buf[...] = x + gate * mix

    @pl.when(jnp.logical_and(j == nj - 1, live))
    def _():
        hist_ref[0] = ubuf[0:HIST_ROWS, :]
        state_ref[0] = sbuf[...]

    @pl.when(g == total)
    def _():
        for cp in export:
            cp.wait()


def _layer_prompt(x, mod, mod_row0, g_mix, w_in, conv_w, conv_b, cln_g, cln_b, rln_g, rln_b,
                  w_out, g_ffn, w1, w2, g_final, rope, decay, tile):
    bsz, seq, d = x.shape
    assert bsz % SUBLANES == 0 and mod_row0 % bsz == 0
    in_hbm = pl.BlockSpec(memory_space=pl.ANY)

    def stage_shape(*ws):
        width = max(w.shape[1] for w in ws)
        rows = WEIGHT_STAGE_ELEMS // width
        assert all(w.shape[0] % rows == 0 for w in ws)
        return pltpu.VMEM((WEIGHT_STAGE_SLOTS, rows, width), F32)

    mod_blk = mod_row0 // bsz
    nj = seq // tile
    total = bsz * nj
    cos, sin = rope
    dec, qdec, kdec, sdec = decay
    mix_tile = lambda g: jnp.minimum(g, total - 1)
    ffn_tile = lambda g: jnp.maximum(g - 1, 0)
    resident = lambda shape: pl.BlockSpec(shape, lambda g: (0,) * len(shape),
                                          pipeline_mode=pl.Buffered(1))
    row = lambda a: a.reshape(1, -1)
    conv_w_rows = jnp.broadcast_to(conv_w[:, None, :], (CONV_K, SUBLANES, CONV_CH))
    rope_spec = pl.BlockSpec((tile, PAIR_LANES), lambda g: (mix_tile(g) % nj, 0))
    return pl.pallas_call(
        functools.partial(_layer_prompt_kernel, tile, nj),
        grid=(total + 1,),
        in_specs=[
            pl.BlockSpec((1, tile, d), lambda g: (mix_tile(g) // nj, mix_tile(g) % nj, 0)),
            pl.BlockSpec((bsz, 3 * d), lambda g: (mod_blk, 0), pipeline_mode=pl.Buffered(1)),
            resident((1, d)),
            in_hbm,
            resident((CONV_K, SUBLANES, CONV_CH)),
            resident((1, CONV_CH)),
            resident((1, CONV_CH)),
            resident((1, CONV_CH)),
            resident((1, RET_W)),
            resident((1, RET_W)),
            in_hbm,
            rope_spec,
            rope_spec,
            resident(dec.shape),
            resident(qdec.shape),
            resident(kdec.shape),
            resident(sdec.shape),
            pl.BlockSpec((bsz, 3 * d), lambda g: (mod_blk, 1), pipeline_mode=pl.Buffered(1)),
            resident((1, d)),
            in_hbm,
            in_hbm,
            resident((1, d)),
        ],
        out_specs=[
            pl.BlockSpec((1, tile, d), lambda g: (ffn_tile(g) // nj, ffn_tile(g) % nj, 0)),
            pl.BlockSpec((1, HIST_ROWS, CONV_CH), lambda g: (mix_tile(g) // nj, 0, 0)),
            pl.BlockSpec((1, N_PAIRS, PAIR_LANES, PAIR_LANES),
                         lambda g: (mix_tile(g) // nj, 0, 0, 0)),
            in_hbm, in_hbm, in_hbm, in_hbm,
        ],
        out_shape=(
            jax.ShapeDtypeStruct((bsz, seq, d), F32),
            jax.ShapeDtypeStruct((bsz, HIST_ROWS, CONV_CH), F32),
            jax.ShapeDtypeStruct((bsz, N_PAIRS, PAIR_LANES, PAIR_LANES), F32),
        ) + tuple(jax.ShapeDtypeStruct(w.shape, BF16) for w in (w_in, w_out, w1, w2)),
        scratch_shapes=[
            pltpu.VMEM((HIST_ROWS + tile, CONV_CH), F32),
            pltpu.VMEM((SUBLANES, HIST_ROWS + tile, CONV_CH), F32),
            pltpu.VMEM((N_PAIRS, PAIR_LANES, PAIR_LANES), F32),
            pltpu.VMEM((tile, D_MODEL), BF16),
            pltpu.VMEM((tile, D_MODEL), F32),
            pltpu.VMEM(w_in.shape, BF16),
            pltpu.VMEM(w_out.shape, BF16),
            pltpu.VMEM(w1.shape, BF16),
            pltpu.VMEM(w2.shape, BF16),
            stage_shape(w_in, w1),
            stage_shape(w_out, w2),
            pltpu.SemaphoreType.DMA((WEIGHT_STAGE_SLOTS,)),
            pltpu.SemaphoreType.DMA((WEIGHT_STAGE_SLOTS,)),
            pltpu.SemaphoreType.DMA((4,)),
        ],
        compiler_params=pltpu.CompilerParams(
            dimension_semantics=("arbitrary",), vmem_limit_bytes=VMEM_LIMIT),
        name="layer_prompt",
    )(x, mod, row(g_mix), w_in, conv_w_rows, row(conv_b), row(cln_g), row(cln_b),
      row(rln_g), row(rln_b), w_out, cos, sin, dec, qdec, kdec, sdec,
      mod, row(g_ffn), w1, w2, row(g_final))


def _ffn_stages(x, mods, gffn_ref, w1_ref, w2_ref, gfin_ref, y_ref):
    shift, scale, gate = mods
    hb = _rmsnorm_mod(x, gffn_ref[...], scale, shift).astype(BF16)
    yield
    half = D_MODEL // 2
    ff = [None, None]
    for c in range(D_FF // (2 * FFN_CHUNK)):
        acts = []
        for h in range(2):
            c0 = (2 * c + h) * FFN_CHUNK
            a = jnp.maximum(_dot(hb, w1_ref[:, c0:c0 + FFN_CHUNK]), 0.0)
            acts.append((a * a).astype(BF16))
            yield
        act = jnp.concatenate(acts, axis=1)
        for n in range(2):
            part = _dot(act, w2_ref[2 * c * FFN_CHUNK:2 * (c + 1) * FFN_CHUNK,
                                    n * half:(n + 1) * half])
            ff[n] = part if ff[n] is None else ff[n] + part
            yield
    z = x + gate * jnp.concatenate(ff, axis=1)
    y_ref[0] = z * lax.rsqrt(jnp.mean(z * z, axis=-1, keepdims=True) + EPS) * gfin_ref[...]


def _pos_cols(x_ref, i):
    return x_ref[:, i * D_MODEL:(i + 1) * D_MODEL]


def _mod_cols(mod_ref, n_seq, k):
    return mod_ref[0:n_seq, k * D_MODEL:(k + 1) * D_MODEL]


def _mlp_sample_kernel(n_pos, n_seq, x_ref, conv_ref, ret_ref, wout_ref, mod_ref,
                       gffn_ref, w1_ref, w2_ref, gfin_ref, y_ref, x1buf, hbuf, acc):
    c = pl.program_id(0)
    rows = lambda i: slice(i * n_seq, (i + 1) * n_seq)

    @pl.when(c == 0)
    def _():
        mix = (_dot(conv_ref[...], wout_ref[0:CONV_CH, :].astype(BF16))
               + _dot(ret_ref[...], wout_ref[CONV_CH:CONV_CH + RET_W, :].astype(BF16)))
        for i in range(n_pos):
            x1 = _pos_cols(x_ref, i) + _mod_cols(mod_ref, n_seq, 2) * mix[rows(i)]
            x1buf[rows(i), :] = x1
            hbuf[rows(i), :] = _rmsnorm_mod(x1, gffn_ref[...], _mod_cols(mod_ref, n_seq, 4),
                                            _mod_cols(mod_ref, n_seq, 3)).astype(BF16)

    a = jnp.maximum(_dot(hbuf[...], w1_ref[...].astype(BF16)), 0.0)
    part = _dot((a * a).astype(BF16), w2_ref[...].astype(BF16))

    @pl.when(c == 0)
    def _():
        acc[...] = part

    @pl.when(c > 0)
    def _():
        acc[...] += part

    @pl.when(c == pl.num_programs(0) - 1)
    def _():
        for i in range(n_pos):
            z = x1buf[rows(i), :] + _mod_cols(mod_ref, n_seq, 5) * acc[rows(i), :]
            y_ref[:, i * D_MODEL:(i + 1) * D_MODEL] = (
                z * lax.rsqrt(jnp.mean(z * z, axis=-1, keepdims=True) + EPS) * gfin_ref[...])


def _mlp_sample(x, conv_out, ret_out, w_out, mod, g_ffn, w1, w2, g_final):
    n_seq, d = x.shape[0], D_MODEL
    n_pos = x.shape[1] // d
    n_tok = n_pos * n_seq
    n_chunks = D_FF // SAMPLE_FFN_CHUNK
    whole = lambda shape: pl.BlockSpec(shape, lambda c: (0,) * len(shape),
                                       pipeline_mode=pl.Buffered(1))
    return pl.pallas_call(
        functools.partial(_mlp_sample_kernel, n_pos, n_seq),
        grid=(n_chunks,),
        in_specs=[whole(x.shape), whole(conv_out.shape), whole(ret_out.shape),
                  whole(w_out.shape), whole(mod.shape), whole((1, d)),
                  pl.BlockSpec((d, SAMPLE_FFN_CHUNK), lambda c: (0, c)),
                  pl.BlockSpec((SAMPLE_FFN_CHUNK, d), lambda c: (c, 0)),
                  whole((1, d))],
        out_specs=pl.BlockSpec(x.shape, lambda c: (0, 0)),
        out_shape=jax.ShapeDtypeStruct(x.shape, F32),
        scratch_shapes=[pltpu.VMEM((n_tok, d), F32), pltpu.VMEM((n_tok, d), BF16),
                        pltpu.VMEM((n_tok, d), F32)],
        compiler_params=pltpu.CompilerParams(
            dimension_semantics=("arbitrary",), vmem_limit_bytes=VMEM_LIMIT),
        name="mlp_sample",
    )(x, conv_out, ret_out, w_out, mod, g_ffn.reshape(1, d), w1, w2, g_final.reshape(1, d))


def _power_table_kernel(o_ref):
    lane_h = (lax.broadcasted_iota(jnp.int32, o_ref.shape, 1) // RET_DK).astype(F32)
    n = lax.broadcasted_iota(jnp.int32, o_ref.shape, 0).astype(F32)
    o_ref[...] = jnp.exp(_log_decay(lane_h) * n)


def _power_table():
    return pl.pallas_call(_power_table_kernel,
                          out_shape=jax.ShapeDtypeStruct((8, RET_W), F32),
                          name="decay_powers")()


def _proj_sample_kernel(n_pos, n_seq, x_ref, mod_ref, gmix_ref, win_ref, cache_ref, convw_ref,
                        convb_ref, clng_ref, clnb_ref, cos_ref, sin_ref,
                        hist_ref, conv_ref, q_ref, k_ref, v_ref, g_ref):
    shift, scale = _mod_cols(mod_ref, n_seq, 0), _mod_cols(mod_ref, n_seq, 1)
    hb = jnp.concatenate(
        [_rmsnorm_mod(_pos_cols(x_ref, i), gmix_ref[...], scale, shift).astype(BF16)
         for i in range(n_pos)], axis=0)

    def proj(k):
        return _dot(hb, win_ref[:, k * CONV_CH:(k + 1) * CONV_CH].astype(BF16))

    u = proj(0) * jax.nn.sigmoid(proj(1))
    n_hist = CONV_K - 1
    rows = lambda a, i: a[i * n_seq:(i + 1) * n_seq]
    for m in range(n_hist - n_pos):
        hist_ref[m] = cache_ref[m + n_pos]
    for i in range(n_pos):
        hist_ref[n_hist - n_pos + i] = rows(u, i)
        dw = jnp.broadcast_to(convb_ref[...], (n_seq, CONV_CH))
        for k in range(CONV_K):
            m = i + k
            src = cache_ref[m] if m < n_hist else rows(u, m - n_hist)
            dw = dw + convw_ref[k:k + 1, :] * src
        conv_ref[i] = _silu(_layernorm(dw, clng_ref[...], clnb_ref[...])).astype(BF16)

    lane = lax.broadcasted_iota(jnp.int32, (1, PAIR_LANES), 1)
    first_half = (lane % RET_DK) < (RET_DK // 2)
    q = proj(2) * (RET_DK ** -0.5)
    kk = proj(3)
    v = proj(4)
    gt = proj(5)
    for i in range(n_pos):
        cos = cos_ref[i:i + 1, :]
        sin = sin_ref[i:i + 1, :]
        for p in range(N_PAIRS):
            sl = slice(p * PAIR_LANES, (p + 1) * PAIR_LANES)
            q_ref[i, :, sl] = _rotary_pair(rows(q, i)[:, sl], cos, sin, first_half)
            k_ref[i, :, sl] = _rotary_pair(rows(kk, i)[:, sl], cos, sin, first_half)
        v_ref[i] = rows(v, i)
        g_ref[i] = rows(gt, i)


def _proj_sample(x, mod, g_mix, w_in, cache_t, conv_w, conv_b, cln_g, cln_b, rope):
    n_seq, d = x.shape[0], D_MODEL
    n_pos = x.shape[1] // d
    blk = SAMPLE_SEQ_BLOCK
    cos, sin = rope
    row = lambda a: a.reshape(1, -1)
    whole = lambda a: pl.BlockSpec(a.shape, lambda s: (0,) * a.ndim, pipeline_mode=pl.Buffered(1))
    seqs = lambda lead, width: pl.BlockSpec((lead, blk, width), lambda s: (0, s, 0))
    tok = lambda dt: jax.ShapeDtypeStruct((n_pos, n_seq, RET_W), dt)
    args = (x, mod, row(g_mix), w_in, cache_t, conv_w, row(conv_b), row(cln_g), row(cln_b),
            cos, sin)
    return pl.pallas_call(
        functools.partial(_proj_sample_kernel, n_pos, blk),
        grid=(n_seq // blk,),
        in_specs=[pl.BlockSpec((blk, n_pos * d), lambda s: (s, 0)),
                  pl.BlockSpec((blk, mod.shape[1]), lambda s: (s, 0)),
                  whole(args[2]), whole(w_in), seqs(cache_t.shape[0], CONV_CH)]
        + [whole(a) for a in args[5:]],
        out_specs=[seqs(cache_t.shape[0], CONV_CH)] + [seqs(n_pos, RET_W)] * 5,
        out_shape=(jax.ShapeDtypeStruct(cache_t.shape, F32), tok(BF16),
                   tok(F32), tok(F32), tok(F32), tok(F32)),
        compiler_params=pltpu.CompilerParams(
            dimension_semantics=("arbitrary",), vmem_limit_bytes=VMEM_LIMIT),
        name="proj_sample",
    )(*args)


def _ret_sample_kernel(n_pos, n_seq, q_ref, k_ref, v_ref, g_ref, st_ref, pw_ref, rlng_ref,
                       rlnb_ref, out_ref, snew_ref, sb_buf, qd_buf, kd_buf, v_buf):
    lane = lax.broadcasted_iota(jnp.int32, (1, PAIR_LANES), 1)
    lo = lane < RET_DK
    rows = lambda a, i: a[i * n_seq:(i + 1) * n_seq]
    cols = lambda a, i: a[:, i * n_seq:(i + 1) * n_seq]
    pw = pw_ref[...]
    q = q_ref[...]
    k = k_ref[...]
    qb = [rows(q, i).astype(BF16) for i in range(n_pos)]
    kb = [rows(k, i).astype(BF16) for i in range(n_pos)]
    vb = [rows(v_ref[...], i).astype(BF16) for i in range(n_pos)]

    o = []
    for i in range(n_pos):
        acc = jnp.zeros((n_seq, PAIR_LANES), F32)
        for j in range(i + 1):
            prod = qb[i].astype(F32) * kb[j].astype(F32)
            s_lo = jnp.sum(jnp.where(lo, prod, 0.0), axis=-1, keepdims=True)
            s_hi = jnp.sum(jnp.where(lo, 0.0, prod), axis=-1, keepdims=True)
            score = jnp.where(lo, s_lo, s_hi) * pw[i - j:i - j + 1, :]
            acc = acc + score.astype(BF16).astype(F32) * vb[j].astype(F32)
        o.append(acc)

    rounded = lambda x: x.astype(BF16).astype(F32)
    qd_buf[...] = jnp.concatenate(
        [rounded(rows(q, i) * pw[i + 1:i + 2, :]) for i in range(n_pos)], axis=0).T
    kd_buf[...] = jnp.concatenate(
        [rounded(rows(k, j) * pw[n_pos - 1 - j:n_pos - j, :]) for j in range(n_pos)], axis=0).T
    v_buf[...] = jnp.concatenate([vb[j].astype(F32) for j in range(n_pos)], axis=0).T
    tile_at = lambda r: pl.ds(pl.multiple_of(r * RET_DV, RET_DV), RET_DV)
    pos = lambda i: slice(i * n_seq, (i + 1) * n_seq)

    for a in range(2):
        gam = pw[n_pos:n_pos + 1, a * RET_DK:a * RET_DK + 1]
        v_rows = slice(a * RET_DV, (a + 1) * RET_DV)

        def update(grp, carry, a=a, gam=gam, v_rows=v_rows):
            r0 = pl.multiple_of(a * RET_DK + grp * SUBLANES, SUBLANES)
            kd_rows = kd_buf[pl.ds(r0, SUBLANES), :]
            for s in range(SUBLANES):
                r = r0 + s
                s_old = st_ref[tile_at(r), :]
                sb_buf[tile_at(r), :] = rounded(s_old)
                s_new = s_old * gam
                for j in range(n_pos):
                    s_new = s_new + kd_rows[s:s + 1, pos(j)] * v_buf[v_rows, pos(j)]
                snew_ref[tile_at(r), :] = s_new
            return carry

        lax.fori_loop(0, RET_DK // SUBLANES, update, 0)

    for i in range(n_pos):
        heads = []
        for a in range(2):
            def cross(grp, acc, a=a, i=i):
                r0 = pl.multiple_of(a * RET_DK + grp * SUBLANES, SUBLANES)
                qd_rows = qd_buf[pl.ds(r0, SUBLANES), pos(i)]
                for s in range(SUBLANES):
                    acc = acc + qd_rows[s:s + 1, :] * sb_buf[tile_at(r0 + s), :]
                return acc

            heads.append(lax.fori_loop(0, RET_DK // SUBLANES, cross,
                                       jnp.zeros((RET_DV, n_seq), F32)))
        oi = o[i] + jnp.concatenate(heads, axis=0).T
        on = _pair_groupnorm(oi, lo, rlng_ref[...], rlnb_ref[...])
        out_ref[i * n_seq:(i + 1) * n_seq, :] = (on * _silu(rows(g_ref[...], i))).astype(BF16)


def _ret_sample(q, k, v, g, state_t, pw, rln_g, rln_b, n_pos, n_seq):
    n_tok = n_pos * n_seq
    pair_state = 2 * RET_DK * RET_DV
    slab = pl.BlockSpec((n_tok, PAIR_LANES), lambda p: (0, p))
    state_spec = pl.BlockSpec((pair_state, n_seq), lambda p: (p, 0))
    return pl.pallas_call(
        functools.partial(_ret_sample_kernel, n_pos, n_seq),
        grid=(N_PAIRS,),
        in_specs=[slab, slab, slab, slab, state_spec,
                  pl.BlockSpec((8, PAIR_LANES), lambda p: (0, p)),
                  pl.BlockSpec((1, PAIR_LANES), lambda p: (0, p)),
                  pl.BlockSpec((1, PAIR_LANES), lambda p: (0, p))],
        out_specs=[slab, state_spec],
        out_shape=(jax.ShapeDtypeStruct((n_tok, RET_W), BF16),
                   jax.ShapeDtypeStruct(state_t.shape, F32)),
        scratch_shapes=[pltpu.VMEM((pair_state, n_seq), F32)]
        + [pltpu.VMEM((PAIR_LANES, n_tok), F32)] * 3,
        compiler_params=pltpu.CompilerParams(
            dimension_semantics=("arbitrary",), vmem_limit_bytes=VMEM_LIMIT),
        name="ret_sample",
    )(q, k, v, g, state_t, pw, rln_g.reshape(1, RET_W), rln_b.reshape(1, RET_W))


def _unpack_state(s2):
    blocks = []
    for p in range(N_PAIRS):
        for a in range(2):
            sl = slice(a * RET_DK, (a + 1) * RET_DK)
            blocks.append(s2[:, p, sl, sl])
    return jnp.stack(blocks, axis=1)


def _prompt_path(x, mod, mod_row0, g_mix, w_in, conv_w, conv_b, cln_g, cln_b, rln_g, rln_b,
                 w_out, g_ffn, w_ff1, w_ff2, g_final, tile=PROMPT_TILE):
    seq = x.shape[1]
    rope = _rope_tables(seq, 0)
    decay = _decay_tables(RET_CHUNK)
    y, hist, s2, *w_bf16 = _layer_prompt(x, mod, mod_row0, g_mix, w_in, conv_w, conv_b, cln_g,
                                         cln_b, rln_g, rln_b, w_out, g_ffn, w_ff1, w_ff2, g_final,
                                         rope, decay, tile)
    return y, hist[:, HIST_ROWS - (CONV_K - 1):], _unpack_state(s2), w_bf16


def _sample_path(x, mod, cache, state, pos0, g_mix, w_in, conv_w, conv_b, cln_g, cln_b,
                 rln_g, rln_b, w_out, g_ffn, w_ff1, w_ff2, g_final):
    n_seq, n_pos, d = x.shape
    xs = x.reshape(n_seq, n_pos * d)
    rope = _rope_tables(8, pos0)
    hist_t, *tok = _proj_sample(xs, mod, g_mix, w_in, cache.transpose(1, 0, 2),
                                conv_w, conv_b, cln_g, cln_b, rope)
    conv_out, q, k, v, g = [a.reshape(n_pos * n_seq, -1) for a in tok]
    ret_out, s_new_t = _ret_sample(q, k, v, g, state.reshape(n_seq, -1).T, _power_table(),
                                   rln_g, rln_b, n_pos, n_seq)
    s_new = s_new_t.T
    y = _mlp_sample(xs, conv_out, ret_out, w_out, mod, g_ffn, w_ff1, w_ff2, g_final)
    return y.reshape(x.shape), hist_t.transpose(1, 0, 2), s_new.reshape(state.shape)


def kernel(x_prompt, x_sample, cache_conv, state_ret, c_prompt, c_sample, w_ada, b_ada, g_mix, w_in, conv_w, conv_b, conv_ln_g, conv_ln_b, ret_ln_g, ret_ln_b, w_out, g_ffn, w_ff1, w_ff2, g_final):
    mod = _modulation(jnp.concatenate([c_sample, c_prompt], axis=0), w_ada[0], b_ada[0])
    params = (g_mix[0], w_in[0], conv_w[0], conv_b[0], conv_ln_g[0], conv_ln_b[0], ret_ln_g[0],
              ret_ln_b[0], w_out[0], g_ffn[0], w_ff1[0], w_ff2[0], g_final)
    y_p, conv_p, ret_p, (w_in_b, w_out_b, w_ff1_b, w_ff2_b) = _prompt_path(
        x_prompt, mod, x_sample.shape[0], *params)
    params_b = (g_mix[0], w_in_b, conv_w[0], conv_b[0], conv_ln_g[0], conv_ln_b[0], ret_ln_g[0],
                ret_ln_b[0], w_out_b, g_ffn[0], w_ff1_b, w_ff2_b, g_final)
    y_s, conv_s, ret_s = _sample_path(x_sample, mod, cache_conv[0], state_ret[0], PAST_LEN,
                                      *params_b)
    return (y_p, y_s, conv_p[None], ret_p[None], conv_s[None], ret_s[None])
```

```python
import functools

import jax
import jax.numpy as jnp
from jax import lax
from jax.experimental import pallas as pl
from jax.experimental.pallas import tpu as pltpu

D_MODEL = 1024
CONV_CH = 512
CONV_K = 31
RET_HEADS = 8
RET_W = 512
RET_DK = 64
RET_DV = 64
D_FF = 4 * D_MODEL
RET_CHUNK = 128
ROPE_THETA = 10000.0
EPS = 1e-6
N_MOD = 6
PAST_LEN = 16384

SUBLANES = 8
HIST_ROWS = 32
CONV_ROW_BLOCK = 32
PROMPT_TILE = 256
FFN_CHUNK = 512
SAMPLE_FFN_CHUNK = 2048
SAMPLE_SEQ_BLOCK = 64
WEIGHT_STAGE_ELEMS = 512 * 1024
WEIGHT_STAGE_SLOTS = 3
LAYER_PROGRAM = ("c cf cf cf cf cf cf cf cf "
                 "rrrr rfrf rfrf r "
                 "rfrf rfrf r")
PAIR_LANES = 2 * RET_DK
N_PAIRS = RET_HEADS // 2
VMEM_LIMIT = 56 * 1024 * 1024

F32 = jnp.float32
BF16 = jnp.bfloat16


def _log_decay(head):
    return jnp.log1p(-jnp.exp2(-5.0 - head))


def _rope_kernel(pos0, cos_ref, sin_ref):
    rows = cos_ref.shape[0]
    lane = lax.broadcasted_iota(jnp.int32, (rows, PAIR_LANES), 1)
    row = lax.broadcasted_iota(jnp.int32, (rows, PAIR_LANES), 0)
    half = RET_DK // 2
    freq = (lane % half).astype(F32)
    inv = jnp.power(jnp.full_like(freq, ROPE_THETA), -freq / half)
    ang = (row + pos0).astype(F32) * inv
    first = (lane % RET_DK) < half
    cos_ref[...] = jnp.cos(ang)
    sin_ref[...] = jnp.where(first, -jnp.sin(ang), jnp.sin(ang))


ROPE_BLOCK = 16


def _rope_blocked_kernel(pos0, cos_ref, sin_ref):
    rows = cos_ref.shape[0]
    n_blk = rows // ROPE_BLOCK
    half = RET_DK // 2

    def angles(n, step, start):
        lane = lax.broadcasted_iota(jnp.int32, (n, PAIR_LANES), 1)
        row = lax.broadcasted_iota(jnp.int32, (n, PAIR_LANES), 0)
        freq = (lane % half).astype(F32)
        inv = jnp.power(jnp.full_like(freq, ROPE_THETA), -freq / half)
        return (row * step + start).astype(F32) * inv

    a = angles(n_blk, ROPE_BLOCK, pos0)
    b = angles(ROPE_BLOCK, 1, 0)
    lane = lax.broadcasted_iota(jnp.int32, (ROPE_BLOCK, PAIR_LANES), 1)
    sign = jnp.where((lane % RET_DK) < half, -1.0, 1.0)
    cos_a, sin_a = jnp.cos(a), jnp.sin(a)
    cos_b, sin_b = jnp.cos(b), jnp.sin(b)
    cos_bs, sin_bs = cos_b * sign, sin_b * sign
    for blk in range(n_blk):
        ca, sa = cos_a[blk:blk + 1, :], sin_a[blk:blk + 1, :]
        rs = slice(blk * ROPE_BLOCK, (blk + 1) * ROPE_BLOCK)
        cos_ref[rs, :] = ca * cos_b - sa * sin_b
        sin_ref[rs, :] = sa * cos_bs + ca * sin_bs


def _rope_tables(rows, pos0):
    blocked = rows % ROPE_BLOCK == 0 and rows // ROPE_BLOCK >= SUBLANES
    return pl.pallas_call(
        functools.partial(_rope_blocked_kernel if blocked else _rope_kernel, pos0),
        out_shape=(jax.ShapeDtypeStruct((rows, PAIR_LANES), F32),
                   jax.ShapeDtypeStruct((rows, PAIR_LANES), F32)),
        name="rope_tables",
    )()


def _decay_kernel(chunk, dec_ref, qdec_ref, kdec_ref, sdec_ref):
    c = chunk
    shape = (N_PAIRS, c, 2 * c)
    col = lax.broadcasted_iota(jnp.int32, shape, 2)
    h = (2 * lax.broadcasted_iota(jnp.int32, shape, 0) + col // c).astype(F32)
    i = lax.broadcasted_iota(jnp.int32, shape, 1).astype(F32)
    j = (col % c).astype(F32)
    diff = i - j
    dec_ref[...] = jnp.where(diff >= 0, jnp.exp(_log_decay(h) * jnp.maximum(diff, 0.0)), 0.0)
    lane_h = (lax.broadcasted_iota(jnp.int32, (c, RET_W), 1) // RET_DK).astype(F32)
    idx = lax.broadcasted_iota(jnp.int32, (c, RET_W), 0).astype(F32)
    lg = _log_decay(lane_h)
    qdec_ref[...] = jnp.exp(lg * (idx + 1.0))
    kdec_ref[...] = jnp.exp(lg * (c - 1.0 - idx))
    p = lax.broadcasted_iota(jnp.int32, (N_PAIRS, PAIR_LANES, PAIR_LANES), 0)
    r = lax.broadcasted_iota(jnp.int32, (N_PAIRS, PAIR_LANES, PAIR_LANES), 1) // RET_DK
    q = lax.broadcasted_iota(jnp.int32, (N_PAIRS, PAIR_LANES, PAIR_LANES), 2) // RET_DK
    hh = (2 * p + r).astype(F32)
    sdec_ref[...] = jnp.where(r == q, jnp.exp(_log_decay(hh) * float(c)), 0.0)


def _decay_tables(chunk):
    return pl.pallas_call(
        functools.partial(_decay_kernel, chunk),
        out_shape=(jax.ShapeDtypeStruct((N_PAIRS, chunk, 2 * chunk), F32),
                   jax.ShapeDtypeStruct((chunk, RET_W), F32),
                   jax.ShapeDtypeStruct((chunk, RET_W), F32),
                   jax.ShapeDtypeStruct((N_PAIRS, PAIR_LANES, PAIR_LANES), F32)),
        name="decay_tables",
    )()


def _mod_kernel(c_ref, w_lo_ref, w_hi_ref, b_ref, o_ref):
    c = c_ref[...]
    s = (c * jax.nn.sigmoid(c)).astype(BF16)
    half = w_lo_ref.shape[1]
    for k, w_ref in enumerate((w_lo_ref, w_hi_ref)):
        cs = slice(k * half, (k + 1) * half)
        o_ref[:, cs] = _dot(s, w_ref[...].astype(BF16)) + b_ref[:, cs]


def _modulation(c, w_ada, b_ada):
    n, d = c.shape
    width = w_ada.shape[1]
    blk = D_MODEL
    half = lambda k: pl.BlockSpec((d, blk // 2), lambda i: (0, 2 * i + k))
    return pl.pallas_call(
        _mod_kernel,
        grid=(width // blk,),
        in_specs=[pl.BlockSpec((n, d), lambda i: (0, 0)), half(0), half(1),
                  pl.BlockSpec((1, blk), lambda i: (0, i))],
        out_specs=pl.BlockSpec((n, blk), lambda i: (0, i)),
        out_shape=jax.ShapeDtypeStruct((n, width), F32),
        compiler_params=pltpu.CompilerParams(vmem_limit_bytes=VMEM_LIMIT),
        name="adaln_mod",
    )(c, w_ada, w_ada, b_ada.reshape(1, width))


def _rmsnorm_mod(x, g, scale, shift):
    y = x * lax.rsqrt(jnp.mean(x * x, axis=-1, keepdims=True) + EPS)
    return (y * g) * (1.0 + scale) + shift


def _layernorm(x, g, b):
    mu = jnp.mean(x, axis=-1, keepdims=True)
    d = x - mu
    var = jnp.mean(d * d, axis=-1, keepdims=True)
    return d * lax.rsqrt(var + EPS) * g + b


def _silu(x):
    return x * jax.nn.sigmoid(x)


def _rotary_pair(x, cos, sin_signed, first_half):
    partner = jnp.where(first_half, pltpu.roll(x, PAIR_LANES - RET_DK // 2, 1),
                        pltpu.roll(x, RET_DK // 2, 1))
    return x * cos + partner * sin_signed


def _pair_groupnorm(o, lo, g, b):
    inv_n = 1.0 / RET_DV
    s_lo = jnp.sum(jnp.where(lo, o, 0.0), axis=-1, keepdims=True)
    s_hi = jnp.sum(jnp.where(lo, 0.0, o), axis=-1, keepdims=True)
    d = o - jnp.where(lo, s_lo, s_hi) * inv_n
    d2 = d * d
    v_lo = jnp.sum(jnp.where(lo, d2, 0.0), axis=-1, keepdims=True)
    v_hi = jnp.sum(jnp.where(lo, 0.0, d2), axis=-1, keepdims=True)
    var = jnp.where(lo, v_lo, v_hi) * inv_n
    return d * lax.rsqrt(var + EPS) * g + b


def _dot(a, b):
    return jnp.dot(a, b, preferred_element_type=F32)


def _dot_nt(a, b):
    return lax.dot_general(a, b, (((1,), (1,)), ((), ())), preferred_element_type=F32)


def _dot_tn(a, b):
    return lax.dot_general(a, b, (((0,), (0,)), ((), ())), preferred_element_type=F32)


def _stream_cast_weights(rings):
    def ring_chunks(pairs, stage):
        rows = stage.shape[1]
        return [(src, dst, r0) for src, dst in pairs for r0 in range(0, src.shape[0], rows)]

    chunks = [ring_chunks(pairs, stage) for pairs, stage, _ in rings]

    def copy(k, c):
        _, stage, sems = rings[k]
        src, _, r0 = chunks[k][c]
        slot = c % stage.shape[0]
        return pltpu.make_async_copy(src.at[pl.ds(r0, stage.shape[1]), :],
                                     stage.at[slot, :, pl.ds(0, src.shape[1])], sems.at[slot])

    for k, (_, stage, _) in enumerate(rings):
        for c in range(min(stage.shape[0] - 1, len(chunks[k]))):
            copy(k, c).start()
    for c in range(max(len(ch) for ch in chunks)):
        for k, (_, stage, _) in enumerate(rings):
            if c >= len(chunks[k]):
                continue
            n_slots, rows = stage.shape[0], stage.shape[1]
            if c + n_slots - 1 < len(chunks[k]):
                copy(k, c + n_slots - 1).start()
            copy(k, c).wait()
            src, dst, r0 = chunks[k][c]
            dst[r0:r0 + rows, :] = stage[c % n_slots, :, 0:src.shape[1]].astype(BF16)


def _layer_prompt_kernel(tile, nj, x_ref, mod_ref, gmix_ref, win_hbm, convw_ref, convb_ref,
                         clng_ref, clnb_ref, rlng_ref, rlnb_ref, wout_hbm,
                         cos_ref, sin_ref, dec_ref, qdec_ref, kdec_ref, sdec_ref,
                         mod2_ref, gffn_ref, w1_hbm, w2_hbm, gfin_ref,
                         y_ref, hist_ref, state_ref, win_out, wout_out, w1_out, w2_out,
                         ubuf, ushift, sbuf, mixbuf, x1buf,
                         win_ref, wout_ref, w1_ref, w2_ref,
                         stage_wide, stage_tall, sems_wide, sems_tall, sems_out):
    g = pl.program_id(0)
    total = pl.num_programs(0) - 1
    j = lax.rem(jnp.minimum(g, total - 1), nj)
    live = g < total
    export = [pltpu.make_async_copy(src, dst, sems_out.at[k]) for k, (src, dst) in enumerate(
        [(win_ref, win_out), (wout_ref, wout_out), (w1_ref, w1_out), (w2_ref, w2_out)])]

    @pl.when(g == 0)
    def _():
        _stream_cast_weights([
            ([(win_hbm, win_ref), (w1_hbm, w1_ref)], stage_wide, sems_wide),
            ([(wout_hbm, wout_ref), (w2_hbm, w2_ref)], stage_tall, sems_tall)])
        for cp in export:
            cp.start()
        x1buf[...] = jnp.zeros((tile, D_MODEL), F32)

    @pl.when(jnp.logical_and(j == 0, live))
    def _():
        ubuf[0:HIST_ROWS, :] = jnp.zeros((HIST_ROWS, CONV_CH), F32)
        sbuf[...] = jnp.zeros(sbuf.shape, F32)

    x = x_ref[0]
    b_mix = lax.div(jnp.minimum(g, total - 1), nj)
    b_mlp = lax.div(jnp.maximum(g - 1, 0), nj)
    mods = lambda ref, b: [ref[pl.ds(b, 1), k * D_MODEL:(k + 1) * D_MODEL] for k in range(3)]
    shift, scale, gate = mods(mod_ref, b_mix)

    def conv_stages(u):
        ubuf[HIST_ROWS:HIST_ROWS + tile, :] = u
        off = HIST_ROWS - (CONV_K - 1)
        for r in range(SUBLANES):
            n = tile + SUBLANES * ((CONV_K - 1 - r) // SUBLANES)
            ushift[r, 0:n, :] = ubuf[off + r:off + r + n, :]
        ubuf[0:HIST_ROWS, :] = ubuf[tile:tile + HIST_ROWS, :]
        yield
        for rb in range(tile // CONV_ROW_BLOCK):
            r0 = rb * CONV_ROW_BLOCK
            dw = jnp.broadcast_to(convb_ref[...], (CONV_ROW_BLOCK, CONV_CH))
            for k in range(CONV_K):
                s0 = r0 + SUBLANES * (k // SUBLANES)
                wk = jnp.concatenate([convw_ref[k]] * (CONV_ROW_BLOCK // SUBLANES), axis=0)
                dw = dw + wk * ushift[k % SUBLANES, s0:s0 + CONV_ROW_BLOCK, :]
            conv_out = _silu(_layernorm(dw, clng_ref[...], clnb_ref[...]))
            mixbuf[r0:r0 + CONV_ROW_BLOCK, 0:CONV_CH] = conv_out.astype(BF16)
            yield

    def ret_stages(proj):
        ret_proj = []
        for k in range(4):
            ret_proj.append(proj(2 + k))
            yield
        q, kk, v, gt = ret_proj
        q = q * (RET_DK ** -0.5)
        lane = lax.broadcasted_iota(jnp.int32, (1, PAIR_LANES), 1)
        lo = lane < RET_DK
        first_half = (lane % RET_DK) < (RET_DK // 2)
        lo_b = lo.astype(BF16)
        hi_b = (~lo).astype(BF16)
        pairs = range(N_PAIRS)
        halves = (pairs[:N_PAIRS // 2], pairs[N_PAIRS // 2:])
        lanes = [slice(p * PAIR_LANES, (p + 1) * PAIR_LANES) for p in pairs]
        for c in range(tile // RET_CHUNK):
            rs = slice(c * RET_CHUNK, (c + 1) * RET_CHUNK)
            cos = cos_ref[rs, :]
            sin = sin_ref[rs, :]
            qc, kc, vc, scores, outs = {}, {}, {}, {}, {}
            for group in halves:
                for p in group:
                    qc[p] = _rotary_pair(q[rs, lanes[p]], cos, sin, first_half)
                    kc[p] = _rotary_pair(kk[rs, lanes[p]], cos, sin, first_half)
                    vc[p] = v[rs, lanes[p]].astype(BF16)
                    kb = kc[p].astype(BF16)
                    k_cat = jnp.concatenate([kb * lo_b, kb * hi_b], axis=0)
                    scores[p] = _dot_nt(qc[p].astype(BF16), k_cat)
                yield
            for group in halves:
                for p in group:
                    state = sbuf[p]
                    lhs = jnp.concatenate([(scores[p] * dec_ref[p]).astype(BF16),
                                           (qc[p] * qdec_ref[:, lanes[p]]).astype(BF16)], axis=1)
                    rhs = jnp.concatenate([vc[p] * lo_b, vc[p] * hi_b, state.astype(BF16)],
                                          axis=0)
                    outs[p] = _dot(lhs, rhs)
                    kd = (kc[p] * kdec_ref[:, lanes[p]]).astype(BF16)
                    sbuf[p] = (state * sdec_ref[p]
                               + jnp.where(sdec_ref[p] > 0.0, _dot_tn(kd, vc[p]), 0.0))
                yield
            for p in pairs:
                on = _pair_groupnorm(outs[p], lo, rlng_ref[:, lanes[p]], rlnb_ref[:, lanes[p]])
                mixbuf[rs, CONV_CH + p * PAIR_LANES:CONV_CH + (p + 1) * PAIR_LANES] = (
                    on * _silu(gt[rs, lanes[p]])).astype(BF16)
            yield

    ffn = _ffn_stages(x1buf[...], mods(mod2_ref, b_mlp), gffn_ref, w1_ref, w2_ref, gfin_ref, y_ref)
    next(ffn)
    hb = _rmsnorm_mod(x, gmix_ref[...], scale, shift).astype(BF16)

    def proj(k):
        return _dot(hb, win_ref[:, k * CONV_CH:(k + 1) * CONV_CH])

    streams = {"f": ffn, "c": conv_stages(proj(0) * jax.nn.sigmoid(proj(1))),
               "r": ret_stages(proj)}
    for name in LAYER_PROGRAM.replace(" ", ""):
        next(streams[name], None)
    for stream in (streams["c"], streams["r"], streams["f"]):
        for _ in stream:
            pass

    mix = _dot(mixbuf[...], wout_ref[...])
    x1buf[...] = x + gate * mix

    @pl.when(jnp.logical_and(j == nj - 1, live))
    def _():
        hist_ref[0] = ubuf[0:HIST_ROWS, :]
        state_ref[0] = sbuf[...]

    @pl.when(g == total)
    def _():
        for cp in export:
            cp.wait()


def _layer_prompt(x, mod, mod_row0, g_mix, w_in, conv_w, conv_b, cln_g, cln_b, rln_g, rln_b,
                  w_out, g_ffn, w1, w2, g_final, rope, decay, tile):
    bsz, seq, d = x.shape
    assert bsz % SUBLANES == 0 and mod_row0 % bsz == 0
    in_hbm = pl.BlockSpec(memory_space=pl.ANY)

    def stage_shape(*ws):
        width = max(w.shape[1] for w in ws)
        rows = WEIGHT_STAGE_ELEMS // width
        assert all(w.shape[0] % rows == 0 for w in ws)
        return pltpu.VMEM((WEIGHT_STAGE_SLOTS, rows, width), F32)

    mod_blk = mod_row0 // bsz
    nj = seq // tile
    total = bsz * nj
    cos, sin = rope
    dec, qdec, kdec, sdec = decay
    mix_tile = lambda g: jnp.minimum(g, total - 1)
    ffn_tile = lambda g: jnp.maximum(g - 1, 0)
    resident = lambda shape: pl.BlockSpec(shape, lambda g: (0,) * len(shape),
                                          pipeline_mode=pl.Buffered(1))
    row = lambda a: a.reshape(1, -1)
    conv_w_rows = jnp.broadcast_to(conv_w[:, None, :], (CONV_K, SUBLANES, CONV_CH))
    rope_spec = pl.BlockSpec((tile, PAIR_LANES), lambda g: (mix_tile(g) % nj, 0))
    return pl.pallas_call(
        functools.partial(_layer_prompt_kernel, tile, nj),
        grid=(total + 1,),
        in_specs=[
            pl.BlockSpec((1, tile, d), lambda g: (mix_tile(g) // nj, mix_tile(g) % nj, 0)),
            pl.BlockSpec((bsz, 3 * d), lambda g: (mod_blk, 0), pipeline_mode=pl.Buffered(1)),
            resident((1, d)),
            in_hbm,
            resident((CONV_K, SUBLANES, CONV_CH)),
            resident((1, CONV_CH)),
            resident((1, CONV_CH)),
            resident((1, CONV_CH)),
            resident((1, RET_W)),
            resident((1, RET_W)),
            in_hbm,
            rope_spec,
            rope_spec,
            resident(dec.shape),
            resident(qdec.shape),
            resident(kdec.shape),
            resident(sdec.shape),
            pl.BlockSpec((bsz, 3 * d), lambda g: (mod_blk, 1), pipeline_mode=pl.Buffered(1)),
            resident((1, d)),
            in_hbm,
            in_hbm,
            resident((1, d)),
        ],
        out_specs=[
            pl.BlockSpec((1, tile, d), lambda g: (ffn_tile(g) // nj, ffn_tile(g) % nj, 0)),
            pl.BlockSpec((1, HIST_ROWS, CONV_CH), lambda g: (mix_tile(g) // nj, 0, 0)),
            pl.BlockSpec((1, N_PAIRS, PAIR_LANES, PAIR_LANES),
                         lambda g: (mix_tile(g) // nj, 0, 0, 0)),
            in_hbm, in_hbm, in_hbm, in_hbm,
        ],
        out_shape=(
            jax.ShapeDtypeStruct((bsz, seq, d), F32),
            jax.ShapeDtypeStruct((bsz, HIST_ROWS, CONV_CH), F32),
            jax.ShapeDtypeStruct((bsz, N_PAIRS, PAIR_LANES, PAIR_LANES), F32),
        ) + tuple(jax.ShapeDtypeStruct(w.shape, BF16) for w in (w_in, w_out, w1, w2)),
        scratch_shapes=[
            pltpu.VMEM((HIST_ROWS + tile, CONV_CH), F32),
            pltpu.VMEM((SUBLANES, HIST_ROWS + tile, CONV_CH), F32),
            pltpu.VMEM((N_PAIRS, PAIR_LANES, PAIR_LANES), F32),
            pltpu.VMEM((tile, D_MODEL), BF16),
            pltpu.VMEM((tile, D_MODEL), F32),
            pltpu.VMEM(w_in.shape, BF16),
            pltpu.VMEM(w_out.shape, BF16),
            pltpu.VMEM(w1.shape, BF16),
            pltpu.VMEM(w2.shape, BF16),
            stage_shape(w_in, w1),
            stage_shape(w_out, w2),
            pltpu.SemaphoreType.DMA((WEIGHT_STAGE_SLOTS,)),
            pltpu.SemaphoreType.DMA((WEIGHT_STAGE_SLOTS,)),
            pltpu.SemaphoreType.DMA((4,)),
        ],
        compiler_params=pltpu.CompilerParams(
            dimension_semantics=("arbitrary",), vmem_limit_bytes=VMEM_LIMIT),
        name="layer_prompt",
    )(x, mod, row(g_mix), w_in, conv_w_rows, row(conv_b), row(cln_g), row(cln_b),
      row(rln_g), row(rln_b), w_out, cos, sin, dec, qdec, kdec, sdec,
      mod, row(g_ffn), w1, w2, row(g_final))


def _ffn_stages(x, mods, gffn_ref, w1_ref, w2_ref, gfin_ref, y_ref):
    shift, scale, gate = mods
    hb = _rmsnorm_mod(x, gffn_ref[...], scale, shift).astype(BF16)
    yield
    half = D_MODEL // 2
    ff = [None, None]
    for c in range(D_FF // (2 * FFN_CHUNK)):
        acts = []
        for h in range(2):
            c0 = (2 * c + h) * FFN_CHUNK
            a = jnp.maximum(_dot(hb, w1_ref[:, c0:c0 + FFN_CHUNK]), 0.0)
            acts.append((a * a).astype(BF16))
            yield
        act = jnp.concatenate(acts, axis=1)
        for n in range(2):
            part = _dot(act, w2_ref[2 * c * FFN_CHUNK:2 * (c + 1) * FFN_CHUNK,
                                    n * half:(n + 1) * half])
            ff[n] = part if ff[n] is None else ff[n] + part
            yield
    z = x + gate * jnp.concatenate(ff, axis=1)
    y_ref[0] = z * lax.rsqrt(jnp.mean(z * z, axis=-1, keepdims=True) + EPS) * gfin_ref[...]


def _pos_cols(x_ref, i):
    return x_ref[:, i * D_MODEL:(i + 1) * D_MODEL]


def _mod_cols(mod_ref, n_seq, k):
    return mod_ref[0:n_seq, k * D_MODEL:(k + 1) * D_MODEL]


def _mlp_sample_kernel(n_pos, n_seq, x_ref, conv_ref, ret_ref, wout_ref, mod_ref,
                       gffn_ref, w1_ref, w2_ref, gfin_ref, y_ref, x1buf, hbuf, acc):
    c = pl.program_id(0)
    rows = lambda i: slice(i * n_seq, (i + 1) * n_seq)

    @pl.when(c == 0)
    def _():
        mix = (_dot(conv_ref[...], wout_ref[0:CONV_CH, :].astype(BF16))
               + _dot(ret_ref[...], wout_ref[CONV_CH:CONV_CH + RET_W, :].astype(BF16)))
        for i in range(n_pos):
            x1 = _pos_cols(x_ref, i) + _mod_cols(mod_ref, n_seq, 2) * mix[rows(i)]
            x1buf[rows(i), :] = x1
            hbuf[rows(i), :] = _rmsnorm_mod(x1, gffn_ref[...], _mod_cols(mod_ref, n_seq, 4),
                                            _mod_cols(mod_ref, n_seq, 3)).astype(BF16)

    a = jnp.maximum(_dot(hbuf[...], w1_ref[...].astype(BF16)), 0.0)
    part = _dot((a * a).astype(BF16), w2_ref[...].astype(BF16))

    @pl.when(c == 0)
    def _():
        acc[...] = part

    @pl.when(c > 0)
    def _():
        acc[...] += part

    @pl.when(c == pl.num_programs(0) - 1)
    def _():
        for i in range(n_pos):
            z = x1buf[rows(i), :] + _mod_cols(mod_ref, n_seq, 5) * acc[rows(i), :]
            y_ref[:, i * D_MODEL:(i + 1) * D_MODEL] = (
                z * lax.rsqrt(jnp.mean(z * z, axis=-1, keepdims=True) + EPS) * gfin_ref[...])


def _mlp_sample(x, conv_out, ret_out, w_out, mod, g_ffn, w1, w2, g_final):
    n_seq, d = x.shape[0], D_MODEL
    n_pos = x.shape[1] // d
    n_tok = n_pos * n_seq
    n_chunks = D_FF // SAMPLE_FFN_CHUNK
    whole = lambda shape: pl.BlockSpec(shape, lambda c: (0,) * len(shape),
                                       pipeline_mode=pl.Buffered(1))
    return pl.pallas_call(
        functools.partial(_mlp_sample_kernel, n_pos, n_seq),
        grid=(n_chunks,),
        in_specs=[whole(x.shape), whole(conv_out.shape), whole(ret_out.shape),
                  whole(w_out.shape), whole(mod.shape), whole((1, d)),
                  pl.BlockSpec((d, SAMPLE_FFN_CHUNK), lambda c: (0, c)),
                  pl.BlockSpec((SAMPLE_FFN_CHUNK, d), lambda c: (c, 0)),
                  whole((1, d))],
        out_specs=pl.BlockSpec(x.shape, lambda c: (0, 0)),
        out_shape=jax.ShapeDtypeStruct(x.shape, F32),
        scratch_shapes=[pltpu.VMEM((n_tok, d), F32), pltpu.VMEM((n_tok, d), BF16),
                        pltpu.VMEM((n_tok, d), F32)],
        compiler_params=pltpu.CompilerParams(
            dimension_semantics=("arbitrary",), vmem_limit_bytes=VMEM_LIMIT),
        name="mlp_sample",
    )(x, conv_out, ret_out, w_out, mod, g_ffn.reshape(1, d), w1, w2, g_final.reshape(1, d))


def _power_table_kernel(o_ref):
    lane_h = (lax.broadcasted_iota(jnp.int32, o_ref.shape, 1) // RET_DK).astype(F32)
    n = lax.broadcasted_iota(jnp.int32, o_ref.shape, 0).astype(F32)
    o_ref[...] = jnp.exp(_log_decay(lane_h) * n)


def _power_table():
    return pl.pallas_call(_power_table_kernel,
                          out_shape=jax.ShapeDtypeStruct((8, RET_W), F32),
                          name="decay_powers")()


def _proj_sample_kernel(n_pos, n_seq, x_ref, mod_ref, gmix_ref, win_ref, cache_ref, convw_ref,
                        convb_ref, clng_ref, clnb_ref, cos_ref, sin_ref,
                        hist_ref, conv_ref, q_ref, k_ref, v_ref, g_ref):
    shift, scale = _mod_cols(mod_ref, n_seq, 0), _mod_cols(mod_ref, n_seq, 1)
    hb = jnp.concatenate(
        [_rmsnorm_mod(_pos_cols(x_ref, i), gmix_ref[...], scale, shift).astype(BF16)
         for i in range(n_pos)], axis=0)

    def proj(k):
        return _dot(hb, win_ref[:, k * CONV_CH:(k + 1) * CONV_CH].astype(BF16))

    u = proj(0) * jax.nn.sigmoid(proj(1))
    n_hist = CONV_K - 1
    rows = lambda a, i: a[i * n_seq:(i + 1) * n_seq]
    for m in range(n_hist - n_pos):
        hist_ref[m] = cache_ref[m + n_pos]
    for i in range(n_pos):
        hist_ref[n_hist - n_pos + i] = rows(u, i)
        dw = jnp.broadcast_to(convb_ref[...], (n_seq, CONV_CH))
        for k in range(CONV_K):
            m = i + k
            src = cache_ref[m] if m < n_hist else rows(u, m - n_hist)
            dw = dw + convw_ref[k:k + 1, :] * src
        conv_ref[i] = _silu(_layernorm(dw, clng_ref[...], clnb_ref[...])).astype(BF16)

    lane = lax.broadcasted_iota(jnp.int32, (1, PAIR_LANES), 1)
    first_half = (lane % RET_DK) < (RET_DK // 2)
    q = proj(2) * (RET_DK ** -0.5)
    kk = proj(3)
    v = proj(4)
    gt = proj(5)
    for i in range(n_pos):
        cos = cos_ref[i:i + 1, :]
        sin = sin_ref[i:i + 1, :]
        for p in range(N_PAIRS):
            sl = slice(p * PAIR_LANES, (p + 1) * PAIR_LANES)
            q_ref[i, :, sl] = _rotary_pair(rows(q, i)[:, sl], cos, sin, first_half)
            k_ref[i, :, sl] = _rotary_pair(rows(kk, i)[:, sl], cos, sin, first_half)
        v_ref[i] = rows(v, i)
        g_ref[i] = rows(gt, i)


def _proj_sample(x, mod, g_mix, w_in, cache_t, conv_w, conv_b, cln_g, cln_b, rope):
    n_seq, d = x.shape[0], D_MODEL
    n_pos = x.shape[1] // d
    blk = SAMPLE_SEQ_BLOCK
    cos, sin = rope
    row = lambda a: a.reshape(1, -1)
    whole = lambda a: pl.BlockSpec(a.shape, lambda s: (0,) * a.ndim, pipeline_mode=pl.Buffered(1))
    seqs = lambda lead, width: pl.BlockSpec((lead, blk, width), lambda s: (0, s, 0))
    tok = lambda dt: jax.ShapeDtypeStruct((n_pos, n_seq, RET_W), dt)
    args = (x, mod, row(g_mix), w_in, cache_t, conv_w, row(conv_b), row(cln_g), row(cln_b),
            cos, sin)
    return pl.pallas_call(
        functools.partial(_proj_sample_kernel, n_pos, blk),
        grid=(n_seq // blk,),
        in_specs=[pl.BlockSpec((blk, n_pos * d), lambda s: (s, 0)),
                  pl.BlockSpec((blk, mod.shape[1]), lambda s: (s, 0)),
                  whole(args[2]), whole(w_in), seqs(cache_t.shape[0], CONV_CH)]
        + [whole(a) for a in args[5:]],
        out_specs=[seqs(cache_t.shape[0], CONV_CH)] + [seqs(n_pos, RET_W)] * 5,
        out_shape=(jax.ShapeDtypeStruct(cache_t.shape, F32), tok(BF16),
                   tok(F32), tok(F32), tok(F32), tok(F32)),
        compiler_params=pltpu.CompilerParams(
            dimension_semantics=("arbitrary",), vmem_limit_bytes=VMEM_LIMIT),
        name="proj_sample",
    )(*args)


def _ret_sample_kernel(n_pos, n_seq, q_ref, k_ref, v_ref, g_ref, st_ref, pw_ref, rlng_ref,
                       rlnb_ref, out_ref, snew_ref, sb_buf, qd_buf, kd_buf, v_buf):
    lane = lax.broadcasted_iota(jnp.int32, (1, PAIR_LANES), 1)
    lo = lane < RET_DK
    rows = lambda a, i: a[i * n_seq:(i + 1) * n_seq]
    cols = lambda a, i: a[:, i * n_seq:(i + 1) * n_seq]
    pw = pw_ref[...]
    q = q_ref[...]
    k = k_ref[...]
    qb = [rows(q, i).astype(BF16) for i in range(n_pos)]
    kb = [rows(k, i).astype(BF16) for i in range(n_pos)]
    vb = [rows(v_ref[...], i).astype(BF16) for i in range(n_pos)]

    o = []
    for i in range(n_pos):
        acc = jnp.zeros((n_seq, PAIR_LANES), F32)
        for j in range(i + 1):
            prod = qb[i].astype(F32) * kb[j].astype(F32)
            s_lo = jnp.sum(jnp.where(lo, prod, 0.0), axis=-1, keepdims=True)
            s_hi = jnp.sum(jnp.where(lo, 0.0, prod), axis=-1, keepdims=True)
            score = jnp.where(lo, s_lo, s_hi) * pw[i - j:i - j + 1, :]
            acc = acc + score.astype(BF16).astype(F32) * vb[j].astype(F32)
        o.append(acc)

    rounded = lambda x: x.astype(BF16).astype(F32)
    qd_buf[...] = jnp.concatenate(
        [rounded(rows(q, i) * pw[i + 1:i + 2, :]) for i in range(n_pos)], axis=0).T
    kd_buf[...] = jnp.concatenate(
        [rounded(rows(k, j) * pw[n_pos - 1 - j:n_pos - j, :]) for j in range(n_pos)], axis=0).T
    v_buf[...] = jnp.concatenate([vb[j].astype(F32) for j in range(n_pos)], axis=0).T
    tile_at = lambda r: pl.ds(pl.multiple_of(r * RET_DV, RET_DV), RET_DV)
    pos = lambda i: slice(i * n_seq, (i + 1) * n_seq)

    for a in range(2):
        gam = pw[n_pos:n_pos + 1, a * RET_DK:a * RET_DK + 1]
        v_rows = slice(a * RET_DV, (a + 1) * RET_DV)

        def update(grp, carry, a=a, gam=gam, v_rows=v_rows):
            r0 = pl.multiple_of(a * RET_DK + grp * SUBLANES, SUBLANES)
            kd_rows = kd_buf[pl.ds(r0, SUBLANES), :]
            for s in range(SUBLANES):
                r = r0 + s
                s_old = st_ref[tile_at(r), :]
                sb_buf[tile_at(r), :] = rounded(s_old)
                s_new = s_old * gam
                for j in range(n_pos):
                    s_new = s_new + kd_rows[s:s + 1, pos(j)] * v_buf[v_rows, pos(j)]
                snew_ref[tile_at(r), :] = s_new
            return carry

        lax.fori_loop(0, RET_DK // SUBLANES, update, 0)

    for i in range(n_pos):
        heads = []
        for a in range(2):
            def cross(grp, acc, a=a, i=i):
                r0 = pl.multiple_of(a * RET_DK + grp * SUBLANES, SUBLANES)
                qd_rows = qd_buf[pl.ds(r0, SUBLANES), pos(i)]
                for s in range(SUBLANES):
                    acc = acc + qd_rows[s:s + 1, :] * sb_buf[tile_at(r0 + s), :]
                return acc

            heads.append(lax.fori_loop(0, RET_DK // SUBLANES, cross,
                                       jnp.zeros((RET_DV, n_seq), F32)))
        oi = o[i] + jnp.concatenate(heads, axis=0).T
        on = _pair_groupnorm(oi, lo, rlng_ref[...], rlnb_ref[...])
        out_ref[i * n_seq:(i + 1) * n_seq, :] = (on * _silu(rows(g_ref[...], i))).astype(BF16)


def _ret_sample(q, k, v, g, state_t, pw, rln_g, rln_b, n_pos, n_seq):
    n_tok = n_pos * n_seq
    pair_state = 2 * RET_DK * RET_DV
    slab = pl.BlockSpec((n_tok, PAIR_LANES), lambda p: (0, p))
    state_spec = pl.BlockSpec((pair_state, n_seq), lambda p: (p, 0))
    return pl.pallas_call(
        functools.partial(_ret_sample_kernel, n_pos, n_seq),
        grid=(N_PAIRS,),
        in_specs=[slab, slab, slab, slab, state_spec,
                  pl.BlockSpec((8, PAIR_LANES), lambda p: (0, p)),
                  pl.BlockSpec((1, PAIR_LANES), lambda p: (0, p)),
                  pl.BlockSpec((1, PAIR_LANES), lambda p: (0, p))],
        out_specs=[slab, state_spec],
        out_shape=(jax.ShapeDtypeStruct((n_tok, RET_W), BF16),
                   jax.ShapeDtypeStruct(state_t.shape, F32)),
        scratch_shapes=[pltpu.VMEM((pair_state, n_seq), F32)]
        + [pltpu.VMEM((PAIR_LANES, n_tok), F32)] * 3,
        compiler_params=pltpu.CompilerParams(
            dimension_semantics=("arbitrary",), vmem_limit_bytes=VMEM_LIMIT),
        name="ret_sample",
    )(q, k, v, g, state_t, pw, rln_g.reshape(1, RET_W), rln_b.reshape(1, RET_W))


def _unpack_state(s2):
    blocks = []
    for p in range(N_PAIRS):
        for a in range(2):
            sl = slice(a * RET_DK, (a + 1) * RET_DK)
            blocks.append(s2[:, p, sl, sl])
    return jnp.stack(blocks, axis=1)


def _prompt_path(x, mod, mod_row0, g_mix, w_in, conv_w, conv_b, cln_g, cln_b, rln_g, rln_b,
                 w_out, g_ffn, w_ff1, w_ff2, g_final, tile=PROMPT_TILE):
    seq = x.shape[1]
    rope = _rope_tables(seq, 0)
    decay = _decay_tables(RET_CHUNK)
    y, hist, s2, *w_bf16 = _layer_prompt(x, mod, mod_row0, g_mix, w_in, conv_w, conv_b, cln_g,
                                         cln_b, rln_g, rln_b, w_out, g_ffn, w_ff1, w_ff2, g_final,
                                         rope, decay, tile)
    return y, hist[:, HIST_ROWS - (CONV_K - 1):], _unpack_state(s2), w_bf16


def _sample_path(x, mod, cache, state, pos0, g_mix, w_in, conv_w, conv_b, cln_g, cln_b,
                 rln_g, rln_b, w_out, g_ffn, w_ff1, w_ff2, g_final):
    n_seq, n_pos, d = x.shape
    xs = x.reshape(n_seq, n_pos * d)
    rope = _rope_tables(8, pos0)
    hist_t, *tok = _proj_sample(xs, mod, g_mix, w_in, cache.transpose(1, 0, 2),
                                conv_w, conv_b, cln_g, cln_b, rope)
    conv_out, q, k, v, g = [a.reshape(n_pos * n_seq, -1) for a in tok]
    ret_out, s_new_t = _ret_sample(q, k, v, g, state.reshape(n_seq, -1).T, _power_table(),
                                   rln_g, rln_b, n_pos, n_seq)
    s_new = s_new_t.T
    y = _mlp_sample(xs, conv_out, ret_out, w_out, mod, g_ffn, w_ff1, w_ff2, g_final)
    return y.reshape(x.shape), hist_t.transpose(1, 0, 2), s_new.reshape(state.shape)


def kernel(x_prompt, x_sample, cache_conv, state_ret, c_prompt, c_sample, w_ada, b_ada, g_mix, w_in, conv_w, conv_b, conv_ln_g, conv_ln_b, ret_ln_g, ret_ln_b, w_out, g_ffn, w_ff1, w_ff2, g_final):
    mod = _modulation(jnp.concatenate([c_sample, c_prompt], axis=0), w_ada[0], b_ada[0])
    params = (g_mix[0], w_in[0], conv_w[0], conv_b[0], conv_ln_g[0], conv_ln_b[0], ret_ln_g[0],
              ret_ln_b[0], w_out[0], g_ffn[0], w_ff1[0], w_ff2[0], g_final)
    y_p, conv_p, ret_p, (w_in_b, w_out_b, w_ff1_b, w_ff2_b) = _prompt_path(
        x_prompt, mod, x_sample.shape[0], *params)
    params_b = (g_mix[0], w_in_b, conv_w[0], conv_b[0], conv_ln_g[0], conv_ln_b[0], ret_ln_g[0],
                ret_ln_b[0], w_out_b, g_ffn[0], w_ff1_b, w_ff2_b, g_final)
    y_s, conv_s, ret_s = _sample_path(x_sample, mod, cache_conv[0], state_ret[0], PAST_LEN,
                                      *params_b)
    return (y_p, y_s, conv_p[None], ret_p[None], conv_s[None], ret_s[None])
```

```python
import functools

import jax
import jax.numpy as jnp
from jax import lax
from jax.experimental import pallas as pl
from jax.experimental.pallas import tpu as pltpu

D_MODEL = 1024
CONV_CH = 512
CONV_K = 31
RET_HEADS = 8
RET_W = 512
RET_DK = 64
RET_DV = 64
D_FF = 4 * D_MODEL
RET_CHUNK = 128
ROPE_THETA = 10000.0
EPS = 1e-6
N_MOD = 6
PAST_LEN = 16384

SUBLANES = 8
HIST_ROWS = 32
CONV_ROW_BLOCK = 32
PROMPT_TILE = 256
FFN_CHUNK = 512
SAMPLE_FFN_CHUNK = 1024
SAMPLE_SEQ_BLOCK = 64
WEIGHT_STAGE_ELEMS = 512 * 1024
WEIGHT_STAGE_SLOTS = 3
LAYER_PROGRAM = ("c cf cf cf cf cf cf cf cf "
                 "rrrr rfrf rfrf r "
                 "rfrf rfrf r")
PAIR_LANES = 2 * RET_DK
N_PAIRS = RET_HEADS // 2
VMEM_LIMIT = 56 * 1024 * 1024

F32 = jnp.float32
BF16 = jnp.bfloat16


def _log_decay(head):
    return jnp.log1p(-jnp.exp2(-5.0 - head))


def _rope_kernel(pos0, cos_ref, sin_ref):
    rows = cos_ref.shape[0]
    lane = lax.broadcasted_iota(jnp.int32, (rows, PAIR_LANES), 1)
    row = lax.broadcasted_iota(jnp.int32, (rows, PAIR_LANES), 0)
    half = RET_DK // 2
    freq = (lane % half).astype(F32)
    inv = jnp.power(jnp.full_like(freq, ROPE_THETA), -freq / half)
    ang = (row + pos0).astype(F32) * inv
    first = (lane % RET_DK) < half
    cos_ref[...] = jnp.cos(ang)
    sin_ref[...] = jnp.where(first, -jnp.sin(ang), jnp.sin(ang))


ROPE_BLOCK = 16


def _rope_blocked_kernel(pos0, cos_ref, sin_ref):
    rows = cos_ref.shape[0]
    n_blk = rows // ROPE_BLOCK
    half = RET_DK // 2

    def angles(n, step, start):
        lane = lax.broadcasted_iota(jnp.int32, (n, PAIR_LANES), 1)
        row = lax.broadcasted_iota(jnp.int32, (n, PAIR_LANES), 0)
        freq = (lane % half).astype(F32)
        inv = jnp.power(jnp.full_like(freq, ROPE_THETA), -freq / half)
        return (row * step + start).astype(F32) * inv

    a = angles(n_blk, ROPE_BLOCK, pos0)
    b = angles(ROPE_BLOCK, 1, 0)
    lane = lax.broadcasted_iota(jnp.int32, (ROPE_BLOCK, PAIR_LANES), 1)
    sign = jnp.where((lane % RET_DK) < half, -1.0, 1.0)
    cos_a, sin_a = jnp.cos(a), jnp.sin(a)
    cos_b, sin_b = jnp.cos(b), jnp.sin(b)
    cos_bs, sin_bs = cos_b * sign, sin_b * sign
    for blk in range(n_blk):
        ca, sa = cos_a[blk:blk + 1, :], sin_a[blk:blk + 1, :]
        rs = slice(blk * ROPE_BLOCK, (blk + 1) * ROPE_BLOCK)
        cos_ref[rs, :] = ca * cos_b - sa * sin_b
        sin_ref[rs, :] = sa * cos_bs + ca * sin_bs


def _decay_kernel(chunk, dec_ref, qdec_ref, kdec_ref, sdec_ref):
    c = chunk
    shape = (N_PAIRS, c, 2 * c)
    col = lax.broadcasted_iota(jnp.int32, shape, 2)
    h = (2 * lax.broadcasted_iota(jnp.int32, shape, 0) + col // c).astype(F32)
    i = lax.broadcasted_iota(jnp.int32, shape, 1).astype(F32)
    j = (col % c).astype(F32)
    diff = i - j
    dec_ref[...] = jnp.where(diff >= 0, jnp.exp(_log_decay(h) * jnp.maximum(diff, 0.0)), 0.0)
    lane_h = (lax.broadcasted_iota(jnp.int32, (c, RET_W), 1) // RET_DK).astype(F32)
    idx = lax.broadcasted_iota(jnp.int32, (c, RET_W), 0).astype(F32)
    lg = _log_decay(lane_h)
    qdec_ref[...] = jnp.exp(lg * (idx + 1.0))
    kdec_ref[...] = jnp.exp(lg * (c - 1.0 - idx))
    p = lax.broadcasted_iota(jnp.int32, (N_PAIRS, PAIR_LANES, PAIR_LANES), 0)
    r = lax.broadcasted_iota(jnp.int32, (N_PAIRS, PAIR_LANES, PAIR_LANES), 1) // RET_DK
    q = lax.broadcasted_iota(jnp.int32, (N_PAIRS, PAIR_LANES, PAIR_LANES), 2) // RET_DK
    hh = (2 * p + r).astype(F32)
    sdec_ref[...] = jnp.where(r == q, jnp.exp(_log_decay(hh) * float(c)), 0.0)


def _mod_kernel(c_ref, w_lo_ref, w_hi_ref, b_ref, o_ref):
    c = c_ref[...]
    s = (c * jax.nn.sigmoid(c)).astype(BF16)
    half = w_lo_ref.shape[1]
    for k, w_ref in enumerate((w_lo_ref, w_hi_ref)):
        cs = slice(k * half, (k + 1) * half)
        o_ref[:, cs] = _dot(s, w_ref[...].astype(BF16)) + b_ref[:, cs]


def _modulation(c, w_ada, b_ada):
    n, d = c.shape
    width = w_ada.shape[1]
    blk = D_MODEL
    half = lambda k: pl.BlockSpec((d, blk // 2), lambda i: (0, 2 * i + k))
    return pl.pallas_call(
        _mod_kernel,
        grid=(width // blk,),
        in_specs=[pl.BlockSpec((n, d), lambda i: (0, 0)), half(0), half(1),
                  pl.BlockSpec((1, blk), lambda i: (0, i))],
        out_specs=pl.BlockSpec((n, blk), lambda i: (0, i)),
        out_shape=jax.ShapeDtypeStruct((n, width), F32),
        compiler_params=pltpu.CompilerParams(vmem_limit_bytes=VMEM_LIMIT),
        name="adaln_mod",
    )(c, w_ada, w_ada, b_ada.reshape(1, width))


def _rmsnorm_mod(x, g, scale, shift):
    y = x * lax.rsqrt(jnp.mean(x * x, axis=-1, keepdims=True) + EPS)
    return (y * g) * (1.0 + scale) + shift


def _layernorm(x, g, b):
    mu = jnp.mean(x, axis=-1, keepdims=True)
    d = x - mu
    var = jnp.mean(d * d, axis=-1, keepdims=True)
    return d * lax.rsqrt(var + EPS) * g + b


def _silu(x):
    return x * jax.nn.sigmoid(x)


def _rotary_pair(x, cos, sin_signed, first_half):
    partner = jnp.where(first_half, pltpu.roll(x, PAIR_LANES - RET_DK // 2, 1),
                        pltpu.roll(x, RET_DK // 2, 1))
    return x * cos + partner * sin_signed


def _pair_groupnorm(o, lo, g, b):
    inv_n = 1.0 / RET_DV
    s_lo = jnp.sum(jnp.where(lo, o, 0.0), axis=-1, keepdims=True)
    s_hi = jnp.sum(jnp.where(lo, 0.0, o), axis=-1, keepdims=True)
    d = o - jnp.where(lo, s_lo, s_hi) * inv_n
    d2 = d * d
    v_lo = jnp.sum(jnp.where(lo, d2, 0.0), axis=-1, keepdims=True)
    v_hi = jnp.sum(jnp.where(lo, 0.0, d2), axis=-1, keepdims=True)
    var = jnp.where(lo, v_lo, v_hi) * inv_n
    return d * lax.rsqrt(var + EPS) * g + b


def _dot(a, b):
    return jnp.dot(a, b, preferred_element_type=F32)


def _dot_nt(a, b):
    return lax.dot_general(a, b, (((1,), (1,)), ((), ())), preferred_element_type=F32)


def _dot_tn(a, b):
    return lax.dot_general(a, b, (((0,), (0,)), ((), ())), preferred_element_type=F32)


def _stream_cast_weights(rings):
    def ring_chunks(pairs, stage):
        rows = stage.shape[1]
        return [(src, dst, r0) for src, dst in pairs for r0 in range(0, src.shape[0], rows)]

    chunks = [ring_chunks(pairs, stage) for pairs, stage, _ in rings]

    def copy(k, c):
        _, stage, sems = rings[k]
        src, _, r0 = chunks[k][c]
        slot = c % stage.shape[0]
        return pltpu.make_async_copy(src.at[pl.ds(r0, stage.shape[1]), :],
                                     stage.at[slot, :, pl.ds(0, src.shape[1])], sems.at[slot])

    for k, (_, stage, _) in enumerate(rings):
        for c in range(min(stage.shape[0] - 1, len(chunks[k]))):
            copy(k, c).start()
    for c in range(max(len(ch) for ch in chunks)):
        for k, (_, stage, _) in enumerate(rings):
            if c >= len(chunks[k]):
                continue
            n_slots, rows = stage.shape[0], stage.shape[1]
            if c + n_slots - 1 < len(chunks[k]):
                copy(k, c + n_slots - 1).start()
            copy(k, c).wait()
            src, dst, r0 = chunks[k][c]
            dst[r0:r0 + rows, :] = stage[c % n_slots, :, 0:src.shape[1]].astype(BF16)


def _layer_prompt_kernel(tile, nj, x_ref, mod_ref, gmix_ref, win_hbm, convw_ref, convb_ref,
                         clng_ref, clnb_ref, rlng_ref, rlnb_ref, wout_hbm,
                         cos_ref, sin_ref, dec_ref, qdec_ref, kdec_ref, sdec_ref,
                         mod2_ref, gffn_ref, w1_hbm, w2_hbm, gfin_ref,
                         y_ref, hist_ref, state_ref, win_out, wout_out, w1_out, w2_out,
                         ubuf, ushift, sbuf, mixbuf, x1buf,
                         win_ref, wout_ref, w1_ref, w2_ref,
                         stage_wide, stage_tall, sems_wide, sems_tall, sems_out):
    g = pl.program_id(0)
    total = pl.num_programs(0) - 1
    j = lax.rem(jnp.minimum(g, total - 1), nj)
    live = g < total
    export = [pltpu.make_async_copy(src, dst, sems_out.at[k]) for k, (src, dst) in enumerate(
        [(win_ref, win_out), (wout_ref, wout_out), (w1_ref, w1_out), (w2_ref, w2_out)])]

    @pl.when(g == 0)
    def _():
        _stream_cast_weights([
            ([(win_hbm, win_ref), (w1_hbm, w1_ref)], stage_wide, sems_wide),
            ([(wout_hbm, wout_ref), (w2_hbm, w2_ref)], stage_tall, sems_tall)])
        for cp in export:
            cp.start()
        x1buf[...] = jnp.zeros((tile, D_MODEL), F32)

    @pl.when(jnp.logical_and(j == 0, live))
    def _():
        ubuf[0:HIST_ROWS, :] = jnp.zeros((HIST_ROWS, CONV_CH), F32)
        sbuf[...] = jnp.zeros(sbuf.shape, F32)

    x = x_ref[0]
    b_mix = lax.div(jnp.minimum(g, total - 1), nj)
    b_mlp = lax.div(jnp.maximum(g - 1, 0), nj)
    mods = lambda ref, b: [ref[pl.ds(b, 1), k * D_MODEL:(k + 1) * D_MODEL] for k in range(3)]
    shift, scale, gate = mods(mod_ref, b_mix)

    def conv_stages(u):
        ubuf[HIST_ROWS:HIST_ROWS + tile, :] = u
        off = HIST_ROWS - (CONV_K - 1)
        for r in range(SUBLANES):
            n = tile + SUBLANES * ((CONV_K - 1 - r) // SUBLANES)
            ushift[r, 0:n, :] = ubuf[off + r:off + r + n, :]
        ubuf[0:HIST_ROWS, :] = ubuf[tile:tile + HIST_ROWS, :]
        yield
        for rb in range(tile // CONV_ROW_BLOCK):
            r0 = rb * CONV_ROW_BLOCK
            dw = jnp.broadcast_to(convb_ref[...], (CONV_ROW_BLOCK, CONV_CH))
            for k in range(CONV_K):
                s0 = r0 + SUBLANES * (k // SUBLANES)
                wk = jnp.concatenate([convw_ref[k]] * (CONV_ROW_BLOCK // SUBLANES), axis=0)
                dw = dw + wk * ushift[k % SUBLANES, s0:s0 + CONV_ROW_BLOCK, :]
            conv_out = _silu(_layernorm(dw, clng_ref[...], clnb_ref[...]))
            mixbuf[r0:r0 + CONV_ROW_BLOCK, 0:CONV_CH] = conv_out.astype(BF16)
            yield

    def ret_stages(proj):
        ret_proj = []
        for k in range(4):
            ret_proj.append(proj(2 + k))
            yield
        q, kk, v, gt = ret_proj
        q = q * (RET_DK ** -0.5)
        lane = lax.broadcasted_iota(jnp.int32, (1, PAIR_LANES), 1)
        lo = lane < RET_DK
        first_half = (lane % RET_DK) < (RET_DK // 2)
        lo_b = lo.astype(BF16)
        hi_b = (~lo).astype(BF16)
        pairs = range(N_PAIRS)
        halves = (pairs[:N_PAIRS // 2], pairs[N_PAIRS // 2:])
        lanes = [slice(p * PAIR_LANES, (p + 1) * PAIR_LANES) for p in pairs]
        for c in range(tile // RET_CHUNK):
            rs = slice(c * RET_CHUNK, (c + 1) * RET_CHUNK)
            cos = cos_ref[rs, :]
            sin = sin_ref[rs, :]
            qc, kc, vc, scores, outs = {}, {}, {}, {}, {}
            for group in halves:
                for p in group:
                    qc[p] = _rotary_pair(q[rs, lanes[p]], cos, sin, first_half)
                    kc[p] = _rotary_pair(kk[rs, lanes[p]], cos, sin, first_half)
                    vc[p] = v[rs, lanes[p]].astype(BF16)
                    kb = kc[p].astype(BF16)
                    k_cat = jnp.concatenate([kb * lo_b, kb * hi_b], axis=0)
                    scores[p] = _dot_nt(qc[p].astype(BF16), k_cat)
                yield
            for group in halves:
                for p in group:
                    state = sbuf[p]
                    lhs = jnp.concatenate([(scores[p] * dec_ref[p]).astype(BF16),
                                           (qc[p] * qdec_ref[:, lanes[p]]).astype(BF16)], axis=1)
                    rhs = jnp.concatenate([vc[p] * lo_b, vc[p] * hi_b, state.astype(BF16)],
                                          axis=0)
                    outs[p] = _dot(lhs, rhs)
                    kd = (kc[p] * kdec_ref[:, lanes[p]]).astype(BF16)
                    sbuf[p] = (state * sdec_ref[p]
                               + jnp.where(sdec_ref[p] > 0.0, _dot_tn(kd, vc[p]), 0.0))
                yield
            for p in pairs:
                on = _pair_groupnorm(outs[p], lo, rlng_ref[:, lanes[p]], rlnb_ref[:, lanes[p]])
                mixbuf[rs, CONV_CH + p * PAIR_LANES:CONV_CH + (p + 1) * PAIR_LANES] = (
                    on * _silu(gt[rs, lanes[p]])).astype(BF16)
            yield

    ffn = _ffn_stages(x1buf[...], mods(mod2_ref, b_mlp), gffn_ref, w1_ref, w2_ref, gfin_ref, y_ref)
    next(ffn)
    hb = _rmsnorm_mod(x, gmix_ref[...], scale, shift).astype(BF16)

    def proj(k):
        return _dot(hb, win_ref[:, k * CONV_CH:(k + 1) * CONV_CH])

    streams = {"f": ffn, "c": conv_stages(proj(0) * jax.nn.sigmoid(proj(1))),
               "r": ret_stages(proj)}
    for name in LAYER_PROGRAM.replace(" ", ""):
        next(streams[name], None)
    for stream in (streams["c"], streams["r"], streams["f"]):
        for _ in stream:
            pass

    mix = _dot(mixbuf[...], wout_ref[...])
    x1buf[...] = x + gate * mix

    @pl.when(jnp.logical_and(j == nj - 1, live))
    def _():
        hist_ref[0] = ubuf[0:HIST_ROWS, :]
        state_ref[0] = sbuf[...]

    @pl.when(g == total)
    def _():
        for cp in export:
            cp.wait()


def _layer_prompt(x, mod, mod_row0, g_mix, w_in, conv_w, conv_b, cln_g, cln_b, rln_g, rln_b,
                  w_out, g_ffn, w1, w2, g_final, rope, decay, tile):
    bsz, seq, d = x.shape
    assert bsz % SUBLANES == 0 and mod_row0 % bsz == 0
    in_hbm = pl.BlockSpec(memory_space=pl.ANY)

    def stage_shape(*ws):
        width = max(w.shape[1] for w in ws)
        rows = WEIGHT_STAGE_ELEMS // width
        assert all(w.shape[0] % rows == 0 for w in ws)
        return pltpu.VMEM((WEIGHT_STAGE_SLOTS, rows, width), F32)

    mod_blk = mod_row0 // bsz
    nj = seq // tile
    total = bsz * nj
    cos, sin = rope
    dec, qdec, kdec, sdec = decay
    mix_tile = lambda g: jnp.minimum(g, total - 1)
    ffn_tile = lambda g: jnp.maximum(g - 1, 0)
    resident = lambda shape: pl.BlockSpec(shape, lambda g: (0,) * len(shape),
                                          pipeline_mode=pl.Buffered(1))
    row = lambda a: a.reshape(1, -1)
    conv_w_rows = jnp.broadcast_to(conv_w[:, None, :], (CONV_K, SUBLANES, CONV_CH))
    rope_spec = pl.BlockSpec((tile, PAIR_LANES), lambda g: (mix_tile(g) % nj, 0))
    return pl.pallas_call(
        functools.partial(_layer_prompt_kernel, tile, nj),
        grid=(total + 1,),
        in_specs=[
            pl.BlockSpec((1, tile, d), lambda g: (mix_tile(g) // nj, mix_tile(g) % nj, 0)),
            pl.BlockSpec((bsz, 3 * d), lambda g: (mod_blk, 0), pipeline_mode=pl.Buffered(1)),
            resident((1, d)),
            in_hbm,
            resident((CONV_K, SUBLANES, CONV_CH)),
            resident((1, CONV_CH)),
            resident((1, CONV_CH)),
            resident((1, CONV_CH)),
            resident((1, RET_W)),
            resident((1, RET_W)),
            in_hbm,
            rope_spec,
            rope_spec,
            resident(dec.shape),
            resident(qdec.shape),
            resident(kdec.shape),
            resident(sdec.shape),
            pl.BlockSpec((bsz, 3 * d), lambda g: (mod_blk, 1), pipeline_mode=pl.Buffered(1)),
            resident((1, d)),
            in_hbm,
            in_hbm,
            resident((1, d)),
        ],
        out_specs=[
            pl.BlockSpec((1, tile, d), lambda g: (ffn_tile(g) // nj, ffn_tile(g) % nj, 0)),
            pl.BlockSpec((1, HIST_ROWS, CONV_CH), lambda g: (mix_tile(g) // nj, 0, 0)),
            pl.BlockSpec((1, N_PAIRS, PAIR_LANES, PAIR_LANES),
                         lambda g: (mix_tile(g) // nj, 0, 0, 0)),
            in_hbm, in_hbm, in_hbm, in_hbm,
        ],
        out_shape=(
            jax.ShapeDtypeStruct((bsz, seq, d), F32),
            jax.ShapeDtypeStruct((bsz, HIST_ROWS, CONV_CH), F32),
            jax.ShapeDtypeStruct((bsz, N_PAIRS, PAIR_LANES, PAIR_LANES), F32),
        ) + tuple(jax.ShapeDtypeStruct(w.shape, BF16) for w in (w_in, w_out, w1, w2)),
        scratch_shapes=[
            pltpu.VMEM((HIST_ROWS + tile, CONV_CH), F32),
            pltpu.VMEM((SUBLANES, HIST_ROWS + tile, CONV_CH), F32),
            pltpu.VMEM((N_PAIRS, PAIR_LANES, PAIR_LANES), F32),
            pltpu.VMEM((tile, D_MODEL), BF16),
            pltpu.VMEM((tile, D_MODEL), F32),
            pltpu.VMEM(w_in.shape, BF16),
            pltpu.VMEM(w_out.shape, BF16),
            pltpu.VMEM(w1.shape, BF16),
            pltpu.VMEM(w2.shape, BF16),
            stage_shape(w_in, w1),
            stage_shape(w_out, w2),
            pltpu.SemaphoreType.DMA((WEIGHT_STAGE_SLOTS,)),
            pltpu.SemaphoreType.DMA((WEIGHT_STAGE_SLOTS,)),
            pltpu.SemaphoreType.DMA((4,)),
        ],
        compiler_params=pltpu.CompilerParams(
            dimension_semantics=("arbitrary",), vmem_limit_bytes=VMEM_LIMIT),
        name="layer_prompt",
    )(x, mod, row(g_mix), w_in, conv_w_rows, row(conv_b), row(cln_g), row(cln_b),
      row(rln_g), row(rln_b), w_out, cos, sin, dec, qdec, kdec, sdec,
      mod, row(g_ffn), w1, w2, row(g_final))


def _ffn_stages(x, mods, gffn_ref, w1_ref, w2_ref, gfin_ref, y_ref):
    shift, scale, gate = mods
    hb = _rmsnorm_mod(x, gffn_ref[...], scale, shift).astype(BF16)
    yield
    half = D_MODEL // 2
    ff = [None, None]
    for c in range(D_FF // (2 * FFN_CHUNK)):
        acts = []
        for h in range(2):
            c0 = (2 * c + h) * FFN_CHUNK
            a = jnp.maximum(_dot(hb, w1_ref[:, c0:c0 + FFN_CHUNK]), 0.0)
            acts.append((a * a).astype(BF16))
            yield
        act = jnp.concatenate(acts, axis=1)
        for n in range(2):
            part = _dot(act, w2_ref[2 * c * FFN_CHUNK:2 * (c + 1) * FFN_CHUNK,
                                    n * half:(n + 1) * half])
            ff[n] = part if ff[n] is None else ff[n] + part
            yield
    z = x + gate * jnp.concatenate(ff, axis=1)
    y_ref[0] = z * lax.rsqrt(jnp.mean(z * z, axis=-1, keepdims=True) + EPS) * gfin_ref[...]


def _pos_cols(x_ref, i):
    return x_ref[:, i * D_MODEL:(i + 1) * D_MODEL]


def _mod_cols(mod_ref, n_seq, k):
    return mod_ref[0:n_seq, k * D_MODEL:(k + 1) * D_MODEL]


def _mlp_sample_kernel(n_pos, n_seq, x_ref, conv_ref, ret_ref, wout_ref, mod_ref,
                       gffn_ref, w1_ref, w2_ref, gfin_ref, y_ref, x1buf, hbuf, acc):
    c = pl.program_id(0)
    rows = lambda i: slice(i * n_seq, (i + 1) * n_seq)

    @pl.when(c == 0)
    def _():
        mix = (_dot(conv_ref[...], wout_ref[0:CONV_CH, :].astype(BF16))
               + _dot(ret_ref[...], wout_ref[CONV_CH:CONV_CH + RET_W, :].astype(BF16)))
        for i in range(n_pos):
            x1 = _pos_cols(x_ref, i) + _mod_cols(mod_ref, n_seq, 2) * mix[rows(i)]
            x1buf[rows(i), :] = x1
            hbuf[rows(i), :] = _rmsnorm_mod(x1, gffn_ref[...], _mod_cols(mod_ref, n_seq, 4),
                                            _mod_cols(mod_ref, n_seq, 3)).astype(BF16)

    a = jnp.maximum(_dot(hbuf[...], w1_ref[...].astype(BF16)), 0.0)
    part = _dot((a * a).astype(BF16), w2_ref[...].astype(BF16))

    @pl.when(c == 0)
    def _():
        acc[...] = part

    @pl.when(c > 0)
    def _():
        acc[...] += part

    @pl.when(c == pl.num_programs(0) - 1)
    def _():
        for i in range(n_pos):
            z = x1buf[rows(i), :] + _mod_cols(mod_ref, n_seq, 5) * acc[rows(i), :]
            y_ref[:, i * D_MODEL:(i + 1) * D_MODEL] = (
                z * lax.rsqrt(jnp.mean(z * z, axis=-1, keepdims=True) + EPS) * gfin_ref[...])


def _mlp_sample(x, conv_out, ret_out, w_out, mod, g_ffn, w1, w2, g_final):
    n_seq, d = x.shape[0], D_MODEL
    n_pos = x.shape[1] // d
    n_tok = n_pos * n_seq
    n_chunks = D_FF // SAMPLE_FFN_CHUNK
    whole = lambda shape: pl.BlockSpec(shape, lambda c: (0,) * len(shape),
                                       pipeline_mode=pl.Buffered(1))
    return pl.pallas_call(
        functools.partial(_mlp_sample_kernel, n_pos, n_seq),
        grid=(n_chunks,),
        in_specs=[whole(x.shape), whole(conv_out.shape), whole(ret_out.shape),
                  whole(w_out.shape), whole(mod.shape), whole((1, d)),
                  pl.BlockSpec((d, SAMPLE_FFN_CHUNK), lambda c: (0, c)),
                  pl.BlockSpec((SAMPLE_FFN_CHUNK, d), lambda c: (c, 0)),
                  whole((1, d))],
        out_specs=pl.BlockSpec(x.shape, lambda c: (0, 0)),
        out_shape=jax.ShapeDtypeStruct(x.shape, F32),
        scratch_shapes=[pltpu.VMEM((n_tok, d), F32), pltpu.VMEM((n_tok, d), BF16),
                        pltpu.VMEM((n_tok, d), F32)],
        compiler_params=pltpu.CompilerParams(
            dimension_semantics=("arbitrary",), vmem_limit_bytes=VMEM_LIMIT),
        name="mlp_sample",
    )(x, conv_out, ret_out, w_out, mod, g_ffn.reshape(1, d), w1, w2, g_final.reshape(1, d))


def _power_table_kernel(o_ref):
    lane_h = (lax.broadcasted_iota(jnp.int32, o_ref.shape, 1) // RET_DK).astype(F32)
    n = lax.broadcasted_iota(jnp.int32, o_ref.shape, 0).astype(F32)
    o_ref[...] = jnp.exp(_log_decay(lane_h) * n)


def _const_tables_kernel(pos_sample, cos_p, sin_p, cos_s, sin_s, dec, qdec, kdec, sdec, powers):
    _rope_blocked_kernel(0, cos_p, sin_p)
    _rope_kernel(pos_sample, cos_s, sin_s)
    _decay_kernel(RET_CHUNK, dec, qdec, kdec, sdec)
    _power_table_kernel(powers)


def _const_tables(seq, pos_sample):
    assert seq % ROPE_BLOCK == 0 and seq // ROPE_BLOCK >= SUBLANES
    f32 = lambda *shape: jax.ShapeDtypeStruct(shape, F32)
    c = RET_CHUNK
    out = pl.pallas_call(
        functools.partial(_const_tables_kernel, pos_sample),
        out_shape=(f32(seq, PAIR_LANES), f32(seq, PAIR_LANES), f32(8, PAIR_LANES),
                   f32(8, PAIR_LANES), f32(N_PAIRS, c, 2 * c), f32(c, RET_W), f32(c, RET_W),
                   f32(N_PAIRS, PAIR_LANES, PAIR_LANES), f32(8, RET_W)),
        name="const_tables",
    )()
    return {"rope_prompt": out[0:2], "rope_sample": out[2:4], "decay": out[4:8],
            "powers": out[8]}


def _proj_sample_kernel(n_pos, n_seq, x_ref, mod_ref, gmix_ref, win_ref, cache_ref, convw_ref,
                        convb_ref, clng_ref, clnb_ref, cos_ref, sin_ref,
                        hist_ref, conv_ref, q_ref, k_ref, v_ref, g_ref):
    shift, scale = _mod_cols(mod_ref, n_seq, 0), _mod_cols(mod_ref, n_seq, 1)
    hb = jnp.concatenate(
        [_rmsnorm_mod(_pos_cols(x_ref, i), gmix_ref[...], scale, shift).astype(BF16)
         for i in range(n_pos)], axis=0)

    def proj(k):
        return _dot(hb, win_ref[:, k * CONV_CH:(k + 1) * CONV_CH].astype(BF16))

    u = proj(0) * jax.nn.sigmoid(proj(1))
    n_hist = CONV_K - 1
    rows = lambda a, i: a[i * n_seq:(i + 1) * n_seq]
    for m in range(n_hist - n_pos):
        hist_ref[m] = cache_ref[m + n_pos]
    for i in range(n_pos):
        hist_ref[n_hist - n_pos + i] = rows(u, i)
        dw = jnp.broadcast_to(convb_ref[...], (n_seq, CONV_CH))
        for k in range(CONV_K):
            m = i + k
            src = cache_ref[m] if m < n_hist else rows(u, m - n_hist)
            dw = dw + convw_ref[k:k + 1, :] * src
        conv_ref[i] = _silu(_layernorm(dw, clng_ref[...], clnb_ref[...])).astype(BF16)

    lane = lax.broadcasted_iota(jnp.int32, (1, PAIR_LANES), 1)
    first_half = (lane % RET_DK) < (RET_DK // 2)
    q = proj(2) * (RET_DK ** -0.5)
    kk = proj(3)
    v = proj(4)
    gt = proj(5)
    for i in range(n_pos):
        cos = cos_ref[i:i + 1, :]
        sin = sin_ref[i:i + 1, :]
        for p in range(N_PAIRS):
            sl = slice(p * PAIR_LANES, (p + 1) * PAIR_LANES)
            q_ref[i, :, sl] = _rotary_pair(rows(q, i)[:, sl], cos, sin, first_half)
            k_ref[i, :, sl] = _rotary_pair(rows(kk, i)[:, sl], cos, sin, first_half)
        v_ref[i] = rows(v, i)
        g_ref[i] = rows(gt, i)


def _proj_sample(x, mod, g_mix, w_in, cache_t, conv_w, conv_b, cln_g, cln_b, rope):
    n_seq, d = x.shape[0], D_MODEL
    n_pos = x.shape[1] // d
    blk = SAMPLE_SEQ_BLOCK
    cos, sin = rope
    row = lambda a: a.reshape(1, -1)
    whole = lambda a: pl.BlockSpec(a.shape, lambda s: (0,) * a.ndim, pipeline_mode=pl.Buffered(1))
    seqs = lambda lead, width: pl.BlockSpec((lead, blk, width), lambda s: (0, s, 0))
    tok = lambda dt: jax.ShapeDtypeStruct((n_pos, n_seq, RET_W), dt)
    args = (x, mod, row(g_mix), w_in, cache_t, conv_w, row(conv_b), row(cln_g), row(cln_b),
            cos, sin)
    return pl.pallas_call(
        functools.partial(_proj_sample_kernel, n_pos, blk),
        grid=(n_seq // blk,),
        in_specs=[pl.BlockSpec((blk, n_pos * d), lambda s: (s, 0)),
                  pl.BlockSpec((blk, mod.shape[1]), lambda s: (s, 0)),
                  whole(args[2]), whole(w_in), seqs(cache_t.shape[0], CONV_CH)]
        + [whole(a) for a in args[5:]],
        out_specs=[seqs(cache_t.shape[0], CONV_CH)] + [seqs(n_pos, RET_W)] * 5,
        out_shape=(jax.ShapeDtypeStruct(cache_t.shape, F32), tok(BF16),
                   tok(F32), tok(F32), tok(F32), tok(F32)),
        compiler_params=pltpu.CompilerParams(
            dimension_semantics=("arbitrary",), vmem_limit_bytes=VMEM_LIMIT),
        name="proj_sample",
    )(*args)


def _ret_sample_kernel(n_pos, n_seq, q_ref, k_ref, v_ref, g_ref, st_ref, pw_ref, rlng_ref,
                       rlnb_ref, out_ref, snew_ref, sb_buf, qd_buf, kd_buf, v_buf):
    lane = lax.broadcasted_iota(jnp.int32, (1, PAIR_LANES), 1)
    lo = lane < RET_DK
    rows = lambda a, i: a[i * n_seq:(i + 1) * n_seq]
    cols = lambda a, i: a[:, i * n_seq:(i + 1) * n_seq]
    pw = pw_ref[...]
    q = q_ref[...]
    k = k_ref[...]
    qb = [rows(q, i).astype(BF16) for i in range(n_pos)]
    kb = [rows(k, i).astype(BF16) for i in range(n_pos)]
    vb = [rows(v_ref[...], i).astype(BF16) for i in range(n_pos)]

    o = []
    for i in range(n_pos):
        acc = jnp.zeros((n_seq, PAIR_LANES), F32)
        for j in range(i + 1):
            prod = qb[i].astype(F32) * kb[j].astype(F32)
            s_lo = jnp.sum(jnp.where(lo, prod, 0.0), axis=-1, keepdims=True)
            s_hi = jnp.sum(jnp.where(lo, 0.0, prod), axis=-1, keepdims=True)
            score = jnp.where(lo, s_lo, s_hi) * pw[i - j:i - j + 1, :]
            acc = acc + score.astype(BF16).astype(F32) * vb[j].astype(F32)
        o.append(acc)

    rounded = lambda x: x.astype(BF16).astype(F32)
    qd_buf[...] = jnp.concatenate(
        [rounded(rows(q, i) * pw[i + 1:i + 2, :]) for i in range(n_pos)], axis=0).T
    kd_buf[...] = jnp.concatenate(
        [rounded(rows(k, j) * pw[n_pos - 1 - j:n_pos - j, :]) for j in range(n_pos)], axis=0).T
    v_buf[...] = jnp.concatenate([vb[j].astype(F32) for j in range(n_pos)], axis=0).T
    tile_at = lambda r: pl.ds(pl.multiple_of(r * RET_DV, RET_DV), RET_DV)
    pos = lambda i: slice(i * n_seq, (i + 1) * n_seq)

    for a in range(2):
        gam = pw[n_pos:n_pos + 1, a * RET_DK:a * RET_DK + 1]
        v_rows = slice(a * RET_DV, (a + 1) * RET_DV)

        def update(grp, carry, a=a, gam=gam, v_rows=v_rows):
            r0 = pl.multiple_of(a * RET_DK + grp * SUBLANES, SUBLANES)
            kd_rows = kd_buf[pl.ds(r0, SUBLANES), :]
            for s in range(SUBLANES):
                r = r0 + s
                s_old = st_ref[tile_at(r), :]
                sb_buf[tile_at(r), :] = rounded(s_old)
                s_new = s_old * gam
                for j in range(n_pos):
                    s_new = s_new + kd_rows[s:s + 1, pos(j)] * v_buf[v_rows, pos(j)]
                snew_ref[tile_at(r), :] = s_new
            return carry

        lax.fori_loop(0, RET_DK // SUBLANES, update, 0)

    for i in range(n_pos):
        heads = []
        for a in range(2):
            def cross(grp, acc, a=a, i=i):
                r0 = pl.multiple_of(a * RET_DK + grp * SUBLANES, SUBLANES)
                qd_rows = qd_buf[pl.ds(r0, SUBLANES), pos(i)]
                for s in range(SUBLANES):
                    acc = acc + qd_rows[s:s + 1, :] * sb_buf[tile_at(r0 + s), :]
                return acc

            heads.append(lax.fori_loop(0, RET_DK // SUBLANES, cross,
                                       jnp.zeros((RET_DV, n_seq), F32)))
        oi = o[i] + jnp.concatenate(heads, axis=0).T
        on = _pair_groupnorm(oi, lo, rlng_ref[...], rlnb_ref[...])
        out_ref[i * n_seq:(i + 1) * n_seq, :] = (on * _silu(rows(g_ref[...], i))).astype(BF16)


def _ret_sample(q, k, v, g, state_t, pw, rln_g, rln_b, n_pos, n_seq):
    n_tok = n_pos * n_seq
    pair_state = 2 * RET_DK * RET_DV
    slab = pl.BlockSpec((n_tok, PAIR_LANES), lambda p: (0, p))
    state_spec = pl.BlockSpec((pair_state, n_seq), lambda p: (p, 0))
    return pl.pallas_call(
        functools.partial(_ret_sample_kernel, n_pos, n_seq),
        grid=(N_PAIRS,),
        in_specs=[slab, slab, slab, slab, state_spec,
                  pl.BlockSpec((8, PAIR_LANES), lambda p: (0, p)),
                  pl.BlockSpec((1, PAIR_LANES), lambda p: (0, p)),
                  pl.BlockSpec((1, PAIR_LANES), lambda p: (0, p))],
        out_specs=[slab, state_spec],
        out_shape=(jax.ShapeDtypeStruct((n_tok, RET_W), BF16),
                   jax.ShapeDtypeStruct(state_t.shape, F32)),
        scratch_shapes=[pltpu.VMEM((pair_state, n_seq), F32)]
        + [pltpu.VMEM((PAIR_LANES, n_tok), F32)] * 3,
        compiler_params=pltpu.CompilerParams(
            dimension_semantics=("arbitrary",), vmem_limit_bytes=VMEM_LIMIT),
        name="ret_sample",
    )(q, k, v, g, state_t, pw, rln_g.reshape(1, RET_W), rln_b.reshape(1, RET_W))


def _unpack_state(s2):
    blocks = []
    for p in range(N_PAIRS):
        for a in range(2):
            sl = slice(a * RET_DK, (a + 1) * RET_DK)
            blocks.append(s2[:, p, sl, sl])
    return jnp.stack(blocks, axis=1)


def _prompt_path(x, mod, mod_row0, g_mix, w_in, conv_w, conv_b, cln_g, cln_b, rln_g, rln_b,
                 w_out, g_ffn, w_ff1, w_ff2, g_final, tables, tile=PROMPT_TILE):
    rope, decay = tables["rope_prompt"], tables["decay"]
    y, hist, s2, *w_bf16 = _layer_prompt(x, mod, mod_row0, g_mix, w_in, conv_w, conv_b, cln_g,
                                         cln_b, rln_g, rln_b, w_out, g_ffn, w_ff1, w_ff2, g_final,
                                         rope, decay, tile)
    return y, hist[:, HIST_ROWS - (CONV_K - 1):], _unpack_state(s2), w_bf16


def _sample_path(x, mod, cache, state, g_mix, w_in, conv_w, conv_b, cln_g, cln_b,
                 rln_g, rln_b, w_out, g_ffn, w_ff1, w_ff2, g_final, tables):
    n_seq, n_pos, d = x.shape
    xs = x.reshape(n_seq, n_pos * d)
    rope = tables["rope_sample"]
    hist_t, *tok = _proj_sample(xs, mod, g_mix, w_in, cache.transpose(1, 0, 2),
                                conv_w, conv_b, cln_g, cln_b, rope)
    conv_out, q, k, v, g = [a.reshape(n_pos * n_seq, -1) for a in tok]
    ret_out, s_new_t = _ret_sample(q, k, v, g, state.reshape(n_seq, -1).T, tables["powers"],
                                   rln_g, rln_b, n_pos, n_seq)
    s_new = s_new_t.T
    y = _mlp_sample(xs, conv_out, ret_out, w_out, mod, g_ffn, w_ff1, w_ff2, g_final)
    return y.reshape(x.shape), hist_t.transpose(1, 0, 2), s_new.reshape(state.shape)


def kernel(x_prompt, x_sample, cache_conv, state_ret, c_prompt, c_sample, w_ada, b_ada, g_mix, w_in, conv_w, conv_b, conv_ln_g, conv_ln_b, ret_ln_g, ret_ln_b, w_out, g_ffn, w_ff1, w_ff2, g_final):
    mod = _modulation(jnp.concatenate([c_sample, c_prompt], axis=0), w_ada[0], b_ada[0])
    params = (g_mix[0], w_in[0], conv_w[0], conv_b[0], conv_ln_g[0], conv_ln_b[0], ret_ln_g[0],
              ret_ln_b[0], w_out[0], g_ffn[0], w_ff1[0], w_ff2[0], g_final)
    tables = _const_tables(x_prompt.shape[1], PAST_LEN)
    y_p, conv_p, ret_p, (w_in_b, w_out_b, w_ff1_b, w_ff2_b) = _prompt_path(
        x_prompt, mod, x_sample.shape[0], *params, tables)
    params_b = (g_mix[0], w_in_b, conv_w[0], conv_b[0], conv_ln_g[0], conv_ln_b[0], ret_ln_g[0],
                ret_ln_b[0], w_out_b, g_ffn[0], w_ff1_b, w_ff2_b, g_final)
    y_s, conv_s, ret_s = _sample_path(x_sample, mod, cache_conv[0], state_ret[0], *params_b,
                                      tables)
    return (y_p, y_s, conv_p[None], ret_p[None], conv_s[None], ret_s[None])
```

```python
import functools

import jax
import jax.numpy as jnp
from jax import lax
from jax.experimental import pallas as pl
from jax.experimental.pallas import tpu as pltpu

D_MODEL = 1024
CONV_CH = 512
CONV_K = 31
RET_HEADS = 8
RET_W = 512
RET_DK = 64
RET_DV = 64
D_FF = 4 * D_MODEL
RET_CHUNK = 128
ROPE_THETA = 10000.0
EPS = 1e-6
N_MOD = 6
PAST_LEN = 16384

SUBLANES = 8
HIST_ROWS = 32
CONV_ROW_BLOCK = 32
PROMPT_TILE = 256
FFN_CHUNK = 512
SAMPLE_FFN_CHUNK = 1024
SAMPLE_SEQ_BLOCK = 64
WEIGHT_STAGE_ELEMS = 512 * 1024
WEIGHT_STAGE_SLOTS = 3
LAYER_PROGRAM = ("c cf cf cf cf cf cf cf cf "
                 "rrrr rfrf rfrf r "
                 "rfrf rfrf r")
PAIR_LANES = 2 * RET_DK
N_PAIRS = RET_HEADS // 2
VMEM_LIMIT = 56 * 1024 * 1024

F32 = jnp.float32
BF16 = jnp.bfloat16


def _log_decay(head):
    return jnp.log1p(-jnp.exp2(-5.0 - head))


def _rope_kernel(pos0, cos_ref, sin_ref):
    rows = cos_ref.shape[0]
    lane = lax.broadcasted_iota(jnp.int32, (rows, PAIR_LANES), 1)
    row = lax.broadcasted_iota(jnp.int32, (rows, PAIR_LANES), 0)
    half = RET_DK // 2
    freq = (lane % half).astype(F32)
    inv = jnp.power(jnp.full_like(freq, ROPE_THETA), -freq / half)
    ang = (row + pos0).astype(F32) * inv
    first = (lane % RET_DK) < half
    cos_ref[...] = jnp.cos(ang)
    sin_ref[...] = jnp.where(first, -jnp.sin(ang), jnp.sin(ang))


ROPE_BLOCK = 16


def _rope_blocked_kernel(pos0, cos_ref, sin_ref):
    rows = cos_ref.shape[0]
    n_blk = rows // ROPE_BLOCK
    half = RET_DK // 2

    def angles(n, step, start):
        lane = lax.broadcasted_iota(jnp.int32, (n, PAIR_LANES), 1)
        row = lax.broadcasted_iota(jnp.int32, (n, PAIR_LANES), 0)
        freq = (lane % half).astype(F32)
        inv = jnp.power(jnp.full_like(freq, ROPE_THETA), -freq / half)
        return (row * step + start).astype(F32) * inv

    a = angles(n_blk, ROPE_BLOCK, pos0)
    b = angles(ROPE_BLOCK, 1, 0)
    lane = lax.broadcasted_iota(jnp.int32, (ROPE_BLOCK, PAIR_LANES), 1)
    sign = jnp.where((lane % RET_DK) < half, -1.0, 1.0)
    cos_a, sin_a = jnp.cos(a), jnp.sin(a)
    cos_b, sin_b = jnp.cos(b), jnp.sin(b)
    cos_bs, sin_bs = cos_b * sign, sin_b * sign
    for blk in range(n_blk):
        ca, sa = cos_a[blk:blk + 1, :], sin_a[blk:blk + 1, :]
        rs = slice(blk * ROPE_BLOCK, (blk + 1) * ROPE_BLOCK)
        cos_ref[rs, :] = ca * cos_b - sa * sin_b
        sin_ref[rs, :] = sa * cos_bs + ca * sin_bs


def _decay_kernel(chunk, dec_ref, qdec_ref, kdec_ref, sdec_ref):
    c = chunk
    shape = (N_PAIRS, c, 2 * c)
    col = lax.broadcasted_iota(jnp.int32, shape, 2)
    h = (2 * lax.broadcasted_iota(jnp.int32, shape, 0) + col // c).astype(F32)
    i = lax.broadcasted_iota(jnp.int32, shape, 1).astype(F32)
    j = (col % c).astype(F32)
    diff = i - j
    dec_ref[...] = jnp.where(diff >= 0, jnp.exp(_log_decay(h) * jnp.maximum(diff, 0.0)), 0.0)
    lane_h = (lax.broadcasted_iota(jnp.int32, (c, RET_W), 1) // RET_DK).astype(F32)
    idx = lax.broadcasted_iota(jnp.int32, (c, RET_W), 0).astype(F32)
    lg = _log_decay(lane_h)
    qdec_ref[...] = jnp.exp(lg * (idx + 1.0))
    kdec_ref[...] = jnp.exp(lg * (c - 1.0 - idx))
    p = lax.broadcasted_iota(jnp.int32, (N_PAIRS, PAIR_LANES, PAIR_LANES), 0)
    r = lax.broadcasted_iota(jnp.int32, (N_PAIRS, PAIR_LANES, PAIR_LANES), 1) // RET_DK
    q = lax.broadcasted_iota(jnp.int32, (N_PAIRS, PAIR_LANES, PAIR_LANES), 2) // RET_DK
    hh = (2 * p + r).astype(F32)
    sdec_ref[...] = jnp.where(r == q, jnp.exp(_log_decay(hh) * float(c)), 0.0)


def _mod_kernel(c_ref, w_lo_ref, w_hi_ref, b_ref, o_ref):
    c = c_ref[...]
    s = (c * jax.nn.sigmoid(c)).astype(BF16)
    half = w_lo_ref.shape[1]
    for k, w_ref in enumerate((w_lo_ref, w_hi_ref)):
        cs = slice(k * half, (k + 1) * half)
        o_ref[:, cs] = _dot(s, w_ref[...].astype(BF16)) + b_ref[:, cs]


def _modulation(c, w_ada, b_ada):
    n, d = c.shape
    width = w_ada.shape[1]
    blk = D_MODEL
    half = lambda k: pl.BlockSpec((d, blk // 2), lambda i: (0, 2 * i + k))
    return pl.pallas_call(
        _mod_kernel,
        grid=(width // blk,),
        in_specs=[pl.BlockSpec((n, d), lambda i: (0, 0)), half(0), half(1),
                  pl.BlockSpec((1, blk), lambda i: (0, i))],
        out_specs=pl.BlockSpec((n, blk), lambda i: (0, i)),
        out_shape=jax.ShapeDtypeStruct((n, width), F32),
        compiler_params=pltpu.CompilerParams(vmem_limit_bytes=VMEM_LIMIT),
        name="adaln_mod",
    )(c, w_ada, w_ada, b_ada.reshape(1, width))


def _rmsnorm_mod(x, g, scale, shift):
    y = x * lax.rsqrt(jnp.mean(x * x, axis=-1, keepdims=True) + EPS)
    return (y * g) * (1.0 + scale) + shift


def _layernorm(x, g, b):
    mu = jnp.mean(x, axis=-1, keepdims=True)
    d = x - mu
    var = jnp.mean(d * d, axis=-1, keepdims=True)
    return d * lax.rsqrt(var + EPS) * g + b


def _silu(x):
    return x * jax.nn.sigmoid(x)


def _rotary_pair(x, cos, sin_signed, first_half):
    partner = jnp.where(first_half, pltpu.roll(x, PAIR_LANES - RET_DK // 2, 1),
                        pltpu.roll(x, RET_DK // 2, 1))
    return x * cos + partner * sin_signed


def _pair_groupnorm(o, lo, g, b):
    inv_n = 1.0 / RET_DV
    s_lo = jnp.sum(jnp.where(lo, o, 0.0), axis=-1, keepdims=True)
    s_hi = jnp.sum(jnp.where(lo, 0.0, o), axis=-1, keepdims=True)
    d = o - jnp.where(lo, s_lo, s_hi) * inv_n
    d2 = d * d
    v_lo = jnp.sum(jnp.where(lo, d2, 0.0), axis=-1, keepdims=True)
    v_hi = jnp.sum(jnp.where(lo, 0.0, d2), axis=-1, keepdims=True)
    var = jnp.where(lo, v_lo, v_hi) * inv_n
    return d * lax.rsqrt(var + EPS) * g + b


def _dot(a, b):
    return jnp.dot(a, b, preferred_element_type=F32)


def _dot_nt(a, b):
    return lax.dot_general(a, b, (((1,), (1,)), ((), ())), preferred_element_type=F32)


def _dot_tn(a, b):
    return lax.dot_general(a, b, (((0,), (0,)), ((), ())), preferred_element_type=F32)


def _stream_cast_weights(rings):
    def ring_chunks(pairs, stage):
        rows = stage.shape[1]
        return [(src, dst, r0) for src, dst in pairs for r0 in range(0, src.shape[0], rows)]

    chunks = [ring_chunks(pairs, stage) for pairs, stage, _ in rings]

    def copy(k, c):
        _, stage, sems = rings[k]
        src, _, r0 = chunks[k][c]
        slot = c % stage.shape[0]
        return pltpu.make_async_copy(src.at[pl.ds(r0, stage.shape[1]), :],
                                     stage.at[slot, :, pl.ds(0, src.shape[1])], sems.at[slot])

    for k, (_, stage, _) in enumerate(rings):
        for c in range(min(stage.shape[0] - 1, len(chunks[k]))):
            copy(k, c).start()
    for c in range(max(len(ch) for ch in chunks)):
        for k, (_, stage, _) in enumerate(rings):
            if c >= len(chunks[k]):
                continue
            n_slots, rows = stage.shape[0], stage.shape[1]
            if c + n_slots - 1 < len(chunks[k]):
                copy(k, c + n_slots - 1).start()
            copy(k, c).wait()
            src, dst, r0 = chunks[k][c]
            dst[r0:r0 + rows, :] = stage[c % n_slots, :, 0:src.shape[1]].astype(BF16)


def _layer_prompt_kernel(tile, nj, x_ref, mod_ref, gmix_ref, win_hbm, convw_ref, convb_ref,
                         clng_ref, clnb_ref, rlng_ref, rlnb_ref, wout_hbm,
                         cos_ref, sin_ref, dec_ref, qdec_ref, kdec_ref, sdec_ref,
                         mod2_ref, gffn_ref, w1_hbm, w2_hbm, gfin_ref,
                         y_ref, hist_ref, state_ref, win_out, wout_out, w1_out, w2_out,
                         ubuf, ushift, sbuf, mixbuf, x1buf,
                         win_ref, wout_ref, w1_ref, w2_ref,
                         stage_wide, stage_tall, sems_wide, sems_tall, sems_out):
    g = pl.program_id(0)
    total = pl.num_programs(0) - 1
    j = lax.rem(jnp.minimum(g, total - 1), nj)
    live = g < total
    export = [pltpu.make_async_copy(src, dst, sems_out.at[k]) for k, (src, dst) in enumerate(
        [(win_ref, win_out), (wout_ref, wout_out), (w1_ref, w1_out), (w2_ref, w2_out)])]

    @pl.when(g == 0)
    def _():
        _stream_cast_weights([
            ([(win_hbm, win_ref), (w1_hbm, w1_ref)], stage_wide, sems_wide),
            ([(wout_hbm, wout_ref), (w2_hbm, w2_ref)], stage_tall, sems_tall)])
        for cp in export:
            cp.start()
        x1buf[...] = jnp.zeros((tile, D_MODEL), F32)

    @pl.when(jnp.logical_and(j == 0, live))
    def _():
        ubuf[0:HIST_ROWS, :] = jnp.zeros((HIST_ROWS, CONV_CH), F32)
        sbuf[...] = jnp.zeros(sbuf.shape, F32)

    x = x_ref[0]
    b_mix = lax.div(jnp.minimum(g, total - 1), nj)
    b_mlp = lax.div(jnp.maximum(g - 1, 0), nj)
    mods = lambda ref, b: [ref[pl.ds(b, 1), k * D_MODEL:(k + 1) * D_MODEL] for k in range(3)]
    shift, scale, gate = mods(mod_ref, b_mix)

    def conv_stages(u):
        ubuf[HIST_ROWS:HIST_ROWS + tile, :] = u
        off = HIST_ROWS - (CONV_K - 1)
        for r in range(SUBLANES):
            n = tile + SUBLANES * ((CONV_K - 1 - r) // SUBLANES)
            ushift[r, 0:n, :] = ubuf[off + r:off + r + n, :]
        ubuf[0:HIST_ROWS, :] = ubuf[tile:tile + HIST_ROWS, :]
        yield
        for rb in range(tile // CONV_ROW_BLOCK):
            r0 = rb * CONV_ROW_BLOCK
            dw = jnp.broadcast_to(convb_ref[...], (CONV_ROW_BLOCK, CONV_CH))
            for k in range(CONV_K):
                s0 = r0 + SUBLANES * (k // SUBLANES)
                wk = jnp.concatenate([convw_ref[k]] * (CONV_ROW_BLOCK // SUBLANES), axis=0)
                dw = dw + wk * ushift[k % SUBLANES, s0:s0 + CONV_ROW_BLOCK, :]
            conv_out = _silu(_layernorm(dw, clng_ref[...], clnb_ref[...]))
            mixbuf[r0:r0 + CONV_ROW_BLOCK, 0:CONV_CH] = conv_out.astype(BF16)
            yield

    def ret_stages(proj):
        ret_proj = []
        for k in range(4):
            ret_proj.append(proj(2 + k))
            yield
        q, kk, v, gt = ret_proj
        q = q * (RET_DK ** -0.5)
        lane = lax.broadcasted_iota(jnp.int32, (1, PAIR_LANES), 1)
        lo = lane < RET_DK
        first_half = (lane % RET_DK) < (RET_DK // 2)
        lo_b = lo.astype(BF16)
        hi_b = (~lo).astype(BF16)
        pairs = range(N_PAIRS)
        halves = (pairs[:N_PAIRS // 2], pairs[N_PAIRS // 2:])
        lanes = [slice(p * PAIR_LANES, (p + 1) * PAIR_LANES) for p in pairs]
        for c in range(tile // RET_CHUNK):
            rs = slice(c * RET_CHUNK, (c + 1) * RET_CHUNK)
            cos = cos_ref[rs, :]
            sin = sin_ref[rs, :]
            qc, kc, vc, scores, outs = {}, {}, {}, {}, {}
            for group in halves:
                for p in group:
                    qc[p] = _rotary_pair(q[rs, lanes[p]], cos, sin, first_half)
                    kc[p] = _rotary_pair(kk[rs, lanes[p]], cos, sin, first_half)
                    vc[p] = v[rs, lanes[p]].astype(BF16)
                    kb = kc[p].astype(BF16)
                    k_cat = jnp.concatenate([kb * lo_b, kb * hi_b], axis=0)
                    scores[p] = _dot_nt(qc[p].astype(BF16), k_cat)
                yield
            for group in halves:
                for p in group:
                    state = sbuf[p]
                    lhs = jnp.concatenate([(scores[p] * dec_ref[p]).astype(BF16),
                                           (qc[p] * qdec_ref[:, lanes[p]]).astype(BF16)], axis=1)
                    rhs = jnp.concatenate([vc[p] * lo_b, vc[p] * hi_b, state.astype(BF16)],
                                          axis=0)
                    outs[p] = _dot(lhs, rhs)
                    kd = (kc[p] * kdec_ref[:, lanes[p]]).astype(BF16)
                    sbuf[p] = (state * sdec_ref[p]
                               + jnp.where(sdec_ref[p] > 0.0, _dot_tn(kd, vc[p]), 0.0))
                yield
            for p in pairs:
                on = _pair_groupnorm(outs[p], lo, rlng_ref[:, lanes[p]], rlnb_ref[:, lanes[p]])
                mixbuf[rs, CONV_CH + p * PAIR_LANES:CONV_CH + (p + 1) * PAIR_LANES] = (
                    on * _silu(gt[rs, lanes[p]])).astype(BF16)
            yield

    ffn = _ffn_stages(x1buf[...], mods(mod2_ref, b_mlp), gffn_ref, w1_ref, w2_ref, gfin_ref, y_ref)
    next(ffn)
    hb = _rmsnorm_mod(x, gmix_ref[...], scale, shift).astype(BF16)

    def proj(k):
        return _dot(hb, win_ref[:, k * CONV_CH:(k + 1) * CONV_CH])

    streams = {"f": ffn, "c": conv_stages(proj(0) * jax.nn.sigmoid(proj(1))),
               "r": ret_stages(proj)}
    for name in LAYER_PROGRAM.replace(" ", ""):
        next(streams[name], None)
    for stream in (streams["c"], streams["r"], streams["f"]):
        for _ in stream:
            pass

    mix = _dot(mixbuf[...], wout_ref[...])
    x1buf[...] = x + gate * mix

    @pl.when(jnp.logical_and(j == nj - 1, live))
    def _():
        hist_ref[0] = ubuf[0:HIST_ROWS, :]
        for p in range(N_PAIRS):
            for a in range(2):
                sl = slice(a * RET_DK, (a + 1) * RET_DK)
                state_ref[0, 2 * p + a] = sbuf[p, sl, sl]

    @pl.when(g == total)
    def _():
        for cp in export:
            cp.wait()


def _layer_prompt(x, mod, mod_row0, g_mix, w_in, conv_w, conv_b, cln_g, cln_b, rln_g, rln_b,
                  w_out, g_ffn, w1, w2, g_final, rope, decay, tile):
    bsz, seq, d = x.shape
    assert bsz % SUBLANES == 0 and mod_row0 % bsz == 0
    in_hbm = pl.BlockSpec(memory_space=pl.ANY)

    def stage_shape(*ws):
        width = max(w.shape[1] for w in ws)
        rows = WEIGHT_STAGE_ELEMS // width
        assert all(w.shape[0] % rows == 0 for w in ws)
        return pltpu.VMEM((WEIGHT_STAGE_SLOTS, rows, width), F32)

    mod_blk = mod_row0 // bsz
    nj = seq // tile
    total = bsz * nj
    cos, sin = rope
    dec, qdec, kdec, sdec = decay
    mix_tile = lambda g: jnp.minimum(g, total - 1)
    ffn_tile = lambda g: jnp.maximum(g - 1, 0)
    resident = lambda shape: pl.BlockSpec(shape, lambda g: (0,) * len(shape),
                                          pipeline_mode=pl.Buffered(1))
    row = lambda a: a.reshape(1, -1)
    conv_w_rows = jnp.broadcast_to(conv_w[:, None, :], (CONV_K, SUBLANES, CONV_CH))
    rope_spec = pl.BlockSpec((tile, PAIR_LANES), lambda g: (mix_tile(g) % nj, 0))
    return pl.pallas_call(
        functools.partial(_layer_prompt_kernel, tile, nj),
        grid=(total + 1,),
        in_specs=[
            pl.BlockSpec((1, tile, d), lambda g: (mix_tile(g) // nj, mix_tile(g) % nj, 0)),
            pl.BlockSpec((bsz, 3 * d), lambda g: (mod_blk, 0), pipeline_mode=pl.Buffered(1)),
            resident((1, d)),
            in_hbm,
            resident((CONV_K, SUBLANES, CONV_CH)),
            resident((1, CONV_CH)),
            resident((1, CONV_CH)),
            resident((1, CONV_CH)),
            resident((1, RET_W)),
            resident((1, RET_W)),
            in_hbm,
            rope_spec,
            rope_spec,
            resident(dec.shape),
            resident(qdec.shape),
            resident(kdec.shape),
            resident(sdec.shape),
            pl.BlockSpec((bsz, 3 * d), lambda g: (mod_blk, 1), pipeline_mode=pl.Buffered(1)),
            resident((1, d)),
            in_hbm,
            in_hbm,
            resident((1, d)),
        ],
        out_specs=[
            pl.BlockSpec((1, tile, d), lambda g: (ffn_tile(g) // nj, ffn_tile(g) % nj, 0)),
            pl.BlockSpec((1, HIST_ROWS, CONV_CH), lambda g: (mix_tile(g) // nj, 0, 0)),
            pl.BlockSpec((1, RET_HEADS, RET_DK, RET_DV),
                         lambda g: (mix_tile(g) // nj, 0, 0, 0)),
            in_hbm, in_hbm, in_hbm, in_hbm,
        ],
        out_shape=(
            jax.ShapeDtypeStruct((bsz, seq, d), F32),
            jax.ShapeDtypeStruct((bsz, HIST_ROWS, CONV_CH), F32),
            jax.ShapeDtypeStruct((bsz, RET_HEADS, RET_DK, RET_DV), F32),
        ) + tuple(jax.ShapeDtypeStruct(w.shape, BF16) for w in (w_in, w_out, w1, w2)),
        scratch_shapes=[
            pltpu.VMEM((HIST_ROWS + tile, CONV_CH), F32),
            pltpu.VMEM((SUBLANES, HIST_ROWS + tile, CONV_CH), F32),
            pltpu.VMEM((N_PAIRS, PAIR_LANES, PAIR_LANES), F32),
            pltpu.VMEM((tile, D_MODEL), BF16),
            pltpu.VMEM((tile, D_MODEL), F32),
            pltpu.VMEM(w_in.shape, BF16),
            pltpu.VMEM(w_out.shape, BF16),
            pltpu.VMEM(w1.shape, BF16),
            pltpu.VMEM(w2.shape, BF16),
            stage_shape(w_in, w1),
            stage_shape(w_out, w2),
            pltpu.SemaphoreType.DMA((WEIGHT_STAGE_SLOTS,)),
            pltpu.SemaphoreType.DMA((WEIGHT_STAGE_SLOTS,)),
            pltpu.SemaphoreType.DMA((4,)),
        ],
        compiler_params=pltpu.CompilerParams(
            dimension_semantics=("arbitrary",), vmem_limit_bytes=VMEM_LIMIT),
        name="layer_prompt",
    )(x, mod, row(g_mix), w_in, conv_w_rows, row(conv_b), row(cln_g), row(cln_b),
      row(rln_g), row(rln_b), w_out, cos, sin, dec, qdec, kdec, sdec,
      mod, row(g_ffn), w1, w2, row(g_final))


def _ffn_stages(x, mods, gffn_ref, w1_ref, w2_ref, gfin_ref, y_ref):
    shift, scale, gate = mods
    hb = _rmsnorm_mod(x, gffn_ref[...], scale, shift).astype(BF16)
    yield
    half = D_MODEL // 2
    ff = [None, None]
    for c in range(D_FF // (2 * FFN_CHUNK)):
        acts = []
        for h in range(2):
            c0 = (2 * c + h) * FFN_CHUNK
            a = jnp.maximum(_dot(hb, w1_ref[:, c0:c0 + FFN_CHUNK]), 0.0)
            acts.append((a * a).astype(BF16))
            yield
        act = jnp.concatenate(acts, axis=1)
        for n in range(2):
            part = _dot(act, w2_ref[2 * c * FFN_CHUNK:2 * (c + 1) * FFN_CHUNK,
                                    n * half:(n + 1) * half])
            ff[n] = part if ff[n] is None else ff[n] + part
            yield
    z = x + gate * jnp.concatenate(ff, axis=1)
    y_ref[0] = z * lax.rsqrt(jnp.mean(z * z, axis=-1, keepdims=True) + EPS) * gfin_ref[...]


def _pos_cols(x_ref, i):
    return x_ref[:, i * D_MODEL:(i + 1) * D_MODEL]


def _mod_cols(mod_ref, n_seq, k):
    return mod_ref[0:n_seq, k * D_MODEL:(k + 1) * D_MODEL]


def _mlp_sample_kernel(n_pos, n_seq, x_ref, conv_ref, ret_ref, wout_ref, mod_ref,
                       gffn_ref, w1_ref, w2_ref, gfin_ref, y_ref, x1buf, hbuf, acc):
    c = pl.program_id(0)
    rows = lambda i: slice(i * n_seq, (i + 1) * n_seq)

    @pl.when(c == 0)
    def _():
        mix = (_dot(conv_ref[...], wout_ref[0:CONV_CH, :].astype(BF16))
               + _dot(ret_ref[...], wout_ref[CONV_CH:CONV_CH + RET_W, :].astype(BF16)))
        for i in range(n_pos):
            x1 = _pos_cols(x_ref, i) + _mod_cols(mod_ref, n_seq, 2) * mix[rows(i)]
            x1buf[rows(i), :] = x1
            hbuf[rows(i), :] = _rmsnorm_mod(x1, gffn_ref[...], _mod_cols(mod_ref, n_seq, 4),
                                            _mod_cols(mod_ref, n_seq, 3)).astype(BF16)

    a = jnp.maximum(_dot(hbuf[...], w1_ref[...].astype(BF16)), 0.0)
    part = _dot((a * a).astype(BF16), w2_ref[...].astype(BF16))

    @pl.when(c == 0)
    def _():
        acc[...] = part

    @pl.when(c > 0)
    def _():
        acc[...] += part

    @pl.when(c == pl.num_programs(0) - 1)
    def _():
        for i in range(n_pos):
            z = x1buf[rows(i), :] + _mod_cols(mod_ref, n_seq, 5) * acc[rows(i), :]
            y_ref[:, i * D_MODEL:(i + 1) * D_MODEL] = (
                z * lax.rsqrt(jnp.mean(z * z, axis=-1, keepdims=True) + EPS) * gfin_ref[...])


def _mlp_sample(x, conv_out, ret_out, w_out, mod, g_ffn, w1, w2, g_final):
    n_seq, d = x.shape[0], D_MODEL
    n_pos = x.shape[1] // d
    n_tok = n_pos * n_seq
    n_chunks = D_FF // SAMPLE_FFN_CHUNK
    whole = lambda shape: pl.BlockSpec(shape, lambda c: (0,) * len(shape),
                                       pipeline_mode=pl.Buffered(1))
    return pl.pallas_call(
        functools.partial(_mlp_sample_kernel, n_pos, n_seq),
        grid=(n_chunks,),
        in_specs=[whole(x.shape), whole(conv_out.shape), whole(ret_out.shape),
                  whole(w_out.shape), whole(mod.shape), whole((1, d)),
                  pl.BlockSpec((d, SAMPLE_FFN_CHUNK), lambda c: (0, c)),
                  pl.BlockSpec((SAMPLE_FFN_CHUNK, d), lambda c: (c, 0)),
                  whole((1, d))],
        out_specs=pl.BlockSpec(x.shape, lambda c: (0, 0)),
        out_shape=jax.ShapeDtypeStruct(x.shape, F32),
        scratch_shapes=[pltpu.VMEM((n_tok, d), F32), pltpu.VMEM((n_tok, d), BF16),
                        pltpu.VMEM((n_tok, d), F32)],
        compiler_params=pltpu.CompilerParams(
            dimension_semantics=("arbitrary",), vmem_limit_bytes=VMEM_LIMIT),
        name="mlp_sample",
    )(x, conv_out, ret_out, w_out, mod, g_ffn.reshape(1, d), w1, w2, g_final.reshape(1, d))


def _power_table_kernel(o_ref):
    lane_h = (lax.broadcasted_iota(jnp.int32, o_ref.shape, 1) // RET_DK).astype(F32)
    n = lax.broadcasted_iota(jnp.int32, o_ref.shape, 0).astype(F32)
    o_ref[...] = jnp.exp(_log_decay(lane_h) * n)


def _const_tables_kernel(pos_sample, cos_p, sin_p, cos_s, sin_s, dec, qdec, kdec, sdec, powers):
    _rope_blocked_kernel(0, cos_p, sin_p)
    _rope_kernel(pos_sample, cos_s, sin_s)
    _decay_kernel(RET_CHUNK, dec, qdec, kdec, sdec)
    _power_table_kernel(powers)


def _const_tables(seq, pos_sample):
    assert seq % ROPE_BLOCK == 0 and seq // ROPE_BLOCK >= SUBLANES
    f32 = lambda *shape: jax.ShapeDtypeStruct(shape, F32)
    c = RET_CHUNK
    out = pl.pallas_call(
        functools.partial(_const_tables_kernel, pos_sample),
        out_shape=(f32(seq, PAIR_LANES), f32(seq, PAIR_LANES), f32(8, PAIR_LANES),
                   f32(8, PAIR_LANES), f32(N_PAIRS, c, 2 * c), f32(c, RET_W), f32(c, RET_W),
                   f32(N_PAIRS, PAIR_LANES, PAIR_LANES), f32(8, RET_W)),
        name="const_tables",
    )()
    return {"rope_prompt": out[0:2], "rope_sample": out[2:4], "decay": out[4:8],
            "powers": out[8]}


def _proj_sample_kernel(n_pos, n_seq, x_ref, mod_ref, gmix_ref, win_ref, cache_ref, convw_ref,
                        convb_ref, clng_ref, clnb_ref, cos_ref, sin_ref,
                        hist_ref, conv_ref, q_ref, k_ref, v_ref, g_ref):
    shift, scale = _mod_cols(mod_ref, n_seq, 0), _mod_cols(mod_ref, n_seq, 1)
    hb = jnp.concatenate(
        [_rmsnorm_mod(_pos_cols(x_ref, i), gmix_ref[...], scale, shift).astype(BF16)
         for i in range(n_pos)], axis=0)

    def proj(k):
        return _dot(hb, win_ref[:, k * CONV_CH:(k + 1) * CONV_CH].astype(BF16))

    u = proj(0) * jax.nn.sigmoid(proj(1))
    n_hist = CONV_K - 1
    rows = lambda a, i: a[i * n_seq:(i + 1) * n_seq]
    for m in range(n_hist - n_pos):
        hist_ref[m] = cache_ref[m + n_pos]
    for i in range(n_pos):
        hist_ref[n_hist - n_pos + i] = rows(u, i)
        dw = jnp.broadcast_to(convb_ref[...], (n_seq, CONV_CH))
        for k in range(CONV_K):
            m = i + k
            src = cache_ref[m] if m < n_hist else rows(u, m - n_hist)
            dw = dw + convw_ref[k:k + 1, :] * src
        conv_ref[i] = _silu(_layernorm(dw, clng_ref[...], clnb_ref[...])).astype(BF16)

    lane = lax.broadcasted_iota(jnp.int32, (1, PAIR_LANES), 1)
    first_half = (lane % RET_DK) < (RET_DK // 2)
    q = proj(2) * (RET_DK ** -0.5)
    kk = proj(3)
    v = proj(4)
    gt = proj(5)
    for i in range(n_pos):
        cos = cos_ref[i:i + 1, :]
        sin = sin_ref[i:i + 1, :]
        for p in range(N_PAIRS):
            sl = slice(p * PAIR_LANES, (p + 1) * PAIR_LANES)
            q_ref[i, :, sl] = _rotary_pair(rows(q, i)[:, sl], cos, sin, first_half)
            k_ref[i, :, sl] = _rotary_pair(rows(kk, i)[:, sl], cos, sin, first_half)
        v_ref[i] = rows(v, i)
        g_ref[i] = rows(gt, i)


def _proj_sample(x, mod, g_mix, w_in, cache_t, conv_w, conv_b, cln_g, cln_b, rope):
    n_seq, d = x.shape[0], D_MODEL
    n_pos = x.shape[1] // d
    blk = SAMPLE_SEQ_BLOCK
    cos, sin = rope
    row = lambda a: a.reshape(1, -1)
    whole = lambda a: pl.BlockSpec(a.shape, lambda s: (0,) * a.ndim, pipeline_mode=pl.Buffered(1))
    seqs = lambda lead, width: pl.BlockSpec((lead, blk, width), lambda s: (0, s, 0))
    tok = lambda dt: jax.ShapeDtypeStruct((n_pos, n_seq, RET_W), dt)
    args = (x, mod, row(g_mix), w_in, cache_t, conv_w, row(conv_b), row(cln_g), row(cln_b),
            cos, sin)
    return pl.pallas_call(
        functools.partial(_proj_sample_kernel, n_pos, blk),
        grid=(n_seq // blk,),
        in_specs=[pl.BlockSpec((blk, n_pos * d), lambda s: (s, 0)),
                  pl.BlockSpec((blk, mod.shape[1]), lambda s: (s, 0)),
                  whole(args[2]), whole(w_in), seqs(cache_t.shape[0], CONV_CH)]
        + [whole(a) for a in args[5:]],
        out_specs=[seqs(cache_t.shape[0], CONV_CH)] + [seqs(n_pos, RET_W)] * 5,
        out_shape=(jax.ShapeDtypeStruct(cache_t.shape, F32), tok(BF16),
                   tok(F32), tok(F32), tok(F32), tok(F32)),
        compiler_params=pltpu.CompilerParams(
            dimension_semantics=("arbitrary",), vmem_limit_bytes=VMEM_LIMIT),
        name="proj_sample",
    )(*args)


def _ret_sample_kernel(n_pos, n_seq, q_ref, k_ref, v_ref, g_ref, st_ref, pw_ref, rlng_ref,
                       rlnb_ref, out_ref, snew_ref, sb_buf, qd_buf, kd_buf, v_buf):
    lane = lax.broadcasted_iota(jnp.int32, (1, PAIR_LANES), 1)
    lo = lane < RET_DK
    rows = lambda a, i: a[i * n_seq:(i + 1) * n_seq]
    cols = lambda a, i: a[:, i * n_seq:(i + 1) * n_seq]
    pw = pw_ref[...]
    q = q_ref[...]
    k = k_ref[...]
    qb = [rows(q, i).astype(BF16) for i in range(n_pos)]
    kb = [rows(k, i).astype(BF16) for i in range(n_pos)]
    vb = [rows(v_ref[...], i).astype(BF16) for i in range(n_pos)]

    o = []
    for i in range(n_pos):
        acc = jnp.zeros((n_seq, PAIR_LANES), F32)
        for j in range(i + 1):
            prod = qb[i].astype(F32) * kb[j].astype(F32)
            s_lo = jnp.sum(jnp.where(lo, prod, 0.0), axis=-1, keepdims=True)
            s_hi = jnp.sum(jnp.where(lo, 0.0, prod), axis=-1, keepdims=True)
            score = jnp.where(lo, s_lo, s_hi) * pw[i - j:i - j + 1, :]
            acc = acc + score.astype(BF16).astype(F32) * vb[j].astype(F32)
        o.append(acc)

    rounded = lambda x: x.astype(BF16).astype(F32)
    qd_buf[...] = jnp.concatenate(
        [rounded(rows(q, i) * pw[i + 1:i + 2, :]) for i in range(n_pos)], axis=0).T
    kd_buf[...] = jnp.concatenate(
        [rounded(rows(k, j) * pw[n_pos - 1 - j:n_pos - j, :]) for j in range(n_pos)], axis=0).T
    v_buf[...] = jnp.concatenate([vb[j].astype(F32) for j in range(n_pos)], axis=0).T
    tile_at = lambda r: pl.ds(pl.multiple_of(r * RET_DV, RET_DV), RET_DV)
    pos = lambda i: slice(i * n_seq, (i + 1) * n_seq)

    for a in range(2):
        gam = pw[n_pos:n_pos + 1, a * RET_DK:a * RET_DK + 1]
        v_rows = slice(a * RET_DV, (a + 1) * RET_DV)

        def update(grp, carry, a=a, gam=gam, v_rows=v_rows):
            r0 = pl.multiple_of(a * RET_DK + grp * SUBLANES, SUBLANES)
            kd_rows = kd_buf[pl.ds(r0, SUBLANES), :]
            for s in range(SUBLANES):
                r = r0 + s
                s_old = st_ref[tile_at(r), :]
                sb_buf[tile_at(r), :] = rounded(s_old)
                s_new = s_old * gam
                for j in range(n_pos):
                    s_new = s_new + kd_rows[s:s + 1, pos(j)] * v_buf[v_rows, pos(j)]
                snew_ref[tile_at(r), :] = s_new
            return carry

        lax.fori_loop(0, RET_DK // SUBLANES, update, 0)

    for i in range(n_pos):
        heads = []
        for a in range(2):
            def cross(grp, acc, a=a, i=i):
                r0 = pl.multiple_of(a * RET_DK + grp * SUBLANES, SUBLANES)
                qd_rows = qd_buf[pl.ds(r0, SUBLANES), pos(i)]
                for s in range(SUBLANES):
                    acc = acc + qd_rows[s:s + 1, :] * sb_buf[tile_at(r0 + s), :]
                return acc

            heads.append(lax.fori_loop(0, RET_DK // SUBLANES, cross,
                                       jnp.zeros((RET_DV, n_seq), F32)))
        oi = o[i] + jnp.concatenate(heads, axis=0).T
        on = _pair_groupnorm(oi, lo, rlng_ref[...], rlnb_ref[...])
        out_ref[i * n_seq:(i + 1) * n_seq, :] = (on * _silu(rows(g_ref[...], i))).astype(BF16)


def _ret_sample(q, k, v, g, state_t, pw, rln_g, rln_b, n_pos, n_seq):
    n_tok = n_pos * n_seq
    pair_state = 2 * RET_DK * RET_DV
    slab = pl.BlockSpec((n_tok, PAIR_LANES), lambda p: (0, p))
    state_spec = pl.BlockSpec((pair_state, n_seq), lambda p: (p, 0))
    return pl.pallas_call(
        functools.partial(_ret_sample_kernel, n_pos, n_seq),
        grid=(N_PAIRS,),
        in_specs=[slab, slab, slab, slab, state_spec,
                  pl.BlockSpec((8, PAIR_LANES), lambda p: (0, p)),
                  pl.BlockSpec((1, PAIR_LANES), lambda p: (0, p)),
                  pl.BlockSpec((1, PAIR_LANES), lambda p: (0, p))],
        out_specs=[slab, state_spec],
        out_shape=(jax.ShapeDtypeStruct((n_tok, RET_W), BF16),
                   jax.ShapeDtypeStruct(state_t.shape, F32)),
        scratch_shapes=[pltpu.VMEM((pair_state, n_seq), F32)]
        + [pltpu.VMEM((PAIR_LANES, n_tok), F32)] * 3,
        compiler_params=pltpu.CompilerParams(
            dimension_semantics=("arbitrary",), vmem_limit_bytes=VMEM_LIMIT),
        name="ret_sample",
    )(q, k, v, g, state_t, pw, rln_g.reshape(1, RET_W), rln_b.reshape(1, RET_W))


def _prompt_path(x, mod, mod_row0, g_mix, w_in, conv_w, conv_b, cln_g, cln_b, rln_g, rln_b,
                 w_out, g_ffn, w_ff1, w_ff2, g_final, tables, tile=PROMPT_TILE):
    rope, decay = tables["rope_prompt"], tables["decay"]
    y, hist, s2, *w_bf16 = _layer_prompt(x, mod, mod_row0, g_mix, w_in, conv_w, conv_b, cln_g,
                                         cln_b, rln_g, rln_b, w_out, g_ffn, w_ff1, w_ff2, g_final,
                                         rope, decay, tile)
    return y, hist[:, HIST_ROWS - (CONV_K - 1):], s2, w_bf16


def _sample_path(x, mod, cache, state, g_mix, w_in, conv_w, conv_b, cln_g, cln_b,
                 rln_g, rln_b, w_out, g_ffn, w_ff1, w_ff2, g_final, tables):
    n_seq, n_pos, d = x.shape
    xs = x.reshape(n_seq, n_pos * d)
    rope = tables["rope_sample"]
    hist_t, *tok = _proj_sample(xs, mod, g_mix, w_in, cache.transpose(1, 0, 2),
                                conv_w, conv_b, cln_g, cln_b, rope)
    conv_out, q, k, v, g = [a.reshape(n_pos * n_seq, -1) for a in tok]
    ret_out, s_new_t = _ret_sample(q, k, v, g, state.reshape(n_seq, -1).T, tables["powers"],
                                   rln_g, rln_b, n_pos, n_seq)
    s_new = s_new_t.T
    y = _mlp_sample(xs, conv_out, ret_out, w_out, mod, g_ffn, w_ff1, w_ff2, g_final)
    return y.reshape(x.shape), hist_t.transpose(1, 0, 2), s_new.reshape(state.shape)


def kernel(x_prompt, x_sample, cache_conv, state_ret, c_prompt, c_sample, w_ada, b_ada, g_mix, w_in, conv_w, conv_b, conv_ln_g, conv_ln_b, ret_ln_g, ret_ln_b, w_out, g_ffn, w_ff1, w_ff2, g_final):
    mod = _modulation(jnp.concatenate([c_sample, c_prompt], axis=0), w_ada[0], b_ada[0])
    params = (g_mix[0], w_in[0], conv_w[0], conv_b[0], conv_ln_g[0], conv_ln_b[0], ret_ln_g[0],
              ret_ln_b[0], w_out[0], g_ffn[0], w_ff1[0], w_ff2[0], g_final)
    tables = _const_tables(x_prompt.shape[1], PAST_LEN)
    y_p, conv_p, ret_p, (w_in_b, w_out_b, w_ff1_b, w_ff2_b) = _prompt_path(
        x_prompt, mod, x_sample.shape[0], *params, tables)
    params_b = (g_mix[0], w_in_b, conv_w[0], conv_b[0], conv_ln_g[0], conv_ln_b[0], ret_ln_g[0],
                ret_ln_b[0], w_out_b, g_ffn[0], w_ff1_b, w_ff2_b, g_final)
    y_s, conv_s, ret_s = _sample_path(x_sample, mod, cache_conv[0], state_ret[0], *params_b,
                                      tables)
    return (y_p, y_s, conv_p[None], ret_p[None], conv_s[None], ret_s[None])
```
